```python
import math
import jax, jax.numpy as jnp
from jax import lax
import numpy as np

D_MODEL = 1024
BATCH = 32
SEQ = 2048
DEPTH = 2

PLE_DIM = 256
N_EVEN = (DEPTH + 1) // 2
N_ODD = DEPTH // 2
D_CONV_A = D_MODEL // 2
CONV_A_WIDTH = 3
GDN_HEADS = 4
GDN_DK = D_MODEL // 8
GDN_DV = D_MODEL // 8
GDN_CONV_WIDTH = 4
GDN_CHUNK = 64
DIFF_HEADS = 4
DIFF_DH = D_MODEL // 16
DIFF_QBLK = 128
NUM_BUCKETS = 32
MAX_DISTANCE = 128
D_CONF = D_MODEL // 2
CONF_WIDTH = 31
D_FF = 11 * D_MODEL // 4
N_EXPERTS = 8
TOP_K = 2
D_FF_EXPERT = 7 * D_MODEL // 2
MOE_BLOCK = 256
EPS = 1e-6

EV_WIDTHS = (D_CONV_A, D_CONV_A, D_CONV_A,
             GDN_HEADS * GDN_DK, GDN_HEADS * GDN_DK, GDN_HEADS * GDN_DV,
             GDN_HEADS * GDN_DV, GDN_HEADS, GDN_HEADS)
EV_IN = 3 * D_CONV_A + 2 * GDN_HEADS * GDN_DK + 2 * GDN_HEADS * GDN_DV + 2 * GDN_HEADS
EV_OUT = D_CONV_A + GDN_HEADS * GDN_DV
OD_WIDTHS = (DIFF_HEADS * 2 * DIFF_DH, DIFF_HEADS * 2 * DIFF_DH, DIFF_HEADS * 2 * DIFF_DH, 2 * D_CONF)
OD_IN = 3 * DIFF_HEADS * 2 * DIFF_DH + 2 * D_CONF
OD_OUT = DIFF_HEADS * 2 * DIFF_DH + D_CONF

kernel_name = 'hybrid_conv_gdn_diffattn_conformer_moe'


def rmsnorm(x, g):
    xf = x.astype(jnp.float32)
    y = xf * lax.rsqrt(jnp.mean(xf * xf, axis=-1, keepdims=True) + EPS) * g.astype(jnp.float32)
    return y.astype(x.dtype)


def layernorm(x, g, b):
    xf = x.astype(jnp.float32)
    mu = jnp.mean(xf, axis=-1, keepdims=True)
    var = jnp.mean(jnp.square(xf - mu), axis=-1, keepdims=True)
    y = (xf - mu) * lax.rsqrt(var + EPS) * g.astype(jnp.float32) + b.astype(jnp.float32)
    return y.astype(x.dtype)


def l2norm(x):
    xf = x.astype(jnp.float32)
    return xf * lax.rsqrt(jnp.sum(xf * xf, axis=-1, keepdims=True) + EPS)


def split_cols(u, widths):
    offs = [int(o) for o in np.cumsum(widths)[:-1]]
    return jnp.split(u, offs, axis=-1)


def causal_dwconv(x, w):
    K, C = w.shape
    return lax.conv_general_dilated(x, w[:, None, :].astype(x.dtype), window_strides=(1,),
                                    padding=[(K - 1, 0)], dimension_numbers=('NWC', 'WIO', 'NWC'),
                                    feature_group_count=C)


def gated_delta_chunked(q, k, v, log_a, beta):
    Bb, Ss, H, dk = q.shape
    dv = v.shape[-1]
    C = GDN_CHUNK
    N = Ss // C
    f32 = jnp.float32
    def chunks(t):
        return t.astype(f32).reshape(Bb, N, C, H, -1).transpose(0, 3, 1, 2, 4)
    qc = chunks(q) * (dk ** -0.5)
    kc = chunks(k)
    vc = chunks(v)
    g = log_a.astype(f32).reshape(Bb, N, C, H).transpose(0, 3, 1, 2)
    bt = beta.astype(f32).reshape(Bb, N, C, H).transpose(0, 3, 1, 2)
    gc = jnp.cumsum(g, axis=-1)
    tril = jnp.tril(jnp.ones((C, C), bool))
    strict = jnp.tril(jnp.ones((C, C), bool), -1)
    diff = gc[..., :, None] - gc[..., None, :]
    decay = jnp.where(tril, jnp.exp(jnp.where(tril, diff, 0.0)), 0.0)
    kb = kc * bt[..., None]
    vb = vc * bt[..., None]
    A = jnp.where(strict, jnp.einsum('bhncd,bhnjd->bhncj', kb, kc) * decay, 0.0)
    M = A + jnp.eye(C, dtype=f32)
    u = lax.linalg.triangular_solve(M, vb, left_side=True, lower=True, unit_diagonal=True)
    w = lax.linalg.triangular_solve(M, kb * jnp.exp(gc)[..., None], left_side=True, lower=True,
                                    unit_diagonal=True)
    qk = jnp.einsum('bhncd,bhnjd->bhncj', qc, kc) * decay

    def step(S, xs):
        q_c, k_c, u_c, w_c, g_c, qk_c = xs
        v_new = u_c - jnp.einsum('bhcd,bhde->bhce', w_c, S)
        o = (jnp.einsum('bhcd,bhde->bhce', q_c * jnp.exp(g_c)[..., None], S)
             + jnp.einsum('bhcj,bhje->bhce', qk_c, v_new))
        g_last = g_c[..., -1]
        S = (S * jnp.exp(g_last)[..., None, None]
             + jnp.einsum('bhcd,bhce->bhde', k_c * jnp.exp(g_last[..., None] - g_c)[..., None], v_new))
        return S, o

    xs = (jnp.moveaxis(qc, 2, 0), jnp.moveaxis(kc, 2, 0), jnp.moveaxis(u, 2, 0),
          jnp.moveaxis(w, 2, 0), jnp.moveaxis(gc, 2, 0), jnp.moveaxis(qk, 2, 0))
    S0 = jnp.zeros((Bb, H, dk, dv), f32)
    _, o = lax.scan(step, S0, xs)
    return o.transpose(1, 0, 3, 2, 4).reshape(Bb, Ss, H, dv).astype(v.dtype)


def rel_bucket(rel):
    n = jnp.maximum(rel, 0)
    max_exact = NUM_BUCKETS // 2
    nf = jnp.maximum(n, 1).astype(jnp.float32)
    large = max_exact + (jnp.log(nf / max_exact) / math.log(MAX_DISTANCE / max_exact)
                         * (NUM_BUCKETS - max_exact)).astype(jnp.int32)
    large = jnp.minimum(large, NUM_BUCKETS - 1)
    return jnp.where(n < max_exact, n, large)


def diff_attention(q, k, v, lam, rel_table):
    Bb, Ss, H, _, dh = q.shape
    nq = Ss // DIFF_QBLK
    scale = dh ** -0.5
    qb = q.reshape(Bb, nq, DIFF_QBLK, H, 2, dh).transpose(1, 0, 2, 3, 4, 5)
    k_pos = jnp.arange(Ss, dtype=jnp.int32)

    def block(args):
        qi, iblk = args
        q_pos = iblk * DIFF_QBLK + jnp.arange(DIFF_QBLK, dtype=jnp.int32)
        rel = q_pos[:, None] - k_pos[None, :]
        bias = rel_table.astype(jnp.float32)[rel_bucket(rel)].transpose(2, 0, 1)
        s = jnp.einsum('bqhmd,bkhmd->bmhqk', qi, k).astype(jnp.float32) * scale + bias
        s = jnp.where(rel >= 0, s, -jnp.inf)
        pr = jax.nn.softmax(s, axis=-1)
        a = pr[:, 0] - lam * pr[:, 1]
        return jnp.einsum('bhqk,bkhe->bqhe', a.astype(v.dtype), v)

    o = lax.map(block, (qb, jnp.arange(nq, dtype=jnp.int32)))
    return o.transpose(1, 0, 2, 3, 4).reshape(Bb, Ss, H, 2 * dh)


def even_mixer(hn, w_in, conv_a, gdn_conv, A_log, dt_bias, gdn_norm_g, w_out):
    Bb, Ss, _ = hn.shape
    u = hn @ w_in
    bg, cg, xin, qf, kf, vf, og, af, bf = split_cols(u, EV_WIDTHS)
    ya = bg * causal_dwconv(cg * xin, conv_a)
    qkv = jax.nn.silu(causal_dwconv(jnp.concatenate([qf, kf, vf], axis=-1), gdn_conv))
    qf, kf, vf = split_cols(qkv, (GDN_HEADS * GDN_DK, GDN_HEADS * GDN_DK, GDN_HEADS * GDN_DV))
    q = l2norm(qf.reshape(Bb, Ss, GDN_HEADS, GDN_DK))
    k = l2norm(kf.reshape(Bb, Ss, GDN_HEADS, GDN_DK))
    v = vf.reshape(Bb, Ss, GDN_HEADS, GDN_DV)
    log_a = -jnp.exp(A_log.astype(jnp.float32)) * jax.nn.softplus(af.astype(jnp.float32) + dt_bias.astype(jnp.float32))
    beta = jax.nn.sigmoid(bf.astype(jnp.float32))
    o = gated_delta_chunked(q, k, v, log_a, beta)
    o = rmsnorm(o, gdn_norm_g) * jax.nn.silu(og.reshape(Bb, Ss, GDN_HEADS, GDN_DV))
    y = jnp.concatenate([ya, o.reshape(Bb, Ss, GDN_HEADS * GDN_DV)], axis=-1)
    return y @ w_out


def odd_mixer(hn, w_in, lam_params, diff_norm_g, conf_dw_w, conf_dw_b, conf_ln_g, conf_ln_b,
              w_out, rel_table, lambda_init):
    Bb, Ss, _ = hn.shape
    u = hn @ w_in
    qf, kf, vf, cf = split_cols(u, OD_WIDTHS)
    lp = lam_params.astype(jnp.float32)
    lam = jnp.exp(jnp.sum(lp[0] * lp[1])) - jnp.exp(jnp.sum(lp[2] * lp[3])) + lambda_init
    q = qf.reshape(Bb, Ss, DIFF_HEADS, 2, DIFF_DH)
    k = kf.reshape(Bb, Ss, DIFF_HEADS, 2, DIFF_DH)
    v = vf.reshape(Bb, Ss, DIFF_HEADS, 2 * DIFF_DH)
    o = diff_attention(q, k, v, lam, rel_table)
    o = rmsnorm(o, diff_norm_g) * (1.0 - lambda_init)
    ga, gb = jnp.split(cf, 2, axis=-1)
    c = ga * jax.nn.sigmoid(gb)
    c = causal_dwconv(c, conf_dw_w) + conf_dw_b.astype(c.dtype)
    c = jax.nn.silu(layernorm(c, conf_ln_g, conf_ln_b))
    y = jnp.concatenate([o.reshape(Bb, Ss, DIFF_HEADS * 2 * DIFF_DH), c], axis=-1)
    return y @ w_out


def swiglu(hn, w_gate_up, w_down):
    g, u = jnp.split(hn @ w_gate_up, 2, axis=-1)
    return (jax.nn.silu(g) * u) @ w_down


def moe_swiglu(hn, w_router, w_gate_up, w_down):
    Bb, Ss, D = hn.shape
    T = Bb * Ss
    xt = hn.reshape(T, D)
    logits = (xt @ w_router).astype(jnp.float32)
    top_val, top_idx = lax.top_k(logits, TOP_K)
    gates = jax.nn.softmax(top_val, axis=-1)
    e_flat = top_idx.reshape(-1).astype(jnp.int32)
    tok_flat = jnp.repeat(jnp.arange(T, dtype=jnp.int32), TOP_K)
    g_flat = gates.reshape(-1)
    order = jnp.argsort(e_flat)
    e_s, tok_s, g_s = e_flat[order], tok_flat[order], g_flat[order]
    counts = jnp.zeros((N_EXPERTS,), jnp.int32).at[e_flat].add(1)
    padded = (counts + MOE_BLOCK - 1) // MOE_BLOCK * MOE_BLOCK
    start = jnp.cumsum(counts) - counts
    pend = jnp.cumsum(padded)
    pstart = pend - padded
    dest = pstart[e_s] + (jnp.arange(T * TOP_K, dtype=jnp.int32) - start[e_s])
    P = T * TOP_K + N_EXPERTS * MOE_BLOCK
    NB = P // MOE_BLOCK
    row_tok = jnp.full((P,), T, jnp.int32).at[dest].set(tok_s)
    xpad = jnp.concatenate([xt, jnp.zeros((1, D), xt.dtype)], axis=0)
    rows = xpad[row_tok].reshape(NB, MOE_BLOCK, D)
    blk_e = jnp.minimum(jnp.searchsorted(pend, jnp.arange(NB, dtype=jnp.int32) * MOE_BLOCK, side='right'),
                        N_EXPERTS - 1)

    def expert_block(args):
        xb, e = args
        g, u = jnp.split(xb @ w_gate_up[e], 2, axis=-1)
        return (jax.nn.silu(g) * u) @ w_down[e]

    y = lax.map(expert_block, (rows, blk_e)).reshape(P, D)
    contrib = y[dest] * g_s[:, None].astype(y.dtype)
    out = jnp.zeros((T, D), y.dtype).at[tok_s].add(contrib)
    return out.reshape(Bb, Ss, D)


def per_layer_embed(h, p_i, g_norm, w_proj, w_gate):
    gate = jax.nn.sigmoid((rmsnorm(h, g_norm) @ w_gate).astype(jnp.float32)).astype(h.dtype)
    return (p_i.astype(h.dtype) @ w_proj) * gate


def setup_inputs(seed: int = 0) -> dict:
    key = jax.random.key(seed)
    ks = jax.random.split(key, 40)
    f32 = jnp.float32
    def nrm(k, shape, scale):
        return jax.random.normal(k, shape, f32) * scale
    def gain(k, shape):
        return 1.0 + 0.02 * jax.random.normal(k, shape, f32)
    D = D_MODEL
    A = jax.random.uniform(ks[9], (N_EVEN, GDN_HEADS), f32, 1.0, 16.0)
    dt = jnp.exp(jax.random.uniform(ks[10], (N_EVEN, GDN_HEADS), f32, math.log(1e-3), math.log(1e-1)))
    return {
        'x': nrm(ks[0], (BATCH, SEQ, D), 1.0),
        'p': nrm(ks[1], (DEPTH, BATCH, SEQ, PLE_DIM), 1.0),
        'norm_mix_g': gain(ks[2], (DEPTH, D)),
        'norm_ffn_g': gain(ks[3], (DEPTH, D)),
        'norm_ple_g': gain(ks[4], (DEPTH, D)),
        'final_norm_g': gain(ks[5], (D,)),
        'ev_w_in': nrm(ks[6], (N_EVEN, D, EV_IN), D ** -0.5),
        'ev_conv_a': nrm(ks[7], (N_EVEN, CONV_A_WIDTH, D_CONV_A), CONV_A_WIDTH ** -0.5),
        'ev_gdn_conv': nrm(ks[8], (N_EVEN, GDN_CONV_WIDTH, 2 * GDN_HEADS * GDN_DK + GDN_HEADS * GDN_DV), GDN_CONV_WIDTH ** -0.5),
        'ev_gdn_A_log': jnp.log(A),
        'ev_gdn_dt_bias': dt + jnp.log(-jnp.expm1(-dt)),
        'ev_gdn_norm_g': gain(ks[11], (N_EVEN, GDN_DV)),
        'ev_w_out': nrm(ks[12], (N_EVEN, EV_OUT, D), EV_OUT ** -0.5),
        'od_w_in': nrm(ks[13], (N_ODD, D, OD_IN), D ** -0.5),
        'od_lambda': nrm(ks[14], (N_ODD, 4, DIFF_DH), 0.1),
        'od_diff_norm_g': gain(ks[15], (N_ODD, 2 * DIFF_DH)),
        'od_conf_dw_w': nrm(ks[16], (N_ODD, CONF_WIDTH, D_CONF), CONF_WIDTH ** -0.5),
        'od_conf_dw_b': nrm(ks[17], (N_ODD, D_CONF), 0.02),
        'od_conf_ln_g': gain(ks[18], (N_ODD, D_CONF)),
        'od_conf_ln_b': nrm(ks[19], (N_ODD, D_CONF), 0.02),
        'od_w_out': nrm(ks[20], (N_ODD, OD_OUT, D), OD_OUT ** -0.5),
        'rel_bias': nrm(ks[21], (NUM_BUCKETS, DIFF_HEADS), 0.1),
        'ffn_w_gate_up': nrm(ks[22], (N_EVEN, D, 2 * D_FF), D ** -0.5),
        'ffn_w_down': nrm(ks[23], (N_EVEN, D_FF, D), D_FF ** -0.5),
        'moe_router': nrm(ks[24], (N_ODD, D, N_EXPERTS), D ** -0.5),
        'moe_w_gate_up': nrm(ks[25], (N_ODD, N_EXPERTS, D, 2 * D_FF_EXPERT), D ** -0.5),
        'moe_w_down': nrm(ks[26], (N_ODD, N_EXPERTS, D_FF_EXPERT, D), D_FF_EXPERT ** -0.5),
        'ple_w_proj': nrm(ks[27], (DEPTH, PLE_DIM, D), PLE_DIM ** -0.5),
        'ple_w_gate': nrm(ks[28], (DEPTH, D, D), D ** -0.5),
    }


def reference(x, p, norm_mix_g, norm_ffn_g, norm_ple_g, final_norm_g,
              ev_w_in, ev_conv_a, ev_gdn_conv, ev_gdn_A_log, ev_gdn_dt_bias, ev_gdn_norm_g, ev_w_out,
              od_w_in, od_lambda, od_diff_norm_g, od_conf_dw_w, od_conf_dw_b, od_conf_ln_g, od_conf_ln_b,
              od_w_out, rel_bias, ffn_w_gate_up, ffn_w_down, moe_router, moe_w_gate_up, moe_w_down,
              ple_w_proj, ple_w_gate):
    h = x
    for i in range(DEPTH):
        j = i // 2
        if i % 2 == 0:
            h = h + even_mixer(rmsnorm(h, norm_mix_g[i]), ev_w_in[j], ev_conv_a[j], ev_gdn_conv[j],
                               ev_gdn_A_log[j], ev_gdn_dt_bias[j], ev_gdn_norm_g[j], ev_w_out[j])
            h = h + swiglu(rmsnorm(h, norm_ffn_g[i]), ffn_w_gate_up[j], ffn_w_down[j])
        else:
            lambda_init = 0.8 - 0.6 * math.exp(-0.3 * i)
            h = h + odd_mixer(rmsnorm(h, norm_mix_g[i]), od_w_in[j], od_lambda[j], od_diff_norm_g[j],
                              od_conf_dw_w[j], od_conf_dw_b[j], od_conf_ln_g[j], od_conf_ln_b[j],
                              od_w_out[j], rel_bias, lambda_init)
            h = h + moe_swiglu(rmsnorm(h, norm_ffn_g[i]), moe_router[j], moe_w_gate_up[j], moe_w_down[j])
        h = h + per_layer_embed(h, p[i], norm_ple_g[i], ple_w_proj[i], ple_w_gate[i])
    return rmsnorm(h, final_norm_g)
```

```python
import functools
import math

import jax
import jax.numpy as jnp
from jax import lax
from jax.experimental import pallas as pl
from jax.experimental.pallas import tpu as pltpu

F32 = jnp.float32
BF16 = jnp.bfloat16
HIGHEST = lax.Precision.HIGHEST

EPS = 1e-6
CONV_A_WIDTH = 3
GDN_HEADS = 4
GDN_CONV_WIDTH = 4
GDN_CHUNK = 64
DIFF_HEADS = 4
NUM_BUCKETS = 32
MAX_DISTANCE = 128
CONF_WIDTH = 31
N_EXPERTS = 8
TOP_K = 2

LANES = 128
SUBLANES = 8
VMEM_LIMIT_BYTES = 56 * 1024 * 1024
MASK_VALUE = -1e30

ROW_TILE = 512
COL_TILE = 512
ELEM_ROWS = 256
CONF_ROWS = 32
ATTN_TILE = 256
MOE_ROWS = 1024
MOE_FF_TILE = 512
GATHER_ROWS = 512
COMBINE_ROWS = 256
TOP_PAD = 32


def _cparams(*sem):
    return pltpu.CompilerParams(dimension_semantics=sem, vmem_limit_bytes=VMEM_LIMIT_BYTES)


def _resident(shape):
    nd = len(shape)
    return pl.BlockSpec(shape, lambda *_: (0,) * nd, pipeline_mode=pl.Buffered(1))


def _rms(x, g):
    return x * lax.rsqrt(jnp.mean(x * x, axis=-1, keepdims=True) + EPS) * g


def _sigmoid(x):
    return jax.nn.sigmoid(x)


def _silu(x):
    return x * jax.nn.sigmoid(x)


def _softplus(x):
    return jnp.maximum(x, 0.0) + jnp.log1p(jnp.exp(-jnp.abs(x)))


def _dot(a, b, **kw):
    return jnp.dot(a, b, preferred_element_type=F32, **kw)


def _dot_nt(a, b):
    return lax.dot_general(a, b, (((1,), (1,)), ((), ())), preferred_element_type=F32)


def _dot_tn(a, b):
    return lax.dot_general(a, b, (((0,), (0,)), ((), ())), preferred_element_type=F32)


def _delayed(pad_ref, r, rows, max_delay):
    lead = -(-max_delay // SUBLANES) * SUBLANES
    win = pad_ref[pl.ds(TOP_PAD + r - lead, rows + lead), :]
    rolled = {0: win}

    def tap(d):
        a, b = divmod(d, SUBLANES)
        if b not in rolled:
            rolled[b] = pltpu.roll(win, b, 0)
        start = lead - SUBLANES * a
        return rolled[b][start:start + rows, :]
    return tap


def _norm_proj_kernel(h_ref, g_ref, w_ref, *rest, tn, with_aux):
    xn = _rms(h_ref[...], g_ref[...]).astype(BF16)
    if with_aux:
        w2_ref, o_ref, o2_ref = rest
        o2_ref[...] = _dot(xn, w2_ref[...])
    else:
        (o_ref,) = rest
    n = w_ref.shape[1]
    for c in range(0, n, tn):
        o_ref[:, c:c + tn] = _dot(xn, w_ref[:, c:c + tn]).astype(o_ref.dtype)


def _norm_proj(h, g, w, w_aux=None, *, name):
    t, d = h.shape
    n = w.shape[1]
    tm = min(ROW_TILE, t)
    tn = COL_TILE if n % COL_TILE == 0 else n
    in_specs = [pl.BlockSpec((tm, d), lambda i: (i, 0)), _resident((1, d)), _resident((d, n))]
    out_shape = [jax.ShapeDtypeStruct((t, n), BF16)]
    out_specs = [pl.BlockSpec((tm, n), lambda i: (i, 0))]
    args = [h, g.reshape(1, d), w]
    if w_aux is not None:
        in_specs.append(_resident(w_aux.shape))
        out_shape.append(jax.ShapeDtypeStruct((t, w_aux.shape[1]), F32))
        out_specs.append(pl.BlockSpec((tm, w_aux.shape[1]), lambda i: (i, 0)))
        args.append(w_aux)
    out = pl.pallas_call(
        functools.partial(_norm_proj_kernel, tn=tn, with_aux=w_aux is not None),
        grid=(t // tm,), in_specs=in_specs, out_specs=out_specs, out_shape=out_shape,
        compiler_params=_cparams("parallel"), name=name)(*args)
    return out if w_aux is not None else out[0]


def _proj_residual_kernel(y_ref, w_ref, h_ref, o_ref):
    o_ref[...] = h_ref[...] + _dot(y_ref[...], w_ref[...])


def _proj_residual(y, w, h, *, name):
    t, k = y.shape
    d = w.shape[1]
    tm = min(ROW_TILE, t)
    return pl.pallas_call(
        _proj_residual_kernel, grid=(t // tm,),
        in_specs=[pl.BlockSpec((tm, k), lambda i: (i, 0)), _resident((k, d)),
                  pl.BlockSpec((tm, d), lambda i: (i, 0))],
        out_specs=pl.BlockSpec((tm, d), lambda i: (i, 0)),
        out_shape=jax.ShapeDtypeStruct((t, d), F32),
        compiler_params=_cparams("parallel"), name=name)(y, w, h)


def _ffn_kernel(h_ref, g_ref, wg_ref, wu_ref, wd_ref, o_ref, *, tf):
    x = h_ref[...]
    xn = _rms(x, g_ref[...]).astype(BF16)
    acc = x
    for c in range(0, wg_ref.shape[1], tf):
        gate = _dot(xn, wg_ref[:, c:c + tf])
        up = _dot(xn, wu_ref[:, c:c + tf])
        hid = (_silu(gate) * up).astype(BF16)
        acc = acc + _dot(hid, wd_ref[c:c + tf, :])
    o_ref[...] = acc


def _ff_tile(f, cap):
    best = LANES
    for c in range(LANES, cap + 1, LANES):
        if f % c == 0:
            best = c
    return best


def _ffn(h, g, wg, wu, wd, *, name):
    t, d = h.shape
    f = wg.shape[1]
    tm = min(ROW_TILE, t)
    return pl.pallas_call(
        functools.partial(_ffn_kernel, tf=_ff_tile(f, 1536)), grid=(t // tm,),
        in_specs=[pl.BlockSpec((tm, d), lambda i: (i, 0)), _resident((1, d)),
                  _resident((d, f)), _resident((d, f)), _resident((f, d))],
        out_specs=pl.BlockSpec((tm, d), lambda i: (i, 0)),
        out_shape=jax.ShapeDtypeStruct((t, d), F32),
        compiler_params=_cparams("parallel"), name=name)(h, g.reshape(1, d), wg, wu, wd)


def _ple_math(x, g_ref, p_ref, wp_ref, wg_ref):
    xn = _rms(x, g_ref[...]).astype(BF16)
    gate = _sigmoid(_dot(xn, wg_ref[...]))
    emb = _dot(p_ref[...].astype(BF16), wp_ref[...])
    return x + emb * gate


def _ple_kernel(h_ref, g_ref, p_ref, wp_ref, wg_ref, o_ref):
    o_ref[...] = _ple_math(h_ref[...], g_ref, p_ref, wp_ref, wg_ref)


def _ple(h, g, p, wp, wg, *, name):
    t, d = h.shape
    e = p.shape[1]
    tm = min(ROW_TILE, t)
    return pl.pallas_call(
        _ple_kernel, grid=(t // tm,),
        in_specs=[pl.BlockSpec((tm, d), lambda i: (i, 0)), _resident((1, d)),
                  pl.BlockSpec((tm, e), lambda i: (i, 0)), _resident((e, d)), _resident((d, d))],
        out_specs=pl.BlockSpec((tm, d), lambda i: (i, 0)),
        out_shape=jax.ShapeDtypeStruct((t, d), F32),
        compiler_params=_cparams("parallel"), name=name)(h, g.reshape(1, d), p, wp, wg)


def _unit_lower_inverse(a):
    n = a.shape[0]
    row = lax.broadcasted_iota(jnp.int32, (n, n), 0)
    col = lax.broadcasted_iota(jnp.int32, (n, n), 1)
    eye = (row == col).astype(F32)
    same16 = (row // 16) == (col // 16)
    same32 = (row // 32) == (col // 32)
    hp = functools.partial(_dot, precision=HIGHEST)
    ad = jnp.where(same16, a, 0.0)
    a2 = hp(ad, ad)
    a4 = hp(a2, a2)
    a8 = hp(a4, a4)
    x = hp(hp(hp(eye - ad, eye + a2), eye + a4), eye + a8)
    a32 = jnp.where(jnp.logical_and(same32, jnp.logical_not(same16)), a, 0.0)
    x = x - hp(x, hp(a32, x))
    a64 = jnp.where(same32, 0.0, a)
    x = x - hp(x, hp(a64, x))
    return x


def _l0_mixer_kernel(bg_ref, cg_ref, xin_ref, q_ref, k_ref, v_ref, og_ref, ab_ref,
                     wa_ref, wq_ref, wk_ref, wv_ref, alog_ref, dtb_ref, gn_ref,
                     o_ref,
                     pad_ref, qs_ref, ks_ref, vs_ref, us_ref, ws_ref, qk_ref, egl_ref, st_ref):
    s, ca = bg_ref.shape
    heads = GDN_HEADS
    dk = q_ref.shape[1] // heads
    c64 = GDN_CHUNK
    n_chunks = s // c64
    rt = min(ELEM_ROWS, s)
    n_tiles = s // rt

    pad_ref[0:TOP_PAD, :] = jnp.zeros((TOP_PAD, pad_ref.shape[1]), F32)

    def rows(i):
        return pl.ds(pl.multiple_of(i * rt, rt), rt)

    def fill_pad(fn):
        def body(i, c):
            r = pl.multiple_of(i * rt, rt)
            pad_ref[pl.ds(TOP_PAD + r, rt), :] = fn(rows(i))
            return c
        lax.fori_loop(0, n_tiles, body, 0)

    def conv_tile(i, w_ref, width):
        tap = _delayed(pad_ref, pl.multiple_of(i * rt, rt), rt, width - 1)
        acc = None
        for j in range(width):
            term = w_ref[j:j + 1, :] * tap(width - 1 - j)
            acc = term if acc is None else acc + term
        return acc

    fill_pad(lambda rs: cg_ref[rs, :].astype(F32) * xin_ref[rs, :].astype(F32))

    def mixer_a(i, c):
        acc = conv_tile(i, wa_ref, CONV_A_WIDTH)
        o_ref[rows(i), 0:ca] = (bg_ref[rows(i), :].astype(F32) * acc).astype(o_ref.dtype)
        return c
    lax.fori_loop(0, n_tiles, mixer_a, 0)

    def l2n(x, scale):
        parts = []
        for h in range(heads):
            xh = x[:, h * dk:(h + 1) * dk]
            inv = lax.rsqrt(jnp.sum(xh * xh, axis=-1, keepdims=True) + EPS)
            parts.append(xh * (inv * scale))
        return jnp.concatenate(parts, axis=1)

    for src_ref, w_ref, dst_ref, post in (
            (q_ref, wq_ref, qs_ref, lambda x: l2n(x, dk ** -0.5)),
            (k_ref, wk_ref, ks_ref, lambda x: l2n(x, 1.0)),
            (v_ref, wv_ref, vs_ref, lambda x: x)):
        fill_pad(lambda rs, src_ref=src_ref: src_ref[rs, :].astype(F32))

        def conv_body(i, c, w_ref=w_ref, dst_ref=dst_ref, post=post):
            dst_ref[rows(i), :] = post(_silu(conv_tile(i, w_ref, GDN_CONV_WIDTH)))
            return c
        lax.fori_loop(0, n_tiles, conv_body, 0)

    ri = lax.broadcasted_iota(jnp.int32, (c64, c64), 0)
    ci = lax.broadcasted_iota(jnp.int32, (c64, c64), 1)
    tril = ri >= ci
    strict = ri > ci
    ltri = tril.astype(F32)

    def chunk_prep(c, carry):
        r = pl.multiple_of(c * c64, c64)
        rs = pl.ds(r, c64)
        ab = ab_ref[rs, :]
        log_a = -jnp.exp(alog_ref[...]) * _softplus(ab + dtb_ref[...])
        beta = _sigmoid(ab)
        gc = _dot(ltri, log_a, precision=HIGHEST)
        gct = gc.T
        for h in range(heads):
            hs = slice(h * dk, (h + 1) * dk)
            gcol = gc[:, h:h + 1]
            grow = gct[h:h + 1, :]
            glast = gc[c64 - 1:c64, h:h + 1]
            bcol = beta[:, heads + h:heads + h + 1]
            decay = jnp.where(tril, jnp.exp(jnp.where(tril, gcol - grow, 0.0)), 0.0)
            kh = ks_ref[rs, hs]
            qh = qs_ref[rs, hs]
            vh = vs_ref[rs, hs]
            kb = kh * bcol
            egc = jnp.exp(gcol)
            kq = _dot_nt(jnp.concatenate([kb, qh], axis=0).astype(BF16), kh.astype(BF16))
            a = jnp.where(strict, kq[0:c64] * decay, 0.0)
            qk_ref[rs, h * c64:(h + 1) * c64] = kq[c64:2 * c64] * decay
            minv = _unit_lower_inverse(a)
            uw = _dot(minv, jnp.concatenate([vh * bcol, kb * egc], axis=1), precision=HIGHEST)
            us_ref[rs, hs] = uw[:, 0:dk]
            ws_ref[rs, hs] = uw[:, dk:2 * dk]
            qs_ref[rs, hs] = qh * egc
            ks_ref[rs, hs] = kh * jnp.exp(glast - gcol)
            e0 = pl.multiple_of((c * heads + h) * SUBLANES, SUBLANES)
            egl_ref[pl.ds(e0, SUBLANES), :] = jnp.broadcast_to(jnp.exp(glast), (SUBLANES, dk))
        return carry
    lax.fori_loop(0, n_chunks, chunk_prep, 0)

    st_ref[...] = jnp.zeros(st_ref.shape, F32)

    def scan(c, carry):
        r = pl.multiple_of(c * c64, c64)
        rs = pl.ds(r, c64)
        for h in range(heads):
            hs = slice(h * dk, (h + 1) * dk)
            st = st_ref[h]
            stb = st.astype(BF16)
            wq = _dot(jnp.concatenate([ws_ref[rs, hs], qs_ref[rs, hs]], axis=0).astype(BF16), stb)
            v_new = us_ref[rs, hs] - wq[0:c64]
            vb = v_new.astype(BF16)
            o = wq[c64:2 * c64] + _dot(qk_ref[rs, h * c64:(h + 1) * c64].astype(BF16), vb)
            e0 = pl.multiple_of((c * heads + h) * SUBLANES, SUBLANES)
            eg = egl_ref[pl.ds(e0, SUBLANES), :][0:1, :]
            st_ref[h] = st * eg + _dot_tn(ks_ref[rs, hs].astype(BF16), vb)
            vs_ref[rs, hs] = o
        return carry
    lax.fori_loop(0, n_chunks, scan, 0)

    def finish(i, c):
        o = vs_ref[rows(i), :]
        og = og_ref[rows(i), :].astype(F32)
        parts = []
        for h in range(heads):
            oh = o[:, h * dk:(h + 1) * dk]
            parts.append(_rms(oh, gn_ref[...]))
        y = jnp.concatenate(parts, axis=1) * _silu(og)
        o_ref[rows(i), ca:ca + heads * dk] = y.astype(o_ref.dtype)
        return c
    lax.fori_loop(0, n_tiles, finish, 0)


def _l0_mixer(u, ab, conv_a, gdn_conv, a_log, dt_bias, gdn_norm_g, batch, seq):
    cw = conv_a.shape[1]
    heads = GDN_HEADS
    dk = cw // heads

    def col(j):
        return pl.BlockSpec((seq, cw), lambda b, j=j: (b, j), pipeline_mode=pl.Buffered(1))

    def lane_row(x):
        return jnp.zeros((1, LANES), F32).at[0, :x.shape[0]].set(x)

    in_specs = [col(j) for j in range(7)]
    in_specs += [pl.BlockSpec((seq, LANES), lambda b: (b, 0)),
                 _resident(conv_a.shape)]
    in_specs += [pl.BlockSpec((GDN_CONV_WIDTH, cw), lambda b, j=j: (0, j), pipeline_mode=pl.Buffered(1))
                 for j in range(3)]
    in_specs += [_resident((1, LANES)), _resident((1, LANES)), _resident((1, dk))]
    big = pltpu.VMEM((seq, cw), F32)
    scratch = [pltpu.VMEM((seq + TOP_PAD, cw), F32), big, big, big, big, big,
               pltpu.VMEM((seq, heads * GDN_CHUNK), F32),
               pltpu.VMEM((seq // GDN_CHUNK * heads * SUBLANES, dk), F32),
               pltpu.VMEM((heads, dk, dk), F32)]
    return pl.pallas_call(
        _l0_mixer_kernel, grid=(batch,), in_specs=in_specs,
        out_specs=pl.BlockSpec((seq, 2 * cw), lambda b: (b, 0), pipeline_mode=pl.Buffered(1)),
        out_shape=jax.ShapeDtypeStruct((batch * seq, 2 * cw), BF16),
        scratch_shapes=scratch, compiler_params=_cparams("parallel"), name="l0_mixers",
    )(u, u, u, u, u, u, u, ab, conv_a, gdn_conv, gdn_conv, gdn_conv,
      lane_row(a_log), lane_row(dt_bias), gdn_norm_g.reshape(1, dk))


def _bias_table_kernel(tab_ref, o_ref, *, t):
    h = pl.program_id(0)
    m = pl.program_id(1)
    ri = lax.broadcasted_iota(jnp.int32, (t, t), 0)
    ci = lax.broadcasted_iota(jnp.int32, (t, t), 1)
    rel = m * t + ri - ci
    n = jnp.maximum(rel, 0)
    max_exact = NUM_BUCKETS // 2
    nf = jnp.maximum(n, 1).astype(F32)
    large = max_exact + (jnp.log(nf / max_exact) / math.log(MAX_DISTANCE / max_exact)
                         * (NUM_BUCKETS - max_exact)).astype(jnp.int32)
    large = jnp.minimum(large, NUM_BUCKETS - 1)
    bucket = jnp.where(n < max_exact, n, large)
    bias = jnp.zeros((t, t), F32)
    for b in range(NUM_BUCKETS):
        bias = jnp.where(bucket == b, tab_ref[b, h], bias)
    o_ref[...] = jnp.where(rel >= 0, bias, MASK_VALUE)


def _bias_table(rel_bias, seq, t):
    nb = seq // t
    heads = rel_bias.shape[1]
    return pl.pallas_call(
        functools.partial(_bias_table_kernel, t=t), grid=(heads, nb),
        in_specs=[pl.BlockSpec(memory_space=pltpu.SMEM)],
        out_specs=pl.BlockSpec((None, None, t, t), lambda h, m: (h, m, 0, 0)),
        out_shape=jax.ShapeDtypeStruct((heads, nb, t, t), F32),
        compiler_params=_cparams("parallel", "parallel"), name="rel_bias_table")(rel_bias)


def _attn_kernel(q_ref, k_ref, v_ref, tb_ref, lam_ref, gn_ref, o_ref, *, t, lambda_init):
    qi = pl.program_id(2)
    dh2 = q_ref.shape[1]
    dh = dh2 // 2
    lane = lax.broadcasted_iota(jnp.int32, (t, dh2), 1)
    qf = q_ref[...].astype(F32) * (dh ** -0.5)
    qq = jnp.concatenate([jnp.where(lane < dh, qf, 0.0), jnp.where(lane >= dh, qf, 0.0)],
                         axis=0).astype(BF16)

    def body(j, carry):
        m, l, acc = carry
        ks = pl.ds(pl.multiple_of(j * t, t), t)
        s = _dot_nt(qq, k_ref[ks, :])
        b = tb_ref[qi - j]
        s = s + jnp.concatenate([b, b], axis=0)
        m_new = jnp.maximum(m, jnp.max(s, axis=-1, keepdims=True))
        alpha = jnp.exp(m - m_new)
        p = jnp.exp(s - m_new)
        l = alpha * l + jnp.sum(p, axis=-1, keepdims=True)
        acc = alpha * acc + _dot(p.astype(BF16), v_ref[ks, :])
        return m_new, l, acc

    init = (jnp.full((2 * t, 1), MASK_VALUE, F32), jnp.zeros((2 * t, 1), F32),
            jnp.zeros((2 * t, dh2), F32))
    _, l, acc = lax.fori_loop(0, qi + 1, body, init)
    o12 = acc / l
    lp = lam_ref[...]
    lam = (jnp.exp(jnp.sum(lp[0:1] * lp[1:2], axis=-1, keepdims=True))
           - jnp.exp(jnp.sum(lp[2:3] * lp[3:4], axis=-1, keepdims=True)) + lambda_init)
    o = o12[0:t] - lam * o12[t:2 * t]
    o_ref[...] = (_rms(o, gn_ref[...]) * (1.0 - lambda_init)).astype(o_ref.dtype)


def _diff_attention(u, table, lam_params, norm_g, batch, seq, lambda_init):
    heads = DIFF_HEADS
    dh2 = norm_g.shape[0]
    t = table.shape[2]
    nq = seq // t
    kcol, vcol = heads, 2 * heads
    return pl.pallas_call(
        functools.partial(_attn_kernel, t=t, lambda_init=lambda_init),
        grid=(batch, heads, nq),
        in_specs=[pl.BlockSpec((t, dh2), lambda b, h, i: (b * nq + i, h)),
                  pl.BlockSpec((seq, dh2), lambda b, h, i: (b, kcol + h)),
                  pl.BlockSpec((seq, dh2), lambda b, h, i: (b, vcol + h)),
                  pl.BlockSpec((None, nq, t, t), lambda b, h, i: (h, 0, 0, 0)),
                  _resident(lam_params.shape), _resident((1, dh2))],
        out_specs=pl.BlockSpec((t, dh2), lambda b, h, i: (b * nq + i, h)),
        out_shape=jax.ShapeDtypeStruct((batch * seq, heads * dh2), BF16),
        compiler_params=_cparams("parallel", "parallel", "arbitrary"), name="diff_attention",
    )(u, u, u, table, lam_params, norm_g.reshape(1, dh2))


def _conformer_kernel(ga_ref, gb_ref, w_ref, b_ref, lg_ref, lb_ref, o_ref, pad_ref):
    s, c = ga_ref.shape
    rt = min(ELEM_ROWS, s)
    pad_ref[0:TOP_PAD, :] = jnp.zeros((TOP_PAD, c), F32)

    def glu(i, carry):
        r = pl.multiple_of(i * rt, rt)
        rs = pl.ds(r, rt)
        pad_ref[pl.ds(TOP_PAD + r, rt), :] = ga_ref[rs, :].astype(F32) * _sigmoid(gb_ref[rs, :].astype(F32))
        return carry
    lax.fori_loop(0, s // rt, glu, 0)

    ct = CONF_ROWS

    def conv(i, carry):
        r = pl.multiple_of(i * ct, ct)
        tap = _delayed(pad_ref, r, ct, CONF_WIDTH - 1)
        acc = jnp.broadcast_to(b_ref[...], (ct, c))
        for j in range(CONF_WIDTH):
            acc = acc + w_ref[j:j + 1, :] * tap(CONF_WIDTH - 1 - j)
        mu = jnp.mean(acc, axis=-1, keepdims=True)
        xc = acc - mu
        var = jnp.mean(xc * xc, axis=-1, keepdims=True)
        y = xc * lax.rsqrt(var + EPS) * lg_ref[...] + lb_ref[...]
        o_ref[pl.ds(r, ct), :] = _silu(y).astype(o_ref.dtype)
        return carry
    lax.fori_loop(0, s // ct, conv, 0)


def _conformer(u, w, b, ln_g, ln_b, batch, seq, col0):
    c = w.shape[1]
    return pl.pallas_call(
        _conformer_kernel, grid=(batch,),
        in_specs=[pl.BlockSpec((seq, c), lambda i: (i, col0)),
                  pl.BlockSpec((seq, c), lambda i: (i, col0 + 1)),
                  _resident(w.shape), _resident((1, c)), _resident((1, c)), _resident((1, c))],
        out_specs=pl.BlockSpec((seq, c), lambda i: (i, 0)),
        out_shape=jax.ShapeDtypeStruct((batch * seq, c), BF16),
        scratch_shapes=[pltpu.VMEM((seq + TOP_PAD, c), F32)],
        compiler_params=_cparams("parallel"), name="conformer_conv",
    )(u, u, w, b.reshape(1, c), ln_g.reshape(1, c), ln_b.reshape(1, c))


def _router_kernel(h_ref, g_ref, wr_ref, hn_ref, idx_ref, gate_ref):
    xn = _rms(h_ref[...], g_ref[...])
    hn_ref[...] = xn.astype(hn_ref.dtype)
    logits = _dot(xn, wr_ref[...], precision=HIGHEST)
    lane = lax.broadcasted_iota(jnp.int32, logits.shape, 1)
    neg = -jnp.inf
    lg = jnp.where(lane < N_EXPERTS, logits, neg)
    m1 = jnp.max(lg, axis=-1, keepdims=True)
    i1 = jnp.min(jnp.where(lg == m1, lane, LANES), axis=-1, keepdims=True)
    lg2 = jnp.where(lane == i1, neg, lg)
    m2 = jnp.max(lg2, axis=-1, keepdims=True)
    i2 = jnp.min(jnp.where(lg2 == m2, lane, LANES), axis=-1, keepdims=True)
    e = jnp.exp(m2 - m1)
    g1 = 1.0 / (1.0 + e)
    idx_ref[...] = jnp.where(lane == 0, i1, jnp.where(lane == 1, i2, 0))
    gate_ref[...] = jnp.where(lane == 0, g1, jnp.where(lane == 1, e * g1, 0.0))


def _router(h, g, wr):
    t, d = h.shape
    tm = min(ROW_TILE, t)
    return pl.pallas_call(
        _router_kernel, grid=(t // tm,),
        in_specs=[pl.BlockSpec((tm, d), lambda i: (i, 0)), _resident((1, d)), _resident((d, LANES))],
        out_specs=[pl.BlockSpec((tm, d), lambda i: (i, 0)),
                   pl.BlockSpec((tm, LANES), lambda i: (i, 0)),
                   pl.BlockSpec((tm, LANES), lambda i: (i, 0))],
        out_shape=[jax.ShapeDtypeStruct((t, d), F32), jax.ShapeDtypeStruct((t, LANES), jnp.int32),
                   jax.ShapeDtypeStruct((t, LANES), F32)],
        compiler_params=_cparams("parallel"), name="moe_router")(h, g.reshape(1, d), wr)


def _row_copy(src_hbm, dst_vmem, sem, src_row, dst_row):
    return pltpu.make_async_copy(src_hbm.at[pl.ds(src_row, 1)], dst_vmem.at[pl.ds(dst_row, 1)], sem)


def _gather_kernel(idx_hbm, src_hbm, o_ref, idx_smem, sem_idx, sem_rows):
    i = pl.program_id(0)
    n = o_ref.shape[0]
    icp = pltpu.make_async_copy(idx_hbm.at[i], idx_smem, sem_idx)
    icp.start()
    icp.wait()

    def issue(r, c):
        _row_copy(src_hbm, o_ref, sem_rows, idx_smem[r], r).start()
        return c
    lax.fori_loop(0, n, issue, 0, unroll=8)

    def drain(r, c):
        _row_copy(src_hbm, o_ref, sem_rows, 0, r).wait()
        return c
    lax.fori_loop(0, n, drain, 0, unroll=8)


def _gather_rows(idx, src, rows_per_step):
    p = idx.shape[0]
    d = src.shape[1]
    n = rows_per_step
    return pl.pallas_call(
        _gather_kernel, grid=(p // n,),
        in_specs=[pl.BlockSpec(memory_space=pl.ANY), pl.BlockSpec(memory_space=pl.ANY)],
        out_specs=pl.BlockSpec((n, d), lambda i: (i, 0)),
        out_shape=jax.ShapeDtypeStruct((p, d), src.dtype),
        scratch_shapes=[pltpu.SMEM((n,), jnp.int32), pltpu.SemaphoreType.DMA, pltpu.SemaphoreType.DMA],
        compiler_params=_cparams("arbitrary"), name="moe_gather")(idx.reshape(p // n, n), src)


def _experts_kernel(blk_e_ref, x_ref, wg_ref, wu_ref, wd_ref, o_ref):
    @pl.when(pl.program_id(1) == 0)
    def _():
        o_ref[...] = jnp.zeros(o_ref.shape, F32)

    x = x_ref[...].astype(BF16)
    hid = (_silu(_dot(x, wg_ref[...])) * _dot(x, wu_ref[...])).astype(BF16)
    o_ref[...] += _dot(hid, wd_ref[...])


def _experts(rows, blk_e, w_gate_up, w_down):
    p, d = rows.shape
    f = w_down.shape[1]
    tm = MOE_ROWS
    tf = MOE_FF_TILE
    nj = f // tf
    grid_spec = pltpu.PrefetchScalarGridSpec(
        num_scalar_prefetch=1, grid=(p // tm, nj),
        in_specs=[pl.BlockSpec((tm, d), lambda i, j, e: (i, 0)),
                  pl.BlockSpec((None, d, tf), lambda i, j, e: (e[i], 0, j)),
                  pl.BlockSpec((None, d, tf), lambda i, j, e: (e[i], 0, nj + j)),
                  pl.BlockSpec((None, tf, d), lambda i, j, e: (e[i], j, 0))],
        out_specs=pl.BlockSpec((tm, d), lambda i, j, e: (i, 0)))
    return pl.pallas_call(
        _experts_kernel, grid_spec=grid_spec, out_shape=jax.ShapeDtypeStruct((p, d), F32),
        compiler_params=_cparams("parallel", "arbitrary"), name="moe_experts",
    )(blk_e, rows, w_gate_up, w_gate_up, w_down)


def _combine_kernel(dest_hbm, y_hbm, gate_ref, h_ref, g_ref, p_ref, wp_ref, wg_ref, fg_ref, o_ref,
                    idx_smem, ybuf, sem_idx, sem_rows):
    i = pl.program_id(0)
    n = h_ref.shape[0]
    icp = pltpu.make_async_copy(dest_hbm.at[i], idx_smem, sem_idx)
    icp.start()
    icp.wait()

    def issue(r, c):
        _row_copy(y_hbm, ybuf, sem_rows, idx_smem[r], r).start()
        return c
    lax.fori_loop(0, TOP_K * n, issue, 0, unroll=8)

    def drain(r, c):
        _row_copy(y_hbm, ybuf, sem_rows, 0, r).wait()
        return c
    lax.fori_loop(0, TOP_K * n, drain, 0, unroll=8)

    gates = gate_ref[...]
    moe = gates[:, 0:1] * ybuf[0:n, :] + gates[:, 1:2] * ybuf[n:2 * n, :]
    x = _ple_math(h_ref[...] + moe, g_ref, p_ref, wp_ref, wg_ref)
    o_ref[...] = _rms(x, fg_ref[...])


def _combine_ple_final(dest, y, gates, h, g, p, wp, wg, final_g):
    t, d = h.shape
    e = p.shape[1]
    n = min(COMBINE_ROWS, t)
    dest_steps = dest.reshape(t // n, n, TOP_K).transpose(0, 2, 1).reshape(t // n, TOP_K * n)
    return pl.pallas_call(
        _combine_kernel, grid=(t // n,),
        in_specs=[pl.BlockSpec(memory_space=pl.ANY), pl.BlockSpec(memory_space=pl.ANY),
                  pl.BlockSpec((n, LANES), lambda i: (i, 0)),
                  pl.BlockSpec((n, d), lambda i: (i, 0)), _resident((1, d)),
                  pl.BlockSpec((n, e), lambda i: (i, 0)), _resident((e, d)), _resident((d, d)),
                  _resident((1, d))],
        out_specs=pl.BlockSpec((n, d), lambda i: (i, 0)),
        out_shape=jax.ShapeDtypeStruct((t, d), F32),
        scratch_shapes=[pltpu.SMEM((TOP_K * n,), jnp.int32), pltpu.VMEM((TOP_K * n, d), y.dtype),
                        pltpu.SemaphoreType.DMA, pltpu.SemaphoreType.DMA],
        compiler_params=_cparams("arbitrary"), name="moe_combine_ple_final",
    )(dest_steps, y, gates, h, g.reshape(1, d), p, wp, wg, final_g.reshape(1, d))


def _dispatch_plan(top_idx, t):
    e_flat = top_idx.reshape(-1)
    onehot = (e_flat[:, None] == jnp.arange(N_EXPERTS, dtype=jnp.int32)[None, :]).astype(jnp.int32)
    csum = jnp.cumsum(onehot, axis=0)
    rank = jnp.take_along_axis(csum, e_flat[:, None], axis=1)[:, 0] - 1
    counts = csum[-1]
    padded = (counts + MOE_ROWS - 1) // MOE_ROWS * MOE_ROWS
    pend = jnp.cumsum(padded)
    pstart = pend - padded
    dest = pstart[e_flat] + rank
    p = t * TOP_K + N_EXPERTS * MOE_ROWS
    tok_flat = jnp.repeat(jnp.arange(t, dtype=jnp.int32), TOP_K)
    row_tok = jnp.zeros((p,), jnp.int32).at[dest].set(tok_flat)
    nb = p // MOE_ROWS
    blk_e = jnp.minimum(jnp.searchsorted(pend, jnp.arange(nb, dtype=jnp.int32) * MOE_ROWS, side='right'),
                        N_EXPERTS - 1).astype(jnp.int32)
    return dest.reshape(t, TOP_K), row_tok, blk_e


def kernel(x, p, norm_mix_g, norm_ffn_g, norm_ple_g, final_norm_g, ev_w_in, ev_conv_a, ev_gdn_conv, ev_gdn_A_log, ev_gdn_dt_bias, ev_gdn_norm_g, ev_w_out, od_w_in, od_lambda, od_diff_norm_g, od_conf_dw_w, od_conf_dw_b, od_conf_ln_g, od_conf_ln_b, od_w_out, rel_bias, ffn_w_gate_up, ffn_w_down, moe_router, moe_w_gate_up, moe_w_down, ple_w_proj, ple_w_gate):
    batch, seq, d = x.shape
    t = batch * seq
    depth = p.shape[0]
    assert depth == 2 and seq % GDN_CHUNK == 0
    h = x.reshape(t, d)
    pf = p.reshape(depth, t, p.shape[-1])

    heads = GDN_HEADS
    n_main = ev_w_in.shape[2] - 2 * heads
    w_in = ev_w_in[0]
    w_ab = jnp.zeros((d, LANES), BF16).at[:, :2 * heads].set(w_in[:, n_main:].astype(BF16))
    u, ab = _norm_proj(h, norm_mix_g[0], w_in[:, :n_main].astype(BF16), w_ab, name="l0_in_proj")
    y = _l0_mixer(u, ab, ev_conv_a[0], ev_gdn_conv[0], ev_gdn_A_log[0], ev_gdn_dt_bias[0],
                  ev_gdn_norm_g[0], batch, seq)
    h = _proj_residual(y, ev_w_out[0].astype(BF16), h, name="l0_out_proj")
    f = ffn_w_down.shape[1]
    h = _ffn(h, norm_ffn_g[0], ffn_w_gate_up[0, :, :f].astype(BF16), ffn_w_gate_up[0, :, f:].astype(BF16),
             ffn_w_down[0].astype(BF16), name="l0_ffn")
    h = _ple(h, norm_ple_g[0], pf[0], ple_w_proj[0].astype(BF16), ple_w_gate[0].astype(BF16), name="l0_ple")

    lambda_init = 0.8 - 0.6 * math.exp(-0.3 * 1)
    u = _norm_proj(h, norm_mix_g[1], od_w_in[0].astype(BF16), name="l1_in_proj")
    table = _bias_table(rel_bias, seq, min(ATTN_TILE, seq))
    o_attn = _diff_attention(u, table, od_lambda[0], od_diff_norm_g[0], batch, seq, lambda_init)
    c_conf = od_conf_dw_w.shape[2]
    o_conf = _conformer(u, od_conf_dw_w[0], od_conf_dw_b[0], od_conf_ln_g[0], od_conf_ln_b[0],
                        batch, seq, 3 * DIFF_HEADS * od_diff_norm_g.shape[1] // c_conf)
    h = _proj_residual(jnp.concatenate([o_attn, o_conf], axis=1), od_w_out[0].astype(BF16), h,
                       name="l1_out_proj")

    wr = jnp.zeros((d, LANES), F32).at[:, :N_EXPERTS].set(moe_router[0])
    hn, idx, gates = _router(h, norm_ffn_g[1], wr)
    dest, row_tok, blk_e = _dispatch_plan(idx[:, :TOP_K], t)
    rows = _gather_rows(row_tok, hn, GATHER_ROWS)
    y = _experts(rows, blk_e, moe_w_gate_up[0].astype(BF16), moe_w_down[0].astype(BF16))
    out = _combine_ple_final(dest, y, gates, h, norm_ple_g[1], pf[1], ple_w_proj[1].astype(BF16),
                             ple_w_gate[1].astype(BF16), final_norm_g)
    return out.reshape(batch, seq, d)
```

```python
import functools
import math

import jax
import jax.numpy as jnp
from jax import lax
from jax.experimental import pallas as pl
from jax.experimental.pallas import tpu as pltpu

F32 = jnp.float32
BF16 = jnp.bfloat16
HIGHEST = lax.Precision.HIGHEST

EPS = 1e-6
CONV_A_WIDTH = 3
GDN_HEADS = 4
GDN_CONV_WIDTH = 4
GDN_CHUNK = 64
DIFF_HEADS = 4
NUM_BUCKETS = 32
MAX_DISTANCE = 128
CONF_WIDTH = 31
N_EXPERTS = 8
TOP_K = 2

LANES = 128
SUBLANES = 8
VMEM_LIMIT_BYTES = 56 * 1024 * 1024
MASK_VALUE = -1e30
LOG2E = math.log2(math.e)

ROW_TILE = 512
COL_TILE = 512
ELEM_ROWS = 256
CONF_ROWS = 32
PREP_CHUNKS = 8
ATTN_TILE = 256
ATTN_HEADS_PER_STEP = 4
MOE_ROWS = 1024
MOE_FF_TILE = 1792
MOE_SUB_TILE = 256
COMBINE_ROWS = 256
TOP_PAD = 32


def _cparams(*sem):
    return pltpu.CompilerParams(dimension_semantics=sem, vmem_limit_bytes=VMEM_LIMIT_BYTES)


def _resident(shape):
    nd = len(shape)
    return pl.BlockSpec(shape, lambda *_: (0,) * nd, pipeline_mode=pl.Buffered(1))


def _rms(x, g):
    return x * lax.rsqrt(jnp.mean(x * x, axis=-1, keepdims=True) + EPS) * g


def _sigmoid(x):
    return jax.nn.sigmoid(x)


def _silu(x):
    return x * jax.nn.sigmoid(x)


def _softplus(x):
    return jnp.maximum(x, 0.0) + jnp.log1p(jnp.exp(-jnp.abs(x)))


def _dot(a, b, **kw):
    return jnp.dot(a, b, preferred_element_type=F32, **kw)


def _dot_nt(a, b):
    return lax.dot_general(a, b, (((1,), (1,)), ((), ())), preferred_element_type=F32)


def _dot_tn(a, b):
    return lax.dot_general(a, b, (((0,), (0,)), ((), ())), preferred_element_type=F32)


def _delayed(pad_ref, r, rows, max_delay):
    lead = -(-max_delay // SUBLANES) * SUBLANES
    win = pad_ref[pl.ds(TOP_PAD + r - lead, rows + lead), :]
    rolled = {0: win}

    def tap(d):
        a, b = divmod(d, SUBLANES)
        if b not in rolled:
            rolled[b] = pltpu.roll(win, b, 0)
        start = lead - SUBLANES * a
        return rolled[b][start:start + rows, :]
    return tap


def _load_token_tiles(ref, first_token, n):
    return jnp.concatenate([ref[pl.ds(first_token * SUBLANES + s, n, stride=SUBLANES), :]
                            for s in range(SUBLANES)], axis=1)


def _store_token_tiles(ref, first_token, x):
    n = x.shape[0]
    for s in range(SUBLANES):
        ref[pl.ds(first_token * SUBLANES + s, n, stride=SUBLANES), :] = x[:, s * LANES:(s + 1) * LANES]


def _token_tile_copy(src_hbm, src_token, dst_vmem, dst_token, sem):
    return pltpu.make_async_copy(src_hbm.at[pl.ds(pl.multiple_of(src_token * SUBLANES, SUBLANES), SUBLANES)],
                                 dst_vmem.at[pl.ds(pl.multiple_of(dst_token * SUBLANES, SUBLANES), SUBLANES)],
                                 sem)


def _token_tiles_wait(src_hbm, dst_vmem, first_token, n, sem):
    pltpu.make_async_copy(src_hbm.at[pl.ds(0, n * SUBLANES)],
                          dst_vmem.at[pl.ds(pl.multiple_of(first_token * SUBLANES, SUBLANES), n * SUBLANES)],
                          sem).wait()


def _norm_proj_kernel(h_ref, g_ref, w_ref, *rest, tn, with_aux):
    xn = _rms(h_ref[...], g_ref[...]).astype(BF16)
    if with_aux:
        w2_ref, o_ref, o2_ref = rest
        o2_ref[...] = _dot(xn, w2_ref[...])
    else:
        (o_ref,) = rest
    n = w_ref.shape[1]
    for c in range(0, n, tn):
        o_ref[:, c:c + tn] = _dot(xn, w_ref[:, c:c + tn]).astype(o_ref.dtype)


def _norm_proj(h, g, w, w_aux=None, *, name):
    t, d = h.shape
    n = w.shape[1]
    tm = min(ROW_TILE, t)
    tn = COL_TILE if n % COL_TILE == 0 else n
    in_specs = [pl.BlockSpec((tm, d), lambda i: (i, 0)), _resident((1, d)), _resident((d, n))]
    out_shape = [jax.ShapeDtypeStruct((t, n), BF16)]
    out_specs = [pl.BlockSpec((tm, n), lambda i: (i, 0))]
    args = [h, g.reshape(1, d), w]
    if w_aux is not None:
        in_specs.append(_resident(w_aux.shape))
        out_shape.append(jax.ShapeDtypeStruct((t, w_aux.shape[1]), F32))
        out_specs.append(pl.BlockSpec((tm, w_aux.shape[1]), lambda i: (i, 0)))
        args.append(w_aux)
    out = pl.pallas_call(
        functools.partial(_norm_proj_kernel, tn=tn, with_aux=w_aux is not None),
        grid=(t // tm,), in_specs=in_specs, out_specs=out_specs, out_shape=out_shape,
        compiler_params=_cparams("parallel"), name=name)(*args)
    return out if w_aux is not None else out[0]


def _proj_residual_kernel(*refs):
    n = (len(refs) - 2) // 2
    y_refs, w_refs, h_ref, o_ref = refs[:n], refs[n:2 * n], refs[2 * n], refs[2 * n + 1]
    acc = h_ref[...]
    for y_ref, w_ref in zip(y_refs, w_refs):
        acc = acc + _dot(y_ref[...], w_ref[...])
    o_ref[...] = acc


def _proj_residual(ys, w, h, *, name):
    t, d = h.shape
    tm = min(ROW_TILE, t)
    ws, r0 = [], 0
    for y in ys:
        ws.append(w[r0:r0 + y.shape[1]])
        r0 += y.shape[1]
    return pl.pallas_call(
        _proj_residual_kernel, grid=(t // tm,),
        in_specs=([pl.BlockSpec((tm, y.shape[1]), lambda i: (i, 0)) for y in ys]
                  + [_resident(wi.shape) for wi in ws] + [pl.BlockSpec((tm, d), lambda i: (i, 0))]),
        out_specs=pl.BlockSpec((tm, d), lambda i: (i, 0)),
        out_shape=jax.ShapeDtypeStruct((t, d), F32),
        compiler_params=_cparams("parallel"), name=name)(*ys, *ws, h)


def _ffn_kernel(h_ref, g_ref, wg_ref, wu_ref, wd_ref, o_ref, *, tf):
    x = h_ref[...]
    xn = _rms(x, g_ref[...]).astype(BF16)
    acc = x
    for c in range(0, wg_ref.shape[1], tf):
        gate = _dot(xn, wg_ref[:, c:c + tf])
        up = _dot(xn, wu_ref[:, c:c + tf])
        hid = (_silu(gate) * up).astype(BF16)
        acc = acc + _dot(hid, wd_ref[c:c + tf, :])
    o_ref[...] = acc


def _ff_tile(f, cap):
    best = LANES
    for c in range(LANES, cap + 1, LANES):
        if f % c == 0:
            best = c
    return best


def _ffn(h, g, wg, wu, wd, *, name):
    t, d = h.shape
    f = wg.shape[1]
    tm = min(ROW_TILE, t)
    return pl.pallas_call(
        functools.partial(_ffn_kernel, tf=_ff_tile(f, 1536)), grid=(t // tm,),
        in_specs=[pl.BlockSpec((tm, d), lambda i: (i, 0)), _resident((1, d)),
                  _resident((d, f)), _resident((d, f)), _resident((f, d))],
        out_specs=pl.BlockSpec((tm, d), lambda i: (i, 0)),
        out_shape=jax.ShapeDtypeStruct((t, d), F32),
        compiler_params=_cparams("parallel"), name=name)(h, g.reshape(1, d), wg, wu, wd)


def _ple_math(x, g_ref, p_ref, wp_ref, wg_ref):
    xn = _rms(x, g_ref[...]).astype(BF16)
    gate = _sigmoid(_dot(xn, wg_ref[...]))
    emb = _dot(p_ref[...].astype(BF16), wp_ref[...])
    return x + emb * gate


def _ple_kernel(h_ref, g_ref, p_ref, wp_ref, wg_ref, o_ref):
    o_ref[...] = _ple_math(h_ref[...], g_ref, p_ref, wp_ref, wg_ref)


def _ple(h, g, p, wp, wg, *, name):
    t, d = h.shape
    e = p.shape[1]
    tm = min(ROW_TILE, t)
    return pl.pallas_call(
        _ple_kernel, grid=(t // tm,),
        in_specs=[pl.BlockSpec((tm, d), lambda i: (i, 0)), _resident((1, d)),
                  pl.BlockSpec((tm, e), lambda i: (i, 0)), _resident((e, d)), _resident((d, d))],
        out_specs=pl.BlockSpec((tm, d), lambda i: (i, 0)),
        out_shape=jax.ShapeDtypeStruct((t, d), F32),
        compiler_params=_cparams("parallel"), name=name)(h, g.reshape(1, d), p, wp, wg)


def _unit_lower_inverses(mats):
    n = mats[0].shape[0]
    row = lax.broadcasted_iota(jnp.int32, (n, n), 0)
    col = lax.broadcasted_iota(jnp.int32, (n, n), 1)
    eye = (row == col).astype(F32)
    same16 = (row // 16) == (col // 16)
    same32 = (row // 32) == (col // 32)
    off32 = jnp.logical_and(same32, jnp.logical_not(same16))

    def mm(ps, qs):
        return [_dot(p.astype(BF16), q.astype(BF16)) for p, q in zip(ps, qs)]

    ad = [jnp.where(same16, a, 0.0) for a in mats]
    a2 = mm(ad, ad)
    x = mm([eye - t for t in ad], [eye + t for t in a2])
    a4 = mm(a2, a2)
    x = mm(x, [eye + t for t in a4])
    a8 = mm(a4, a4)
    x = mm(x, [eye + t for t in a8])
    y = mm([jnp.where(off32, a, 0.0) for a in mats], x)
    x = [t - c for t, c in zip(x, mm(x, y))]
    y = mm([jnp.where(same32, 0.0, a) for a in mats], x)
    return [t - c for t, c in zip(x, mm(x, y))]


def _l0_mixer_kernel(bg_ref, cg_ref, xin_ref, q_ref, k_ref, v_ref, og_ref, ab_ref,
                     wa_ref, wq_ref, wk_ref, wv_ref, alog_ref, dtb_ref, gn_ref,
                     o_ref,
                     pad_ref, qs_ref, ks_ref, vs_ref, us_ref, ws_ref, qk_ref, egl_ref, st_ref):
    s, ca = bg_ref.shape
    heads = GDN_HEADS
    dk = q_ref.shape[1] // heads
    c64 = GDN_CHUNK
    n_chunks = s // c64
    rt = min(ELEM_ROWS, s)
    n_tiles = s // rt

    pad_ref[0:TOP_PAD, :] = jnp.zeros((TOP_PAD, pad_ref.shape[1]), F32)

    def rows(i):
        return pl.ds(pl.multiple_of(i * rt, rt), rt)

    def fill_pad(fn):
        def body(i, c):
            r = pl.multiple_of(i * rt, rt)
            pad_ref[pl.ds(TOP_PAD + r, rt), :] = fn(rows(i))
            return c
        lax.fori_loop(0, n_tiles, body, 0)

    def conv_tile(i, w_ref, width):
        tap = _delayed(pad_ref, pl.multiple_of(i * rt, rt), rt, width - 1)
        acc = None
        for j in range(width):
            term = w_ref[j:j + 1, :] * tap(width - 1 - j)
            acc = term if acc is None else acc + term
        return acc

    fill_pad(lambda rs: cg_ref[rs, :].astype(F32) * xin_ref[rs, :].astype(F32))

    def mixer_a(i, c):
        acc = conv_tile(i, wa_ref, CONV_A_WIDTH)
        o_ref[rows(i), 0:ca] = (bg_ref[rows(i), :].astype(F32) * acc).astype(o_ref.dtype)
        return c
    lax.fori_loop(0, n_tiles, mixer_a, 0)

    def l2n(x, scale):
        parts = []
        for h in range(heads):
            xh = x[:, h * dk:(h + 1) * dk]
            inv = lax.rsqrt(jnp.sum(xh * xh, axis=-1, keepdims=True) + EPS)
            parts.append(xh * (inv * scale))
        return jnp.concatenate(parts, axis=1)

    for src_ref, w_ref, dst_ref, post in (
            (q_ref, wq_ref, qs_ref, lambda x: l2n(x, dk ** -0.5)),
            (k_ref, wk_ref, ks_ref, lambda x: l2n(x, 1.0)),
            (v_ref, wv_ref, vs_ref, lambda x: x)):
        fill_pad(lambda rs, src_ref=src_ref: src_ref[rs, :].astype(F32))

        def conv_body(i, c, w_ref=w_ref, dst_ref=dst_ref, post=post):
            dst_ref[rows(i), :] = post(_silu(conv_tile(i, w_ref, GDN_CONV_WIDTH)))
            return c
        lax.fori_loop(0, n_tiles, conv_body, 0)

    ri = lax.broadcasted_iota(jnp.int32, (c64, c64), 0)
    ci = lax.broadcasted_iota(jnp.int32, (c64, c64), 1)
    tril = ri >= ci
    strict = ri > ci
    ltri = tril.astype(F32)

    group = PREP_CHUNKS if n_chunks % PREP_CHUNKS == 0 else 1

    def chunk_prep(cg, carry):
        chunks = []
        for cc in range(group):
            c = cg * group + cc
            rs = pl.ds(pl.multiple_of(c * c64, c64), c64)
            chunks.append((c, rs, ab_ref[rs, :], qs_ref[rs, :], ks_ref[rs, :], vs_ref[rs, :]))
        inst = []
        gcs = [_dot(ltri, -jnp.exp(alog_ref[...]) * _softplus(ab + dtb_ref[...]), precision=HIGHEST)
               for _, _, ab, _, _, _ in chunks]
        for (c, rs, ab, q_all, k_all, v_all), gc in zip(chunks, gcs):
            beta = _sigmoid(ab)
            gct = gc.T
            for h in range(heads):
                hs = slice(h * dk, (h + 1) * dk)
                gcol = gc[:, h:h + 1]
                glast = gc[c64 - 1:c64, h:h + 1]
                bcol = beta[:, heads + h:heads + h + 1]
                decay = jnp.where(tril, jnp.exp(jnp.where(tril, gcol - gct[h:h + 1, :], 0.0)), 0.0)
                kh, qh, vh = k_all[:, hs], q_all[:, hs], v_all[:, hs]
                kb = kh * bcol
                egc = jnp.exp(gcol)
                inst.append(dict(decay=decay, kh=kh, qh=qh, kb=kb, egc=egc,
                                 rhs=jnp.concatenate([vh * bcol, kb * egc], axis=1).astype(BF16),
                                 kd=kh * jnp.exp(glast - gcol),
                                 eg=jnp.broadcast_to(jnp.exp(glast), (SUBLANES, dk))))
        kqs = [_dot_nt(jnp.concatenate([t["kb"], t["qh"]], axis=0).astype(BF16), t["kh"].astype(BF16))
               for t in inst]
        minvs = _unit_lower_inverses([jnp.where(strict, kq[0:c64] * t["decay"], 0.0)
                                      for kq, t in zip(kqs, inst)])
        uws = [_dot(m.astype(BF16), t["rhs"]) for m, t in zip(minvs, inst)]
        for ci, (c, rs, _, _, _, _) in enumerate(chunks):
            sl = slice(ci * heads, (ci + 1) * heads)
            us_ref[rs, :] = jnp.concatenate([uw[:, 0:dk] for uw in uws[sl]], axis=1)
            ws_ref[rs, :] = jnp.concatenate([uw[:, dk:2 * dk] for uw in uws[sl]], axis=1)
            qs_ref[rs, :] = jnp.concatenate([t["qh"] * t["egc"] for t in inst[sl]], axis=1)
            ks_ref[rs, :] = jnp.concatenate([t["kd"] for t in inst[sl]], axis=1)
            qk_ref[rs, :] = jnp.concatenate([kq[c64:2 * c64] * t["decay"]
                                             for kq, t in zip(kqs[sl], inst[sl])], axis=1)
            e0 = pl.multiple_of(c * (heads * SUBLANES), heads * SUBLANES)
            egl_ref[pl.ds(e0, heads * SUBLANES), :] = jnp.concatenate([t["eg"] for t in inst[sl]], axis=0)
        return carry
    lax.fori_loop(0, n_chunks // group, chunk_prep, 0)

    st_ref[...] = jnp.zeros(st_ref.shape, F32)

    def scan(c, carry):
        rs = pl.ds(pl.multiple_of(c * c64, c64), c64)
        e0 = pl.multiple_of(c * (heads * SUBLANES), heads * SUBLANES)
        w_all, qg_all, u_all, kd_all = ws_ref[rs, :], qs_ref[rs, :], us_ref[rs, :], ks_ref[rs, :]
        qk_all = qk_ref[rs, :]
        eg_all = egl_ref[pl.ds(e0, heads * SUBLANES), :]
        hsl = [slice(h * dk, (h + 1) * dk) for h in range(heads)]
        sts = [st_ref[h] for h in range(heads)]
        wqs = [_dot(jnp.concatenate([w_all[:, hs], qg_all[:, hs]], axis=0).astype(BF16), st.astype(BF16))
               for hs, st in zip(hsl, sts)]
        vbs = [(u_all[:, hs] - wq[0:c64]).astype(BF16) for hs, wq in zip(hsl, wqs)]
        upd = [_dot_tn(kd_all[:, hs].astype(BF16), vb) for hs, vb in zip(hsl, vbs)]
        intra = [_dot(qk_all[:, h * c64:(h + 1) * c64].astype(BF16), vbs[h]) for h in range(heads)]
        for h in range(heads):
            st_ref[h] = sts[h] * eg_all[h * SUBLANES:h * SUBLANES + 1, :] + upd[h]
        vs_ref[rs, :] = jnp.concatenate([wq[c64:2 * c64] + o for wq, o in zip(wqs, intra)], axis=1)
        return carry
    lax.fori_loop(0, n_chunks, scan, 0)

    def finish(i, c):
        o = vs_ref[rows(i), :]
        og = og_ref[rows(i), :].astype(F32)
        parts = []
        for h in range(heads):
            oh = o[:, h * dk:(h + 1) * dk]
            parts.append(_rms(oh, gn_ref[...]))
        y = jnp.concatenate(parts, axis=1) * _silu(og)
        o_ref[rows(i), ca:ca + heads * dk] = y.astype(o_ref.dtype)
        return c
    lax.fori_loop(0, n_tiles, finish, 0)


def _l0_mixer(u, ab, conv_a, gdn_conv, a_log, dt_bias, gdn_norm_g, batch, seq):
    cw = conv_a.shape[1]
    heads = GDN_HEADS
    dk = cw // heads

    def col(j):
        return pl.BlockSpec((seq, cw), lambda b, j=j: (b, j), pipeline_mode=pl.Buffered(1))

    def lane_row(x):
        return jnp.zeros((1, LANES), F32).at[0, :x.shape[0]].set(x)

    in_specs = [col(j) for j in range(7)]
    in_specs += [pl.BlockSpec((seq, LANES), lambda b: (b, 0)),
                 _resident(conv_a.shape)]
    in_specs += [pl.BlockSpec((GDN_CONV_WIDTH, cw), lambda b, j=j: (0, j), pipeline_mode=pl.Buffered(1))
                 for j in range(3)]
    in_specs += [_resident((1, LANES)), _resident((1, LANES)), _resident((1, dk))]
    big = pltpu.VMEM((seq, cw), F32)
    scratch = [pltpu.VMEM((seq + TOP_PAD, cw), F32), big, big, big, big, big,
               pltpu.VMEM((seq, heads * GDN_CHUNK), F32),
               pltpu.VMEM((seq // GDN_CHUNK * heads * SUBLANES, dk), F32),
               pltpu.VMEM((heads, dk, dk), F32)]
    return pl.pallas_call(
        _l0_mixer_kernel, grid=(batch,), in_specs=in_specs,
        out_specs=pl.BlockSpec((seq, 2 * cw), lambda b: (b, 0), pipeline_mode=pl.Buffered(1)),
        out_shape=jax.ShapeDtypeStruct((batch * seq, 2 * cw), BF16),
        scratch_shapes=scratch, compiler_params=_cparams("parallel"), name="l0_mixers",
    )(u, u, u, u, u, u, u, ab, conv_a, gdn_conv, gdn_conv, gdn_conv,
      lane_row(a_log), lane_row(dt_bias), gdn_norm_g.reshape(1, dk))


def _bias_table_kernel(tab_ref, o_ref, *, t):
    h = pl.program_id(0)
    m = pl.program_id(1)
    ri = lax.broadcasted_iota(jnp.int32, (t, t), 0)
    ci = lax.broadcasted_iota(jnp.int32, (t, t), 1)
    rel = m * t + ci - ri
    n = jnp.maximum(rel, 0)
    max_exact = NUM_BUCKETS // 2
    nf = jnp.maximum(n, 1).astype(F32)
    large = max_exact + (jnp.log(nf / max_exact) / math.log(MAX_DISTANCE / max_exact)
                         * (NUM_BUCKETS - max_exact)).astype(jnp.int32)
    large = jnp.minimum(large, NUM_BUCKETS - 1)
    bucket = jnp.where(n < max_exact, n, large)
    bias = jnp.zeros((t, t), F32)
    for b in range(NUM_BUCKETS):
        bias = jnp.where(bucket == b, tab_ref[b, h], bias)
    o_ref[...] = jnp.where(rel >= 0, bias * LOG2E, MASK_VALUE)


def _bias_table(rel_bias, seq, t):
    nb = seq // t
    heads = rel_bias.shape[1]
    return pl.pallas_call(
        functools.partial(_bias_table_kernel, t=t), grid=(heads, nb),
        in_specs=[pl.BlockSpec(memory_space=pltpu.SMEM)],
        out_specs=pl.BlockSpec((None, None, t, t), lambda h, m: (h, m, 0, 0)),
        out_shape=jax.ShapeDtypeStruct((heads, nb, t, t), F32),
        compiler_params=_cparams("parallel", "parallel"), name="rel_bias_table")(rel_bias)


def _attn_kernel(q_ref, k_ref, v_ref, tb_ref, lam_ref, gn_ref, o_ref, *, t, heads, lambda_init):
    qi = pl.program_id(2)
    dh2 = q_ref.shape[1] // heads
    dh = dh2 // 2
    lane = lax.broadcasted_iota(jnp.int32, (t, dh2), 1)
    qqs = []
    for h in range(heads):
        qf = q_ref[:, h * dh2:(h + 1) * dh2].astype(F32) * (dh ** -0.5 * LOG2E)
        qqs.append(jnp.concatenate([jnp.where(lane < dh, qf, 0.0), jnp.where(lane >= dh, qf, 0.0)],
                                   axis=0).astype(BF16))

    def body(j, carry):
        ks = pl.ds(pl.multiple_of(j * t, t), t)
        k_all = k_ref[ks, :]
        v_all = v_ref[ks, :]
        hs = [slice(h * dh2, (h + 1) * dh2) for h in range(heads)]
        s_t = [_dot_nt(k_all[:, hs[h]], qqs[h]) for h in range(heads)]
        ps, stats = [], []
        for h in range(heads):
            m, l, _ = carry[h]
            b = tb_ref[h, qi - j]
            s = jnp.concatenate([s_t[h][:, 0:t] + b, s_t[h][:, t:2 * t] + b], axis=1)
            m_new = jnp.maximum(m, jnp.max(s, axis=0, keepdims=True))
            alpha = jnp.exp2(m - m_new)
            p = jnp.exp2(s - m_new)
            stats.append((m_new, alpha * l + jnp.sum(p, axis=0, keepdims=True), alpha))
            ps.append(p.astype(BF16))
        pv = [_dot_tn(v_all[:, hs[h]], ps[h]) for h in range(heads)]
        return tuple((stats[h][0], stats[h][1], stats[h][2] * carry[h][2] + pv[h]) for h in range(heads))

    init = tuple((jnp.full((1, 2 * t), MASK_VALUE, F32), jnp.zeros((1, 2 * t), F32),
                  jnp.zeros((dh2, 2 * t), F32)) for _ in range(heads))
    final = lax.fori_loop(0, qi + 1, body, init)
    lp = lam_ref[...]
    lam = (jnp.exp(jnp.sum(lp[0:1] * lp[1:2], axis=-1, keepdims=True))
           - jnp.exp(jnp.sum(lp[2:3] * lp[3:4], axis=-1, keepdims=True)) + lambda_init)
    for h in range(heads):
        _, l, acc = final[h]
        o12 = acc / l
        o = (o12[:, 0:t] - lam * o12[:, t:2 * t]).T
        o_ref[:, h * dh2:(h + 1) * dh2] = (_rms(o, gn_ref[...]) * (1.0 - lambda_init)).astype(o_ref.dtype)


def _diff_attention(u, table, lam_params, norm_g, batch, seq, lambda_init):
    heads = DIFF_HEADS
    dh2 = norm_g.shape[0]
    hg = ATTN_HEADS_PER_STEP
    w = hg * dh2
    ng = heads // hg
    t = table.shape[2]
    nq = seq // t
    return pl.pallas_call(
        functools.partial(_attn_kernel, t=t, heads=hg, lambda_init=lambda_init),
        grid=(batch, ng, nq),
        in_specs=[pl.BlockSpec((t, w), lambda b, g, i: (b * nq + i, g)),
                  pl.BlockSpec((seq, w), lambda b, g, i: (b, ng + g)),
                  pl.BlockSpec((seq, w), lambda b, g, i: (b, 2 * ng + g)),
                  pl.BlockSpec((hg, nq, t, t), lambda b, g, i: (g, 0, 0, 0)),
                  _resident(lam_params.shape), _resident((1, dh2))],
        out_specs=pl.BlockSpec((t, w), lambda b, g, i: (b * nq + i, g)),
        out_shape=jax.ShapeDtypeStruct((batch * seq, heads * dh2), BF16),
        compiler_params=_cparams("parallel", "parallel", "arbitrary"), name="diff_attention",
    )(u, u, u, table, lam_params, norm_g.reshape(1, dh2))


def _conformer_kernel(ga_ref, gb_ref, w_ref, b_ref, lg_ref, lb_ref, o_ref, pad_ref):
    s, c = ga_ref.shape
    rt = min(ELEM_ROWS, s)
    pad_ref[0:TOP_PAD, :] = jnp.zeros((TOP_PAD, c), F32)

    def glu(i, carry):
        r = pl.multiple_of(i * rt, rt)
        rs = pl.ds(r, rt)
        pad_ref[pl.ds(TOP_PAD + r, rt), :] = ga_ref[rs, :].astype(F32) * _sigmoid(gb_ref[rs, :].astype(F32))
        return carry
    lax.fori_loop(0, s // rt, glu, 0)

    ct = CONF_ROWS

    def conv(i, carry):
        r = pl.multiple_of(i * ct, ct)
        tap = _delayed(pad_ref, r, ct, CONF_WIDTH - 1)
        acc = jnp.broadcast_to(b_ref[...], (ct, c))
        for j in range(CONF_WIDTH):
            acc = acc + w_ref[j:j + 1, :] * tap(CONF_WIDTH - 1 - j)
        mu = jnp.mean(acc, axis=-1, keepdims=True)
        xc = acc - mu
        var = jnp.mean(xc * xc, axis=-1, keepdims=True)
        y = xc * lax.rsqrt(var + EPS) * lg_ref[...] + lb_ref[...]
        o_ref[pl.ds(r, ct), :] = _silu(y).astype(o_ref.dtype)
        return carry
    lax.fori_loop(0, s // ct, conv, 0)


def _conformer(u, w, b, ln_g, ln_b, batch, seq, col0):
    c = w.shape[1]
    return pl.pallas_call(
        _conformer_kernel, grid=(batch,),
        in_specs=[pl.BlockSpec((seq, c), lambda i: (i, col0)),
                  pl.BlockSpec((seq, c), lambda i: (i, col0 + 1)),
                  _resident(w.shape), _resident((1, c)), _resident((1, c)), _resident((1, c))],
        out_specs=pl.BlockSpec((seq, c), lambda i: (i, 0)),
        out_shape=jax.ShapeDtypeStruct((batch * seq, c), BF16),
        scratch_shapes=[pltpu.VMEM((seq + TOP_PAD, c), F32)],
        compiler_params=_cparams("parallel"), name="conformer_conv",
    )(u, u, w, b.reshape(1, c), ln_g.reshape(1, c), ln_b.reshape(1, c))


def _router_kernel(h_ref, g_ref, wr_ref, hn_ref, idx_ref, gate_ref):
    xn = _rms(h_ref[...], g_ref[...])
    _store_token_tiles(hn_ref, 0, xn)
    logits = _dot(xn, wr_ref[...], precision=HIGHEST)
    lane = lax.broadcasted_iota(jnp.int32, logits.shape, 1)
    neg = -jnp.inf
    lg = jnp.where(lane < N_EXPERTS, logits, neg)
    m1 = jnp.max(lg, axis=-1, keepdims=True)
    i1 = jnp.min(jnp.where(lg == m1, lane, LANES), axis=-1, keepdims=True)
    lg2 = jnp.where(lane == i1, neg, lg)
    m2 = jnp.max(lg2, axis=-1, keepdims=True)
    i2 = jnp.min(jnp.where(lg2 == m2, lane, LANES), axis=-1, keepdims=True)
    e = jnp.exp(m2 - m1)
    g1 = 1.0 / (1.0 + e)
    idx_ref[...] = jnp.where(lane == 0, i1, jnp.where(lane == 1, i2, 0))
    gate_ref[...] = jnp.where(lane == 0, g1, jnp.where(lane == 1, e * g1, 0.0))


def _router(h, g, wr):
    t, d = h.shape
    tm = min(ROW_TILE, t)
    return pl.pallas_call(
        _router_kernel, grid=(t // tm,),
        in_specs=[pl.BlockSpec((tm, d), lambda i: (i, 0)), _resident((1, d)), _resident((d, LANES))],
        out_specs=[pl.BlockSpec((tm * d // LANES, LANES), lambda i: (i, 0)),
                   pl.BlockSpec((tm, LANES), lambda i: (i, 0)),
                   pl.BlockSpec((tm, LANES), lambda i: (i, 0))],
        out_shape=[jax.ShapeDtypeStruct((t * d // LANES, LANES), F32), jax.ShapeDtypeStruct((t, LANES), jnp.int32),
                   jax.ShapeDtypeStruct((t, LANES), F32)],
        compiler_params=_cparams("parallel"), name="moe_router")(h, g.reshape(1, d), wr)


def _experts_kernel(blk_e_ref, tok_hbm, x_hbm, wg_ref, wu_ref, wd_ref, o_ref,
                    xbuf, acc_ref, tok0, tok1, sem_tok, sem_rows):
    i = pl.program_id(0)
    j = pl.program_id(1)
    nb = pl.num_programs(0)
    nj = pl.num_programs(1)
    tm = acc_ref.shape[0]
    per_step = tm // nj
    slot = i % 2
    toks = (tok0, tok1)

    def tok_copy(blk, s):
        return pltpu.make_async_copy(tok_hbm.at[blk], toks[s], sem_tok.at[s])

    def issue_tiles(s, first, count):
        def body(k, c):
            r = first + k
            _token_tile_copy(x_hbm, toks[s][r], xbuf, s * tm + r, sem_rows.at[s]).start()
            return c
        lax.fori_loop(0, count, body, 0, unroll=16)

    @pl.when(jnp.logical_and(i == 0, j == 0))
    def _():
        tok_copy(0, 0).start()
        tok_copy(0, 0).wait()
        tok_copy(1, 1).start()
        issue_tiles(0, 0, tm)

    for s in range(2):
        @pl.when(slot == s)
        def _(s=s):
            @pl.when(j == 0)
            def _():
                tok_copy(i + 1, 1 - s).wait()
                _token_tiles_wait(x_hbm, xbuf, s * tm, tm, sem_rows.at[s])
                acc_ref[...] = jnp.zeros(acc_ref.shape, F32)

            @pl.when(jnp.logical_and(j == 1, i + 2 <= nb))
            def _():
                tok_copy(i + 2, s).start()

            issue_tiles(1 - s, j * per_step, per_step)

    x = _load_token_tiles(xbuf, slot * tm, tm).astype(BF16)
    acc = None
    for c in range(0, wd_ref.shape[0], MOE_SUB_TILE):
        cs = slice(c, c + MOE_SUB_TILE)
        hid = (_silu(_dot(x, wg_ref[:, cs])) * _dot(x, wu_ref[:, cs])).astype(BF16)
        part = _dot(hid, wd_ref[cs, :])
        acc = part if acc is None else acc + part
    acc_ref[...] += acc

    @pl.when(j == nj - 1)
    def _():
        _store_token_tiles(o_ref, 0, acc_ref[...])

    @pl.when(jnp.logical_and(i == nb - 1, j == nj - 1))
    def _():
        _token_tiles_wait(x_hbm, xbuf, (1 - slot) * tm, tm, sem_rows.at[1 - slot])


def _experts(x_tiles, row_tok, blk_e, w_gate_up, w_down):
    f, d = w_down.shape[1], w_down.shape[2]
    tm = MOE_ROWS
    tf = MOE_FF_TILE
    nj = f // tf
    nb = row_tok.shape[0] // tm - 1
    tpt = d // LANES
    assert tm % nj == 0 and tpt == SUBLANES
    grid_spec = pltpu.PrefetchScalarGridSpec(
        num_scalar_prefetch=1, grid=(nb, nj),
        in_specs=[pl.BlockSpec(memory_space=pl.ANY), pl.BlockSpec(memory_space=pl.ANY),
                  pl.BlockSpec((None, d, tf), lambda i, j, e: (e[i], 0, j)),
                  pl.BlockSpec((None, d, tf), lambda i, j, e: (e[i], 0, nj + j)),
                  pl.BlockSpec((None, tf, d), lambda i, j, e: (e[i], j, 0))],
        out_specs=pl.BlockSpec((tm * tpt, LANES), lambda i, j, e: (i, 0)),
        scratch_shapes=[pltpu.VMEM((2 * tm * tpt, LANES), F32), pltpu.VMEM((tm, d), F32),
                        pltpu.SMEM((tm,), jnp.int32), pltpu.SMEM((tm,), jnp.int32),
                        pltpu.SemaphoreType.DMA((2,)), pltpu.SemaphoreType.DMA((2,))])
    return pl.pallas_call(
        _experts_kernel, grid_spec=grid_spec,
        out_shape=jax.ShapeDtypeStruct((nb * tm * tpt, LANES), F32),
        compiler_params=_cparams("arbitrary", "arbitrary"), name="moe_experts",
    )(blk_e, row_tok.reshape(nb + 1, tm), x_tiles, w_gate_up, w_gate_up, w_down)


def _combine_kernel(dest_hbm, y_hbm, gate_ref, h_ref, g_ref, p_ref, wp_ref, wg_ref, fg_ref, o_ref,
                    idx0, idx1, ybuf, sem_idx, sem_rows):
    i = pl.program_id(0)
    steps = pl.num_programs(0)
    n = h_ref.shape[0]
    m = TOP_K * n
    slot = i % 2
    idxs = (idx0, idx1)

    def idx_copy(blk, s):
        return pltpu.make_async_copy(dest_hbm.at[blk], idxs[s], sem_idx.at[s])

    def issue_tiles(s):
        def body(r, c):
            _token_tile_copy(y_hbm, idxs[s][r], ybuf, s * m + r, sem_rows.at[s]).start()
            return c
        lax.fori_loop(0, m, body, 0, unroll=16)

    @pl.when(i == 0)
    def _():
        idx_copy(0, 0).start()
        idx_copy(0, 0).wait()
        idx_copy(1, 1).start()
        issue_tiles(0)

    for s in range(2):
        @pl.when(slot == s)
        def _(s=s):
            idx_copy(i + 1, 1 - s).wait()

            @pl.when(i + 2 <= steps)
            def _():
                idx_copy(i + 2, s).start()

            issue_tiles(1 - s)
            _token_tiles_wait(y_hbm, ybuf, s * m, m, sem_rows.at[s])

    gates = gate_ref[...]
    moe = (gates[:, 0:1] * _load_token_tiles(ybuf, slot * m, n)
           + gates[:, 1:2] * _load_token_tiles(ybuf, slot * m + n, n))
    x = _ple_math(h_ref[...] + moe, g_ref, p_ref, wp_ref, wg_ref)
    o_ref[...] = _rms(x, fg_ref[...])

    @pl.when(i == steps - 1)
    def _():
        _token_tiles_wait(y_hbm, ybuf, (1 - slot) * m, m, sem_rows.at[1 - slot])


def _combine_ple_final(dest, y_tiles, gates, h, g, p, wp, wg, final_g):
    t, d = h.shape
    e = p.shape[1]
    n = min(COMBINE_ROWS, t)
    m = TOP_K * n
    steps = t // n
    dest_steps = dest.reshape(steps, n, TOP_K).transpose(0, 2, 1).reshape(steps, m)
    dest_steps = jnp.concatenate([dest_steps, jnp.zeros((1, m), jnp.int32)], axis=0)
    return pl.pallas_call(
        _combine_kernel, grid=(steps,),
        in_specs=[pl.BlockSpec(memory_space=pl.ANY), pl.BlockSpec(memory_space=pl.ANY),
                  pl.BlockSpec((n, LANES), lambda i: (i, 0)),
                  pl.BlockSpec((n, d), lambda i: (i, 0)), _resident((1, d)),
                  pl.BlockSpec((n, e), lambda i: (i, 0)), _resident((e, d)), _resident((d, d)),
                  _resident((1, d))],
        out_specs=pl.BlockSpec((n, d), lambda i: (i, 0)),
        out_shape=jax.ShapeDtypeStruct((t, d), F32),
        scratch_shapes=[pltpu.SMEM((m,), jnp.int32), pltpu.SMEM((m,), jnp.int32),
                        pltpu.VMEM((2 * m * d // LANES, LANES), F32),
                        pltpu.SemaphoreType.DMA((2,)), pltpu.SemaphoreType.DMA((2,))],
        compiler_params=_cparams("arbitrary"), name="moe_combine_ple_final",
    )(dest_steps, y_tiles, gates, h, g.reshape(1, d), p, wp, wg, final_g.reshape(1, d))


def _dispatch_plan(top_idx, t):
    e_flat = top_idx.reshape(-1)
    onehot = (e_flat[:, None] == jnp.arange(N_EXPERTS, dtype=jnp.int32)[None, :]).astype(jnp.int32)
    csum = jnp.cumsum(onehot, axis=0)
    rank = jnp.take_along_axis(csum, e_flat[:, None], axis=1)[:, 0] - 1
    counts = csum[-1]
    padded = (counts + MOE_ROWS - 1) // MOE_ROWS * MOE_ROWS
    pend = jnp.cumsum(padded)
    pstart = pend - padded
    dest = pstart[e_flat] + rank
    p = t * TOP_K + N_EXPERTS * MOE_ROWS
    tok_flat = jnp.repeat(jnp.arange(t, dtype=jnp.int32), TOP_K)
    row_tok = jnp.zeros((p + MOE_ROWS,), jnp.int32).at[dest].set(
        tok_flat, unique_indices=True, mode="promise_in_bounds")
    nb = p // MOE_ROWS
    blk_e = jnp.minimum(jnp.searchsorted(pend, jnp.arange(nb, dtype=jnp.int32) * MOE_ROWS, side='right'),
                        N_EXPERTS - 1).astype(jnp.int32)
    return dest.reshape(t, TOP_K), row_tok, blk_e


def kernel(x, p, norm_mix_g, norm_ffn_g, norm_ple_g, final_norm_g, ev_w_in, ev_conv_a, ev_gdn_conv, ev_gdn_A_log, ev_gdn_dt_bias, ev_gdn_norm_g, ev_w_out, od_w_in, od_lambda, od_diff_norm_g, od_conf_dw_w, od_conf_dw_b, od_conf_ln_g, od_conf_ln_b, od_w_out, rel_bias, ffn_w_gate_up, ffn_w_down, moe_router, moe_w_gate_up, moe_w_down, ple_w_proj, ple_w_gate):
    batch, seq, d = x.shape
    t = batch * seq
    depth = p.shape[0]
    assert depth == 2 and seq % GDN_CHUNK == 0
    h = x.reshape(t, d)
    pf = p.reshape(depth, t, p.shape[-1])

    heads = GDN_HEADS
    n_main = ev_w_in.shape[2] - 2 * heads
    w_in = ev_w_in[0]
    w_ab = jnp.zeros((d, LANES), BF16).at[:, :2 * heads].set(w_in[:, n_main:].astype(BF16))
    u, ab = _norm_proj(h, norm_mix_g[0], w_in[:, :n_main].astype(BF16), w_ab, name="l0_in_proj")
    y = _l0_mixer(u, ab, ev_conv_a[0], ev_gdn_conv[0], ev_gdn_A_log[0], ev_gdn_dt_bias[0],
                  ev_gdn_norm_g[0], batch, seq)
    h = _proj_residual([y], ev_w_out[0].astype(BF16), h, name="l0_out_proj")
    f = ffn_w_down.shape[1]
    h = _ffn(h, norm_ffn_g[0], ffn_w_gate_up[0, :, :f].astype(BF16), ffn_w_gate_up[0, :, f:].astype(BF16),
             ffn_w_down[0].astype(BF16), name="l0_ffn")
    h = _ple(h, norm_ple_g[0], pf[0], ple_w_proj[0].astype(BF16), ple_w_gate[0].astype(BF16), name="l0_ple")

    lambda_init = 0.8 - 0.6 * math.exp(-0.3 * 1)
    u = _norm_proj(h, norm_mix_g[1], od_w_in[0].astype(BF16), name="l1_in_proj")
    table = _bias_table(rel_bias, seq, min(ATTN_TILE, seq))
    o_attn = _diff_attention(u, table, od_lambda[0], od_diff_norm_g[0], batch, seq, lambda_init)
    c_conf = od_conf_dw_w.shape[2]
    o_conf = _conformer(u, od_conf_dw_w[0], od_conf_dw_b[0], od_conf_ln_g[0], od_conf_ln_b[0],
                        batch, seq, 3 * DIFF_HEADS * od_diff_norm_g.shape[1] // c_conf)
    h = _proj_residual([o_attn, o_conf], od_w_out[0].astype(BF16), h, name="l1_out_proj")

    wr = jnp.zeros((d, LANES), F32).at[:, :N_EXPERTS].set(moe_router[0])
    hn, idx, gates = _router(h, norm_ffn_g[1], wr)
    dest, row_tok, blk_e = _dispatch_plan(idx[:, :TOP_K], t)
    y = _experts(hn, row_tok, blk_e, moe_w_gate_up[0].astype(BF16), moe_w_down[0].astype(BF16))
    out = _combine_ple_final(dest, y, gates, h, norm_ple_g[1], pf[1], ple_w_proj[1].astype(BF16),
                             ple_w_gate[1].astype(BF16), final_norm_g)
    return out.reshape(batch, seq, d)
```

```python
import functools
import math

import jax
import jax.numpy as jnp
from jax import lax
from jax.experimental import pallas as pl
from jax.experimental.pallas import tpu as pltpu

F32 = jnp.float32
BF16 = jnp.bfloat16
HIGHEST = lax.Precision.HIGHEST

EPS = 1e-6
CONV_A_WIDTH = 3
GDN_HEADS = 4
GDN_CONV_WIDTH = 4
GDN_CHUNK = 64
DIFF_HEADS = 4
NUM_BUCKETS = 32
MAX_DISTANCE = 128
CONF_WIDTH = 31
N_EXPERTS = 8
TOP_K = 2

LANES = 128
SUBLANES = 8
VMEM_LIMIT_BYTES = 56 * 1024 * 1024
MASK_VALUE = -1e30
LOG2E = math.log2(math.e)

ROW_TILE = 512
COL_TILE = 512
ELEM_ROWS = 256
CONF_ROWS = 256
PREP_CHUNKS = 8
ATTN_TILE = 256
ATTN_HEADS_PER_STEP = 4
MOE_ROWS = 1024
MOE_SUB_TILE = 256
COMBINE_ROWS = 256
TOP_PAD = 32


def _cparams(*sem):
    return pltpu.CompilerParams(dimension_semantics=sem, vmem_limit_bytes=VMEM_LIMIT_BYTES)


def _resident(shape):
    nd = len(shape)
    return pl.BlockSpec(shape, lambda *_: (0,) * nd, pipeline_mode=pl.Buffered(1))


def _rms(x, g):
    return x * lax.rsqrt(jnp.mean(x * x, axis=-1, keepdims=True) + EPS) * g


def _sigmoid(x):
    return jax.nn.sigmoid(x)


def _silu(x):
    return x * jax.nn.sigmoid(x)


def _softplus(x):
    return jnp.maximum(x, 0.0) + jnp.log1p(jnp.exp(-jnp.abs(x)))


def _dot(a, b, **kw):
    return jnp.dot(a, b, preferred_element_type=F32, **kw)


def _dot_nt(a, b):
    return lax.dot_general(a, b, (((1,), (1,)), ((), ())), preferred_element_type=F32)


def _dot_tn(a, b):
    return lax.dot_general(a, b, (((0,), (0,)), ((), ())), preferred_element_type=F32)


def _delayed(pad_ref, r, rows, max_delay):
    lead = -(-max_delay // SUBLANES) * SUBLANES
    win = pad_ref[pl.ds(TOP_PAD + r - lead, rows + lead), :]
    rolled = {0: win}

    def tap(d):
        a, b = divmod(d, SUBLANES)
        if b not in rolled:
            rolled[b] = pltpu.roll(win, b, 0)
        start = lead - SUBLANES * a
        return rolled[b][start:start + rows, :]
    return tap


def _load_token_tiles(ref, first_token, n):
    return jnp.concatenate([ref[pl.ds(first_token * SUBLANES + s, n, stride=SUBLANES), :]
                            for s in range(SUBLANES)], axis=1)


def _store_token_tiles(ref, first_token, x):
    n = x.shape[0]
    for s in range(SUBLANES):
        ref[pl.ds(first_token * SUBLANES + s, n, stride=SUBLANES), :] = x[:, s * LANES:(s + 1) * LANES]


def _token_tile_copy(src_hbm, src_token, dst_vmem, dst_token, sem):
    return pltpu.make_async_copy(src_hbm.at[pl.ds(pl.multiple_of(src_token * SUBLANES, SUBLANES), SUBLANES)],
                                 dst_vmem.at[pl.ds(pl.multiple_of(dst_token * SUBLANES, SUBLANES), SUBLANES)],
                                 sem)


def _token_tiles_wait(src_hbm, dst_vmem, first_token, n, sem):
    pltpu.make_async_copy(src_hbm.at[pl.ds(0, n * SUBLANES)],
                          dst_vmem.at[pl.ds(pl.multiple_of(first_token * SUBLANES, SUBLANES), n * SUBLANES)],
                          sem).wait()


def _norm_proj_kernel(h_ref, g_ref, w_ref, *rest, tn, with_aux):
    xn = _rms(h_ref[...], g_ref[...]).astype(BF16)
    if with_aux:
        w2_ref, o_ref, o2_ref = rest
        o2_ref[...] = _dot(xn, w2_ref[...])
    else:
        (o_ref,) = rest
    n = w_ref.shape[1]
    for c in range(0, n, tn):
        o_ref[:, c:c + tn] = _dot(xn, w_ref[:, c:c + tn]).astype(o_ref.dtype)


def _norm_proj(h, g, w, w_aux=None, *, name):
    t, d = h.shape
    n = w.shape[1]
    tm = min(ROW_TILE, t)
    tn = COL_TILE if n % COL_TILE == 0 else n
    in_specs = [pl.BlockSpec((tm, d), lambda i: (i, 0)), _resident((1, d)), _resident((d, n))]
    out_shape = [jax.ShapeDtypeStruct((t, n), BF16)]
    out_specs = [pl.BlockSpec((tm, n), lambda i: (i, 0))]
    args = [h, g.reshape(1, d), w]
    if w_aux is not None:
        in_specs.append(_resident(w_aux.shape))
        out_shape.append(jax.ShapeDtypeStruct((t, w_aux.shape[1]), F32))
        out_specs.append(pl.BlockSpec((tm, w_aux.shape[1]), lambda i: (i, 0)))
        args.append(w_aux)
    out = pl.pallas_call(
        functools.partial(_norm_proj_kernel, tn=tn, with_aux=w_aux is not None),
        grid=(t // tm,), in_specs=in_specs, out_specs=out_specs, out_shape=out_shape,
        compiler_params=_cparams("parallel"), name=name)(*args)
    return out if w_aux is not None else out[0]


def _proj_residual_kernel(*refs):
    n = (len(refs) - 2) // 2
    y_refs, w_refs, h_ref, o_ref = refs[:n], refs[n:2 * n], refs[2 * n], refs[2 * n + 1]
    acc = h_ref[...]
    for y_ref, w_ref in zip(y_refs, w_refs):
        acc = acc + _dot(y_ref[...], w_ref[...])
    o_ref[...] = acc


def _proj_residual(ys, w, h, *, name):
    t, d = h.shape
    tm = min(ROW_TILE, t)
    ws, r0 = [], 0
    for y in ys:
        ws.append(w[r0:r0 + y.shape[1]])
        r0 += y.shape[1]
    return pl.pallas_call(
        _proj_residual_kernel, grid=(t // tm,),
        in_specs=([pl.BlockSpec((tm, y.shape[1]), lambda i: (i, 0)) for y in ys]
                  + [_resident(wi.shape) for wi in ws] + [pl.BlockSpec((tm, d), lambda i: (i, 0))]),
        out_specs=pl.BlockSpec((tm, d), lambda i: (i, 0)),
        out_shape=jax.ShapeDtypeStruct((t, d), F32),
        compiler_params=_cparams("parallel"), name=name)(*ys, *ws, h)


def _swiglu_residual(x, g_ref, wg_ref, wu_ref, wd_ref, tf):
    xn = _rms(x, g_ref[...]).astype(BF16)
    acc = x
    for c in range(0, wg_ref.shape[1], tf):
        gate = _dot(xn, wg_ref[:, c:c + tf])
        up = _dot(xn, wu_ref[:, c:c + tf])
        hid = (_silu(gate) * up).astype(BF16)
        acc = acc + _dot(hid, wd_ref[c:c + tf, :])
    return acc


def _ffn_ple_proj_kernel(h_ref, gf_ref, wg_ref, wu_ref, wd_ref, gp_ref, p_ref, wp_ref, wpg_ref,
                         gm_ref, win_ref, h_out_ref, u_ref, *, tf, tn):
    h2 = _swiglu_residual(h_ref[...], gf_ref, wg_ref, wu_ref, wd_ref, tf)
    h3 = _ple_math(h2, gp_ref, p_ref, wp_ref, wpg_ref)
    h_out_ref[...] = h3
    xn = _rms(h3, gm_ref[...]).astype(BF16)
    for c in range(0, win_ref.shape[1], tn):
        u_ref[:, c:c + tn] = _dot(xn, win_ref[:, c:c + tn]).astype(u_ref.dtype)


def _ff_tile(f, cap):
    best = LANES
    for c in range(LANES, cap + 1, LANES):
        if f % c == 0:
            best = c
    return best


def _ffn_ple_proj(h, g_ffn, wg, wu, wd, g_ple, p, wp, wpg, g_mix, w_in, *, name):
    t, d = h.shape
    f = wg.shape[1]
    e = p.shape[1]
    n = w_in.shape[1]
    tm = min(ROW_TILE, t)
    tn = COL_TILE if n % COL_TILE == 0 else n
    row = lambda w: pl.BlockSpec((tm, w), lambda i: (i, 0))
    return pl.pallas_call(
        functools.partial(_ffn_ple_proj_kernel, tf=_ff_tile(f, 1536), tn=tn), grid=(t // tm,),
        in_specs=[row(d), _resident((1, d)), _resident((d, f)), _resident((d, f)), _resident((f, d)),
                  _resident((1, d)), row(e), _resident((e, d)), _resident((d, d)),
                  _resident((1, d)), _resident((d, n))],
        out_specs=[row(d), row(n)],
        out_shape=[jax.ShapeDtypeStruct((t, d), F32), jax.ShapeDtypeStruct((t, n), BF16)],
        compiler_params=_cparams("parallel"), name=name,
    )(h, g_ffn.reshape(1, d), wg, wu, wd, g_ple.reshape(1, d), p, wp, wpg, g_mix.reshape(1, d), w_in)


def _ple_math(x, g_ref, p_ref, wp_ref, wg_ref):
    xn = _rms(x, g_ref[...]).astype(BF16)
    gate = _sigmoid(_dot(xn, wg_ref[...]))
    emb = _dot(p_ref[...].astype(BF16), wp_ref[...])
    return x + emb * gate


def _unit_lower_inverses(mats):
    n = mats[0].shape[0]
    row = lax.broadcasted_iota(jnp.int32, (n, n), 0)
    col = lax.broadcasted_iota(jnp.int32, (n, n), 1)
    eye = (row == col).astype(F32)
    same16 = (row // 16) == (col // 16)
    same32 = (row // 32) == (col // 32)
    off32 = jnp.logical_and(same32, jnp.logical_not(same16))

    def mm(ps, qs):
        return [_dot(p.astype(BF16), q.astype(BF16)) for p, q in zip(ps, qs)]

    ad = [jnp.where(same16, a, 0.0) for a in mats]
    a2 = mm(ad, ad)
    x = mm([eye - t for t in ad], [eye + t for t in a2])
    a4 = mm(a2, a2)
    x = mm(x, [eye + t for t in a4])
    a8 = mm(a4, a4)
    x = mm(x, [eye + t for t in a8])
    y = mm([jnp.where(off32, a, 0.0) for a in mats], x)
    x = [t - c for t, c in zip(x, mm(x, y))]
    y = mm([jnp.where(same32, 0.0, a) for a in mats], x)
    return [t - c for t, c in zip(x, mm(x, y))]


def _l0_mixer_kernel(bg_ref, cg_ref, xin_ref, q_ref, k_ref, v_ref, og_ref, ab_ref,
                     wa_ref, wq_ref, wk_ref, wv_ref, alog_ref, dtb_ref, gn_ref,
                     o_ref,
                     pad_ref, qs_ref, ks_ref, vs_ref, us_ref, ws_ref, qk_ref, egl_ref, st_ref):
    s, ca = bg_ref.shape
    heads = GDN_HEADS
    dk = q_ref.shape[1] // heads
    c64 = GDN_CHUNK
    n_chunks = s // c64
    rt = min(ELEM_ROWS, s)
    n_tiles = s // rt

    pad_ref[0:TOP_PAD, :] = jnp.zeros((TOP_PAD, pad_ref.shape[1]), F32)

    def rows(i):
        return pl.ds(pl.multiple_of(i * rt, rt), rt)

    def fill_pad(fn):
        def body(i, c):
            r = pl.multiple_of(i * rt, rt)
            pad_ref[pl.ds(TOP_PAD + r, rt), :] = fn(rows(i))
            return c
        lax.fori_loop(0, n_tiles, body, 0)

    def conv_tile(i, w_ref, width):
        tap = _delayed(pad_ref, pl.multiple_of(i * rt, rt), rt, width - 1)
        acc = None
        for j in range(width):
            term = w_ref[j:j + 1, :] * tap(width - 1 - j)
            acc = term if acc is None else acc + term
        return acc

    fill_pad(lambda rs: cg_ref[rs, :].astype(F32) * xin_ref[rs, :].astype(F32))

    def mixer_a(i, c):
        acc = conv_tile(i, wa_ref, CONV_A_WIDTH)
        o_ref[rows(i), 0:ca] = (bg_ref[rows(i), :].astype(F32) * acc).astype(o_ref.dtype)
        return c
    lax.fori_loop(0, n_tiles, mixer_a, 0)

    def l2n(x, scale):
        parts = []
        for h in range(heads):
            xh = x[:, h * dk:(h + 1) * dk]
            inv = lax.rsqrt(jnp.sum(xh * xh, axis=-1, keepdims=True) + EPS)
            parts.append(xh * (inv * scale))
        return jnp.concatenate(parts, axis=1)

    for src_ref, w_ref, dst_ref, post in (
            (q_ref, wq_ref, qs_ref, lambda x: l2n(x, dk ** -0.5)),
            (k_ref, wk_ref, ks_ref, lambda x: l2n(x, 1.0)),
            (v_ref, wv_ref, vs_ref, lambda x: x)):
        fill_pad(lambda rs, src_ref=src_ref: src_ref[rs, :].astype(F32))

        def conv_body(i, c, w_ref=w_ref, dst_ref=dst_ref, post=post):
            dst_ref[rows(i), :] = post(_silu(conv_tile(i, w_ref, GDN_CONV_WIDTH)))
            return c
        lax.fori_loop(0, n_tiles, conv_body, 0)

    ri = lax.broadcasted_iota(jnp.int32, (c64, c64), 0)
    ci = lax.broadcasted_iota(jnp.int32, (c64, c64), 1)
    tril = ri >= ci
    strict = ri > ci
    ltri = tril.astype(F32)

    group = PREP_CHUNKS if n_chunks % PREP_CHUNKS == 0 else 1

    def chunk_prep(cg, carry):
        chunks = []
        for cc in range(group):
            c = cg * group + cc
            rs = pl.ds(pl.multiple_of(c * c64, c64), c64)
            chunks.append((c, rs, ab_ref[rs, :], qs_ref[rs, :], ks_ref[rs, :], vs_ref[rs, :]))
        inst = []
        gcs = [_dot(ltri, -jnp.exp(alog_ref[...]) * _softplus(ab + dtb_ref[...]), precision=HIGHEST)
               for _, _, ab, _, _, _ in chunks]
        for (c, rs, ab, q_all, k_all, v_all), gc in zip(chunks, gcs):
            beta = _sigmoid(ab)
            gct = gc.T
            for h in range(heads):
                hs = slice(h * dk, (h + 1) * dk)
                gcol = gc[:, h:h + 1]
                glast = gc[c64 - 1:c64, h:h + 1]
                bcol = beta[:, heads + h:heads + h + 1]
                decay = jnp.where(tril, jnp.exp(jnp.where(tril, gcol - gct[h:h + 1, :], 0.0)), 0.0)
                kh, qh, vh = k_all[:, hs], q_all[:, hs], v_all[:, hs]
                kb = kh * bcol
                egc = jnp.exp(gcol)
                inst.append(dict(decay=decay, kh=kh, qh=qh, kb=kb, egc=egc,
                                 rhs=jnp.concatenate([vh * bcol, kb * egc], axis=1).astype(BF16),
                                 kd=kh * jnp.exp(glast - gcol),
                                 eg=jnp.broadcast_to(jnp.exp(glast), (SUBLANES, dk))))
        kqs = [_dot_nt(jnp.concatenate([t["kb"], t["qh"]], axis=0).astype(BF16), t["kh"].astype(BF16))
               for t in inst]
        minvs = _unit_lower_inverses([jnp.where(strict, kq[0:c64] * t["decay"], 0.0)
                                      for kq, t in zip(kqs, inst)])
        uws = [_dot(m.astype(BF16), t["rhs"]) for m, t in zip(minvs, inst)]
        for ci, (c, rs, _, _, _, _) in enumerate(chunks):
            sl = slice(ci * heads, (ci + 1) * heads)
            us_ref[rs, :] = jnp.concatenate([uw[:, 0:dk] for uw in uws[sl]], axis=1)
            ws_ref[rs, :] = jnp.concatenate([uw[:, dk:2 * dk] for uw in uws[sl]], axis=1)
            qs_ref[rs, :] = jnp.concatenate([t["qh"] * t["egc"] for t in inst[sl]], axis=1)
            ks_ref[rs, :] = jnp.concatenate([t["kd"] for t in inst[sl]], axis=1)
            qk_ref[rs, :] = jnp.concatenate([kq[c64:2 * c64] * t["decay"]
                                             for kq, t in zip(kqs[sl], inst[sl])], axis=1)
            e0 = pl.multiple_of(c * (heads * SUBLANES), heads * SUBLANES)
            egl_ref[pl.ds(e0, heads * SUBLANES), :] = jnp.concatenate([t["eg"] for t in inst[sl]], axis=0)
        return carry
    lax.fori_loop(0, n_chunks // group, chunk_prep, 0)

    st_ref[...] = jnp.zeros(st_ref.shape, F32)

    def scan(c, carry):
        rs = pl.ds(pl.multiple_of(c * c64, c64), c64)
        e0 = pl.multiple_of(c * (heads * SUBLANES), heads * SUBLANES)
        w_all, qg_all, u_all, kd_all = ws_ref[rs, :], qs_ref[rs, :], us_ref[rs, :], ks_ref[rs, :]
        qk_all = qk_ref[rs, :]
        eg_all = egl_ref[pl.ds(e0, heads * SUBLANES), :]
        hsl = [slice(h * dk, (h + 1) * dk) for h in range(heads)]
        sts = [st_ref[h] for h in range(heads)]
        wqs = [_dot(jnp.concatenate([w_all[:, hs], qg_all[:, hs]], axis=0).astype(BF16), st.astype(BF16))
               for hs, st in zip(hsl, sts)]
        vbs = [(u_all[:, hs] - wq[0:c64]).astype(BF16) for hs, wq in zip(hsl, wqs)]
        upd = [_dot_tn(kd_all[:, hs].astype(BF16), vb) for hs, vb in zip(hsl, vbs)]
        intra = [_dot(qk_all[:, h * c64:(h + 1) * c64].astype(BF16), vbs[h]) for h in range(heads)]
        for h in range(heads):
            st_ref[h] = sts[h] * eg_all[h * SUBLANES:h * SUBLANES + 1, :] + upd[h]
        vs_ref[rs, :] = jnp.concatenate([wq[c64:2 * c64] + o for wq, o in zip(wqs, intra)], axis=1)
        return carry
    lax.fori_loop(0, n_chunks, scan, 0)

    def finish(i, c):
        o = vs_ref[rows(i), :]
        og = og_ref[rows(i), :].astype(F32)
        parts = []
        for h in range(heads):
            oh = o[:, h * dk:(h + 1) * dk]
            parts.append(_rms(oh, gn_ref[...]))
        y = jnp.concatenate(parts, axis=1) * _silu(og)
        o_ref[rows(i), ca:ca + heads * dk] = y.astype(o_ref.dtype)
        return c
    lax.fori_loop(0, n_tiles, finish, 0)


def _l0_mixer(u, ab, conv_a, gdn_conv, a_log, dt_bias, gdn_norm_g, batch, seq):
    cw = conv_a.shape[1]
    heads = GDN_HEADS
    dk = cw // heads

    def col(j):
        return pl.BlockSpec((seq, cw), lambda b, j=j: (b, j), pipeline_mode=pl.Buffered(1))

    def lane_row(x):
        return jnp.zeros((1, LANES), F32).at[0, :x.shape[0]].set(x)

    in_specs = [col(j) for j in range(7)]
    in_specs += [pl.BlockSpec((seq, LANES), lambda b: (b, 0)),
                 _resident(conv_a.shape)]
    in_specs += [pl.BlockSpec((GDN_CONV_WIDTH, cw), lambda b, j=j: (0, j), pipeline_mode=pl.Buffered(1))
                 for j in range(3)]
    in_specs += [_resident((1, LANES)), _resident((1, LANES)), _resident((1, dk))]
    big = pltpu.VMEM((seq, cw), F32)
    scratch = [pltpu.VMEM((seq + TOP_PAD, cw), F32), big, big, big, big, big,
               pltpu.VMEM((seq, heads * GDN_CHUNK), F32),
               pltpu.VMEM((seq // GDN_CHUNK * heads * SUBLANES, dk), F32),
               pltpu.VMEM((heads, dk, dk), F32)]
    return pl.pallas_call(
        _l0_mixer_kernel, grid=(batch,), in_specs=in_specs,
        out_specs=pl.BlockSpec((seq, 2 * cw), lambda b: (b, 0), pipeline_mode=pl.Buffered(1)),
        out_shape=jax.ShapeDtypeStruct((batch * seq, 2 * cw), BF16),
        scratch_shapes=scratch, compiler_params=_cparams("parallel"), name="l0_mixers",
    )(u, u, u, u, u, u, u, ab, conv_a, gdn_conv, gdn_conv, gdn_conv,
      lane_row(a_log), lane_row(dt_bias), gdn_norm_g.reshape(1, dk))


def _bias_table_kernel(tab_ref, o_ref, *, t):
    h = pl.program_id(0)
    m = pl.program_id(1)
    ri = lax.broadcasted_iota(jnp.int32, (t, t), 0)
    ci = lax.broadcasted_iota(jnp.int32, (t, t), 1)
    rel = m * t + ci - ri
    n = jnp.maximum(rel, 0)
    max_exact = NUM_BUCKETS // 2
    nf = jnp.maximum(n, 1).astype(F32)
    large = max_exact + (jnp.log(nf / max_exact) / math.log(MAX_DISTANCE / max_exact)
                         * (NUM_BUCKETS - max_exact)).astype(jnp.int32)
    large = jnp.minimum(large, NUM_BUCKETS - 1)
    bucket = jnp.where(n < max_exact, n, large)
    bias = jnp.zeros((t, t), F32)
    for b in range(NUM_BUCKETS):
        bias = jnp.where(bucket == b, tab_ref[b, h], bias)
    o_ref[...] = jnp.where(rel >= 0, bias * LOG2E, MASK_VALUE)


def _bias_table(rel_bias, seq, t):
    nb = seq // t
    heads = rel_bias.shape[1]
    return pl.pallas_call(
        functools.partial(_bias_table_kernel, t=t), grid=(heads, nb),
        in_specs=[pl.BlockSpec(memory_space=pltpu.SMEM)],
        out_specs=pl.BlockSpec((None, None, t, t), lambda h, m: (h, m, 0, 0)),
        out_shape=jax.ShapeDtypeStruct((heads, nb, t, t), F32),
        compiler_params=_cparams("parallel", "parallel"), name="rel_bias_table")(rel_bias)


def _attn_kernel(q_ref, k_ref, v_ref, tb_ref, lam_ref, gn_ref, o_ref, *, t, heads, lambda_init):
    qi = pl.program_id(2)
    dh2 = q_ref.shape[1] // heads
    dh = dh2 // 2
    lane = lax.broadcasted_iota(jnp.int32, (t, dh2), 1)
    qqs = []
    for h in range(heads):
        qf = q_ref[:, h * dh2:(h + 1) * dh2].astype(F32) * (dh ** -0.5 * LOG2E)
        qqs.append(jnp.concatenate([jnp.where(lane < dh, qf, 0.0), jnp.where(lane >= dh, qf, 0.0)],
                                   axis=0).astype(BF16))

    def body(j, carry):
        ks = pl.ds(pl.multiple_of(j * t, t), t)
        k_all = k_ref[ks, :]
        v_all = v_ref[ks, :]
        hs = [slice(h * dh2, (h + 1) * dh2) for h in range(heads)]
        s_t = [_dot_nt(k_all[:, hs[h]], qqs[h]) for h in range(heads)]
        ps, stats = [], []
        for h in range(heads):
            m, l, _ = carry[h]
            b = tb_ref[h, qi - j]
            s = jnp.concatenate([s_t[h][:, 0:t] + b, s_t[h][:, t:2 * t] + b], axis=1)
            m_new = jnp.maximum(m, jnp.max(s, axis=0, keepdims=True))
            alpha = jnp.exp2(m - m_new)
            p = jnp.exp2(s - m_new)
            stats.append((m_new, alpha * l + jnp.sum(p, axis=0, keepdims=True), alpha))
            ps.append(p.astype(BF16))
        pv = [_dot_tn(v_all[:, hs[h]], ps[h]) for h in range(heads)]
        return tuple((stats[h][0], stats[h][1], stats[h][2] * carry[h][2] + pv[h]) for h in range(heads))

    init = tuple((jnp.full((1, 2 * t), MASK_VALUE, F32), jnp.zeros((1, 2 * t), F32),
                  jnp.zeros((dh2, 2 * t), F32)) for _ in range(heads))
    final = lax.fori_loop(0, qi + 1, body, init)
    lp = lam_ref[...]
    lam = (jnp.exp(jnp.sum(lp[0:1] * lp[1:2], axis=-1, keepdims=True))
           - jnp.exp(jnp.sum(lp[2:3] * lp[3:4], axis=-1, keepdims=True)) + lambda_init)
    for h in range(heads):
        _, l, acc = final[h]
        o12 = acc / l
        o = (o12[:, 0:t] - lam * o12[:, t:2 * t]).T
        o_ref[:, h * dh2:(h + 1) * dh2] = (_rms(o, gn_ref[...]) * (1.0 - lambda_init)).astype(o_ref.dtype)


def _diff_attention(u, table, lam_params, norm_g, batch, seq, lambda_init):
    heads = DIFF_HEADS
    dh2 = norm_g.shape[0]
    hg = ATTN_HEADS_PER_STEP
    w = hg * dh2
    ng = heads // hg
    t = table.shape[2]
    nq = seq // t
    return pl.pallas_call(
        functools.partial(_attn_kernel, t=t, heads=hg, lambda_init=lambda_init),
        grid=(batch, ng, nq),
        in_specs=[pl.BlockSpec((t, w), lambda b, g, i: (b * nq + i, g)),
                  pl.BlockSpec((seq, w), lambda b, g, i: (b, ng + g)),
                  pl.BlockSpec((seq, w), lambda b, g, i: (b, 2 * ng + g)),
                  pl.BlockSpec((hg, nq, t, t), lambda b, g, i: (g, 0, 0, 0)),
                  _resident(lam_params.shape), _resident((1, dh2))],
        out_specs=pl.BlockSpec((t, w), lambda b, g, i: (b * nq + i, g)),
        out_shape=jax.ShapeDtypeStruct((batch * seq, heads * dh2), BF16),
        compiler_params=_cparams("parallel", "parallel", "arbitrary"), name="diff_attention",
    )(u, u, u, table, lam_params, norm_g.reshape(1, dh2))


def _conformer_kernel(ga_ref, gb_ref, w_ref, b_ref, lg_ref, lb_ref, o_ref, pad_ref):
    s, c = ga_ref.shape
    rt = min(ELEM_ROWS, s)
    pad_ref[0:TOP_PAD, :] = jnp.zeros((TOP_PAD, c), F32)

    def glu(i, carry):
        r = pl.multiple_of(i * rt, rt)
        rs = pl.ds(r, rt)
        pad_ref[pl.ds(TOP_PAD + r, rt), :] = ga_ref[rs, :].astype(F32) * _sigmoid(gb_ref[rs, :].astype(F32))
        return carry
    lax.fori_loop(0, s // rt, glu, 0)

    ct = CONF_ROWS

    def conv(i, carry):
        r = pl.multiple_of(i * ct, ct)
        tap = _delayed(pad_ref, r, ct, CONF_WIDTH - 1)
        acc = jnp.broadcast_to(b_ref[...], (ct, c))
        for j in range(CONF_WIDTH):
            acc = acc + w_ref[j:j + 1, :] * tap(CONF_WIDTH - 1 - j)
        mu = jnp.mean(acc, axis=-1, keepdims=True)
        xc = acc - mu
        var = jnp.mean(xc * xc, axis=-1, keepdims=True)
        y = xc * lax.rsqrt(var + EPS) * lg_ref[...] + lb_ref[...]
        o_ref[pl.ds(r, ct), :] = _silu(y).astype(o_ref.dtype)
        return carry
    lax.fori_loop(0, s // ct, conv, 0)


def _conformer(u, w, b, ln_g, ln_b, batch, seq, col0):
    c = w.shape[1]
    return pl.pallas_call(
        _conformer_kernel, grid=(batch,),
        in_specs=[pl.BlockSpec((seq, c), lambda i: (i, col0)),
                  pl.BlockSpec((seq, c), lambda i: (i, col0 + 1)),
                  _resident(w.shape), _resident((1, c)), _resident((1, c)), _resident((1, c))],
        out_specs=pl.BlockSpec((seq, c), lambda i: (i, 0)),
        out_shape=jax.ShapeDtypeStruct((batch * seq, c), BF16),
        scratch_shapes=[pltpu.VMEM((seq + TOP_PAD, c), F32)],
        compiler_params=_cparams("parallel"), name="conformer_conv",
    )(u, u, w, b.reshape(1, c), ln_g.reshape(1, c), ln_b.reshape(1, c))


def _proj_router_kernel(y1_ref, y2_ref, w1_ref, w2_ref, h_ref, g_ref, wr_ref,
                        h_out_ref, hn_ref, idx_ref, gate_ref):
    h = h_ref[...] + _dot(y1_ref[...], w1_ref[...]) + _dot(y2_ref[...], w2_ref[...])
    h_out_ref[...] = h
    xn = _rms(h, g_ref[...])
    _store_token_tiles(hn_ref, 0, xn)
    logits = _dot(xn.astype(BF16), wr_ref[...])
    lane = lax.broadcasted_iota(jnp.int32, logits.shape, 1)
    neg = -jnp.inf
    lg = jnp.where(lane < N_EXPERTS, logits, neg)
    m1 = jnp.max(lg, axis=-1, keepdims=True)
    i1 = jnp.min(jnp.where(lg == m1, lane, LANES), axis=-1, keepdims=True)
    lg2 = jnp.where(lane == i1, neg, lg)
    m2 = jnp.max(lg2, axis=-1, keepdims=True)
    i2 = jnp.min(jnp.where(lg2 == m2, lane, LANES), axis=-1, keepdims=True)
    e = jnp.exp(m2 - m1)
    g1 = 1.0 / (1.0 + e)
    idx_ref[...] = jnp.where(lane == 0, i1, jnp.where(lane == 1, i2, 0))
    gate_ref[...] = jnp.where(lane == 0, g1, jnp.where(lane == 1, e * g1, 0.0))


def _proj_router(y1, y2, w, h, g, wr):
    t, d = h.shape
    k1, k2 = y1.shape[1], y2.shape[1]
    tm = min(ROW_TILE, t)
    row = lambda width: pl.BlockSpec((tm, width), lambda i: (i, 0))
    return pl.pallas_call(
        _proj_router_kernel, grid=(t // tm,),
        in_specs=[row(k1), row(k2), _resident((k1, d)), _resident((k2, d)), row(d),
                  _resident((1, d)), _resident((d, LANES))],
        out_specs=[row(d), pl.BlockSpec((tm * d // LANES, LANES), lambda i: (i, 0)), row(LANES), row(LANES)],
        out_shape=[jax.ShapeDtypeStruct((t, d), F32), jax.ShapeDtypeStruct((t * d // LANES, LANES), F32),
                   jax.ShapeDtypeStruct((t, LANES), jnp.int32), jax.ShapeDtypeStruct((t, LANES), F32)],
        compiler_params=_cparams("parallel"), name="l1_out_proj_router",
    )(y1, y2, w[:k1], w[k1:], h, g.reshape(1, d), wr)


def _experts_kernel(blk_e_ref, tok_hbm, x_hbm, wgu_ref, wd_ref, o_ref,
                    xbuf, tok0, tok1, sem_tok, sem_rows):
    i = pl.program_id(0)
    nb = pl.num_programs(0)
    f = wd_ref.shape[0]
    tm = xbuf.shape[0] // (2 * SUBLANES)
    slot = i % 2
    toks = (tok0, tok1)

    def tok_copy(blk, s):
        return pltpu.make_async_copy(tok_hbm.at[blk], toks[s], sem_tok.at[s])

    def issue_tiles(s):
        def body(r, c):
            _token_tile_copy(x_hbm, toks[s][r], xbuf, s * tm + r, sem_rows.at[s]).start()
            return c
        lax.fori_loop(0, tm, body, 0, unroll=16)

    @pl.when(i == 0)
    def _():
        tok_copy(0, 0).start()
        tok_copy(0, 0).wait()
        tok_copy(1, 1).start()
        issue_tiles(0)

    for s in range(2):
        @pl.when(slot == s)
        def _(s=s):
            tok_copy(i + 1, 1 - s).wait()
            _token_tiles_wait(x_hbm, xbuf, s * tm, tm, sem_rows.at[s])
            issue_tiles(1 - s)

            @pl.when(i + 2 <= nb)
            def _():
                tok_copy(i + 2, s).start()

    x = _load_token_tiles(xbuf, slot * tm, tm).astype(BF16)
    acc = None
    for c in range(0, f, MOE_SUB_TILE):
        hid = (_silu(_dot(x, wgu_ref[:, c:c + MOE_SUB_TILE]))
               * _dot(x, wgu_ref[:, f + c:f + c + MOE_SUB_TILE])).astype(BF16)
        part = _dot(hid, wd_ref[c:c + MOE_SUB_TILE, :])
        acc = part if acc is None else acc + part
    _store_token_tiles(o_ref, 0, acc)

    @pl.when(i == nb - 1)
    def _():
        _token_tiles_wait(x_hbm, xbuf, (1 - slot) * tm, tm, sem_rows.at[1 - slot])


def _experts(x_tiles, row_tok, blk_e, w_gate_up, w_down):
    f, d = w_down.shape[1], w_down.shape[2]
    tm = MOE_ROWS
    nb = row_tok.shape[0] // tm - 1
    tpt = d // LANES
    assert tpt == SUBLANES and f % MOE_SUB_TILE == 0
    grid_spec = pltpu.PrefetchScalarGridSpec(
        num_scalar_prefetch=1, grid=(nb,),
        in_specs=[pl.BlockSpec(memory_space=pl.ANY), pl.BlockSpec(memory_space=pl.ANY),
                  pl.BlockSpec((None, d, 2 * f), lambda i, e: (e[i], 0, 0), pipeline_mode=pl.Buffered(1)),
                  pl.BlockSpec((None, f, d), lambda i, e: (e[i], 0, 0), pipeline_mode=pl.Buffered(1))],
        out_specs=pl.BlockSpec((tm * tpt, LANES), lambda i, e: (i, 0)),
        scratch_shapes=[pltpu.VMEM((2 * tm * tpt, LANES), F32),
                        pltpu.SMEM((tm,), jnp.int32), pltpu.SMEM((tm,), jnp.int32),
                        pltpu.SemaphoreType.DMA((2,)), pltpu.SemaphoreType.DMA((2,))])
    return pl.pallas_call(
        _experts_kernel, grid_spec=grid_spec,
        out_shape=jax.ShapeDtypeStruct((nb * tm * tpt, LANES), F32),
        compiler_params=_cparams("arbitrary"), name="moe_experts",
    )(blk_e, row_tok.reshape(nb + 1, tm), x_tiles, w_gate_up, w_down)


def _combine_kernel(dest_hbm, y_hbm, gate_ref, h_ref, g_ref, p_ref, wp_ref, wg_ref, fg_ref, o_ref,
                    idx0, idx1, ybuf, sem_idx, sem_rows):
    i = pl.program_id(0)
    steps = pl.num_programs(0)
    n = h_ref.shape[0]
    m = TOP_K * n
    slot = i % 2
    idxs = (idx0, idx1)

    def idx_copy(blk, s):
        return pltpu.make_async_copy(dest_hbm.at[blk], idxs[s], sem_idx.at[s])

    def issue_tiles(s):
        def body(r, c):
            _token_tile_copy(y_hbm, idxs[s][r], ybuf, s * m + r, sem_rows.at[s]).start()
            return c
        lax.fori_loop(0, m, body, 0, unroll=16)

    @pl.when(i == 0)
    def _():
        idx_copy(0, 0).start()
        idx_copy(0, 0).wait()
        idx_copy(1, 1).start()
        issue_tiles(0)

    for s in range(2):
        @pl.when(slot == s)
        def _(s=s):
            idx_copy(i + 1, 1 - s).wait()

            @pl.when(i + 2 <= steps)
            def _():
                idx_copy(i + 2, s).start()

            issue_tiles(1 - s)
            _token_tiles_wait(y_hbm, ybuf, s * m, m, sem_rows.at[s])

    gates = gate_ref[...]
    moe = (gates[:, 0:1] * _load_token_tiles(ybuf, slot * m, n)
           + gates[:, 1:2] * _load_token_tiles(ybuf, slot * m + n, n))
    x = _ple_math(h_ref[...] + moe, g_ref, p_ref, wp_ref, wg_ref)
    o_ref[...] = _rms(x, fg_ref[...])

    @pl.when(i == steps - 1)
    def _():
        _token_tiles_wait(y_hbm, ybuf, (1 - slot) * m, m, sem_rows.at[1 - slot])


def _combine_ple_final(dest, y_tiles, gates, h, g, p, wp, wg, final_g):
    t, d = h.shape
    e = p.shape[1]
    n = min(COMBINE_ROWS, t)
    m = TOP_K * n
    steps = t // n
    dest_steps = jnp.concatenate([dest[k].reshape(steps, n) for k in range(TOP_K)], axis=1)
    dest_steps = jnp.concatenate([dest_steps, jnp.zeros((1, m), jnp.int32)], axis=0)
    return pl.pallas_call(
        _combine_kernel, grid=(steps,),
        in_specs=[pl.BlockSpec(memory_space=pl.ANY), pl.BlockSpec(memory_space=pl.ANY),
                  pl.BlockSpec((n, LANES), lambda i: (i, 0)),
                  pl.BlockSpec((n, d), lambda i: (i, 0)), _resident((1, d)),
                  pl.BlockSpec((n, e), lambda i: (i, 0)), _resident((e, d)), _resident((d, d)),
                  _resident((1, d))],
        out_specs=pl.BlockSpec((n, d), lambda i: (i, 0)),
        out_shape=jax.ShapeDtypeStruct((t, d), F32),
        scratch_shapes=[pltpu.SMEM((m,), jnp.int32), pltpu.SMEM((m,), jnp.int32),
                        pltpu.VMEM((2 * m * d // LANES, LANES), F32),
                        pltpu.SemaphoreType.DMA((2,)), pltpu.SemaphoreType.DMA((2,))],
        compiler_params=_cparams("arbitrary"), name="moe_combine_ple_final",
    )(dest_steps, y_tiles, gates, h, g.reshape(1, d), p, wp, wg, final_g.reshape(1, d))


def _dispatch_plan(first, second, t):
    experts = jnp.arange(N_EXPERTS, dtype=jnp.int32)[:, None]
    m1 = (first[None, :] == experts).astype(jnp.int32)
    m2 = (second[None, :] == experts).astype(jnp.int32)
    both = m1 + m2
    csum = jnp.cumsum(both, axis=1)
    before = csum - both
    counts = csum[:, -1]
    padded = (counts + MOE_ROWS - 1) // MOE_ROWS * MOE_ROWS
    pend = jnp.cumsum(padded)
    base = (pend - padded)[:, None] + before
    dest = jnp.stack([jnp.sum(m1 * base, axis=0), jnp.sum(m2 * base, axis=0)])
    p = t * TOP_K + N_EXPERTS * MOE_ROWS
    tok = jnp.arange(t, dtype=jnp.int32)
    row_tok = jnp.zeros((p + MOE_ROWS,), jnp.int32).at[dest.reshape(-1)].set(
        jnp.concatenate([tok, tok]), unique_indices=True, mode="promise_in_bounds")
    nb = p // MOE_ROWS
    blk_e = jnp.minimum(jnp.searchsorted(pend, jnp.arange(nb, dtype=jnp.int32) * MOE_ROWS, side='right'),
                        N_EXPERTS - 1).astype(jnp.int32)
    return dest, row_tok, blk_e


def kernel(x, p, norm_mix_g, norm_ffn_g, norm_ple_g, final_norm_g, ev_w_in, ev_conv_a, ev_gdn_conv, ev_gdn_A_log, ev_gdn_dt_bias, ev_gdn_norm_g, ev_w_out, od_w_in, od_lambda, od_diff_norm_g, od_conf_dw_w, od_conf_dw_b, od_conf_ln_g, od_conf_ln_b, od_w_out, rel_bias, ffn_w_gate_up, ffn_w_down, moe_router, moe_w_gate_up, moe_w_down, ple_w_proj, ple_w_gate):
    batch, seq, d = x.shape
    t = batch * seq
    depth = p.shape[0]
    assert depth == 2 and seq % GDN_CHUNK == 0
    h = x.reshape(t, d)
    pf = p.reshape(depth, t, p.shape[-1])

    heads = GDN_HEADS
    n_main = ev_w_in.shape[2] - 2 * heads
    w_in = ev_w_in[0]
    w_ab = jnp.zeros((d, LANES), BF16).at[:, :2 * heads].set(w_in[:, n_main:].astype(BF16))
    u, ab = _norm_proj(h, norm_mix_g[0], w_in[:, :n_main].astype(BF16), w_ab, name="l0_in_proj")
    y = _l0_mixer(u, ab, ev_conv_a[0], ev_gdn_conv[0], ev_gdn_A_log[0], ev_gdn_dt_bias[0],
                  ev_gdn_norm_g[0], batch, seq)
    h = _proj_residual([y], ev_w_out[0].astype(BF16), h, name="l0_out_proj")
    f = ffn_w_down.shape[1]
    lambda_init = 0.8 - 0.6 * math.exp(-0.3 * 1)
    h, u = _ffn_ple_proj(h, norm_ffn_g[0], ffn_w_gate_up[0, :, :f].astype(BF16),
                         ffn_w_gate_up[0, :, f:].astype(BF16), ffn_w_down[0].astype(BF16),
                         norm_ple_g[0], pf[0], ple_w_proj[0].astype(BF16), ple_w_gate[0].astype(BF16),
                         norm_mix_g[1], od_w_in[0].astype(BF16), name="l0_ffn_ple_l1_in_proj")

    table = _bias_table(rel_bias, seq, min(ATTN_TILE, seq))
    o_attn = _diff_attention(u, table, od_lambda[0], od_diff_norm_g[0], batch, seq, lambda_init)
    c_conf = od_conf_dw_w.shape[2]
    o_conf = _conformer(u, od_conf_dw_w[0], od_conf_dw_b[0], od_conf_ln_g[0], od_conf_ln_b[0],
                        batch, seq, 3 * DIFF_HEADS * od_diff_norm_g.shape[1] // c_conf)
    wr = jnp.zeros((d, LANES), BF16).at[:, :N_EXPERTS].set(moe_router[0].astype(BF16))
    h, hn, idx, gates = _proj_router(o_attn, o_conf, od_w_out[0].astype(BF16), h, norm_ffn_g[1], wr)
    dest, row_tok, blk_e = _dispatch_plan(idx[:, 0], idx[:, 1], t)
    y = _experts(hn, row_tok, blk_e, moe_w_gate_up[0].astype(BF16), moe_w_down[0].astype(BF16))
    out = _combine_ple_final(dest, y, gates, h, norm_ple_g[1], pf[1], ple_w_proj[1].astype(BF16),
                             ple_w_gate[1].astype(BF16), final_norm_g)
    return out.reshape(batch, seq, d)
```

```python
import functools
import math

import jax
import jax.numpy as jnp
from jax import lax
from jax.experimental import pallas as pl
from jax.experimental.pallas import tpu as pltpu

F32 = jnp.float32
BF16 = jnp.bfloat16
HIGHEST = lax.Precision.HIGHEST

EPS = 1e-6
CONV_A_WIDTH = 3
GDN_HEADS = 4
GDN_CONV_WIDTH = 4
GDN_CHUNK = 64
DIFF_HEADS = 4
NUM_BUCKETS = 32
MAX_DISTANCE = 128
CONF_WIDTH = 31
N_EXPERTS = 8
TOP_K = 2

LANES = 128
SUBLANES = 8
VMEM_LIMIT_BYTES = 56 * 1024 * 1024
MASK_VALUE = -1e30
LOG2E = math.log2(math.e)

ROW_TILE = 512
COL_TILE = 512
ELEM_ROWS = 256
CONF_ROWS = 256
PREP_CHUNKS = 8
ATTN_TILE = 256
ATTN_HEADS_PER_STEP = 4
MOE_ROWS = 1024
MOE_SUB_TILE = 256
COMBINE_ROWS = 256
TOP_PAD = 32


def _cparams(*sem):
    return pltpu.CompilerParams(dimension_semantics=sem, vmem_limit_bytes=VMEM_LIMIT_BYTES)


def _resident(shape):
    nd = len(shape)
    return pl.BlockSpec(shape, lambda *_: (0,) * nd, pipeline_mode=pl.Buffered(1))


def _rms(x, g):
    return x * lax.rsqrt(jnp.mean(x * x, axis=-1, keepdims=True) + EPS) * g


def _sigmoid(x):
    return jax.nn.sigmoid(x)


def _silu(x):
    return x * jax.nn.sigmoid(x)


def _softplus(x):
    return jnp.maximum(x, 0.0) + jnp.log1p(jnp.exp(-jnp.abs(x)))


def _dot(a, b, **kw):
    return jnp.dot(a, b, preferred_element_type=F32, **kw)


def _dot_nt(a, b):
    return lax.dot_general(a, b, (((1,), (1,)), ((), ())), preferred_element_type=F32)


def _dot_tn(a, b):
    return lax.dot_general(a, b, (((0,), (0,)), ((), ())), preferred_element_type=F32)


def _delayed(pad_ref, r, rows, max_delay):
    lead = -(-max_delay // SUBLANES) * SUBLANES
    win = pad_ref[pl.ds(TOP_PAD + r - lead, rows + lead), :]
    rolled = {0: win}

    def tap(d):
        a, b = divmod(d, SUBLANES)
        if b not in rolled:
            rolled[b] = pltpu.roll(win, b, 0)
        start = lead - SUBLANES * a
        return rolled[b][start:start + rows, :]
    return tap


def _load_token_tiles(ref, first_token, n):
    return jnp.concatenate([ref[pl.ds(first_token * SUBLANES + s, n, stride=SUBLANES), :]
                            for s in range(SUBLANES)], axis=1)


def _store_token_tiles(ref, first_token, x):
    n = x.shape[0]
    for s in range(SUBLANES):
        ref[pl.ds(first_token * SUBLANES + s, n, stride=SUBLANES), :] = x[:, s * LANES:(s + 1) * LANES]


def _token_tile_copy(src_hbm, src_token, dst_vmem, dst_token, sem):
    return pltpu.make_async_copy(src_hbm.at[pl.ds(pl.multiple_of(src_token * SUBLANES, SUBLANES), SUBLANES)],
                                 dst_vmem.at[pl.ds(pl.multiple_of(dst_token * SUBLANES, SUBLANES), SUBLANES)],
                                 sem)


def _token_tiles_wait(src_hbm, dst_vmem, first_token, n, sem):
    pltpu.make_async_copy(src_hbm.at[pl.ds(0, n * SUBLANES)],
                          dst_vmem.at[pl.ds(pl.multiple_of(first_token * SUBLANES, SUBLANES), n * SUBLANES)],
                          sem).wait()


def _norm_proj_kernel(h_ref, g_ref, w_ref, *rest, tn, with_aux):
    xn = _rms(h_ref[...], g_ref[...]).astype(BF16)
    if with_aux:
        w2_ref, o_ref, o2_ref = rest
        o2_ref[...] = _dot(xn, w2_ref[...])
    else:
        (o_ref,) = rest
    n = w_ref.shape[1]
    for c in range(0, n, tn):
        o_ref[:, c:c + tn] = _dot(xn, w_ref[:, c:c + tn]).astype(o_ref.dtype)


def _norm_proj(h, g, w, w_aux=None, *, name):
    t, d = h.shape
    n = w.shape[1]
    tm = min(ROW_TILE, t)
    tn = COL_TILE if n % COL_TILE == 0 else n
    in_specs = [pl.BlockSpec((tm, d), lambda i: (i, 0)), _resident((1, d)), _resident((d, n))]
    out_shape = [jax.ShapeDtypeStruct((t, n), BF16)]
    out_specs = [pl.BlockSpec((tm, n), lambda i: (i, 0))]
    args = [h, g.reshape(1, d), w]
    if w_aux is not None:
        in_specs.append(_resident(w_aux.shape))
        out_shape.append(jax.ShapeDtypeStruct((t, w_aux.shape[1]), F32))
        out_specs.append(pl.BlockSpec((tm, w_aux.shape[1]), lambda i: (i, 0)))
        args.append(w_aux)
    out = pl.pallas_call(
        functools.partial(_norm_proj_kernel, tn=tn, with_aux=w_aux is not None),
        grid=(t // tm,), in_specs=in_specs, out_specs=out_specs, out_shape=out_shape,
        compiler_params=_cparams("parallel"), name=name)(*args)
    return out if w_aux is not None else out[0]


def _proj_residual_kernel(*refs):
    n = (len(refs) - 2) // 2
    y_refs, w_refs, h_ref, o_ref = refs[:n], refs[n:2 * n], refs[2 * n], refs[2 * n + 1]
    acc = h_ref[...]
    for y_ref, w_ref in zip(y_refs, w_refs):
        acc = acc + _dot(y_ref[...], w_ref[...])
    o_ref[...] = acc


def _proj_residual(ys, w, h, *, name):
    t, d = h.shape
    tm = min(ROW_TILE, t)
    ws, r0 = [], 0
    for y in ys:
        ws.append(w[r0:r0 + y.shape[1]])
        r0 += y.shape[1]
    return pl.pallas_call(
        _proj_residual_kernel, grid=(t // tm,),
        in_specs=([pl.BlockSpec((tm, y.shape[1]), lambda i: (i, 0)) for y in ys]
                  + [_resident(wi.shape) for wi in ws] + [pl.BlockSpec((tm, d), lambda i: (i, 0))]),
        out_specs=pl.BlockSpec((tm, d), lambda i: (i, 0)),
        out_shape=jax.ShapeDtypeStruct((t, d), F32),
        compiler_params=_cparams("parallel"), name=name)(*ys, *ws, h)


def _swiglu_residual(x, g_ref, wg_ref, wu_ref, wd_ref, tf):
    xn = _rms(x, g_ref[...]).astype(BF16)
    acc = x
    for c in range(0, wg_ref.shape[1], tf):
        gate = _dot(xn, wg_ref[:, c:c + tf])
        up = _dot(xn, wu_ref[:, c:c + tf])
        hid = (_silu(gate) * up).astype(BF16)
        acc = acc + _dot(hid, wd_ref[c:c + tf, :])
    return acc


def _ffn_ple_proj_kernel(h_ref, gf_ref, wg_ref, wu_ref, wd_ref, gp_ref, p_ref, wp_ref, wpg_ref,
                         gm_ref, win_ref, h_out_ref, u_ref, *, tf, tn):
    h2 = _swiglu_residual(h_ref[...], gf_ref, wg_ref, wu_ref, wd_ref, tf)
    h3 = _ple_math(h2, gp_ref, p_ref, wp_ref, wpg_ref)
    h_out_ref[...] = h3
    xn = _rms(h3, gm_ref[...]).astype(BF16)
    for c in range(0, win_ref.shape[1], tn):
        u_ref[:, c:c + tn] = _dot(xn, win_ref[:, c:c + tn]).astype(u_ref.dtype)


def _ff_tile(f, cap):
    best = LANES
    for c in range(LANES, cap + 1, LANES):
        if f % c == 0:
            best = c
    return best


def _ffn_ple_proj(h, g_ffn, wg, wu, wd, g_ple, p, wp, wpg, g_mix, w_in, *, name):
    t, d = h.shape
    f = wg.shape[1]
    e = p.shape[1]
    n = w_in.shape[1]
    tm = min(ROW_TILE, t)
    tn = COL_TILE if n % COL_TILE == 0 else n
    row = lambda w: pl.BlockSpec((tm, w), lambda i: (i, 0))
    return pl.pallas_call(
        functools.partial(_ffn_ple_proj_kernel, tf=_ff_tile(f, 1536), tn=tn), grid=(t // tm,),
        in_specs=[row(d), _resident((1, d)), _resident((d, f)), _resident((d, f)), _resident((f, d)),
                  _resident((1, d)), row(e), _resident((e, d)), _resident((d, d)),
                  _resident((1, d)), _resident((d, n))],
        out_specs=[row(d), row(n)],
        out_shape=[jax.ShapeDtypeStruct((t, d), F32), jax.ShapeDtypeStruct((t, n), BF16)],
        compiler_params=_cparams("parallel"), name=name,
    )(h, g_ffn.reshape(1, d), wg, wu, wd, g_ple.reshape(1, d), p, wp, wpg, g_mix.reshape(1, d), w_in)


def _ple_math(x, g_ref, p_ref, wp_ref, wg_ref):
    xn = _rms(x, g_ref[...]).astype(BF16)
    gate = _sigmoid(_dot(xn, wg_ref[...]))
    emb = _dot(p_ref[...].astype(BF16), wp_ref[...])
    return x + emb * gate


def _unit_lower_inverses(mats):
    n = mats[0].shape[0]
    row = lax.broadcasted_iota(jnp.int32, (n, n), 0)
    col = lax.broadcasted_iota(jnp.int32, (n, n), 1)
    eye = (row == col).astype(F32)
    same16 = (row // 16) == (col // 16)
    same32 = (row // 32) == (col // 32)
    off32 = jnp.logical_and(same32, jnp.logical_not(same16))

    def mm(ps, qs):
        return [_dot(p.astype(BF16), q.astype(BF16)) for p, q in zip(ps, qs)]

    ad = [jnp.where(same16, a, 0.0) for a in mats]
    a2 = mm(ad, ad)
    x = mm([eye - t for t in ad], [eye + t for t in a2])
    a4 = mm(a2, a2)
    x = mm(x, [eye + t for t in a4])
    a8 = mm(a4, a4)
    x = mm(x, [eye + t for t in a8])
    y = mm([jnp.where(off32, a, 0.0) for a in mats], x)
    x = [t - c for t, c in zip(x, mm(x, y))]
    y = mm([jnp.where(same32, 0.0, a) for a in mats], x)
    return [t - c for t, c in zip(x, mm(x, y))]


def _seq_tiles(pad_ref, s):
    rt = min(ELEM_ROWS, s)
    n_tiles = s // rt

    def rows(i):
        return pl.ds(pl.multiple_of(i * rt, rt), rt)

    def fill_pad(fn):
        pad_ref[0:TOP_PAD, :] = jnp.zeros((TOP_PAD, pad_ref.shape[1]), F32)

        def body(i, c):
            r = pl.multiple_of(i * rt, rt)
            pad_ref[pl.ds(TOP_PAD + r, rt), :] = fn(rows(i))
            return c
        lax.fori_loop(0, n_tiles, body, 0)

    def conv_tile(i, w_ref, width):
        tap = _delayed(pad_ref, pl.multiple_of(i * rt, rt), rt, width - 1)
        acc = None
        for j in range(width):
            term = w_ref[j:j + 1, :] * tap(width - 1 - j)
            acc = term if acc is None else acc + term
        return acc

    return n_tiles, rows, fill_pad, conv_tile


def _gated_conv_kernel(bg_ref, cg_ref, xin_ref, w_ref, o_ref, pad_ref):
    n_tiles, rows, fill_pad, conv_tile = _seq_tiles(pad_ref, bg_ref.shape[0])
    fill_pad(lambda rs: cg_ref[rs, :].astype(F32) * xin_ref[rs, :].astype(F32))

    def body(i, c):
        acc = conv_tile(i, w_ref, CONV_A_WIDTH)
        o_ref[rows(i), :] = (bg_ref[rows(i), :].astype(F32) * acc).astype(o_ref.dtype)
        return c
    lax.fori_loop(0, n_tiles, body, 0)


def _gated_conv(u, conv_a, batch, seq):
    cw = conv_a.shape[1]
    return pl.pallas_call(
        _gated_conv_kernel, grid=(batch,),
        in_specs=[pl.BlockSpec((seq, cw), lambda b, j=j: (b, j)) for j in range(3)] + [_resident(conv_a.shape)],
        out_specs=pl.BlockSpec((seq, cw), lambda b: (b, 0)),
        out_shape=jax.ShapeDtypeStruct((batch * seq, cw), BF16),
        scratch_shapes=[pltpu.VMEM((seq + TOP_PAD, cw), F32)],
        compiler_params=_cparams("parallel"), name="l0_gated_conv")(u, u, u, conv_a)


def _l0_mixer_kernel(q_ref, k_ref, v_ref, og_ref, ab_ref, wq_ref, wk_ref, wv_ref, alog_ref, dtb_ref, gn_ref,
                     o_ref,
                     pad_ref, qs_ref, ks_ref, vs_ref, us_ref, ws_ref, qk_ref, egl_ref, st_ref):
    s = q_ref.shape[0]
    heads = GDN_HEADS
    dk = q_ref.shape[1] // heads
    c64 = GDN_CHUNK
    n_chunks = s // c64
    n_tiles, rows, fill_pad, conv_tile = _seq_tiles(pad_ref, s)

    def l2n(x, scale):
        parts = []
        for h in range(heads):
            xh = x[:, h * dk:(h + 1) * dk]
            inv = lax.rsqrt(jnp.sum(xh * xh, axis=-1, keepdims=True) + EPS)
            parts.append(xh * (inv * scale))
        return jnp.concatenate(parts, axis=1)

    for src_ref, w_ref, dst_ref, post in (
            (q_ref, wq_ref, qs_ref, lambda x: l2n(x, dk ** -0.5)),
            (k_ref, wk_ref, ks_ref, lambda x: l2n(x, 1.0)),
            (v_ref, wv_ref, vs_ref, lambda x: x)):
        fill_pad(lambda rs, src_ref=src_ref: src_ref[rs, :].astype(F32))

        def conv_body(i, c, w_ref=w_ref, dst_ref=dst_ref, post=post):
            dst_ref[rows(i), :] = post(_silu(conv_tile(i, w_ref, GDN_CONV_WIDTH)))
            return c
        lax.fori_loop(0, n_tiles, conv_body, 0)

    ri = lax.broadcasted_iota(jnp.int32, (c64, c64), 0)
    ci = lax.broadcasted_iota(jnp.int32, (c64, c64), 1)
    tril = ri >= ci
    strict = ri > ci
    ltri = tril.astype(F32)

    group = PREP_CHUNKS if n_chunks % PREP_CHUNKS == 0 else 1

    def chunk_prep(cg, carry):
        chunks = []
        for cc in range(group):
            c = cg * group + cc
            rs = pl.ds(pl.multiple_of(c * c64, c64), c64)
            chunks.append((c, rs, ab_ref[rs, :], qs_ref[rs, :], ks_ref[rs, :], vs_ref[rs, :]))
        inst = []
        gcs = [_dot(ltri, -jnp.exp(alog_ref[...]) * _softplus(ab + dtb_ref[...]), precision=HIGHEST)
               for _, _, ab, _, _, _ in chunks]
        for (c, rs, ab, q_all, k_all, v_all), gc in zip(chunks, gcs):
            beta = _sigmoid(ab)
            gct = gc.T
            for h in range(heads):
                hs = slice(h * dk, (h + 1) * dk)
                gcol = gc[:, h:h + 1]
                glast = gc[c64 - 1:c64, h:h + 1]
                bcol = beta[:, heads + h:heads + h + 1]
                decay = jnp.where(tril, jnp.exp(jnp.where(tril, gcol - gct[h:h + 1, :], 0.0)), 0.0)
                kh, qh, vh = k_all[:, hs], q_all[:, hs], v_all[:, hs]
                kb = kh * bcol
                egc = jnp.exp(gcol)
                inst.append(dict(decay=decay, kh=kh, qh=qh, kb=kb, egc=egc,
                                 rhs=jnp.concatenate([vh * bcol, kb * egc], axis=1).astype(BF16),
                                 kd=kh * jnp.exp(glast - gcol),
                                 eg=jnp.broadcast_to(jnp.exp(glast), (SUBLANES, dk))))
        kqs = [_dot_nt(jnp.concatenate([t["kb"], t["qh"]], axis=0).astype(BF16), t["kh"].astype(BF16))
               for t in inst]
        minvs = _unit_lower_inverses([jnp.where(strict, kq[0:c64] * t["decay"], 0.0)
                                      for kq, t in zip(kqs, inst)])
        uws = [_dot(m.astype(BF16), t["rhs"]) for m, t in zip(minvs, inst)]
        for ci, (c, rs, _, _, _, _) in enumerate(chunks):
            sl = slice(ci * heads, (ci + 1) * heads)
            us_ref[rs, :] = jnp.concatenate([uw[:, 0:dk] for uw in uws[sl]], axis=1)
            ws_ref[rs, :] = jnp.concatenate([uw[:, dk:2 * dk] for uw in uws[sl]], axis=1).astype(BF16)
            qs_ref[rs, :] = jnp.concatenate([t["qh"] * t["egc"] for t in inst[sl]], axis=1)
            ks_ref[rs, :] = jnp.concatenate([t["kd"] for t in inst[sl]], axis=1)
            qk_ref[rs, :] = jnp.concatenate([kq[c64:2 * c64] * t["decay"]
                                             for kq, t in zip(kqs[sl], inst[sl])], axis=1).astype(BF16)
            e0 = pl.multiple_of(c * (heads * SUBLANES), heads * SUBLANES)
            egl_ref[pl.ds(e0, heads * SUBLANES), :] = jnp.concatenate([t["eg"] for t in inst[sl]], axis=0)
        return carry
    lax.fori_loop(0, n_chunks // group, chunk_prep, 0)

    st_ref[...] = jnp.zeros(st_ref.shape, F32)

    def scan(c, carry):
        rs = pl.ds(pl.multiple_of(c * c64, c64), c64)
        e0 = pl.multiple_of(c * (heads * SUBLANES), heads * SUBLANES)
        w_all, qg_all, u_all, kd_all = ws_ref[rs, :], qs_ref[rs, :], us_ref[rs, :], ks_ref[rs, :]
        qk_all = qk_ref[rs, :]
        eg_all = egl_ref[pl.ds(e0, heads * SUBLANES), :]
        hsl = [slice(h * dk, (h + 1) * dk) for h in range(heads)]
        sts = [st_ref[h] for h in range(heads)]
        wqs = [_dot(jnp.concatenate([w_all[:, hs], qg_all[:, hs].astype(BF16)], axis=0), st.astype(BF16))
               for hs, st in zip(hsl, sts)]
        vbs = [(u_all[:, hs] - wq[0:c64]).astype(BF16) for hs, wq in zip(hsl, wqs)]
        upd = [_dot_tn(kd_all[:, hs].astype(BF16), vb) for hs, vb in zip(hsl, vbs)]
        intra = [_dot(qk_all[:, h * c64:(h + 1) * c64], vbs[h]) for h in range(heads)]
        for h in range(heads):
            st_ref[h] = sts[h] * eg_all[h * SUBLANES:h * SUBLANES + 1, :] + upd[h]
        vs_ref[rs, :] = jnp.concatenate([wq[c64:2 * c64] + o for wq, o in zip(wqs, intra)], axis=1)
        return carry
    lax.fori_loop(0, n_chunks, scan, 0)

    def finish(i, c):
        o = vs_ref[rows(i), :]
        og = og_ref[rows(i), :].astype(F32)
        parts = []
        for h in range(heads):
            oh = o[:, h * dk:(h + 1) * dk]
            parts.append(_rms(oh, gn_ref[...]))
        y = jnp.concatenate(parts, axis=1) * _silu(og)
        o_ref[rows(i), :] = y.astype(o_ref.dtype)
        return c
    lax.fori_loop(0, n_tiles, finish, 0)


def _l0_mixer(u, ab, gdn_conv, a_log, dt_bias, gdn_norm_g, batch, seq):
    dk = gdn_norm_g.shape[0]
    heads = GDN_HEADS
    cw = heads * dk

    def lane_row(x):
        return jnp.zeros((1, LANES), F32).at[0, :x.shape[0]].set(x)

    in_specs = [pl.BlockSpec((seq, cw), lambda b, j=j: (b, j)) for j in range(3, 7)]
    in_specs += [pl.BlockSpec((seq, LANES), lambda b: (b, 0))]
    in_specs += [pl.BlockSpec((GDN_CONV_WIDTH, cw), lambda b, j=j: (0, j), pipeline_mode=pl.Buffered(1))
                 for j in range(3)]
    in_specs += [_resident((1, LANES)), _resident((1, LANES)), _resident((1, dk))]
    big = pltpu.VMEM((seq, cw), F32)
    scratch = [pltpu.VMEM((seq + TOP_PAD, cw), F32), big, big, big, big, pltpu.VMEM((seq, cw), BF16),
               pltpu.VMEM((seq, heads * GDN_CHUNK), BF16),
               pltpu.VMEM((seq // GDN_CHUNK * heads * SUBLANES, dk), F32),
               pltpu.VMEM((heads, dk, dk), F32)]
    return pl.pallas_call(
        _l0_mixer_kernel, grid=(batch,), in_specs=in_specs,
        out_specs=pl.BlockSpec((seq, cw), lambda b: (b, 0)),
        out_shape=jax.ShapeDtypeStruct((batch * seq, cw), BF16),
        scratch_shapes=scratch, compiler_params=_cparams("parallel"), name="l0_deltanet",
    )(u, u, u, u, ab, gdn_conv, gdn_conv, gdn_conv,
      lane_row(a_log), lane_row(dt_bias), gdn_norm_g.reshape(1, dk))


def _bias_table_kernel(tab_ref, o_ref, *, t):
    h = pl.program_id(0)
    m = pl.program_id(1)
    ri = lax.broadcasted_iota(jnp.int32, (t, t), 0)
    ci = lax.broadcasted_iota(jnp.int32, (t, t), 1)
    rel = m * t + ci - ri
    n = jnp.maximum(rel, 0)
    max_exact = NUM_BUCKETS // 2
    nf = jnp.maximum(n, 1).astype(F32)
    large = max_exact + (jnp.log(nf / max_exact) / math.log(MAX_DISTANCE / max_exact)
                         * (NUM_BUCKETS - max_exact)).astype(jnp.int32)
    large = jnp.minimum(large, NUM_BUCKETS - 1)
    bucket = jnp.where(n < max_exact, n, large)
    bias = jnp.zeros((t, t), F32)
    for b in range(NUM_BUCKETS):
        bias = jnp.where(bucket == b, tab_ref[b, h], bias)
    o_ref[...] = jnp.where(rel >= 0, bias * LOG2E, MASK_VALUE)


def _bias_table(rel_bias, seq, t):
    nb = seq // t
    heads = rel_bias.shape[1]
    return pl.pallas_call(
        functools.partial(_bias_table_kernel, t=t), grid=(heads, nb),
        in_specs=[pl.BlockSpec(memory_space=pltpu.SMEM)],
        out_specs=pl.BlockSpec((None, None, t, t), lambda h, m: (h, m, 0, 0)),
        out_shape=jax.ShapeDtypeStruct((heads, nb, t, t), F32),
        compiler_params=_cparams("parallel", "parallel"), name="rel_bias_table")(rel_bias)


def _attn_kernel(q_ref, k_ref, v_ref, tb_ref, lam_ref, gn_ref, o_ref, m_ref, l_ref, acc_ref,
                 *, t, heads, lambda_init):
    qi = pl.program_id(2)
    dh2 = q_ref.shape[1] // heads
    dh = dh2 // 2
    lane = lax.broadcasted_iota(jnp.int32, (t, dh2), 1)
    qqs = []
    for h in range(heads):
        qf = q_ref[:, h * dh2:(h + 1) * dh2].astype(F32) * (dh ** -0.5 * LOG2E)
        qqs.append(jnp.concatenate([jnp.where(lane < dh, qf, 0.0), jnp.where(lane >= dh, qf, 0.0)],
                                   axis=0).astype(BF16))

    hs = [slice(h * dh2, (h + 1) * dh2) for h in range(heads)]

    def update(j0, nblk):
        ks = pl.ds(pl.multiple_of(j0 * t, t), nblk * t)
        k_all = k_ref[ks, :]
        v_all = v_ref[ks, :]
        s_t = [_dot_nt(k_all[:, hs[h]], qqs[h]) for h in range(heads)]
        ps, alphas = [], []
        for h in range(heads):
            m = m_ref[h, 0:1, :]
            b = jnp.concatenate([tb_ref[h, qi - j0 - i] for i in range(nblk)], axis=0)
            s = jnp.concatenate([s_t[h][:, 0:t] + b, s_t[h][:, t:2 * t] + b], axis=1)
            m_new = jnp.maximum(m, jnp.max(s, axis=0, keepdims=True))
            alpha = jnp.exp2(m - m_new)
            p = jnp.exp2(s - m_new)
            m_ref[h] = jnp.broadcast_to(m_new, m_ref.shape[1:])
            l_ref[h] = jnp.broadcast_to(alpha * l_ref[h, 0:1, :] + jnp.sum(p, axis=0, keepdims=True),
                                        l_ref.shape[1:])
            alphas.append(alpha)
            ps.append(p.astype(BF16))
        pv = [_dot_tn(v_all[:, hs[h]], ps[h]) for h in range(heads)]
        for h in range(heads):
            acc_ref[h] = alphas[h] * acc_ref[h] + pv[h]

    m_ref[...] = jnp.full(m_ref.shape, MASK_VALUE, F32)
    l_ref[...] = jnp.zeros(l_ref.shape, F32)
    acc_ref[...] = jnp.zeros(acc_ref.shape, F32)
    odd = (qi + 1) % 2

    @pl.when(odd == 1)
    def _():
        update(0, 1)

    def pair(i, c):
        update(odd + 2 * i, 2)
        return c
    lax.fori_loop(0, (qi + 1) // 2, pair, 0)
    lp = lam_ref[...]
    lam = (jnp.exp(jnp.sum(lp[0:1] * lp[1:2], axis=-1, keepdims=True))
           - jnp.exp(jnp.sum(lp[2:3] * lp[3:4], axis=-1, keepdims=True)) + lambda_init)
    for h in range(heads):
        o12 = acc_ref[h] / l_ref[h, 0:1, :]
        o = (o12[:, 0:t] - lam * o12[:, t:2 * t]).T
        o_ref[:, h * dh2:(h + 1) * dh2] = (_rms(o, gn_ref[...]) * (1.0 - lambda_init)).astype(o_ref.dtype)


def _diff_attention(u, table, lam_params, norm_g, batch, seq, lambda_init):
    heads = DIFF_HEADS
    dh2 = norm_g.shape[0]
    hg = ATTN_HEADS_PER_STEP
    w = hg * dh2
    ng = heads // hg
    t = table.shape[2]
    nq = seq // t
    return pl.pallas_call(
        functools.partial(_attn_kernel, t=t, heads=hg, lambda_init=lambda_init),
        grid=(batch, ng, nq),
        in_specs=[pl.BlockSpec((t, w), lambda b, g, i: (b * nq + i, g)),
                  pl.BlockSpec((seq, w), lambda b, g, i: (b, ng + g)),
                  pl.BlockSpec((seq, w), lambda b, g, i: (b, 2 * ng + g)),
                  pl.BlockSpec((hg, nq, t, t), lambda b, g, i: (g, 0, 0, 0)),
                  _resident(lam_params.shape), _resident((1, dh2))],
        out_specs=pl.BlockSpec((t, w), lambda b, g, i: (b * nq + i, g)),
        out_shape=jax.ShapeDtypeStruct((batch * seq, heads * dh2), BF16),
        scratch_shapes=[pltpu.VMEM((hg, SUBLANES, 2 * t), F32), pltpu.VMEM((hg, SUBLANES, 2 * t), F32),
                        pltpu.VMEM((hg, dh2, 2 * t), F32)],
        compiler_params=_cparams("parallel", "parallel", "arbitrary"), name="diff_attention",
    )(u, u, u, table, lam_params, norm_g.reshape(1, dh2))


def _conformer_kernel(ga_ref, gb_ref, w_ref, b_ref, lg_ref, lb_ref, o_ref, pad_ref):
    s, c = ga_ref.shape
    rt = min(ELEM_ROWS, s)
    pad_ref[0:TOP_PAD, :] = jnp.zeros((TOP_PAD, c), F32)

    def glu(i, carry):
        r = pl.multiple_of(i * rt, rt)
        rs = pl.ds(r, rt)
        pad_ref[pl.ds(TOP_PAD + r, rt), :] = ga_ref[rs, :].astype(F32) * _sigmoid(gb_ref[rs, :].astype(F32))
        return carry
    lax.fori_loop(0, s // rt, glu, 0)

    ct = CONF_ROWS

    def conv(i, carry):
        r = pl.multiple_of(i * ct, ct)
        tap = _delayed(pad_ref, r, ct, CONF_WIDTH - 1)
        acc = jnp.broadcast_to(b_ref[...], (ct, c))
        for j in range(CONF_WIDTH):
            acc = acc + w_ref[j:j + 1, :] * tap(CONF_WIDTH - 1 - j)
        mu = jnp.mean(acc, axis=-1, keepdims=True)
        xc = acc - mu
        var = jnp.mean(xc * xc, axis=-1, keepdims=True)
        y = xc * lax.rsqrt(var + EPS) * lg_ref[...] + lb_ref[...]
        o_ref[pl.ds(r, ct), :] = _silu(y).astype(o_ref.dtype)
        return carry
    lax.fori_loop(0, s // ct, conv, 0)


def _conformer(u, w, b, ln_g, ln_b, batch, seq, col0):
    c = w.shape[1]
    return pl.pallas_call(
        _conformer_kernel, grid=(batch,),
        in_specs=[pl.BlockSpec((seq, c), lambda i: (i, col0)),
                  pl.BlockSpec((seq, c), lambda i: (i, col0 + 1)),
                  _resident(w.shape), _resident((1, c)), _resident((1, c)), _resident((1, c))],
        out_specs=pl.BlockSpec((seq, c), lambda i: (i, 0)),
        out_shape=jax.ShapeDtypeStruct((batch * seq, c), BF16),
        scratch_shapes=[pltpu.VMEM((seq + TOP_PAD, c), F32)],
        compiler_params=_cparams("parallel"), name="conformer_conv",
    )(u, u, w, b.reshape(1, c), ln_g.reshape(1, c), ln_b.reshape(1, c))


def _proj_router_kernel(y1_ref, y2_ref, w1_ref, w2_ref, h_ref, g_ref, wr_ref,
                        h_out_ref, hn_ref, idx_ref, gate_ref):
    h = h_ref[...] + _dot(y1_ref[...], w1_ref[...]) + _dot(y2_ref[...], w2_ref[...])
    h_out_ref[...] = h
    xn = _rms(h, g_ref[...])
    _store_token_tiles(hn_ref, 0, xn)
    logits = _dot(xn.astype(BF16), wr_ref[...])
    lane = lax.broadcasted_iota(jnp.int32, logits.shape, 1)
    neg = -jnp.inf
    lg = jnp.where(lane < N_EXPERTS, logits, neg)
    m1 = jnp.max(lg, axis=-1, keepdims=True)
    i1 = jnp.min(jnp.where(lg == m1, lane, LANES), axis=-1, keepdims=True)
    lg2 = jnp.where(lane == i1, neg, lg)
    m2 = jnp.max(lg2, axis=-1, keepdims=True)
    i2 = jnp.min(jnp.where(lg2 == m2, lane, LANES), axis=-1, keepdims=True)
    e = jnp.exp(m2 - m1)
    g1 = 1.0 / (1.0 + e)
    idx_ref[...] = jnp.where(lane == 0, i1, jnp.where(lane == 1, i2, 0))
    gate_ref[...] = jnp.where(lane == 0, g1, jnp.where(lane == 1, e * g1, 0.0))


def _proj_router(y1, y2, w, h, g, wr):
    t, d = h.shape
    k1, k2 = y1.shape[1], y2.shape[1]
    tm = min(ROW_TILE, t)
    row = lambda width: pl.BlockSpec((tm, width), lambda i: (i, 0))
    return pl.pallas_call(
        _proj_router_kernel, grid=(t // tm,),
        in_specs=[row(k1), row(k2), _resident((k1, d)), _resident((k2, d)), row(d),
                  _resident((1, d)), _resident((d, LANES))],
        out_specs=[row(d), pl.BlockSpec((tm * d // LANES, LANES), lambda i: (i, 0)), row(LANES), row(LANES)],
        out_shape=[jax.ShapeDtypeStruct((t, d), F32), jax.ShapeDtypeStruct((t * d // LANES, LANES), F32),
                   jax.ShapeDtypeStruct((t, LANES), jnp.int32), jax.ShapeDtypeStruct((t, LANES), F32)],
        compiler_params=_cparams("parallel"), name="l1_out_proj_router",
    )(y1, y2, w[:k1], w[k1:], h, g.reshape(1, d), wr)


def _experts_kernel(blk_e_ref, tok_hbm, x_hbm, wgu_ref, wd_ref, o_ref,
                    xbuf, hid_ref, tok0, tok1, sem_tok, sem_rows):
    i = pl.program_id(0)
    nb = pl.num_programs(0)
    f = wd_ref.shape[0]
    tm = xbuf.shape[0] // (2 * SUBLANES)
    slot = i % 2
    toks = (tok0, tok1)

    def tok_copy(blk, s):
        return pltpu.make_async_copy(tok_hbm.at[blk], toks[s], sem_tok.at[s])

    def issue_tiles(s):
        def body(r, c):
            _token_tile_copy(x_hbm, toks[s][r], xbuf, s * tm + r, sem_rows.at[s]).start()
            return c
        lax.fori_loop(0, tm, body, 0, unroll=16)

    @pl.when(i == 0)
    def _():
        tok_copy(0, 0).start()
        tok_copy(0, 0).wait()
        tok_copy(1, 1).start()
        issue_tiles(0)

    for s in range(2):
        @pl.when(slot == s)
        def _(s=s):
            tok_copy(i + 1, 1 - s).wait()
            _token_tiles_wait(x_hbm, xbuf, s * tm, tm, sem_rows.at[s])
            issue_tiles(1 - s)

            @pl.when(i + 2 <= nb)
            def _():
                tok_copy(i + 2, s).start()

    x = _load_token_tiles(xbuf, slot * tm, tm).astype(BF16)
    for c in range(0, f, MOE_SUB_TILE):
        hid_ref[:, c:c + MOE_SUB_TILE] = (
            _silu(_dot(x, wgu_ref[:, c:c + MOE_SUB_TILE]))
            * _dot(x, wgu_ref[:, f + c:f + c + MOE_SUB_TILE])).astype(BF16)
    _store_token_tiles(o_ref, 0, _dot(hid_ref[...], wd_ref[...]))

    @pl.when(i == nb - 1)
    def _():
        _token_tiles_wait(x_hbm, xbuf, (1 - slot) * tm, tm, sem_rows.at[1 - slot])


def _experts(x_tiles, row_tok, blk_e, w_gate_up, w_down):
    f, d = w_down.shape[1], w_down.shape[2]
    tm = MOE_ROWS
    nb = row_tok.shape[0] // tm - 1
    tpt = d // LANES
    assert tpt == SUBLANES and f % MOE_SUB_TILE == 0
    grid_spec = pltpu.PrefetchScalarGridSpec(
        num_scalar_prefetch=1, grid=(nb,),
        in_specs=[pl.BlockSpec(memory_space=pl.ANY), pl.BlockSpec(memory_space=pl.ANY),
                  pl.BlockSpec((None, d, 2 * f), lambda i, e: (e[i], 0, 0), pipeline_mode=pl.Buffered(1)),
                  pl.BlockSpec((None, f, d), lambda i, e: (e[i], 0, 0), pipeline_mode=pl.Buffered(1))],
        out_specs=pl.BlockSpec((tm * tpt, LANES), lambda i, e: (i, 0)),
        scratch_shapes=[pltpu.VMEM((2 * tm * tpt, LANES), F32), pltpu.VMEM((tm, f), BF16),
                        pltpu.SMEM((tm,), jnp.int32), pltpu.SMEM((tm,), jnp.int32),
                        pltpu.SemaphoreType.DMA((2,)), pltpu.SemaphoreType.DMA((2,))])
    return pl.pallas_call(
        _experts_kernel, grid_spec=grid_spec,
        out_shape=jax.ShapeDtypeStruct((nb * tm * tpt, LANES), F32),
        compiler_params=_cparams("arbitrary"), name="moe_experts",
    )(blk_e, row_tok.reshape(nb + 1, tm), x_tiles, w_gate_up, w_down)


def _combine_kernel(dest_hbm, y_hbm, gate_ref, h_ref, g_ref, p_ref, wp_ref, wg_ref, fg_ref, o_ref,
                    idx0, idx1, ybuf, sem_idx, sem_rows):
    i = pl.program_id(0)
    steps = pl.num_programs(0)
    n = h_ref.shape[0]
    m = TOP_K * n
    slot = i % 2
    idxs = (idx0, idx1)

    def idx_copy(blk, s):
        return pltpu.make_async_copy(dest_hbm.at[blk], idxs[s], sem_idx.at[s])

    def issue_tiles(s):
        def body(r, c):
            _token_tile_copy(y_hbm, idxs[s][r], ybuf, s * m + r, sem_rows.at[s]).start()
            return c
        lax.fori_loop(0, m, body, 0, unroll=16)

    @pl.when(i == 0)
    def _():
        idx_copy(0, 0).start()
        idx_copy(0, 0).wait()
        idx_copy(1, 1).start()
        issue_tiles(0)

    for s in range(2):
        @pl.when(slot == s)
        def _(s=s):
            idx_copy(i + 1, 1 - s).wait()

            @pl.when(i + 2 <= steps)
            def _():
                idx_copy(i + 2, s).start()

            issue_tiles(1 - s)
            _token_tiles_wait(y_hbm, ybuf, s * m, m, sem_rows.at[s])

    gates = gate_ref[...]
    moe = (gates[:, 0:1] * _load_token_tiles(ybuf, slot * m, n)
           + gates[:, 1:2] * _load_token_tiles(ybuf, slot * m + n, n))
    x = _ple_math(h_ref[...] + moe, g_ref, p_ref, wp_ref, wg_ref)
    o_ref[...] = _rms(x, fg_ref[...])

    @pl.when(i == steps - 1)
    def _():
        _token_tiles_wait(y_hbm, ybuf, (1 - slot) * m, m, sem_rows.at[1 - slot])


def _combine_ple_final(dest, y_tiles, gates, h, g, p, wp, wg, final_g):
    t, d = h.shape
    e = p.shape[1]
    n = min(COMBINE_ROWS, t)
    m = TOP_K * n
    steps = t // n
    dest_steps = jnp.concatenate([dest[k].reshape(steps, n) for k in range(TOP_K)], axis=1)
    dest_steps = jnp.concatenate([dest_steps, jnp.zeros((1, m), jnp.int32)], axis=0)
    return pl.pallas_call(
        _combine_kernel, grid=(steps,),
        in_specs=[pl.BlockSpec(memory_space=pl.ANY), pl.BlockSpec(memory_space=pl.ANY),
                  pl.BlockSpec((n, LANES), lambda i: (i, 0)),
                  pl.BlockSpec((n, d), lambda i: (i, 0)), _resident((1, d)),
                  pl.BlockSpec((n, e), lambda i: (i, 0)), _resident((e, d)), _resident((d, d)),
                  _resident((1, d))],
        out_specs=pl.BlockSpec((n, d), lambda i: (i, 0)),
        out_shape=jax.ShapeDtypeStruct((t, d), F32),
        scratch_shapes=[pltpu.SMEM((m,), jnp.int32), pltpu.SMEM((m,), jnp.int32),
                        pltpu.VMEM((2 * m * d // LANES, LANES), F32),
                        pltpu.SemaphoreType.DMA((2,)), pltpu.SemaphoreType.DMA((2,))],
        compiler_params=_cparams("arbitrary"), name="moe_combine_ple_final",
    )(dest_steps, y_tiles, gates, h, g.reshape(1, d), p, wp, wg, final_g.reshape(1, d))


def _dispatch_plan(first, second, t):
    experts = jnp.arange(N_EXPERTS, dtype=jnp.int32)[:, None]
    m1 = (first[None, :] == experts).astype(jnp.int32)
    m2 = (second[None, :] == experts).astype(jnp.int32)
    both = m1 + m2
    csum = jnp.cumsum(both, axis=1)
    before = csum - both
    counts = csum[:, -1]
    padded = (counts + MOE_ROWS - 1) // MOE_ROWS * MOE_ROWS
    pend = jnp.cumsum(padded)
    base = (pend - padded)[:, None] + before
    dest = jnp.stack([jnp.sum(m1 * base, axis=0), jnp.sum(m2 * base, axis=0)])
    p = t * TOP_K + N_EXPERTS * MOE_ROWS
    tok = jnp.arange(t, dtype=jnp.int32)
    row_tok = jnp.zeros((p + MOE_ROWS,), jnp.int32).at[dest.reshape(-1)].set(
        jnp.concatenate([tok, tok]), unique_indices=True, mode="promise_in_bounds")
    nb = p // MOE_ROWS
    blk_e = jnp.minimum(jnp.searchsorted(pend, jnp.arange(nb, dtype=jnp.int32) * MOE_ROWS, side='right'),
                        N_EXPERTS - 1).astype(jnp.int32)
    return dest, row_tok, blk_e


def kernel(x, p, norm_mix_g, norm_ffn_g, norm_ple_g, final_norm_g, ev_w_in, ev_conv_a, ev_gdn_conv, ev_gdn_A_log, ev_gdn_dt_bias, ev_gdn_norm_g, ev_w_out, od_w_in, od_lambda, od_diff_norm_g, od_conf_dw_w, od_conf_dw_b, od_conf_ln_g, od_conf_ln_b, od_w_out, rel_bias, ffn_w_gate_up, ffn_w_down, moe_router, moe_w_gate_up, moe_w_down, ple_w_proj, ple_w_gate):
    batch, seq, d = x.shape
    t = batch * seq
    depth = p.shape[0]
    assert depth == 2 and seq % GDN_CHUNK == 0
    h = x.reshape(t, d)
    pf = p.reshape(depth, t, p.shape[-1])

    heads = GDN_HEADS
    n_main = ev_w_in.shape[2] - 2 * heads
    w_in = ev_w_in[0]
    w_ab = jnp.zeros((d, LANES), BF16).at[:, :2 * heads].set(w_in[:, n_main:].astype(BF16))
    u, ab = _norm_proj(h, norm_mix_g[0], w_in[:, :n_main].astype(BF16), w_ab, name="l0_in_proj")
    ya = _gated_conv(u, ev_conv_a[0], batch, seq)
    yb = _l0_mixer(u, ab, ev_gdn_conv[0], ev_gdn_A_log[0], ev_gdn_dt_bias[0], ev_gdn_norm_g[0], batch, seq)
    h = _proj_residual([ya, yb], ev_w_out[0].astype(BF16), h, name="l0_out_proj")
    f = ffn_w_down.shape[1]
    lambda_init = 0.8 - 0.6 * math.exp(-0.3 * 1)
    h, u = _ffn_ple_proj(h, norm_ffn_g[0], ffn_w_gate_up[0, :, :f].astype(BF16),
                         ffn_w_gate_up[0, :, f:].astype(BF16), ffn_w_down[0].astype(BF16),
                         norm_ple_g[0], pf[0], ple_w_proj[0].astype(BF16), ple_w_gate[0].astype(BF16),
                         norm_mix_g[1], od_w_in[0].astype(BF16), name="l0_ffn_ple_l1_in_proj")

    table = _bias_table(rel_bias, seq, min(ATTN_TILE, seq))
    o_attn = _diff_attention(u, table, od_lambda[0], od_diff_norm_g[0], batch, seq, lambda_init)
    c_conf = od_conf_dw_w.shape[2]
    o_conf = _conformer(u, od_conf_dw_w[0], od_conf_dw_b[0], od_conf_ln_g[0], od_conf_ln_b[0],
                        batch, seq, 3 * DIFF_HEADS * od_diff_norm_g.shape[1] // c_conf)
    wr = jnp.zeros((d, LANES), BF16).at[:, :N_EXPERTS].set(moe_router[0].astype(BF16))
    h, hn, idx, gates = _proj_router(o_attn, o_conf, od_w_out[0].astype(BF16), h, norm_ffn_g[1], wr)
    dest, row_tok, blk_e = _dispatch_plan(idx[:, 0], idx[:, 1], t)
    y = _experts(hn, row_tok, blk_e, moe_w_gate_up[0].astype(BF16), moe_w_down[0].astype(BF16))
    out = _combine_ple_final(dest, y, gates, h, norm_ple_g[1], pf[1], ple_w_proj[1].astype(BF16),
                             ple_w_gate[1].astype(BF16), final_norm_g)
    return out.reshape(batch, seq, d)
```

```python
import functools
import math

import jax
import jax.numpy as jnp
from jax import lax
from jax.experimental import pallas as pl
from jax.experimental.pallas import tpu as pltpu

F32 = jnp.float32
BF16 = jnp.bfloat16
HIGHEST = lax.Precision.HIGHEST

EPS = 1e-6
CONV_A_WIDTH = 3
GDN_HEADS = 4
GDN_CONV_WIDTH = 4
GDN_CHUNK = 64
DIFF_HEADS = 4
NUM_BUCKETS = 32
MAX_DISTANCE = 128
CONF_WIDTH = 31
N_EXPERTS = 8
TOP_K = 2

LANES = 128
SUBLANES = 8
VMEM_LIMIT_BYTES = 56 * 1024 * 1024
MASK_VALUE = -1e30
LOG2E = math.log2(math.e)

ROW_TILE = 512
COL_TILE = 512
ELEM_ROWS = 256
CONF_ROWS = 256
PREP_CHUNKS = 8
ATTN_TILE = 256
ATTN_HEADS_PER_STEP = 4
MOE_ROWS = 1024
MOE_SUB_TILE = 256
COMBINE_ROWS = 256
TOP_PAD = 32


def _cparams(*sem):
    return pltpu.CompilerParams(dimension_semantics=sem, vmem_limit_bytes=VMEM_LIMIT_BYTES)


def _resident(shape):
    nd = len(shape)
    return pl.BlockSpec(shape, lambda *_: (0,) * nd, pipeline_mode=pl.Buffered(1))


def _rms(x, g):
    return x * lax.rsqrt(jnp.mean(x * x, axis=-1, keepdims=True) + EPS) * g


def _sigmoid(x):
    return jax.nn.sigmoid(x)


def _silu(x):
    return x * jax.nn.sigmoid(x)


def _softplus(x):
    return jnp.maximum(x, 0.0) + jnp.log1p(jnp.exp(-jnp.abs(x)))


def _dot(a, b, **kw):
    return jnp.dot(a, b, preferred_element_type=F32, **kw)


def _dot_nt(a, b):
    return lax.dot_general(a, b, (((1,), (1,)), ((), ())), preferred_element_type=F32)


def _dot_tn(a, b):
    return lax.dot_general(a, b, (((0,), (0,)), ((), ())), preferred_element_type=F32)


def _delayed(pad_ref, r, rows, max_delay):
    lead = -(-max_delay // SUBLANES) * SUBLANES
    win = pad_ref[pl.ds(TOP_PAD + r - lead, rows + lead), :]
    rolled = {0: win}

    def tap(d):
        a, b = divmod(d, SUBLANES)
        if b not in rolled:
            rolled[b] = pltpu.roll(win, b, 0)
        start = lead - SUBLANES * a
        return rolled[b][start:start + rows, :]
    return tap


def _load_token_tiles(ref, first_token, n):
    return jnp.concatenate([ref[pl.ds(first_token * SUBLANES + s, n, stride=SUBLANES), :]
                            for s in range(SUBLANES)], axis=1)


def _store_token_tiles(ref, first_token, x):
    n = x.shape[0]
    for s in range(SUBLANES):
        ref[pl.ds(first_token * SUBLANES + s, n, stride=SUBLANES), :] = x[:, s * LANES:(s + 1) * LANES]


def _token_tile_copy(src_hbm, src_token, dst_vmem, dst_token, sem):
    return pltpu.make_async_copy(src_hbm.at[pl.ds(pl.multiple_of(src_token * SUBLANES, SUBLANES), SUBLANES)],
                                 dst_vmem.at[pl.ds(pl.multiple_of(dst_token * SUBLANES, SUBLANES), SUBLANES)],
                                 sem)


def _token_tiles_wait(src_hbm, dst_vmem, first_token, n, sem):
    pltpu.make_async_copy(src_hbm.at[pl.ds(0, n * SUBLANES)],
                          dst_vmem.at[pl.ds(pl.multiple_of(first_token * SUBLANES, SUBLANES), n * SUBLANES)],
                          sem).wait()


def _norm_proj_kernel(h_ref, g_ref, w_ref, *rest, tn, with_aux):
    xn = _rms(h_ref[...], g_ref[...]).astype(BF16)
    if with_aux:
        w2_ref, o_ref, o2_ref = rest
        o2_ref[...] = _dot(xn, w2_ref[...])
    else:
        (o_ref,) = rest
    n = w_ref.shape[1]
    for c in range(0, n, tn):
        o_ref[:, c:c + tn] = _dot(xn, w_ref[:, c:c + tn]).astype(o_ref.dtype)


def _norm_proj(h, g, w, w_aux=None, *, name):
    t, d = h.shape
    n = w.shape[1]
    tm = min(ROW_TILE, t)
    tn = COL_TILE if n % COL_TILE == 0 else n
    in_specs = [pl.BlockSpec((tm, d), lambda i: (i, 0)), _resident((1, d)), _resident((d, n))]
    out_shape = [jax.ShapeDtypeStruct((t, n), BF16)]
    out_specs = [pl.BlockSpec((tm, n), lambda i: (i, 0))]
    args = [h, g.reshape(1, d), w]
    if w_aux is not None:
        in_specs.append(_resident(w_aux.shape))
        out_shape.append(jax.ShapeDtypeStruct((t, w_aux.shape[1]), F32))
        out_specs.append(pl.BlockSpec((tm, w_aux.shape[1]), lambda i: (i, 0)))
        args.append(w_aux)
    out = pl.pallas_call(
        functools.partial(_norm_proj_kernel, tn=tn, with_aux=w_aux is not None),
        grid=(t // tm,), in_specs=in_specs, out_specs=out_specs, out_shape=out_shape,
        compiler_params=_cparams("parallel"), name=name)(*args)
    return out if w_aux is not None else out[0]


def _proj_residual_kernel(*refs):
    n = (len(refs) - 2) // 2
    y_refs, w_refs, h_ref, o_ref = refs[:n], refs[n:2 * n], refs[2 * n], refs[2 * n + 1]
    acc = h_ref[...]
    for y_ref, w_ref in zip(y_refs, w_refs):
        acc = acc + _dot(y_ref[...], w_ref[...])
    o_ref[...] = acc


def _proj_residual(ys, w, h, *, name):
    t, d = h.shape
    tm = min(ROW_TILE, t)
    ws, r0 = [], 0
    for y in ys:
        ws.append(w[r0:r0 + y.shape[1]])
        r0 += y.shape[1]
    return pl.pallas_call(
        _proj_residual_kernel, grid=(t // tm,),
        in_specs=([pl.BlockSpec((tm, y.shape[1]), lambda i: (i, 0)) for y in ys]
                  + [_resident(wi.shape) for wi in ws] + [pl.BlockSpec((tm, d), lambda i: (i, 0))]),
        out_specs=pl.BlockSpec((tm, d), lambda i: (i, 0)),
        out_shape=jax.ShapeDtypeStruct((t, d), F32),
        compiler_params=_cparams("parallel"), name=name)(*ys, *ws, h)


def _swiglu_residual(x, g_ref, wg_ref, wu_ref, wd_ref, tf):
    xn = _rms(x, g_ref[...]).astype(BF16)
    acc = x
    for c in range(0, wg_ref.shape[1], tf):
        gate = _dot(xn, wg_ref[:, c:c + tf])
        up = _dot(xn, wu_ref[:, c:c + tf])
        hid = (_silu(gate) * up).astype(BF16)
        acc = acc + _dot(hid, wd_ref[c:c + tf, :])
    return acc


def _ffn_ple_proj_kernel(h_ref, gf_ref, wg_ref, wu_ref, wd_ref, gp_ref, p_ref, wp_ref, wpg_ref,
                         gm_ref, win_ref, h_out_ref, u_ref, *, tf, tn):
    h2 = _swiglu_residual(h_ref[...], gf_ref, wg_ref, wu_ref, wd_ref, tf)
    h3 = _ple_math(h2, gp_ref, p_ref, wp_ref, wpg_ref)
    h_out_ref[...] = h3
    xn = _rms(h3, gm_ref[...]).astype(BF16)
    for c in range(0, win_ref.shape[1], tn):
        u_ref[:, c:c + tn] = _dot(xn, win_ref[:, c:c + tn]).astype(u_ref.dtype)


def _ff_tile(f, cap):
    best = LANES
    for c in range(LANES, cap + 1, LANES):
        if f % c == 0:
            best = c
    return best


def _ffn_ple_proj(h, g_ffn, wg, wu, wd, g_ple, p, wp, wpg, g_mix, w_in, *, name):
    t, d = h.shape
    f = wg.shape[1]
    e = p.shape[1]
    n = w_in.shape[1]
    tm = min(ROW_TILE, t)
    tn = COL_TILE if n % COL_TILE == 0 else n
    row = lambda w: pl.BlockSpec((tm, w), lambda i: (i, 0))
    return pl.pallas_call(
        functools.partial(_ffn_ple_proj_kernel, tf=_ff_tile(f, 1536), tn=tn), grid=(t // tm,),
        in_specs=[row(d), _resident((1, d)), _resident((d, f)), _resident((d, f)), _resident((f, d)),
                  _resident((1, d)), row(e), _resident((e, d)), _resident((d, d)),
                  _resident((1, d)), _resident((d, n))],
        out_specs=[row(d), row(n)],
        out_shape=[jax.ShapeDtypeStruct((t, d), F32), jax.ShapeDtypeStruct((t, n), BF16)],
        compiler_params=_cparams("parallel"), name=name,
    )(h, g_ffn.reshape(1, d), wg, wu, wd, g_ple.reshape(1, d), p, wp, wpg, g_mix.reshape(1, d), w_in)


def _ple_math(x, g_ref, p_ref, wp_ref, wg_ref):
    xn = _rms(x, g_ref[...]).astype(BF16)
    gate = _sigmoid(_dot(xn, wg_ref[...]))
    emb = _dot(p_ref[...].astype(BF16), wp_ref[...])
    return x + emb * gate


def _unit_lower_inverses(mats):
    n = mats[0].shape[0]
    row = lax.broadcasted_iota(jnp.int32, (n, n), 0)
    col = lax.broadcasted_iota(jnp.int32, (n, n), 1)
    eye = (row == col).astype(F32)
    same16 = (row // 16) == (col // 16)
    same32 = (row // 32) == (col // 32)
    off32 = jnp.logical_and(same32, jnp.logical_not(same16))

    def mm(ps, qs):
        return [_dot(p.astype(BF16), q.astype(BF16)) for p, q in zip(ps, qs)]

    ad = [jnp.where(same16, a, 0.0) for a in mats]
    a2 = mm(ad, ad)
    x = mm([eye - t for t in ad], [eye + t for t in a2])
    a4 = mm(a2, a2)
    x = mm(x, [eye + t for t in a4])
    a8 = mm(a4, a4)
    x = mm(x, [eye + t for t in a8])
    y = mm([jnp.where(off32, a, 0.0) for a in mats], x)
    x = [t - c for t, c in zip(x, mm(x, y))]
    y = mm([jnp.where(same32, 0.0, a) for a in mats], x)
    return [t - c for t, c in zip(x, mm(x, y))]


def _seq_tiles(pad_ref, s):
    rt = min(ELEM_ROWS, s)
    n_tiles = s // rt

    def rows(i):
        return pl.ds(pl.multiple_of(i * rt, rt), rt)

    def fill_pad(fn):
        pad_ref[0:TOP_PAD, :] = jnp.zeros((TOP_PAD, pad_ref.shape[1]), F32)

        def body(i, c):
            r = pl.multiple_of(i * rt, rt)
            pad_ref[pl.ds(TOP_PAD + r, rt), :] = fn(rows(i))
            return c
        lax.fori_loop(0, n_tiles, body, 0)

    def conv_tile(i, w_ref, width):
        tap = _delayed(pad_ref, pl.multiple_of(i * rt, rt), rt, width - 1)
        acc = None
        for j in range(width):
            term = w_ref[j:j + 1, :] * tap(width - 1 - j)
            acc = term if acc is None else acc + term
        return acc

    return n_tiles, rows, fill_pad, conv_tile


def _gated_conv_kernel(bg_ref, cg_ref, xin_ref, w_ref, o_ref, pad_ref):
    n_tiles, rows, fill_pad, conv_tile = _seq_tiles(pad_ref, bg_ref.shape[0])
    fill_pad(lambda rs: cg_ref[rs, :].astype(F32) * xin_ref[rs, :].astype(F32))

    def body(i, c):
        acc = conv_tile(i, w_ref, CONV_A_WIDTH)
        o_ref[rows(i), :] = (bg_ref[rows(i), :].astype(F32) * acc).astype(o_ref.dtype)
        return c
    lax.fori_loop(0, n_tiles, body, 0)


def _gated_conv(u, conv_a, batch, seq):
    cw = conv_a.shape[1]
    return pl.pallas_call(
        _gated_conv_kernel, grid=(batch,),
        in_specs=[pl.BlockSpec((seq, cw), lambda b, j=j: (b, j)) for j in range(3)] + [_resident(conv_a.shape)],
        out_specs=pl.BlockSpec((seq, cw), lambda b: (b, 0)),
        out_shape=jax.ShapeDtypeStruct((batch * seq, cw), BF16),
        scratch_shapes=[pltpu.VMEM((seq + TOP_PAD, cw), F32)],
        compiler_params=_cparams("parallel"), name="l0_gated_conv")(u, u, u, conv_a)


def _l0_mixer_kernel(q_ref, k_ref, v_ref, og_ref, ab_ref, wq_ref, wk_ref, wv_ref, alog_ref, dtb_ref, gn_ref,
                     o_ref,
                     pad_ref, qs_ref, ks_ref, vs_ref, us_ref, ws_ref, qk_ref, egl_ref, st_ref):
    s = q_ref.shape[0]
    heads = GDN_HEADS
    dk = q_ref.shape[1] // heads
    c64 = GDN_CHUNK
    n_chunks = s // c64
    n_tiles, rows, fill_pad, conv_tile = _seq_tiles(pad_ref, s)

    def l2n(x, scale):
        parts = []
        for h in range(heads):
            xh = x[:, h * dk:(h + 1) * dk]
            inv = lax.rsqrt(jnp.sum(xh * xh, axis=-1, keepdims=True) + EPS)
            parts.append(xh * (inv * scale))
        return jnp.concatenate(parts, axis=1)

    for src_ref, w_ref, dst_ref, post in (
            (q_ref, wq_ref, qs_ref, lambda x: l2n(x, dk ** -0.5)),
            (k_ref, wk_ref, ks_ref, lambda x: l2n(x, 1.0)),
            (v_ref, wv_ref, vs_ref, lambda x: x)):
        fill_pad(lambda rs, src_ref=src_ref: src_ref[rs, :].astype(F32))

        def conv_body(i, c, w_ref=w_ref, dst_ref=dst_ref, post=post):
            dst_ref[rows(i), :] = post(_silu(conv_tile(i, w_ref, GDN_CONV_WIDTH)))
            return c
        lax.fori_loop(0, n_tiles, conv_body, 0)

    ri = lax.broadcasted_iota(jnp.int32, (c64, c64), 0)
    ci = lax.broadcasted_iota(jnp.int32, (c64, c64), 1)
    tril = ri >= ci
    strict = ri > ci
    ltri = tril.astype(F32)

    group = PREP_CHUNKS if n_chunks % PREP_CHUNKS == 0 else 1

    def chunk_prep(cg, carry):
        chunks = []
        for cc in range(group):
            c = cg * group + cc
            rs = pl.ds(pl.multiple_of(c * c64, c64), c64)
            chunks.append((c, rs, ab_ref[rs, :], qs_ref[rs, :], ks_ref[rs, :], vs_ref[rs, :]))
        inst = []
        gcs = [_dot(ltri, -jnp.exp(alog_ref[...]) * _softplus(ab + dtb_ref[...]), precision=HIGHEST)
               for _, _, ab, _, _, _ in chunks]
        for (c, rs, ab, q_all, k_all, v_all), gc in zip(chunks, gcs):
            beta = _sigmoid(ab)
            gct = gc.T
            for h in range(heads):
                hs = slice(h * dk, (h + 1) * dk)
                gcol = gc[:, h:h + 1]
                glast = gc[c64 - 1:c64, h:h + 1]
                bcol = beta[:, heads + h:heads + h + 1]
                decay = jnp.where(tril, jnp.exp(jnp.where(tril, gcol - gct[h:h + 1, :], 0.0)), 0.0)
                kh, qh, vh = k_all[:, hs], q_all[:, hs], v_all[:, hs]
                kb = kh * bcol
                egc = jnp.exp(gcol)
                inst.append(dict(decay=decay, kh=kh, qh=qh, kb=kb, egc=egc,
                                 rhs=jnp.concatenate([vh * bcol, kb * egc], axis=1).astype(BF16),
                                 kd=kh * jnp.exp(glast - gcol),
                                 eg=jnp.broadcast_to(jnp.exp(glast), (SUBLANES, dk))))
        kqs = [_dot_nt(jnp.concatenate([t["kb"], t["qh"]], axis=0).astype(BF16), t["kh"].astype(BF16))
               for t in inst]
        minvs = _unit_lower_inverses([jnp.where(strict, kq[0:c64] * t["decay"], 0.0)
                                      for kq, t in zip(kqs, inst)])
        uws = [_dot(m.astype(BF16), t["rhs"]) for m, t in zip(minvs, inst)]
        for ci, (c, rs, _, _, _, _) in enumerate(chunks):
            sl = slice(ci * heads, (ci + 1) * heads)
            us_ref[rs, :] = jnp.concatenate([uw[:, 0:dk] for uw in uws[sl]], axis=1)
            ws_ref[rs, :] = jnp.concatenate([uw[:, dk:2 * dk] for uw in uws[sl]], axis=1).astype(BF16)
            qs_ref[rs, :] = jnp.concatenate([t["qh"] * t["egc"] for t in inst[sl]], axis=1)
            ks_ref[rs, :] = jnp.concatenate([t["kd"] for t in inst[sl]], axis=1)
            qk_ref[rs, :] = jnp.concatenate([kq[c64:2 * c64] * t["decay"]
                                             for kq, t in zip(kqs[sl], inst[sl])], axis=1).astype(BF16)
            e0 = pl.multiple_of(c * (heads * SUBLANES), heads * SUBLANES)
            egl_ref[pl.ds(e0, heads * SUBLANES), :] = jnp.concatenate([t["eg"] for t in inst[sl]], axis=0)
        return carry
    lax.fori_loop(0, n_chunks // group, chunk_prep, 0)

    st_ref[...] = jnp.zeros(st_ref.shape, F32)

    def scan(c, carry):
        rs = pl.ds(pl.multiple_of(c * c64, c64), c64)
        e0 = pl.multiple_of(c * (heads * SUBLANES), heads * SUBLANES)
        w_all, qg_all, u_all, kd_all = ws_ref[rs, :], qs_ref[rs, :], us_ref[rs, :], ks_ref[rs, :]
        qk_all = qk_ref[rs, :]
        eg_all = egl_ref[pl.ds(e0, heads * SUBLANES), :]
        hsl = [slice(h * dk, (h + 1) * dk) for h in range(heads)]
        sts = [st_ref[h] for h in range(heads)]
        wqs = [_dot(jnp.concatenate([w_all[:, hs], qg_all[:, hs].astype(BF16)], axis=0), st.astype(BF16))
               for hs, st in zip(hsl, sts)]
        vbs = [(u_all[:, hs] - wq[0:c64]).astype(BF16) for hs, wq in zip(hsl, wqs)]
        upd = [_dot_tn(kd_all[:, hs].astype(BF16), vb) for hs, vb in zip(hsl, vbs)]
        intra = [_dot(qk_all[:, h * c64:(h + 1) * c64], vbs[h]) for h in range(heads)]
        for h in range(heads):
            st_ref[h] = sts[h] * eg_all[h * SUBLANES:h * SUBLANES + 1, :] + upd[h]
        vs_ref[rs, :] = jnp.concatenate([wq[c64:2 * c64] + o for wq, o in zip(wqs, intra)], axis=1)
        return carry
    lax.fori_loop(0, n_chunks, scan, 0)

    def finish(i, c):
        o = vs_ref[rows(i), :]
        og = og_ref[rows(i), :].astype(F32)
        parts = []
        for h in range(heads):
            oh = o[:, h * dk:(h + 1) * dk]
            parts.append(_rms(oh, gn_ref[...]))
        y = jnp.concatenate(parts, axis=1) * _silu(og)
        o_ref[rows(i), :] = y.astype(o_ref.dtype)
        return c
    lax.fori_loop(0, n_tiles, finish, 0)


def _l0_mixer(u, ab, gdn_conv, a_log, dt_bias, gdn_norm_g, batch, seq):
    dk = gdn_norm_g.shape[0]
    heads = GDN_HEADS
    cw = heads * dk

    def lane_row(x):
        return jnp.zeros((1, LANES), F32).at[0, :x.shape[0]].set(x)

    in_specs = [pl.BlockSpec((seq, cw), lambda b, j=j: (b, j)) for j in range(3, 7)]
    in_specs += [pl.BlockSpec((seq, LANES), lambda b: (b, 0))]
    in_specs += [pl.BlockSpec((GDN_CONV_WIDTH, cw), lambda b, j=j: (0, j), pipeline_mode=pl.Buffered(1))
                 for j in range(3)]
    in_specs += [_resident((1, LANES)), _resident((1, LANES)), _resident((1, dk))]
    big = pltpu.VMEM((seq, cw), F32)
    scratch = [pltpu.VMEM((seq + TOP_PAD, cw), F32), big, big, big, big, pltpu.VMEM((seq, cw), BF16),
               pltpu.VMEM((seq, heads * GDN_CHUNK), BF16),
               pltpu.VMEM((seq // GDN_CHUNK * heads * SUBLANES, dk), F32),
               pltpu.VMEM((heads, dk, dk), F32)]
    return pl.pallas_call(
        _l0_mixer_kernel, grid=(batch,), in_specs=in_specs,
        out_specs=pl.BlockSpec((seq, cw), lambda b: (b, 0)),
        out_shape=jax.ShapeDtypeStruct((batch * seq, cw), BF16),
        scratch_shapes=scratch, compiler_params=_cparams("parallel"), name="l0_deltanet",
    )(u, u, u, u, ab, gdn_conv, gdn_conv, gdn_conv,
      lane_row(a_log), lane_row(dt_bias), gdn_norm_g.reshape(1, dk))


def _bias_table_kernel(tab_ref, o_ref, *, t):
    h = pl.program_id(0)
    m = pl.program_id(1)
    ri = lax.broadcasted_iota(jnp.int32, (t, t), 0)
    ci = lax.broadcasted_iota(jnp.int32, (t, t), 1)
    rel = m * t + ci - ri
    n = jnp.maximum(rel, 0)
    max_exact = NUM_BUCKETS // 2
    nf = jnp.maximum(n, 1).astype(F32)
    large = max_exact + (jnp.log(nf / max_exact) / math.log(MAX_DISTANCE / max_exact)
                         * (NUM_BUCKETS - max_exact)).astype(jnp.int32)
    large = jnp.minimum(large, NUM_BUCKETS - 1)
    bucket = jnp.where(n < max_exact, n, large)
    bias = jnp.zeros((t, t), F32)
    for b in range(NUM_BUCKETS):
        bias = jnp.where(bucket == b, tab_ref[b, h], bias)
    o_ref[...] = jnp.where(rel >= 0, bias * LOG2E, MASK_VALUE)


def _bias_table(rel_bias, seq, t):
    nb = seq // t
    heads = rel_bias.shape[1]
    return pl.pallas_call(
        functools.partial(_bias_table_kernel, t=t), grid=(heads, nb),
        in_specs=[pl.BlockSpec(memory_space=pltpu.SMEM)],
        out_specs=pl.BlockSpec((None, None, t, t), lambda h, m: (h, m, 0, 0)),
        out_shape=jax.ShapeDtypeStruct((heads, nb, t, t), F32),
        compiler_params=_cparams("parallel", "parallel"), name="rel_bias_table")(rel_bias)


def _attn_kernel(q_ref, k_ref, v_ref, tb_ref, lam_ref, gn_ref, o_ref, m_ref, l_ref, acc_ref,
                 *, t, heads, lambda_init):
    qi = pl.program_id(2)
    dh2 = q_ref.shape[1] // heads
    dh = dh2 // 2
    lane = lax.broadcasted_iota(jnp.int32, (t, dh2), 1)
    qqs = []
    for h in range(heads):
        qf = q_ref[:, h * dh2:(h + 1) * dh2].astype(F32) * (dh ** -0.5 * LOG2E)
        qqs.append(jnp.concatenate([jnp.where(lane < dh, qf, 0.0), jnp.where(lane >= dh, qf, 0.0)],
                                   axis=0).astype(BF16))

    hs = [slice(h * dh2, (h + 1) * dh2) for h in range(heads)]

    def update(j0, nblk):
        ks = pl.ds(pl.multiple_of(j0 * t, t), nblk * t)
        k_all = k_ref[ks, :]
        v_all = v_ref[ks, :]
        s_t = [_dot_nt(k_all[:, hs[h]], qqs[h]) for h in range(heads)]
        ps, alphas = [], []
        for h in range(heads):
            m = m_ref[h, 0:1, :]
            b = jnp.concatenate([tb_ref[h, qi - j0 - i] for i in range(nblk)], axis=0)
            s = jnp.concatenate([s_t[h][:, 0:t] + b, s_t[h][:, t:2 * t] + b], axis=1)
            m_new = jnp.maximum(m, jnp.max(s, axis=0, keepdims=True))
            alpha = jnp.exp2(m - m_new)
            p = jnp.exp2(s - m_new)
            m_ref[h] = jnp.broadcast_to(m_new, m_ref.shape[1:])
            l_ref[h] = jnp.broadcast_to(alpha * l_ref[h, 0:1, :] + jnp.sum(p, axis=0, keepdims=True),
                                        l_ref.shape[1:])
            alphas.append(alpha)
            ps.append(p.astype(BF16))
        pv = [_dot_tn(v_all[:, hs[h]], ps[h]) for h in range(heads)]
        for h in range(heads):
            acc_ref[h] = alphas[h] * acc_ref[h] + pv[h]

    m_ref[...] = jnp.full(m_ref.shape, MASK_VALUE, F32)
    l_ref[...] = jnp.zeros(l_ref.shape, F32)
    acc_ref[...] = jnp.zeros(acc_ref.shape, F32)
    odd = (qi + 1) % 2

    @pl.when(odd == 1)
    def _():
        update(0, 1)

    def pair(i, c):
        update(odd + 2 * i, 2)
        return c
    lax.fori_loop(0, (qi + 1) // 2, pair, 0)
    lp = lam_ref[...]
    lam = (jnp.exp(jnp.sum(lp[0:1] * lp[1:2], axis=-1, keepdims=True))
           - jnp.exp(jnp.sum(lp[2:3] * lp[3:4], axis=-1, keepdims=True)) + lambda_init)
    for h in range(heads):
        o12 = acc_ref[h] / l_ref[h, 0:1, :]
        o = (o12[:, 0:t] - lam * o12[:, t:2 * t]).T
        o_ref[:, h * dh2:(h + 1) * dh2] = (_rms(o, gn_ref[...]) * (1.0 - lambda_init)).astype(o_ref.dtype)


def _diff_attention(u, table, lam_params, norm_g, batch, seq, lambda_init):
    heads = DIFF_HEADS
    dh2 = norm_g.shape[0]
    hg = ATTN_HEADS_PER_STEP
    w = hg * dh2
    ng = heads // hg
    t = table.shape[2]
    nq = seq // t
    return pl.pallas_call(
        functools.partial(_attn_kernel, t=t, heads=hg, lambda_init=lambda_init),
        grid=(batch, ng, nq),
        in_specs=[pl.BlockSpec((t, w), lambda b, g, i: (b * nq + i, g)),
                  pl.BlockSpec((seq, w), lambda b, g, i: (b, ng + g)),
                  pl.BlockSpec((seq, w), lambda b, g, i: (b, 2 * ng + g)),
                  pl.BlockSpec((hg, nq, t, t), lambda b, g, i: (g, 0, 0, 0)),
                  _resident(lam_params.shape), _resident((1, dh2))],
        out_specs=pl.BlockSpec((t, w), lambda b, g, i: (b * nq + i, g)),
        out_shape=jax.ShapeDtypeStruct((batch * seq, heads * dh2), BF16),
        scratch_shapes=[pltpu.VMEM((hg, SUBLANES, 2 * t), F32), pltpu.VMEM((hg, SUBLANES, 2 * t), F32),
                        pltpu.VMEM((hg, dh2, 2 * t), F32)],
        compiler_params=_cparams("parallel", "parallel", "arbitrary"), name="diff_attention",
    )(u, u, u, table, lam_params, norm_g.reshape(1, dh2))


def _conformer_kernel(ga_ref, gb_ref, w_ref, b_ref, lg_ref, lb_ref, o_ref, pad_ref):
    s, c = ga_ref.shape
    rt = min(ELEM_ROWS, s)
    pad_ref[0:TOP_PAD, :] = jnp.zeros((TOP_PAD, c), F32)

    def glu(i, carry):
        r = pl.multiple_of(i * rt, rt)
        rs = pl.ds(r, rt)
        pad_ref[pl.ds(TOP_PAD + r, rt), :] = ga_ref[rs, :].astype(F32) * _sigmoid(gb_ref[rs, :].astype(F32))
        return carry
    lax.fori_loop(0, s // rt, glu, 0)

    ct = CONF_ROWS

    def conv(i, carry):
        r = pl.multiple_of(i * ct, ct)
        tap = _delayed(pad_ref, r, ct, CONF_WIDTH - 1)
        acc = jnp.broadcast_to(b_ref[...], (ct, c))
        for j in range(CONF_WIDTH):
            acc = acc + w_ref[j:j + 1, :] * tap(CONF_WIDTH - 1 - j)
        mu = jnp.mean(acc, axis=-1, keepdims=True)
        xc = acc - mu
        var = jnp.mean(xc * xc, axis=-1, keepdims=True)
        y = xc * lax.rsqrt(var + EPS) * lg_ref[...] + lb_ref[...]
        o_ref[pl.ds(r, ct), :] = _silu(y).astype(o_ref.dtype)
        return carry
    lax.fori_loop(0, s // ct, conv, 0)


def _conformer(u, w, b, ln_g, ln_b, batch, seq, col0):
    c = w.shape[1]
    return pl.pallas_call(
        _conformer_kernel, grid=(batch,),
        in_specs=[pl.BlockSpec((seq, c), lambda i: (i, col0)),
                  pl.BlockSpec((seq, c), lambda i: (i, col0 + 1)),
                  _resident(w.shape), _resident((1, c)), _resident((1, c)), _resident((1, c))],
        out_specs=pl.BlockSpec((seq, c), lambda i: (i, 0)),
        out_shape=jax.ShapeDtypeStruct((batch * seq, c), BF16),
        scratch_shapes=[pltpu.VMEM((seq + TOP_PAD, c), F32)],
        compiler_params=_cparams("parallel"), name="conformer_conv",
    )(u, u, w, b.reshape(1, c), ln_g.reshape(1, c), ln_b.reshape(1, c))


def _proj_router_kernel(y1_ref, y2_ref, w1_ref, w2_ref, h_ref, g_ref, wrt_ref, upper_ref,
                        h_out_ref, hn_ref, route_ref, gate_ref, cnt_ref, carry_ref):
    i = pl.program_id(0)
    tm = h_ref.shape[0]
    n_e = N_EXPERTS

    @pl.when(i == 0)
    def _():
        carry_ref[...] = jnp.zeros(carry_ref.shape, F32)

    h = h_ref[...] + _dot(y1_ref[...], w1_ref[...]) + _dot(y2_ref[...], w2_ref[...])
    h_out_ref[...] = h
    xn = _rms(h, g_ref[...])
    _store_token_tiles(hn_ref, 0, xn)

    logits = _dot_nt(wrt_ref[...], xn.astype(BF16))[0:n_e, :]
    sub = lax.broadcasted_iota(jnp.int32, logits.shape, 0)
    m1 = jnp.max(logits, axis=0, keepdims=True)
    i1 = jnp.min(jnp.where(logits == m1, sub, n_e), axis=0, keepdims=True)
    rest = jnp.where(sub == i1, -jnp.inf, logits)
    m2 = jnp.max(rest, axis=0, keepdims=True)
    i2 = jnp.min(jnp.where(rest == m2, sub, n_e), axis=0, keepdims=True)
    e = jnp.exp(m2 - m1)
    g1 = 1.0 / (1.0 + e)
    oh1 = sub == i1
    oh2 = sub == i2
    both = oh1.astype(F32) + oh2.astype(F32)
    csum = _dot(both.astype(BF16), upper_ref[...])
    carry = carry_ref[:, 0:1]
    before = csum - both + carry
    total = carry + csum[:, tm - 1:tm]
    carry_ref[...] = jnp.broadcast_to(total, carry_ref.shape)
    cnt_ref[...] = jnp.broadcast_to(total, cnt_ref.shape).astype(jnp.int32)
    r1 = jnp.sum(jnp.where(oh1, before, 0.0), axis=0, keepdims=True).astype(jnp.int32)
    r2 = jnp.sum(jnp.where(oh2, before, 0.0), axis=0, keepdims=True).astype(jnp.int32)
    route_ref[...] = jnp.where(sub == 0, i1, jnp.where(sub == 1, i2, jnp.where(sub == 2, r1,
                               jnp.where(sub == 3, r2, 0))))
    grow = jnp.where(sub == 0, g1, jnp.where(sub == 1, e * g1, 0.0))
    sel = (lax.broadcasted_iota(jnp.int32, (n_e, LANES), 0)
           == lax.broadcasted_iota(jnp.int32, (n_e, LANES), 1)).astype(F32)
    gate_ref[...] = lax.dot_general(grow, sel, (((0,), (0,)), ((), ())), precision=HIGHEST,
                                    preferred_element_type=F32)


def _proj_router(y1, y2, w, h, g, wr):
    t, d = h.shape
    k1, k2 = y1.shape[1], y2.shape[1]
    tm = min(ROW_TILE, t)
    wrt = jnp.zeros((LANES, d), BF16).at[:N_EXPERTS].set(wr.T.astype(BF16))
    upper = (jnp.arange(tm)[:, None] <= jnp.arange(tm)[None, :]).astype(BF16)
    row = lambda width: pl.BlockSpec((tm, width), lambda i: (i, 0))
    return pl.pallas_call(
        _proj_router_kernel, grid=(t // tm,),
        in_specs=[row(k1), row(k2), _resident((k1, d)), _resident((k2, d)), row(d),
                  _resident((1, d)), _resident((LANES, d)), _resident((tm, tm))],
        out_specs=[row(d), pl.BlockSpec((tm * d // LANES, LANES), lambda i: (i, 0)),
                   pl.BlockSpec((N_EXPERTS, tm), lambda i: (0, i)), row(LANES),
                   pl.BlockSpec((N_EXPERTS, LANES), lambda i: (0, 0))],
        out_shape=[jax.ShapeDtypeStruct((t, d), F32), jax.ShapeDtypeStruct((t * d // LANES, LANES), F32),
                   jax.ShapeDtypeStruct((N_EXPERTS, t), jnp.int32), jax.ShapeDtypeStruct((t, LANES), F32),
                   jax.ShapeDtypeStruct((N_EXPERTS, LANES), jnp.int32)],
        scratch_shapes=[pltpu.VMEM((N_EXPERTS, LANES), F32)],
        compiler_params=_cparams("arbitrary"), name="l1_out_proj_router",
    )(y1, y2, w[:k1], w[k1:], h, g.reshape(1, d), wrt, upper)


def _dispatch_kernel(dest_hbm, pad_hbm, x_ref, xs_hbm, idx_smem, pad_smem, zero_ref, sem_idx, sem_x):
    i = pl.program_id(0)
    n = x_ref.shape[0] // SUBLANES
    m = TOP_K * n
    cp = pltpu.make_async_copy(dest_hbm.at[i], idx_smem, sem_idx)
    cp.start()
    cp.wait()

    def body(r, c):
        for k in range(TOP_K):
            _token_tile_copy(x_ref, r, xs_hbm, idx_smem[k * n + r], sem_x).start()
        return c
    lax.fori_loop(0, n, body, 0, unroll=8)
    for _ in range(TOP_K):
        pltpu.make_async_copy(x_ref, xs_hbm.at[pl.ds(0, n * SUBLANES)], sem_x).wait()

    @pl.when(i == pl.num_programs(0) - 1)
    def _():
        n_pad = pad_smem.shape[0]
        cp = pltpu.make_async_copy(pad_hbm, pad_smem, sem_idx)
        cp.start()
        cp.wait()
        zero_ref[...] = jnp.zeros(zero_ref.shape, F32)

        def fill(j, c):
            _token_tile_copy(zero_ref, 0, xs_hbm, pad_smem[j], sem_x).start()
            return c
        lax.fori_loop(0, n_pad, fill, 0, unroll=8)
        pltpu.make_async_copy(xs_hbm.at[pl.ds(0, n_pad * SUBLANES)], xs_hbm.at[pl.ds(0, n_pad * SUBLANES)],
                              sem_x).wait()


def _dispatch(hn_tiles, dest, pad_rows, t, d):
    n = min(MOE_ROWS, t)
    steps = t // n
    tpt = d // LANES
    p = t * TOP_K + N_EXPERTS * MOE_ROWS
    dest_steps = jnp.concatenate([dest[k].reshape(steps, n) for k in range(TOP_K)], axis=1)
    return pl.pallas_call(
        _dispatch_kernel, grid=(steps,),
        in_specs=[pl.BlockSpec(memory_space=pl.ANY), pl.BlockSpec(memory_space=pl.ANY),
                  pl.BlockSpec((n * tpt, LANES), lambda i: (i, 0))],
        out_specs=pl.BlockSpec(memory_space=pl.ANY),
        out_shape=jax.ShapeDtypeStruct((p * tpt, LANES), F32),
        scratch_shapes=[pltpu.SMEM((TOP_K * n,), jnp.int32), pltpu.SMEM(pad_rows.shape, jnp.int32),
                        pltpu.VMEM((tpt, LANES), F32), pltpu.SemaphoreType.DMA, pltpu.SemaphoreType.DMA],
        compiler_params=_cparams("arbitrary"), name="moe_dispatch")(dest_steps, pad_rows, hn_tiles)


def _dispatch_plan(route, counts, t):
    tm = MOE_ROWS
    p = t * TOP_K + N_EXPERTS * tm
    padded = (counts + tm - 1) // tm * tm
    pend = jnp.cumsum(padded)
    pstart = pend - padded
    experts = jnp.arange(N_EXPERTS, dtype=jnp.int32)[:, None]
    dest = jnp.stack([jnp.sum(jnp.where(route[k][None, :] == experts, pstart[:, None], 0), axis=0)
                      + route[TOP_K + k] for k in range(TOP_K)])
    n_pad = p - t * TOP_K
    gaps = jnp.concatenate([padded - counts, (p - pend[-1])[None]])
    gap_end = jnp.cumsum(gaps)
    gap_row0 = jnp.concatenate([pstart + counts, pend[-1:]])
    j = jnp.arange(n_pad, dtype=jnp.int32)
    which = jnp.searchsorted(gap_end, j, side='right')
    pad_rows = (gap_row0[which] + j - (gap_end - gaps)[which]).astype(jnp.int32)
    nb = p // tm
    blk_e = jnp.minimum(jnp.searchsorted(pend, jnp.arange(nb, dtype=jnp.int32) * tm, side='right'),
                        N_EXPERTS - 1).astype(jnp.int32)
    return dest.astype(jnp.int32), pad_rows, blk_e


def _experts_kernel(blk_e_ref, x_ref, wgu_ref, wd_ref, o_ref, hid_ref):
    f = wd_ref.shape[0]
    tm = hid_ref.shape[0]
    x = _load_token_tiles(x_ref, 0, tm).astype(BF16)
    for c in range(0, f, MOE_SUB_TILE):
        hid_ref[:, c:c + MOE_SUB_TILE] = (
            _silu(_dot(x, wgu_ref[:, c:c + MOE_SUB_TILE]))
            * _dot(x, wgu_ref[:, f + c:f + c + MOE_SUB_TILE])).astype(BF16)
    _store_token_tiles(o_ref, 0, _dot(hid_ref[...], wd_ref[...]))


def _experts(xs, blk_e, w_gate_up, w_down):
    f, d = w_down.shape[1], w_down.shape[2]
    tm = MOE_ROWS
    tpt = d // LANES
    assert f % MOE_SUB_TILE == 0
    grid_spec = pltpu.PrefetchScalarGridSpec(
        num_scalar_prefetch=1, grid=(blk_e.shape[0],),
        in_specs=[pl.BlockSpec((tm * tpt, LANES), lambda i, e: (i, 0)),
                  pl.BlockSpec((None, d, 2 * f), lambda i, e: (e[i], 0, 0), pipeline_mode=pl.Buffered(1)),
                  pl.BlockSpec((None, f, d), lambda i, e: (e[i], 0, 0), pipeline_mode=pl.Buffered(1))],
        out_specs=pl.BlockSpec((tm * tpt, LANES), lambda i, e: (i, 0)),
        scratch_shapes=[pltpu.VMEM((tm, f), BF16)])
    return pl.pallas_call(
        _experts_kernel, grid_spec=grid_spec, out_shape=jax.ShapeDtypeStruct(xs.shape, F32),
        compiler_params=_cparams("parallel"), name="moe_experts",
    )(blk_e, xs, w_gate_up, w_down)


def _combine_kernel(dest_hbm, y_hbm, gate_ref, h_ref, g_ref, p_ref, wp_ref, wg_ref, fg_ref, o_ref,
                    idx0, idx1, ybuf, sem_idx, sem_rows):
    i = pl.program_id(0)
    steps = pl.num_programs(0)
    n = h_ref.shape[0]
    m = TOP_K * n
    slot = i % 2
    idxs = (idx0, idx1)

    def idx_copy(blk, s):
        return pltpu.make_async_copy(dest_hbm.at[blk], idxs[s], sem_idx.at[s])

    def issue_tiles(s):
        def body(r, c):
            _token_tile_copy(y_hbm, idxs[s][r], ybuf, s * m + r, sem_rows.at[s]).start()
            return c
        lax.fori_loop(0, m, body, 0, unroll=16)

    @pl.when(i == 0)
    def _():
        idx_copy(0, 0).start()
        idx_copy(0, 0).wait()
        idx_copy(1, 1).start()
        issue_tiles(0)

    for s in range(2):
        @pl.when(slot == s)
        def _(s=s):
            idx_copy(i + 1, 1 - s).wait()

            @pl.when(i + 2 <= steps)
            def _():
                idx_copy(i + 2, s).start()

            issue_tiles(1 - s)
            _token_tiles_wait(y_hbm, ybuf, s * m, m, sem_rows.at[s])

    gates = gate_ref[...]
    moe = (gates[:, 0:1] * _load_token_tiles(ybuf, slot * m, n)
           + gates[:, 1:2] * _load_token_tiles(ybuf, slot * m + n, n))
    x = _ple_math(h_ref[...] + moe, g_ref, p_ref, wp_ref, wg_ref)
    o_ref[...] = _rms(x, fg_ref[...])

    @pl.when(i == steps - 1)
    def _():
        _token_tiles_wait(y_hbm, ybuf, (1 - slot) * m, m, sem_rows.at[1 - slot])


def _combine_ple_final(dest, y_tiles, gates, h, g, p, wp, wg, final_g):
    t, d = h.shape
    e = p.shape[1]
    n = min(COMBINE_ROWS, t)
    m = TOP_K * n
    steps = t // n
    dest_steps = jnp.concatenate([dest[k].reshape(steps, n) for k in range(TOP_K)], axis=1)
    dest_steps = jnp.concatenate([dest_steps, jnp.zeros((1, m), jnp.int32)], axis=0)
    return pl.pallas_call(
        _combine_kernel, grid=(steps,),
        in_specs=[pl.BlockSpec(memory_space=pl.ANY), pl.BlockSpec(memory_space=pl.ANY),
                  pl.BlockSpec((n, LANES), lambda i: (i, 0)),
                  pl.BlockSpec((n, d), lambda i: (i, 0)), _resident((1, d)),
                  pl.BlockSpec((n, e), lambda i: (i, 0)), _resident((e, d)), _resident((d, d)),
                  _resident((1, d))],
        out_specs=pl.BlockSpec((n, d), lambda i: (i, 0)),
        out_shape=jax.ShapeDtypeStruct((t, d), F32),
        scratch_shapes=[pltpu.SMEM((m,), jnp.int32), pltpu.SMEM((m,), jnp.int32),
                        pltpu.VMEM((2 * m * d // LANES, LANES), F32),
                        pltpu.SemaphoreType.DMA((2,)), pltpu.SemaphoreType.DMA((2,))],
        compiler_params=_cparams("arbitrary"), name="moe_combine_ple_final",
    )(dest_steps, y_tiles, gates, h, g.reshape(1, d), p, wp, wg, final_g.reshape(1, d))


def kernel(x, p, norm_mix_g, norm_ffn_g, norm_ple_g, final_norm_g, ev_w_in, ev_conv_a, ev_gdn_conv, ev_gdn_A_log, ev_gdn_dt_bias, ev_gdn_norm_g, ev_w_out, od_w_in, od_lambda, od_diff_norm_g, od_conf_dw_w, od_conf_dw_b, od_conf_ln_g, od_conf_ln_b, od_w_out, rel_bias, ffn_w_gate_up, ffn_w_down, moe_router, moe_w_gate_up, moe_w_down, ple_w_proj, ple_w_gate):
    batch, seq, d = x.shape
    t = batch * seq
    depth = p.shape[0]
    assert depth == 2 and seq % GDN_CHUNK == 0
    h = x.reshape(t, d)
    pf = p.reshape(depth, t, p.shape[-1])

    heads = GDN_HEADS
    n_main = ev_w_in.shape[2] - 2 * heads
    w_in = ev_w_in[0]
    w_ab = jnp.zeros((d, LANES), BF16).at[:, :2 * heads].set(w_in[:, n_main:].astype(BF16))
    u, ab = _norm_proj(h, norm_mix_g[0], w_in[:, :n_main].astype(BF16), w_ab, name="l0_in_proj")
    ya = _gated_conv(u, ev_conv_a[0], batch, seq)
    yb = _l0_mixer(u, ab, ev_gdn_conv[0], ev_gdn_A_log[0], ev_gdn_dt_bias[0], ev_gdn_norm_g[0], batch, seq)
    h = _proj_residual([ya, yb], ev_w_out[0].astype(BF16), h, name="l0_out_proj")
    f = ffn_w_down.shape[1]
    lambda_init = 0.8 - 0.6 * math.exp(-0.3 * 1)
    h, u = _ffn_ple_proj(h, norm_ffn_g[0], ffn_w_gate_up[0, :, :f].astype(BF16),
                         ffn_w_gate_up[0, :, f:].astype(BF16), ffn_w_down[0].astype(BF16),
                         norm_ple_g[0], pf[0], ple_w_proj[0].astype(BF16), ple_w_gate[0].astype(BF16),
                         norm_mix_g[1], od_w_in[0].astype(BF16), name="l0_ffn_ple_l1_in_proj")

    table = _bias_table(rel_bias, seq, min(ATTN_TILE, seq))
    o_attn = _diff_attention(u, table, od_lambda[0], od_diff_norm_g[0], batch, seq, lambda_init)
    c_conf = od_conf_dw_w.shape[2]
    o_conf = _conformer(u, od_conf_dw_w[0], od_conf_dw_b[0], od_conf_ln_g[0], od_conf_ln_b[0],
                        batch, seq, 3 * DIFF_HEADS * od_diff_norm_g.shape[1] // c_conf)
    h, hn, route, gates, counts = _proj_router(o_attn, o_conf, od_w_out[0].astype(BF16), h, norm_ffn_g[1],
                                               moe_router[0])
    dest, pad_rows, blk_e = _dispatch_plan(route, counts[:, 0], t)
    xs = _dispatch(hn, dest, pad_rows, t, d)
    y = _experts(xs, blk_e, moe_w_gate_up[0].astype(BF16), moe_w_down[0].astype(BF16))
    out = _combine_ple_final(dest, y, gates, h, norm_ple_g[1], pf[1], ple_w_proj[1].astype(BF16),
                             ple_w_gate[1].astype(BF16), final_norm_g)
    return out.reshape(batch, seq, d)
```

```python
import functools
import math

import jax
import jax.numpy as jnp
from jax import lax
from jax.experimental import pallas as pl
from jax.experimental.pallas import tpu as pltpu

F32 = jnp.float32
BF16 = jnp.bfloat16
HIGHEST = lax.Precision.HIGHEST

EPS = 1e-6
CONV_A_WIDTH = 3
GDN_HEADS = 4
GDN_CONV_WIDTH = 4
GDN_CHUNK = 64
DIFF_HEADS = 4
NUM_BUCKETS = 32
MAX_DISTANCE = 128
CONF_WIDTH = 31
N_EXPERTS = 8
TOP_K = 2

LANES = 128
SUBLANES = 8
VMEM_LIMIT_BYTES = 56 * 1024 * 1024
MASK_VALUE = -1e30
LOG2E = math.log2(math.e)

ROW_TILE = 512
COL_TILE = 512
ELEM_ROWS = 256
CONF_ROWS = 256
PREP_CHUNKS = 8
ATTN_TILE = 256
ATTN_HEADS_PER_STEP = 4
MOE_ROWS = 1024
MOE_SUB_TILE = 256
COMBINE_ROWS = 256
TOP_PAD = 32


def _cparams(*sem):
    return pltpu.CompilerParams(dimension_semantics=sem, vmem_limit_bytes=VMEM_LIMIT_BYTES)


def _resident(shape):
    nd = len(shape)
    return pl.BlockSpec(shape, lambda *_: (0,) * nd, pipeline_mode=pl.Buffered(1))


def _rms(x, g):
    return x * lax.rsqrt(jnp.mean(x * x, axis=-1, keepdims=True) + EPS) * g


def _sigmoid(x):
    return jax.nn.sigmoid(x)


def _silu(x):
    return x * jax.nn.sigmoid(x)


def _softplus(x):
    return jnp.maximum(x, 0.0) + jnp.log1p(jnp.exp(-jnp.abs(x)))


def _dot(a, b, **kw):
    return jnp.dot(a, b, preferred_element_type=F32, **kw)


def _dot_nt(a, b):
    return lax.dot_general(a, b, (((1,), (1,)), ((), ())), preferred_element_type=F32)


def _dot_tn(a, b):
    return lax.dot_general(a, b, (((0,), (0,)), ((), ())), preferred_element_type=F32)


def _delayed(pad_ref, r, rows, max_delay):
    lead = -(-max_delay // SUBLANES) * SUBLANES
    win = pad_ref[pl.ds(TOP_PAD + r - lead, rows + lead), :]
    rolled = {0: win}

    def tap(d):
        a, b = divmod(d, SUBLANES)
        if b not in rolled:
            rolled[b] = pltpu.roll(win, b, 0)
        start = lead - SUBLANES * a
        return rolled[b][start:start + rows, :]
    return tap


def _load_token_tiles(ref, first_token, n):
    return jnp.concatenate([ref[pl.ds(first_token * SUBLANES + s, n, stride=SUBLANES), :]
                            for s in range(SUBLANES)], axis=1)


def _store_token_tiles(ref, first_token, x):
    n = x.shape[0]
    for s in range(SUBLANES):
        ref[pl.ds(first_token * SUBLANES + s, n, stride=SUBLANES), :] = x[:, s * LANES:(s + 1) * LANES]


def _token_tile_copy(src_hbm, src_token, dst_vmem, dst_token, sem):
    return pltpu.make_async_copy(src_hbm.at[pl.ds(pl.multiple_of(src_token * SUBLANES, SUBLANES), SUBLANES)],
                                 dst_vmem.at[pl.ds(pl.multiple_of(dst_token * SUBLANES, SUBLANES), SUBLANES)],
                                 sem)


def _token_tiles_wait(src_hbm, dst_vmem, first_token, n, sem):
    pltpu.make_async_copy(src_hbm.at[pl.ds(0, n * SUBLANES)],
                          dst_vmem.at[pl.ds(pl.multiple_of(first_token * SUBLANES, SUBLANES), n * SUBLANES)],
                          sem).wait()


def _norm_proj_kernel(h_ref, g_ref, w_ref, *rest, tn, with_aux):
    xn = _rms(h_ref[...], g_ref[...]).astype(BF16)
    if with_aux:
        w2_ref, o_ref, o2_ref = rest
        o2_ref[...] = _dot(xn, w2_ref[...])
    else:
        (o_ref,) = rest
    n = w_ref.shape[1]
    for c in range(0, n, tn):
        o_ref[:, c:c + tn] = _dot(xn, w_ref[:, c:c + tn]).astype(o_ref.dtype)


def _norm_proj(h, g, w, w_aux=None, *, name):
    t, d = h.shape
    n = w.shape[1]
    tm = min(ROW_TILE, t)
    tn = COL_TILE if n % COL_TILE == 0 else n
    in_specs = [pl.BlockSpec((tm, d), lambda i: (i, 0)), _resident((1, d)), _resident((d, n))]
    out_shape = [jax.ShapeDtypeStruct((t, n), BF16)]
    out_specs = [pl.BlockSpec((tm, n), lambda i: (i, 0))]
    args = [h, g.reshape(1, d), w]
    if w_aux is not None:
        in_specs.append(_resident(w_aux.shape))
        out_shape.append(jax.ShapeDtypeStruct((t, w_aux.shape[1]), F32))
        out_specs.append(pl.BlockSpec((tm, w_aux.shape[1]), lambda i: (i, 0)))
        args.append(w_aux)
    out = pl.pallas_call(
        functools.partial(_norm_proj_kernel, tn=tn, with_aux=w_aux is not None),
        grid=(t // tm,), in_specs=in_specs, out_specs=out_specs, out_shape=out_shape,
        compiler_params=_cparams("parallel"), name=name)(*args)
    return out if w_aux is not None else out[0]


def _proj_residual_kernel(*refs):
    n = (len(refs) - 2) // 2
    y_refs, w_refs, h_ref, o_ref = refs[:n], refs[n:2 * n], refs[2 * n], refs[2 * n + 1]
    acc = h_ref[...]
    for y_ref, w_ref in zip(y_refs, w_refs):
        acc = acc + _dot(y_ref[...], w_ref[...])
    o_ref[...] = acc


def _proj_residual(ys, w, h, *, name):
    t, d = h.shape
    tm = min(ROW_TILE, t)
    ws, r0 = [], 0
    for y in ys:
        ws.append(w[r0:r0 + y.shape[1]])
        r0 += y.shape[1]
    return pl.pallas_call(
        _proj_residual_kernel, grid=(t // tm,),
        in_specs=([pl.BlockSpec((tm, y.shape[1]), lambda i: (i, 0)) for y in ys]
                  + [_resident(wi.shape) for wi in ws] + [pl.BlockSpec((tm, d), lambda i: (i, 0))]),
        out_specs=pl.BlockSpec((tm, d), lambda i: (i, 0)),
        out_shape=jax.ShapeDtypeStruct((t, d), F32),
        compiler_params=_cparams("parallel"), name=name)(*ys, *ws, h)


def _swiglu_residual(x, g_ref, wg_ref, wu_ref, wd_ref, tf):
    xn = _rms(x, g_ref[...]).astype(BF16)
    acc = x
    for c in range(0, wg_ref.shape[1], tf):
        gate = _dot(xn, wg_ref[:, c:c + tf])
        up = _dot(xn, wu_ref[:, c:c + tf])
        hid = (_silu(gate) * up).astype(BF16)
        acc = acc + _dot(hid, wd_ref[c:c + tf, :])
    return acc


def _ffn_ple_proj_kernel(h_ref, gf_ref, wg_ref, wu_ref, wd_ref, gp_ref, p_ref, wp_ref, wpg_ref,
                         gm_ref, win_ref, h_out_ref, u_ref, *, tf, tn):
    h2 = _swiglu_residual(h_ref[...], gf_ref, wg_ref, wu_ref, wd_ref, tf)
    h3 = _ple_math(h2, gp_ref, p_ref, wp_ref, wpg_ref)
    h_out_ref[...] = h3
    xn = _rms(h3, gm_ref[...]).astype(BF16)
    for c in range(0, win_ref.shape[1], tn):
        u_ref[:, c:c + tn] = _dot(xn, win_ref[:, c:c + tn]).astype(u_ref.dtype)


def _ff_tile(f, cap):
    best = LANES
    for c in range(LANES, cap + 1, LANES):
        if f % c == 0:
            best = c
    return best


def _ffn_ple_proj(h, g_ffn, wg, wu, wd, g_ple, p, wp, wpg, g_mix, w_in, *, name):
    t, d = h.shape
    f = wg.shape[1]
    e = p.shape[1]
    n = w_in.shape[1]
    tm = min(ROW_TILE, t)
    tn = COL_TILE if n % COL_TILE == 0 else n
    row = lambda w: pl.BlockSpec((tm, w), lambda i: (i, 0))
    return pl.pallas_call(
        functools.partial(_ffn_ple_proj_kernel, tf=_ff_tile(f, 1536), tn=tn), grid=(t // tm,),
        in_specs=[row(d), _resident((1, d)), _resident((d, f)), _resident((d, f)), _resident((f, d)),
                  _resident((1, d)), row(e), _resident((e, d)), _resident((d, d)),
                  _resident((1, d)), _resident((d, n))],
        out_specs=[row(d), row(n)],
        out_shape=[jax.ShapeDtypeStruct((t, d), F32), jax.ShapeDtypeStruct((t, n), BF16)],
        compiler_params=_cparams("parallel"), name=name,
    )(h, g_ffn.reshape(1, d), wg, wu, wd, g_ple.reshape(1, d), p, wp, wpg, g_mix.reshape(1, d), w_in)


def _ple_math(x, g_ref, p_ref, wp_ref, wg_ref):
    xn = _rms(x, g_ref[...]).astype(BF16)
    gate = _sigmoid(_dot(xn, wg_ref[...]))
    emb = _dot(p_ref[...].astype(BF16), wp_ref[...])
    return x + emb * gate


def _unit_lower_inverses(mats):
    n = mats[0].shape[0]
    row = lax.broadcasted_iota(jnp.int32, (n, n), 0)
    col = lax.broadcasted_iota(jnp.int32, (n, n), 1)
    eye = (row == col).astype(F32)
    same16 = (row // 16) == (col // 16)
    same32 = (row // 32) == (col // 32)
    off32 = jnp.logical_and(same32, jnp.logical_not(same16))

    def mm(ps, qs):
        return [_dot(p.astype(BF16), q.astype(BF16)) for p, q in zip(ps, qs)]

    ad = [jnp.where(same16, a, 0.0) for a in mats]
    a2 = mm(ad, ad)
    x = mm([eye - t for t in ad], [eye + t for t in a2])
    a4 = mm(a2, a2)
    x = mm(x, [eye + t for t in a4])
    a8 = mm(a4, a4)
    x = mm(x, [eye + t for t in a8])
    y = mm([jnp.where(off32, a, 0.0) for a in mats], x)
    x = [t - c for t, c in zip(x, mm(x, y))]
    y = mm([jnp.where(same32, 0.0, a) for a in mats], x)
    return [t - c for t, c in zip(x, mm(x, y))]


def _seq_tiles(pad_ref, s):
    rt = min(ELEM_ROWS, s)
    n_tiles = s // rt

    def rows(i):
        return pl.ds(pl.multiple_of(i * rt, rt), rt)

    def fill_pad(fn):
        pad_ref[0:TOP_PAD, :] = jnp.zeros((TOP_PAD, pad_ref.shape[1]), F32)

        def body(i, c):
            r = pl.multiple_of(i * rt, rt)
            pad_ref[pl.ds(TOP_PAD + r, rt), :] = fn(rows(i))
            return c
        lax.fori_loop(0, n_tiles, body, 0)

    def conv_tile(i, w_ref, width):
        tap = _delayed(pad_ref, pl.multiple_of(i * rt, rt), rt, width - 1)
        acc = None
        for j in range(width):
            term = w_ref[j:j + 1, :] * tap(width - 1 - j)
            acc = term if acc is None else acc + term
        return acc

    return n_tiles, rows, fill_pad, conv_tile


def _gated_conv_kernel(bg_ref, cg_ref, xin_ref, w_ref, o_ref, pad_ref):
    n_tiles, rows, fill_pad, conv_tile = _seq_tiles(pad_ref, bg_ref.shape[0])
    fill_pad(lambda rs: cg_ref[rs, :].astype(F32) * xin_ref[rs, :].astype(F32))

    def body(i, c):
        acc = conv_tile(i, w_ref, CONV_A_WIDTH)
        o_ref[rows(i), :] = (bg_ref[rows(i), :].astype(F32) * acc).astype(o_ref.dtype)
        return c
    lax.fori_loop(0, n_tiles, body, 0)


def _gated_conv(u, conv_a, batch, seq):
    cw = conv_a.shape[1]
    return pl.pallas_call(
        _gated_conv_kernel, grid=(batch,),
        in_specs=[pl.BlockSpec((seq, cw), lambda b, j=j: (b, j)) for j in range(3)] + [_resident(conv_a.shape)],
        out_specs=pl.BlockSpec((seq, cw), lambda b: (b, 0)),
        out_shape=jax.ShapeDtypeStruct((batch * seq, cw), BF16),
        scratch_shapes=[pltpu.VMEM((seq + TOP_PAD, cw), F32)],
        compiler_params=_cparams("parallel"), name="l0_gated_conv")(u, u, u, conv_a)


def _l0_mixer_kernel(q_ref, k_ref, v_ref, og_ref, ab_ref, wq_ref, wk_ref, wv_ref, alog_ref, dtb_ref, gn_ref,
                     o_ref,
                     pad_ref, qs_ref, ks_ref, vs_ref, us_ref, ws_ref, qk_ref, egl_ref, st_ref):
    s = q_ref.shape[0]
    heads = GDN_HEADS
    dk = q_ref.shape[1] // heads
    c64 = GDN_CHUNK
    n_chunks = s // c64
    n_tiles, rows, fill_pad, conv_tile = _seq_tiles(pad_ref, s)

    def l2n(x, scale):
        parts = []
        for h in range(heads):
            xh = x[:, h * dk:(h + 1) * dk]
            inv = lax.rsqrt(jnp.sum(xh * xh, axis=-1, keepdims=True) + EPS)
            parts.append(xh * (inv * scale))
        return jnp.concatenate(parts, axis=1)

    for src_ref, w_ref, dst_ref, post in (
            (q_ref, wq_ref, qs_ref, lambda x: l2n(x, dk ** -0.5)),
            (k_ref, wk_ref, ks_ref, lambda x: l2n(x, 1.0)),
            (v_ref, wv_ref, vs_ref, lambda x: x)):
        fill_pad(lambda rs, src_ref=src_ref: src_ref[rs, :].astype(F32))

        def conv_body(i, c, w_ref=w_ref, dst_ref=dst_ref, post=post):
            dst_ref[rows(i), :] = post(_silu(conv_tile(i, w_ref, GDN_CONV_WIDTH)))
            return c
        lax.fori_loop(0, n_tiles, conv_body, 0)

    ri = lax.broadcasted_iota(jnp.int32, (c64, c64), 0)
    ci = lax.broadcasted_iota(jnp.int32, (c64, c64), 1)
    tril = ri >= ci
    strict = ri > ci
    ltri = tril.astype(F32)

    group = PREP_CHUNKS if n_chunks % PREP_CHUNKS == 0 else 1

    def chunk_prep(cg, carry):
        chunks = []
        for cc in range(group):
            c = cg * group + cc
            rs = pl.ds(pl.multiple_of(c * c64, c64), c64)
            chunks.append((c, rs, ab_ref[rs, :], qs_ref[rs, :], ks_ref[rs, :], vs_ref[rs, :]))
        inst = []
        gcs = [_dot(ltri, -jnp.exp(alog_ref[...]) * _softplus(ab + dtb_ref[...]), precision=HIGHEST)
               for _, _, ab, _, _, _ in chunks]
        for (c, rs, ab, q_all, k_all, v_all), gc in zip(chunks, gcs):
            beta = _sigmoid(ab)
            gct = gc.T
            for h in range(heads):
                hs = slice(h * dk, (h + 1) * dk)
                gcol = gc[:, h:h + 1]
                glast = gc[c64 - 1:c64, h:h + 1]
                bcol = beta[:, heads + h:heads + h + 1]
                decay = jnp.where(tril, jnp.exp(jnp.where(tril, gcol - gct[h:h + 1, :], 0.0)), 0.0)
                kh, qh, vh = k_all[:, hs], q_all[:, hs], v_all[:, hs]
                kb = kh * bcol
                egc = jnp.exp(gcol)
                inst.append(dict(decay=decay, kh=kh, qh=qh, kb=kb, egc=egc,
                                 rhs=jnp.concatenate([vh * bcol, kb * egc], axis=1).astype(BF16),
                                 kd=kh * jnp.exp(glast - gcol),
                                 eg=jnp.broadcast_to(jnp.exp(glast), (SUBLANES, dk))))
        kqs = [_dot_nt(jnp.concatenate([t["kb"], t["qh"]], axis=0).astype(BF16), t["kh"].astype(BF16))
               for t in inst]
        minvs = _unit_lower_inverses([jnp.where(strict, kq[0:c64] * t["decay"], 0.0)
                                      for kq, t in zip(kqs, inst)])
        uws = [_dot(m.astype(BF16), t["rhs"]) for m, t in zip(minvs, inst)]
        for ci, (c, rs, _, _, _, _) in enumerate(chunks):
            sl = slice(ci * heads, (ci + 1) * heads)
            us_ref[rs, :] = jnp.concatenate([uw[:, 0:dk] for uw in uws[sl]], axis=1)
            ws_ref[rs, :] = jnp.concatenate([uw[:, dk:2 * dk] for uw in uws[sl]], axis=1).astype(BF16)
            qs_ref[rs, :] = jnp.concatenate([t["qh"] * t["egc"] for t in inst[sl]], axis=1)
            ks_ref[rs, :] = jnp.concatenate([t["kd"] for t in inst[sl]], axis=1)
            qk_ref[rs, :] = jnp.concatenate([kq[c64:2 * c64] * t["decay"]
                                             for kq, t in zip(kqs[sl], inst[sl])], axis=1).astype(BF16)
            e0 = pl.multiple_of(c * (heads * SUBLANES), heads * SUBLANES)
            egl_ref[pl.ds(e0, heads * SUBLANES), :] = jnp.concatenate([t["eg"] for t in inst[sl]], axis=0)
        return carry
    lax.fori_loop(0, n_chunks // group, chunk_prep, 0)

    st_ref[...] = jnp.zeros(st_ref.shape, F32)

    def scan(c, carry):
        rs = pl.ds(pl.multiple_of(c * c64, c64), c64)
        e0 = pl.multiple_of(c * (heads * SUBLANES), heads * SUBLANES)
        w_all, qg_all, u_all, kd_all = ws_ref[rs, :], qs_ref[rs, :], us_ref[rs, :], ks_ref[rs, :]
        qk_all = qk_ref[rs, :]
        eg_all = egl_ref[pl.ds(e0, heads * SUBLANES), :]
        hsl = [slice(h * dk, (h + 1) * dk) for h in range(heads)]
        sts = [st_ref[h] for h in range(heads)]
        wqs = [_dot(jnp.concatenate([w_all[:, hs], qg_all[:, hs].astype(BF16)], axis=0), st.astype(BF16))
               for hs, st in zip(hsl, sts)]
        vbs = [(u_all[:, hs] - wq[0:c64]).astype(BF16) for hs, wq in zip(hsl, wqs)]
        upd = [_dot_tn(kd_all[:, hs].astype(BF16), vb) for hs, vb in zip(hsl, vbs)]
        intra = [_dot(qk_all[:, h * c64:(h + 1) * c64], vbs[h]) for h in range(heads)]
        for h in range(heads):
            st_ref[h] = sts[h] * eg_all[h * SUBLANES:h * SUBLANES + 1, :] + upd[h]
        vs_ref[rs, :] = jnp.concatenate([wq[c64:2 * c64] + o for wq, o in zip(wqs, intra)], axis=1)
        return carry
    lax.fori_loop(0, n_chunks, scan, 0)

    def finish(i, c):
        o = vs_ref[rows(i), :]
        og = og_ref[rows(i), :].astype(F32)
        parts = []
        for h in range(heads):
            oh = o[:, h * dk:(h + 1) * dk]
            parts.append(_rms(oh, gn_ref[...]))
        y = jnp.concatenate(parts, axis=1) * _silu(og)
        o_ref[rows(i), :] = y.astype(o_ref.dtype)
        return c
    lax.fori_loop(0, n_tiles, finish, 0)


def _l0_mixer(u, ab, gdn_conv, a_log, dt_bias, gdn_norm_g, batch, seq):
    dk = gdn_norm_g.shape[0]
    heads = GDN_HEADS
    cw = heads * dk

    def lane_row(x):
        return jnp.zeros((1, LANES), F32).at[0, :x.shape[0]].set(x)

    in_specs = [pl.BlockSpec((seq, cw), lambda b, j=j: (b, j)) for j in range(3, 7)]
    in_specs += [pl.BlockSpec((seq, LANES), lambda b: (b, 0))]
    in_specs += [pl.BlockSpec((GDN_CONV_WIDTH, cw), lambda b, j=j: (0, j), pipeline_mode=pl.Buffered(1))
                 for j in range(3)]
    in_specs += [_resident((1, LANES)), _resident((1, LANES)), _resident((1, dk))]
    big = pltpu.VMEM((seq, cw), F32)
    scratch = [pltpu.VMEM((seq + TOP_PAD, cw), F32), big, big, big, big, pltpu.VMEM((seq, cw), BF16),
               pltpu.VMEM((seq, heads * GDN_CHUNK), BF16),
               pltpu.VMEM((seq // GDN_CHUNK * heads * SUBLANES, dk), F32),
               pltpu.VMEM((heads, dk, dk), F32)]
    return pl.pallas_call(
        _l0_mixer_kernel, grid=(batch,), in_specs=in_specs,
        out_specs=pl.BlockSpec((seq, cw), lambda b: (b, 0)),
        out_shape=jax.ShapeDtypeStruct((batch * seq, cw), BF16),
        scratch_shapes=scratch, compiler_params=_cparams("parallel"), name="l0_deltanet",
    )(u, u, u, u, ab, gdn_conv, gdn_conv, gdn_conv,
      lane_row(a_log), lane_row(dt_bias), gdn_norm_g.reshape(1, dk))


def _bias_table_kernel(tab_ref, o_ref, *, t):
    h = pl.program_id(0)
    m = pl.program_id(1)
    ri = lax.broadcasted_iota(jnp.int32, (t, t), 0)
    ci = lax.broadcasted_iota(jnp.int32, (t, t), 1)
    rel = m * t + ci - ri
    n = jnp.maximum(rel, 0)
    max_exact = NUM_BUCKETS // 2
    nf = jnp.maximum(n, 1).astype(F32)
    large = max_exact + (jnp.log(nf / max_exact) / math.log(MAX_DISTANCE / max_exact)
                         * (NUM_BUCKETS - max_exact)).astype(jnp.int32)
    large = jnp.minimum(large, NUM_BUCKETS - 1)
    bucket = jnp.where(n < max_exact, n, large)
    bias = jnp.zeros((t, t), F32)
    for b in range(NUM_BUCKETS):
        bias = jnp.where(bucket == b, tab_ref[b, h], bias)
    o_ref[...] = jnp.where(rel >= 0, bias * LOG2E, MASK_VALUE)


def _bias_table(rel_bias, seq, t):
    nb = seq // t
    heads = rel_bias.shape[1]
    return pl.pallas_call(
        functools.partial(_bias_table_kernel, t=t), grid=(heads, nb),
        in_specs=[pl.BlockSpec(memory_space=pltpu.SMEM)],
        out_specs=pl.BlockSpec((None, None, t, t), lambda h, m: (h, m, 0, 0)),
        out_shape=jax.ShapeDtypeStruct((heads, nb, t, t), F32),
        compiler_params=_cparams("parallel", "parallel"), name="rel_bias_table")(rel_bias)


def _attn_kernel(q_ref, k_ref, v_ref, tb_ref, lam_ref, gn_ref, o_ref, m_ref, l_ref, acc_ref,
                 *, t, heads, lambda_init):
    qi = pl.program_id(2)
    dh2 = q_ref.shape[1] // heads
    dh = dh2 // 2
    lane = lax.broadcasted_iota(jnp.int32, (t, dh2), 1)
    qqs = []
    for h in range(heads):
        qf = q_ref[:, h * dh2:(h + 1) * dh2].astype(F32) * (dh ** -0.5 * LOG2E)
        qqs.append(jnp.concatenate([jnp.where(lane < dh, qf, 0.0), jnp.where(lane >= dh, qf, 0.0)],
                                   axis=0).astype(BF16))

    hs = [slice(h * dh2, (h + 1) * dh2) for h in range(heads)]

    def update(j0, nblk):
        ks = pl.ds(pl.multiple_of(j0 * t, t), nblk * t)
        k_all = k_ref[ks, :]
        v_all = v_ref[ks, :]
        s_t = [_dot_nt(k_all[:, hs[h]], qqs[h]) for h in range(heads)]
        ps, alphas = [], []
        for h in range(heads):
            m = m_ref[h, 0:1, :]
            b = jnp.concatenate([tb_ref[h, qi - j0 - i] for i in range(nblk)], axis=0)
            s = jnp.concatenate([s_t[h][:, 0:t] + b, s_t[h][:, t:2 * t] + b], axis=1)
            m_new = jnp.maximum(m, jnp.max(s, axis=0, keepdims=True))
            alpha = jnp.exp2(m - m_new)
            p = jnp.exp2(s - m_new)
            m_ref[h] = jnp.broadcast_to(m_new, m_ref.shape[1:])
            l_ref[h] = jnp.broadcast_to(alpha * l_ref[h, 0:1, :] + jnp.sum(p, axis=0, keepdims=True),
                                        l_ref.shape[1:])
            alphas.append(alpha)
            ps.append(p.astype(BF16))
        pv = [_dot_tn(v_all[:, hs[h]], ps[h]) for h in range(heads)]
        for h in range(heads):
            acc_ref[h] = alphas[h] * acc_ref[h] + pv[h]

    m_ref[...] = jnp.full(m_ref.shape, MASK_VALUE, F32)
    l_ref[...] = jnp.zeros(l_ref.shape, F32)
    acc_ref[...] = jnp.zeros(acc_ref.shape, F32)
    odd = (qi + 1) % 2

    @pl.when(odd == 1)
    def _():
        update(0, 1)

    def pair(i, c):
        update(odd + 2 * i, 2)
        return c
    lax.fori_loop(0, (qi + 1) // 2, pair, 0)
    lp = lam_ref[...]
    lam = (jnp.exp(jnp.sum(lp[0:1] * lp[1:2], axis=-1, keepdims=True))
           - jnp.exp(jnp.sum(lp[2:3] * lp[3:4], axis=-1, keepdims=True)) + lambda_init)
    for h in range(heads):
        o12 = acc_ref[h] / l_ref[h, 0:1, :]
        o = (o12[:, 0:t] - lam * o12[:, t:2 * t]).T
        o_ref[:, h * dh2:(h + 1) * dh2] = (_rms(o, gn_ref[...]) * (1.0 - lambda_init)).astype(o_ref.dtype)


def _diff_attention(u, table, lam_params, norm_g, batch, seq, lambda_init):
    heads = DIFF_HEADS
    dh2 = norm_g.shape[0]
    hg = ATTN_HEADS_PER_STEP
    w = hg * dh2
    ng = heads // hg
    t = table.shape[2]
    nq = seq // t
    return pl.pallas_call(
        functools.partial(_attn_kernel, t=t, heads=hg, lambda_init=lambda_init),
        grid=(batch, ng, nq),
        in_specs=[pl.BlockSpec((t, w), lambda b, g, i: (b * nq + i, g)),
                  pl.BlockSpec((seq, w), lambda b, g, i: (b, ng + g)),
                  pl.BlockSpec((seq, w), lambda b, g, i: (b, 2 * ng + g)),
                  pl.BlockSpec((hg, nq, t, t), lambda b, g, i: (g, 0, 0, 0)),
                  _resident(lam_params.shape), _resident((1, dh2))],
        out_specs=pl.BlockSpec((t, w), lambda b, g, i: (b * nq + i, g)),
        out_shape=jax.ShapeDtypeStruct((batch * seq, heads * dh2), BF16),
        scratch_shapes=[pltpu.VMEM((hg, SUBLANES, 2 * t), F32), pltpu.VMEM((hg, SUBLANES, 2 * t), F32),
                        pltpu.VMEM((hg, dh2, 2 * t), F32)],
        compiler_params=_cparams("parallel", "parallel", "arbitrary"), name="diff_attention",
    )(u, u, u, table, lam_params, norm_g.reshape(1, dh2))


def _conformer_kernel(ga_ref, gb_ref, w_ref, b_ref, lg_ref, lb_ref, o_ref, pad_ref):
    s, c = ga_ref.shape
    rt = min(ELEM_ROWS, s)
    pad_ref[0:TOP_PAD, :] = jnp.zeros((TOP_PAD, c), F32)

    def glu(i, carry):
        r = pl.multiple_of(i * rt, rt)
        rs = pl.ds(r, rt)
        pad_ref[pl.ds(TOP_PAD + r, rt), :] = ga_ref[rs, :].astype(F32) * _sigmoid(gb_ref[rs, :].astype(F32))
        return carry
    lax.fori_loop(0, s // rt, glu, 0)

    ct = CONF_ROWS

    def conv(i, carry):
        r = pl.multiple_of(i * ct, ct)
        tap = _delayed(pad_ref, r, ct, CONF_WIDTH - 1)
        acc = jnp.broadcast_to(b_ref[...], (ct, c))
        for j in range(CONF_WIDTH):
            acc = acc + w_ref[j:j + 1, :] * tap(CONF_WIDTH - 1 - j)
        mu = jnp.mean(acc, axis=-1, keepdims=True)
        xc = acc - mu
        var = jnp.mean(xc * xc, axis=-1, keepdims=True)
        y = xc * lax.rsqrt(var + EPS) * lg_ref[...] + lb_ref[...]
        o_ref[pl.ds(r, ct), :] = _silu(y).astype(o_ref.dtype)
        return carry
    lax.fori_loop(0, s // ct, conv, 0)


def _conformer(u, w, b, ln_g, ln_b, batch, seq, col0):
    c = w.shape[1]
    return pl.pallas_call(
        _conformer_kernel, grid=(batch,),
        in_specs=[pl.BlockSpec((seq, c), lambda i: (i, col0)),
                  pl.BlockSpec((seq, c), lambda i: (i, col0 + 1)),
                  _resident(w.shape), _resident((1, c)), _resident((1, c)), _resident((1, c))],
        out_specs=pl.BlockSpec((seq, c), lambda i: (i, 0)),
        out_shape=jax.ShapeDtypeStruct((batch * seq, c), BF16),
        scratch_shapes=[pltpu.VMEM((seq + TOP_PAD, c), F32)],
        compiler_params=_cparams("parallel"), name="conformer_conv",
    )(u, u, w, b.reshape(1, c), ln_g.reshape(1, c), ln_b.reshape(1, c))


def _proj_router_kernel(y1_ref, y2_ref, w1_ref, w2_ref, h_ref, g_ref, wrt_ref, upper_ref,
                        h_out_ref, hn_ref, route_ref, gate_ref, cnt_ref, carry_ref):
    i = pl.program_id(0)
    tm = h_ref.shape[0]
    n_e = N_EXPERTS

    @pl.when(i == 0)
    def _():
        carry_ref[...] = jnp.zeros(carry_ref.shape, F32)

    h = h_ref[...] + _dot(y1_ref[...], w1_ref[...]) + _dot(y2_ref[...], w2_ref[...])
    h_out_ref[...] = h
    xn = _rms(h, g_ref[...])
    _store_token_tiles(hn_ref, 0, xn)

    logits = _dot_nt(wrt_ref[...], xn.astype(BF16))[0:n_e, :]
    sub = lax.broadcasted_iota(jnp.int32, logits.shape, 0)
    m1 = jnp.max(logits, axis=0, keepdims=True)
    i1 = jnp.min(jnp.where(logits == m1, sub, n_e), axis=0, keepdims=True)
    rest = jnp.where(sub == i1, -jnp.inf, logits)
    m2 = jnp.max(rest, axis=0, keepdims=True)
    i2 = jnp.min(jnp.where(rest == m2, sub, n_e), axis=0, keepdims=True)
    e = jnp.exp(m2 - m1)
    g1 = 1.0 / (1.0 + e)
    oh1 = sub == i1
    oh2 = sub == i2
    both = oh1.astype(F32) + oh2.astype(F32)
    csum = _dot(both.astype(BF16), upper_ref[...])
    carry = carry_ref[:, 0:1]
    before = csum - both + carry
    total = carry + csum[:, tm - 1:tm]
    carry_ref[...] = jnp.broadcast_to(total, carry_ref.shape)
    cnt_ref[...] = jnp.broadcast_to(total, cnt_ref.shape).astype(jnp.int32)
    r1 = jnp.sum(jnp.where(oh1, before, 0.0), axis=0, keepdims=True).astype(jnp.int32)
    r2 = jnp.sum(jnp.where(oh2, before, 0.0), axis=0, keepdims=True).astype(jnp.int32)
    route_ref[...] = jnp.where(sub == 0, i1, jnp.where(sub == 1, i2, jnp.where(sub == 2, r1,
                               jnp.where(sub == 3, r2, 0))))
    grow = jnp.where(sub == 0, g1, jnp.where(sub == 1, e * g1, 0.0))
    sel = (lax.broadcasted_iota(jnp.int32, (n_e, LANES), 0)
           == lax.broadcasted_iota(jnp.int32, (n_e, LANES), 1)).astype(F32)
    gate_ref[...] = lax.dot_general(grow, sel, (((0,), (0,)), ((), ())), precision=HIGHEST,
                                    preferred_element_type=F32)


def _proj_router(y1, y2, w, h, g, wr):
    t, d = h.shape
    k1, k2 = y1.shape[1], y2.shape[1]
    tm = min(ROW_TILE, t)
    wrt = jnp.zeros((LANES, d), BF16).at[:N_EXPERTS].set(wr.T.astype(BF16))
    upper = (jnp.arange(tm)[:, None] <= jnp.arange(tm)[None, :]).astype(BF16)
    row = lambda width: pl.BlockSpec((tm, width), lambda i: (i, 0))
    return pl.pallas_call(
        _proj_router_kernel, grid=(t // tm,),
        in_specs=[row(k1), row(k2), _resident((k1, d)), _resident((k2, d)), row(d),
                  _resident((1, d)), _resident((LANES, d)), _resident((tm, tm))],
        out_specs=[row(d), pl.BlockSpec((tm * d // LANES, LANES), lambda i: (i, 0)),
                   pl.BlockSpec((N_EXPERTS, tm), lambda i: (0, i)), row(LANES),
                   pl.BlockSpec((N_EXPERTS, LANES), lambda i: (0, 0))],
        out_shape=[jax.ShapeDtypeStruct((t, d), F32), jax.ShapeDtypeStruct((t * d // LANES, LANES), F32),
                   jax.ShapeDtypeStruct((N_EXPERTS, t), jnp.int32), jax.ShapeDtypeStruct((t, LANES), F32),
                   jax.ShapeDtypeStruct((N_EXPERTS, LANES), jnp.int32)],
        scratch_shapes=[pltpu.VMEM((N_EXPERTS, LANES), F32)],
        compiler_params=_cparams("arbitrary"), name="l1_out_proj_router",
    )(y1, y2, w[:k1], w[k1:], h, g.reshape(1, d), wrt, upper)


def _dispatch_kernel(dest_hbm, pad_hbm, x_ref, xs_hbm, idx_smem, pad_smem, zero_ref, sem_idx, sem_x):
    i = pl.program_id(0)
    n = x_ref.shape[0] // SUBLANES
    m = TOP_K * n
    cp = pltpu.make_async_copy(dest_hbm.at[i], idx_smem, sem_idx)
    cp.start()
    cp.wait()

    def body(r, c):
        for k in range(TOP_K):
            _token_tile_copy(x_ref, r, xs_hbm, idx_smem[k * n + r], sem_x).start(priority=k % 2)
        return c
    lax.fori_loop(0, n, body, 0, unroll=8)
    for _ in range(TOP_K):
        pltpu.make_async_copy(x_ref, xs_hbm.at[pl.ds(0, n * SUBLANES)], sem_x).wait()

    @pl.when(i == pl.num_programs(0) - 1)
    def _():
        n_pad = pad_smem.shape[0]
        cp = pltpu.make_async_copy(pad_hbm, pad_smem, sem_idx)
        cp.start()
        cp.wait()
        zero_ref[...] = jnp.zeros(zero_ref.shape, F32)

        def fill(j, c):
            for q in range(2):
                _token_tile_copy(zero_ref, 0, xs_hbm, pad_smem[2 * j + q], sem_x).start(priority=q)
            return c
        lax.fori_loop(0, n_pad // 2, fill, 0, unroll=8)
        pltpu.make_async_copy(xs_hbm.at[pl.ds(0, n_pad * SUBLANES)], xs_hbm.at[pl.ds(0, n_pad * SUBLANES)],
                              sem_x).wait()


def _dispatch(hn_tiles, dest, pad_rows, t, d):
    n = min(MOE_ROWS, t)
    steps = t // n
    tpt = d // LANES
    p = t * TOP_K + N_EXPERTS * MOE_ROWS
    dest_steps = jnp.concatenate([dest[k].reshape(steps, n) for k in range(TOP_K)], axis=1)
    return pl.pallas_call(
        _dispatch_kernel, grid=(steps,),
        in_specs=[pl.BlockSpec(memory_space=pl.ANY), pl.BlockSpec(memory_space=pl.ANY),
                  pl.BlockSpec((n * tpt, LANES), lambda i: (i, 0))],
        out_specs=pl.BlockSpec(memory_space=pl.ANY),
        out_shape=jax.ShapeDtypeStruct((p * tpt, LANES), F32),
        scratch_shapes=[pltpu.SMEM((TOP_K * n,), jnp.int32), pltpu.SMEM(pad_rows.shape, jnp.int32),
                        pltpu.VMEM((tpt, LANES), F32), pltpu.SemaphoreType.DMA, pltpu.SemaphoreType.DMA],
        compiler_params=_cparams("arbitrary"), name="moe_dispatch")(dest_steps, pad_rows, hn_tiles)


def _dispatch_plan(route, counts, t):
    tm = MOE_ROWS
    p = t * TOP_K + N_EXPERTS * tm
    padded = (counts + tm - 1) // tm * tm
    pend = jnp.cumsum(padded)
    pstart = pend - padded
    experts = jnp.arange(N_EXPERTS, dtype=jnp.int32)[:, None]
    dest = jnp.stack([jnp.sum(jnp.where(route[k][None, :] == experts, pstart[:, None], 0), axis=0)
                      + route[TOP_K + k] for k in range(TOP_K)])
    n_pad = p - t * TOP_K
    gaps = jnp.concatenate([padded - counts, (p - pend[-1])[None]])
    gap_end = jnp.cumsum(gaps)
    gap_row0 = jnp.concatenate([pstart + counts, pend[-1:]])
    j = jnp.arange(n_pad, dtype=jnp.int32)
    which = jnp.searchsorted(gap_end, j, side='right')
    pad_rows = (gap_row0[which] + j - (gap_end - gaps)[which]).astype(jnp.int32)
    nb = p // tm
    blk_e = jnp.minimum(jnp.searchsorted(pend, jnp.arange(nb, dtype=jnp.int32) * tm, side='right'),
                        N_EXPERTS - 1).astype(jnp.int32)
    return dest.astype(jnp.int32), pad_rows, blk_e, (pend[-1:] // tm).astype(jnp.int32)


def _experts_kernel(blk_e_ref, used_ref, x_ref, wgu_ref, wd_ref, o_ref, hid_ref):
    f = wd_ref.shape[0]
    tm = hid_ref.shape[0]
    live = pl.program_id(0) < used_ref[0]

    @pl.when(live)
    def _():
        x = _load_token_tiles(x_ref, 0, tm).astype(BF16)
        for c in range(0, f, MOE_SUB_TILE):
            hid_ref[:, c:c + MOE_SUB_TILE] = (
                _silu(_dot(x, wgu_ref[:, c:c + MOE_SUB_TILE]))
                * _dot(x, wgu_ref[:, f + c:f + c + MOE_SUB_TILE])).astype(BF16)
        _store_token_tiles(o_ref, 0, _dot(hid_ref[...], wd_ref[...]))

    @pl.when(jnp.logical_not(live))
    def _():
        o_ref[...] = jnp.zeros(o_ref.shape, F32)


def _experts(xs, blk_e, n_used, w_gate_up, w_down):
    f, d = w_down.shape[1], w_down.shape[2]
    tm = MOE_ROWS
    tpt = d // LANES
    assert f % MOE_SUB_TILE == 0
    grid_spec = pltpu.PrefetchScalarGridSpec(
        num_scalar_prefetch=2, grid=(blk_e.shape[0],),
        in_specs=[pl.BlockSpec((tm * tpt, LANES), lambda i, e, u: (i, 0)),
                  pl.BlockSpec((None, d, 2 * f), lambda i, e, u: (e[i], 0, 0), pipeline_mode=pl.Buffered(1)),
                  pl.BlockSpec((None, f, d), lambda i, e, u: (e[i], 0, 0), pipeline_mode=pl.Buffered(1))],
        out_specs=pl.BlockSpec((tm * tpt, LANES), lambda i, e, u: (i, 0)),
        scratch_shapes=[pltpu.VMEM((tm, f), BF16)])
    return pl.pallas_call(
        _experts_kernel, grid_spec=grid_spec, out_shape=jax.ShapeDtypeStruct(xs.shape, F32),
        compiler_params=_cparams("parallel"), name="moe_experts",
    )(blk_e, n_used, xs, w_gate_up, w_down)


def _combine_kernel(dest_hbm, y_hbm, gate_ref, h_ref, g_ref, p_ref, wp_ref, wg_ref, fg_ref, o_ref,
                    idx0, idx1, ybuf, sem_idx, sem_rows):
    i = pl.program_id(0)
    steps = pl.num_programs(0)
    n = h_ref.shape[0]
    m = TOP_K * n
    slot = i % 2
    idxs = (idx0, idx1)

    def idx_copy(blk, s):
        return pltpu.make_async_copy(dest_hbm.at[blk], idxs[s], sem_idx.at[s])

    def issue_tiles(s):
        def body(r, c):
            for k in range(TOP_K):
                _token_tile_copy(y_hbm, idxs[s][k * n + r], ybuf, s * m + k * n + r,
                                 sem_rows.at[s]).start(priority=k % 2)
            return c
        lax.fori_loop(0, n, body, 0, unroll=8)

    @pl.when(i == 0)
    def _():
        idx_copy(0, 0).start()
        idx_copy(0, 0).wait()
        idx_copy(1, 1).start()
        issue_tiles(0)

    for s in range(2):
        @pl.when(slot == s)
        def _(s=s):
            idx_copy(i + 1, 1 - s).wait()

            @pl.when(i + 2 <= steps)
            def _():
                idx_copy(i + 2, s).start()

            issue_tiles(1 - s)
            _token_tiles_wait(y_hbm, ybuf, s * m, m, sem_rows.at[s])

    gates = gate_ref[...]
    moe = (gates[:, 0:1] * _load_token_tiles(ybuf, slot * m, n)
           + gates[:, 1:2] * _load_token_tiles(ybuf, slot * m + n, n))
    x = _ple_math(h_ref[...] + moe, g_ref, p_ref, wp_ref, wg_ref)
    o_ref[...] = _rms(x, fg_ref[...])

    @pl.when(i == steps - 1)
    def _():
        _token_tiles_wait(y_hbm, ybuf, (1 - slot) * m, m, sem_rows.at[1 - slot])


def _combine_ple_final(dest, y_tiles, gates, h, g, p, wp, wg, final_g):
    t, d = h.shape
    e = p.shape[1]
    n = min(COMBINE_ROWS, t)
    m = TOP_K * n
    steps = t // n
    dest_steps = jnp.concatenate([dest[k].reshape(steps, n) for k in range(TOP_K)], axis=1)
    dest_steps = jnp.concatenate([dest_steps, jnp.zeros((1, m), jnp.int32)], axis=0)
    return pl.pallas_call(
        _combine_kernel, grid=(steps,),
        in_specs=[pl.BlockSpec(memory_space=pl.ANY), pl.BlockSpec(memory_space=pl.ANY),
                  pl.BlockSpec((n, LANES), lambda i: (i, 0)),
                  pl.BlockSpec((n, d), lambda i: (i, 0)), _resident((1, d)),
                  pl.BlockSpec((n, e), lambda i: (i, 0)), _resident((e, d)), _resident((d, d)),
                  _resident((1, d))],
        out_specs=pl.BlockSpec((n, d), lambda i: (i, 0)),
        out_shape=jax.ShapeDtypeStruct((t, d), F32),
        scratch_shapes=[pltpu.SMEM((m,), jnp.int32), pltpu.SMEM((m,), jnp.int32),
                        pltpu.VMEM((2 * m * d // LANES, LANES), F32),
                        pltpu.SemaphoreType.DMA((2,)), pltpu.SemaphoreType.DMA((2,))],
        compiler_params=_cparams("arbitrary"), name="moe_combine_ple_final",
    )(dest_steps, y_tiles, gates, h, g.reshape(1, d), p, wp, wg, final_g.reshape(1, d))


def kernel(x, p, norm_mix_g, norm_ffn_g, norm_ple_g, final_norm_g, ev_w_in, ev_conv_a, ev_gdn_conv, ev_gdn_A_log, ev_gdn_dt_bias, ev_gdn_norm_g, ev_w_out, od_w_in, od_lambda, od_diff_norm_g, od_conf_dw_w, od_conf_dw_b, od_conf_ln_g, od_conf_ln_b, od_w_out, rel_bias, ffn_w_gate_up, ffn_w_down, moe_router, moe_w_gate_up, moe_w_down, ple_w_proj, ple_w_gate):
    batch, seq, d = x.shape
    t = batch * seq
    depth = p.shape[0]
    assert depth == 2 and seq % GDN_CHUNK == 0
    h = x.reshape(t, d)
    pf = p.reshape(depth, t, p.shape[-1])

    heads = GDN_HEADS
    n_main = ev_w_in.shape[2] - 2 * heads
    w_in = ev_w_in[0]
    w_ab = jnp.zeros((d, LANES), BF16).at[:, :2 * heads].set(w_in[:, n_main:].astype(BF16))
    u, ab = _norm_proj(h, norm_mix_g[0], w_in[:, :n_main].astype(BF16), w_ab, name="l0_in_proj")
    ya = _gated_conv(u, ev_conv_a[0], batch, seq)
    yb = _l0_mixer(u, ab, ev_gdn_conv[0], ev_gdn_A_log[0], ev_gdn_dt_bias[0], ev_gdn_norm_g[0], batch, seq)
    h = _proj_residual([ya, yb], ev_w_out[0].astype(BF16), h, name="l0_out_proj")
    f = ffn_w_down.shape[1]
    lambda_init = 0.8 - 0.6 * math.exp(-0.3 * 1)
    h, u = _ffn_ple_proj(h, norm_ffn_g[0], ffn_w_gate_up[0, :, :f].astype(BF16),
                         ffn_w_gate_up[0, :, f:].astype(BF16), ffn_w_down[0].astype(BF16),
                         norm_ple_g[0], pf[0], ple_w_proj[0].astype(BF16), ple_w_gate[0].astype(BF16),
                         norm_mix_g[1], od_w_in[0].astype(BF16), name="l0_ffn_ple_l1_in_proj")

    table = _bias_table(rel_bias, seq, min(ATTN_TILE, seq))
    o_attn = _diff_attention(u, table, od_lambda[0], od_diff_norm_g[0], batch, seq, lambda_init)
    c_conf = od_conf_dw_w.shape[2]
    o_conf = _conformer(u, od_conf_dw_w[0], od_conf_dw_b[0], od_conf_ln_g[0], od_conf_ln_b[0],
                        batch, seq, 3 * DIFF_HEADS * od_diff_norm_g.shape[1] // c_conf)
    h, hn, route, gates, counts = _proj_router(o_attn, o_conf, od_w_out[0].astype(BF16), h, norm_ffn_g[1],
                                               moe_router[0])
    dest, pad_rows, blk_e, n_used = _dispatch_plan(route, counts[:, 0], t)
    xs = _dispatch(hn, dest, pad_rows, t, d)
    y = _experts(xs, blk_e, n_used, moe_w_gate_up[0].astype(BF16), moe_w_down[0].astype(BF16))
    out = _combine_ple_final(dest, y, gates, h, norm_ple_g[1], pf[1], ple_w_proj[1].astype(BF16),
                             ple_w_gate[1].astype(BF16), final_norm_g)
    return out.reshape(batch, seq, d)
```

```python
import functools
import math

import jax
import jax.numpy as jnp
from jax import lax
from jax.experimental import pallas as pl
from jax.experimental.pallas import tpu as pltpu

F32 = jnp.float32
BF16 = jnp.bfloat16
HIGHEST = lax.Precision.HIGHEST

EPS = 1e-6
CONV_A_WIDTH = 3
GDN_HEADS = 4
GDN_CONV_WIDTH = 4
GDN_CHUNK = 64
DIFF_HEADS = 4
NUM_BUCKETS = 32
MAX_DISTANCE = 128
CONF_WIDTH = 31
N_EXPERTS = 8
TOP_K = 2

LANES = 128
SUBLANES = 8
VMEM_LIMIT_BYTES = 56 * 1024 * 1024
MASK_VALUE = -1e30
LOG2E = math.log2(math.e)

ROW_TILE = 512
COL_TILE = 512
ELEM_ROWS = 256
CONF_ROWS = 256
PREP_CHUNKS = 8
ATTN_TILE = 256
ATTN_HEADS_PER_STEP = 4
MOE_ROWS = 1024
MOE_SUB_TILE = 256
COMBINE_ROWS = 512
COMBINE_CHUNKS = 4
TOP_PAD = 32


def _cparams(*sem):
    return pltpu.CompilerParams(dimension_semantics=sem, vmem_limit_bytes=VMEM_LIMIT_BYTES)


def _resident(shape):
    nd = len(shape)
    return pl.BlockSpec(shape, lambda *_: (0,) * nd, pipeline_mode=pl.Buffered(1))


def _rms(x, g):
    return x * lax.rsqrt(jnp.mean(x * x, axis=-1, keepdims=True) + EPS) * g


def _sigmoid(x):
    return jax.nn.sigmoid(x)


def _silu(x):
    return x * jax.nn.sigmoid(x)


def _softplus(x):
    return jnp.maximum(x, 0.0) + jnp.log1p(jnp.exp(-jnp.abs(x)))


def _dot(a, b, **kw):
    return jnp.dot(a, b, preferred_element_type=F32, **kw)


def _dot_nt(a, b):
    return lax.dot_general(a, b, (((1,), (1,)), ((), ())), preferred_element_type=F32)


def _dot_tn(a, b):
    return lax.dot_general(a, b, (((0,), (0,)), ((), ())), preferred_element_type=F32)


def _delayed(pad_ref, r, rows, max_delay):
    lead = -(-max_delay // SUBLANES) * SUBLANES
    win = pad_ref[pl.ds(TOP_PAD + r - lead, rows + lead), :]
    rolled = {0: win}

    def tap(d):
        a, b = divmod(d, SUBLANES)
        if b not in rolled:
            rolled[b] = pltpu.roll(win, b, 0)
        start = lead - SUBLANES * a
        return rolled[b][start:start + rows, :]
    return tap


def _load_token_tiles(ref, first_token, n):
    return jnp.concatenate([ref[pl.ds(first_token * SUBLANES + s, n, stride=SUBLANES), :]
                            for s in range(SUBLANES)], axis=1)


def _store_token_tiles(ref, first_token, x):
    n = x.shape[0]
    for s in range(SUBLANES):
        ref[pl.ds(first_token * SUBLANES + s, n, stride=SUBLANES), :] = x[:, s * LANES:(s + 1) * LANES]


def _token_tile_copy(src_hbm, src_token, dst_vmem, dst_token, sem):
    return pltpu.make_async_copy(src_hbm.at[pl.ds(pl.multiple_of(src_token * SUBLANES, SUBLANES), SUBLANES)],
                                 dst_vmem.at[pl.ds(pl.multiple_of(dst_token * SUBLANES, SUBLANES), SUBLANES)],
                                 sem)


def _token_tiles_wait(src_hbm, dst_vmem, first_token, n, sem):
    pltpu.make_async_copy(src_hbm.at[pl.ds(0, n * SUBLANES)],
                          dst_vmem.at[pl.ds(pl.multiple_of(first_token * SUBLANES, SUBLANES), n * SUBLANES)],
                          sem).wait()


def _norm_proj_kernel(h_ref, g_ref, w_ref, *rest, tn, with_aux):
    xn = _rms(h_ref[...], g_ref[...]).astype(BF16)
    if with_aux:
        w2_ref, o_ref, o2_ref = rest
        o2_ref[...] = _dot(xn, w2_ref[...])
    else:
        (o_ref,) = rest
    n = w_ref.shape[1]
    for c in range(0, n, tn):
        o_ref[:, c:c + tn] = _dot(xn, w_ref[:, c:c + tn]).astype(o_ref.dtype)


def _norm_proj(h, g, w, w_aux=None, *, name):
    t, d = h.shape
    n = w.shape[1]
    tm = min(ROW_TILE, t)
    tn = COL_TILE if n % COL_TILE == 0 else n
    in_specs = [pl.BlockSpec((tm, d), lambda i: (i, 0)), _resident((1, d)), _resident((d, n))]
    out_shape = [jax.ShapeDtypeStruct((t, n), BF16)]
    out_specs = [pl.BlockSpec((tm, n), lambda i: (i, 0))]
    args = [h, g.reshape(1, d), w]
    if w_aux is not None:
        in_specs.append(_resident(w_aux.shape))
        out_shape.append(jax.ShapeDtypeStruct((t, w_aux.shape[1]), F32))
        out_specs.append(pl.BlockSpec((tm, w_aux.shape[1]), lambda i: (i, 0)))
        args.append(w_aux)
    out = pl.pallas_call(
        functools.partial(_norm_proj_kernel, tn=tn, with_aux=w_aux is not None),
        grid=(t // tm,), in_specs=in_specs, out_specs=out_specs, out_shape=out_shape,
        compiler_params=_cparams("parallel"), name=name)(*args)
    return out if w_aux is not None else out[0]


def _proj_residual_kernel(*refs):
    n = (len(refs) - 2) // 2
    y_refs, w_refs, h_ref, o_ref = refs[:n], refs[n:2 * n], refs[2 * n], refs[2 * n + 1]
    acc = h_ref[...]
    for y_ref, w_ref in zip(y_refs, w_refs):
        acc = acc + _dot(y_ref[...], w_ref[...])
    o_ref[...] = acc


def _proj_residual(ys, w, h, *, name):
    t, d = h.shape
    tm = min(ROW_TILE, t)
    ws, r0 = [], 0
    for y in ys:
        ws.append(w[r0:r0 + y.shape[1]])
        r0 += y.shape[1]
    return pl.pallas_call(
        _proj_residual_kernel, grid=(t // tm,),
        in_specs=([pl.BlockSpec((tm, y.shape[1]), lambda i: (i, 0)) for y in ys]
                  + [_resident(wi.shape) for wi in ws] + [pl.BlockSpec((tm, d), lambda i: (i, 0))]),
        out_specs=pl.BlockSpec((tm, d), lambda i: (i, 0)),
        out_shape=jax.ShapeDtypeStruct((t, d), F32),
        compiler_params=_cparams("parallel"), name=name)(*ys, *ws, h)


def _swiglu_residual(x, g_ref, wg_ref, wu_ref, wd_ref, tf):
    xn = _rms(x, g_ref[...]).astype(BF16)
    acc = x
    for c in range(0, wg_ref.shape[1], tf):
        gate = _dot(xn, wg_ref[:, c:c + tf])
        up = _dot(xn, wu_ref[:, c:c + tf])
        hid = (_silu(gate) * up).astype(BF16)
        acc = acc + _dot(hid, wd_ref[c:c + tf, :])
    return acc


def _ffn_ple_proj_kernel(h_ref, gf_ref, wg_ref, wu_ref, wd_ref, gp_ref, p_ref, wp_ref, wpg_ref,
                         gm_ref, win_ref, h_out_ref, u_ref, *, tf, tn):
    h2 = _swiglu_residual(h_ref[...], gf_ref, wg_ref, wu_ref, wd_ref, tf)
    h3 = _ple_math(h2, gp_ref, p_ref, wp_ref, wpg_ref)
    h_out_ref[...] = h3
    xn = _rms(h3, gm_ref[...]).astype(BF16)
    for c in range(0, win_ref.shape[1], tn):
        u_ref[:, c:c + tn] = _dot(xn, win_ref[:, c:c + tn]).astype(u_ref.dtype)


def _ff_tile(f, cap):
    best = LANES
    for c in range(LANES, cap + 1, LANES):
        if f % c == 0:
            best = c
    return best


def _ffn_ple_proj(h, g_ffn, wg, wu, wd, g_ple, p, layer, wp, wpg, g_mix, w_in, *, name):
    t, d = h.shape
    f = wg.shape[1]
    e = p.shape[2]
    n = w_in.shape[1]
    tm = min(ROW_TILE, t)
    tn = COL_TILE if n % COL_TILE == 0 else n
    row = lambda w: pl.BlockSpec((tm, w), lambda i: (i, 0))
    return pl.pallas_call(
        functools.partial(_ffn_ple_proj_kernel, tf=_ff_tile(f, 1536), tn=tn), grid=(t // tm,),
        in_specs=[row(d), _resident((1, d)), _resident((d, f)), _resident((d, f)), _resident((f, d)),
                  _resident((1, d)), pl.BlockSpec((None, tm, e), lambda i: (layer, i, 0)),
                  _resident((e, d)), _resident((d, d)), _resident((1, d)), _resident((d, n))],
        out_specs=[row(d), row(n)],
        out_shape=[jax.ShapeDtypeStruct((t, d), F32), jax.ShapeDtypeStruct((t, n), BF16)],
        compiler_params=_cparams("parallel"), name=name,
    )(h, g_ffn.reshape(1, d), wg, wu, wd, g_ple.reshape(1, d), p, wp, wpg, g_mix.reshape(1, d), w_in)


def _ple_math(x, g_ref, p_ref, wp_ref, wg_ref):
    xn = _rms(x, g_ref[...]).astype(BF16)
    gate = _sigmoid(_dot(xn, wg_ref[...]))
    emb = _dot(p_ref[...].astype(BF16), wp_ref[...])
    return x + emb * gate


def _unit_lower_inverses(mats):
    n = mats[0].shape[0]
    row = lax.broadcasted_iota(jnp.int32, (n, n), 0)
    col = lax.broadcasted_iota(jnp.int32, (n, n), 1)
    eye = (row == col).astype(F32)
    same16 = (row // 16) == (col // 16)
    same32 = (row // 32) == (col // 32)
    off32 = jnp.logical_and(same32, jnp.logical_not(same16))

    def mm(ps, qs):
        return [_dot(p.astype(BF16), q.astype(BF16)) for p, q in zip(ps, qs)]

    ad = [jnp.where(same16, a, 0.0) for a in mats]
    a2 = mm(ad, ad)
    x = mm([eye - t for t in ad], [eye + t for t in a2])
    a4 = mm(a2, a2)
    x = mm(x, [eye + t for t in a4])
    a8 = mm(a4, a4)
    x = mm(x, [eye + t for t in a8])
    y = mm([jnp.where(off32, a, 0.0) for a in mats], x)
    x = [t - c for t, c in zip(x, mm(x, y))]
    y = mm([jnp.where(same32, 0.0, a) for a in mats], x)
    return [t - c for t, c in zip(x, mm(x, y))]


def _seq_tiles(pad_ref, s):
    rt = min(ELEM_ROWS, s)
    n_tiles = s // rt

    def rows(i):
        return pl.ds(pl.multiple_of(i * rt, rt), rt)

    def fill_pad(fn):
        pad_ref[0:TOP_PAD, :] = jnp.zeros((TOP_PAD, pad_ref.shape[1]), F32)

        def body(i, c):
            r = pl.multiple_of(i * rt, rt)
            pad_ref[pl.ds(TOP_PAD + r, rt), :] = fn(rows(i))
            return c
        lax.fori_loop(0, n_tiles, body, 0)

    def conv_tile(i, w_ref, width):
        tap = _delayed(pad_ref, pl.multiple_of(i * rt, rt), rt, width - 1)
        acc = None
        for j in range(width):
            term = w_ref[j:j + 1, :] * tap(width - 1 - j)
            acc = term if acc is None else acc + term
        return acc

    return n_tiles, rows, fill_pad, conv_tile


def _gated_conv_kernel(bg_ref, cg_ref, xin_ref, w_ref, o_ref, pad_ref):
    n_tiles, rows, fill_pad, conv_tile = _seq_tiles(pad_ref, bg_ref.shape[0])
    fill_pad(lambda rs: cg_ref[rs, :].astype(F32) * xin_ref[rs, :].astype(F32))

    def body(i, c):
        acc = conv_tile(i, w_ref, CONV_A_WIDTH)
        o_ref[rows(i), :] = (bg_ref[rows(i), :].astype(F32) * acc).astype(o_ref.dtype)
        return c
    lax.fori_loop(0, n_tiles, body, 0)


def _gated_conv(u, conv_a, batch, seq):
    cw = conv_a.shape[1]
    return pl.pallas_call(
        _gated_conv_kernel, grid=(batch,),
        in_specs=[pl.BlockSpec((seq, cw), lambda b, j=j: (b, j)) for j in range(3)] + [_resident(conv_a.shape)],
        out_specs=pl.BlockSpec((seq, cw), lambda b: (b, 0)),
        out_shape=jax.ShapeDtypeStruct((batch * seq, cw), BF16),
        scratch_shapes=[pltpu.VMEM((seq + TOP_PAD, cw), F32)],
        compiler_params=_cparams("parallel"), name="l0_gated_conv")(u, u, u, conv_a)


def _l0_mixer_kernel(q_ref, k_ref, v_ref, og_ref, ab_ref, wq_ref, wk_ref, wv_ref, alog_ref, dtb_ref, gn_ref,
                     o_ref,
                     pad_ref, qs_ref, ks_ref, vs_ref, us_ref, ws_ref, qk_ref, egl_ref, st_ref):
    s = q_ref.shape[0]
    heads = GDN_HEADS
    dk = q_ref.shape[1] // heads
    c64 = GDN_CHUNK
    n_chunks = s // c64
    n_tiles, rows, fill_pad, conv_tile = _seq_tiles(pad_ref, s)

    def l2n(x, scale):
        parts = []
        for h in range(heads):
            xh = x[:, h * dk:(h + 1) * dk]
            inv = lax.rsqrt(jnp.sum(xh * xh, axis=-1, keepdims=True) + EPS)
            parts.append(xh * (inv * scale))
        return jnp.concatenate(parts, axis=1)

    for src_ref, w_ref, dst_ref, post in (
            (q_ref, wq_ref, qs_ref, lambda x: l2n(x, dk ** -0.5)),
            (k_ref, wk_ref, ks_ref, lambda x: l2n(x, 1.0)),
            (v_ref, wv_ref, vs_ref, lambda x: x)):
        fill_pad(lambda rs, src_ref=src_ref: src_ref[rs, :].astype(F32))

        def conv_body(i, c, w_ref=w_ref, dst_ref=dst_ref, post=post):
            dst_ref[rows(i), :] = post(_silu(conv_tile(i, w_ref, GDN_CONV_WIDTH)))
            return c
        lax.fori_loop(0, n_tiles, conv_body, 0)

    ri = lax.broadcasted_iota(jnp.int32, (c64, c64), 0)
    ci = lax.broadcasted_iota(jnp.int32, (c64, c64), 1)
    tril = ri >= ci
    strict = ri > ci
    ltri = tril.astype(F32)

    group = PREP_CHUNKS if n_chunks % PREP_CHUNKS == 0 else 1

    def chunk_prep(cg, carry):
        chunks = []
        for cc in range(group):
            c = cg * group + cc
            rs = pl.ds(pl.multiple_of(c * c64, c64), c64)
            chunks.append((c, rs, ab_ref[rs, :], qs_ref[rs, :], ks_ref[rs, :], vs_ref[rs, :]))
        inst = []
        gcs = [_dot(ltri, -jnp.exp(alog_ref[...]) * _softplus(ab + dtb_ref[...]), precision=HIGHEST)
               for _, _, ab, _, _, _ in chunks]
        for (c, rs, ab, q_all, k_all, v_all), gc in zip(chunks, gcs):
            beta = _sigmoid(ab)
            gct = gc.T
            for h in range(heads):
                hs = slice(h * dk, (h + 1) * dk)
                gcol = gc[:, h:h + 1]
                glast = gc[c64 - 1:c64, h:h + 1]
                bcol = beta[:, heads + h:heads + h + 1]
                decay = jnp.where(tril, jnp.exp(jnp.where(tril, gcol - gct[h:h + 1, :], 0.0)), 0.0)
                kh, qh, vh = k_all[:, hs], q_all[:, hs], v_all[:, hs]
                kb = kh * bcol
                egc = jnp.exp(gcol)
                inst.append(dict(decay=decay, kh=kh, qh=qh, kb=kb, egc=egc,
                                 rhs=jnp.concatenate([vh * bcol, kb * egc], axis=1).astype(BF16),
                                 kd=kh * jnp.exp(glast - gcol),
                                 eg=jnp.broadcast_to(jnp.exp(glast), (SUBLANES, dk))))
        kqs = [_dot_nt(jnp.concatenate([t["kb"], t["qh"]], axis=0).astype(BF16), t["kh"].astype(BF16))
               for t in inst]
        minvs = _unit_lower_inverses([jnp.where(strict, kq[0:c64] * t["decay"], 0.0)
                                      for kq, t in zip(kqs, inst)])
        uws = [_dot(m.astype(BF16), t["rhs"]) for m, t in zip(minvs, inst)]
        for ci, (c, rs, _, _, _, _) in enumerate(chunks):
            sl = slice(ci * heads, (ci + 1) * heads)
            us_ref[rs, :] = jnp.concatenate([uw[:, 0:dk] for uw in uws[sl]], axis=1)
            ws_ref[rs, :] = jnp.concatenate([uw[:, dk:2 * dk] for uw in uws[sl]], axis=1).astype(BF16)
            qs_ref[rs, :] = jnp.concatenate([t["qh"] * t["egc"] for t in inst[sl]], axis=1)
            ks_ref[rs, :] = jnp.concatenate([t["kd"] for t in inst[sl]], axis=1)
            qk_ref[rs, :] = jnp.concatenate([kq[c64:2 * c64] * t["decay"]
                                             for kq, t in zip(kqs[sl], inst[sl])], axis=1).astype(BF16)
            e0 = pl.multiple_of(c * (heads * SUBLANES), heads * SUBLANES)
            egl_ref[pl.ds(e0, heads * SUBLANES), :] = jnp.concatenate([t["eg"] for t in inst[sl]], axis=0)
        return carry
    lax.fori_loop(0, n_chunks // group, chunk_prep, 0)

    st_ref[...] = jnp.zeros(st_ref.shape, F32)

    def scan(c, carry):
        rs = pl.ds(pl.multiple_of(c * c64, c64), c64)
        e0 = pl.multiple_of(c * (heads * SUBLANES), heads * SUBLANES)
        w_all, qg_all, u_all, kd_all = ws_ref[rs, :], qs_ref[rs, :], us_ref[rs, :], ks_ref[rs, :]
        qk_all = qk_ref[rs, :]
        eg_all = egl_ref[pl.ds(e0, heads * SUBLANES), :]
        hsl = [slice(h * dk, (h + 1) * dk) for h in range(heads)]
        sts = [st_ref[h] for h in range(heads)]
        wqs = [_dot(jnp.concatenate([w_all[:, hs], qg_all[:, hs].astype(BF16)], axis=0), st.astype(BF16))
               for hs, st in zip(hsl, sts)]
        vbs = [(u_all[:, hs] - wq[0:c64]).astype(BF16) for hs, wq in zip(hsl, wqs)]
        upd = [_dot_tn(kd_all[:, hs].astype(BF16), vb) for hs, vb in zip(hsl, vbs)]
        intra = [_dot(qk_all[:, h * c64:(h + 1) * c64], vbs[h]) for h in range(heads)]
        for h in range(heads):
            st_ref[h] = sts[h] * eg_all[h * SUBLANES:h * SUBLANES + 1, :] + upd[h]
        vs_ref[rs, :] = jnp.concatenate([wq[c64:2 * c64] + o for wq, o in zip(wqs, intra)], axis=1)
        return carry
    lax.fori_loop(0, n_chunks, scan, 0)

    def finish(i, c):
        o = vs_ref[rows(i), :]
        og = og_ref[rows(i), :].astype(F32)
        parts = []
        for h in range(heads):
            oh = o[:, h * dk:(h + 1) * dk]
            parts.append(_rms(oh, gn_ref[...]))
        y = jnp.concatenate(parts, axis=1) * _silu(og)
        o_ref[rows(i), :] = y.astype(o_ref.dtype)
        return c
    lax.fori_loop(0, n_tiles, finish, 0)


def _l0_mixer(u, ab, gdn_conv, a_log, dt_bias, gdn_norm_g, batch, seq):
    dk = gdn_norm_g.shape[0]
    heads = GDN_HEADS
    cw = heads * dk

    def lane_row(x):
        return jnp.zeros((1, LANES), F32).at[0, :x.shape[0]].set(x)

    in_specs = [pl.BlockSpec((seq, cw), lambda b, j=j: (b, j)) for j in range(3, 7)]
    in_specs += [pl.BlockSpec((seq, LANES), lambda b: (b, 0))]
    in_specs += [pl.BlockSpec((GDN_CONV_WIDTH, cw), lambda b, j=j: (0, j), pipeline_mode=pl.Buffered(1))
                 for j in range(3)]
    in_specs += [_resident((1, LANES)), _resident((1, LANES)), _resident((1, dk))]
    big = pltpu.VMEM((seq, cw), F32)
    scratch = [pltpu.VMEM((seq + TOP_PAD, cw), F32), big, big, big, big, pltpu.VMEM((seq, cw), BF16),
               pltpu.VMEM((seq, heads * GDN_CHUNK), BF16),
               pltpu.VMEM((seq // GDN_CHUNK * heads * SUBLANES, dk), F32),
               pltpu.VMEM((heads, dk, dk), F32)]
    return pl.pallas_call(
        _l0_mixer_kernel, grid=(batch,), in_specs=in_specs,
        out_specs=pl.BlockSpec((seq, cw), lambda b: (b, 0)),
        out_shape=jax.ShapeDtypeStruct((batch * seq, cw), BF16),
        scratch_shapes=scratch, compiler_params=_cparams("parallel"), name="l0_deltanet",
    )(u, u, u, u, ab, gdn_conv, gdn_conv, gdn_conv,
      lane_row(a_log), lane_row(dt_bias), gdn_norm_g.reshape(1, dk))


def _bias_table_kernel(tab_ref, o_ref, *, t):
    h = pl.program_id(0)
    m = pl.program_id(1)
    ri = lax.broadcasted_iota(jnp.int32, (t, t), 0)
    ci = lax.broadcasted_iota(jnp.int32, (t, t), 1)
    rel = m * t + ci - ri
    n = jnp.maximum(rel, 0)
    max_exact = NUM_BUCKETS // 2
    nf = jnp.maximum(n, 1).astype(F32)
    large = max_exact + (jnp.log(nf / max_exact) / math.log(MAX_DISTANCE / max_exact)
                         * (NUM_BUCKETS - max_exact)).astype(jnp.int32)
    large = jnp.minimum(large, NUM_BUCKETS - 1)
    bucket = jnp.where(n < max_exact, n, large)
    bias = jnp.zeros((t, t), F32)
    for b in range(NUM_BUCKETS):
        bias = jnp.where(bucket == b, tab_ref[b, h], bias)
    o_ref[...] = jnp.where(rel >= 0, bias * LOG2E, MASK_VALUE)


def _bias_table(rel_bias, seq, t):
    nb = seq // t
    heads = rel_bias.shape[1]
    return pl.pallas_call(
        functools.partial(_bias_table_kernel, t=t), grid=(heads, nb),
        in_specs=[pl.BlockSpec(memory_space=pltpu.SMEM)],
        out_specs=pl.BlockSpec((None, None, t, t), lambda h, m: (h, m, 0, 0)),
        out_shape=jax.ShapeDtypeStruct((heads, nb, t, t), F32),
        compiler_params=_cparams("parallel", "parallel"), name="rel_bias_table")(rel_bias)


def _attn_kernel(q_ref, k_ref, v_ref, tb_ref, lam_ref, gn_ref, o_ref, m_ref, l_ref, acc_ref,
                 *, t, heads, lambda_init):
    qi = pl.program_id(2)
    dh2 = q_ref.shape[1] // heads
    dh = dh2 // 2
    lane = lax.broadcasted_iota(jnp.int32, (t, dh2), 1)
    qqs = []
    for h in range(heads):
        qf = q_ref[:, h * dh2:(h + 1) * dh2].astype(F32) * (dh ** -0.5 * LOG2E)
        qqs.append(jnp.concatenate([jnp.where(lane < dh, qf, 0.0), jnp.where(lane >= dh, qf, 0.0)],
                                   axis=0).astype(BF16))

    hs = [slice(h * dh2, (h + 1) * dh2) for h in range(heads)]

    def update(j0, nblk):
        ks = pl.ds(pl.multiple_of(j0 * t, t), nblk * t)
        k_all = k_ref[ks, :]
        v_all = v_ref[ks, :]
        s_t = [_dot_nt(k_all[:, hs[h]], qqs[h]) for h in range(heads)]
        ps, alphas = [], []
        for h in range(heads):
            m = m_ref[h, 0:1, :]
            b = jnp.concatenate([tb_ref[h, qi - j0 - i] for i in range(nblk)], axis=0)
            s = jnp.concatenate([s_t[h][:, 0:t] + b, s_t[h][:, t:2 * t] + b], axis=1)
            m_new = jnp.maximum(m, jnp.max(s, axis=0, keepdims=True))
            alpha = jnp.exp2(m - m_new)
            p = jnp.exp2(s - m_new)
            m_ref[h] = jnp.broadcast_to(m_new, m_ref.shape[1:])
            l_ref[h] = jnp.broadcast_to(alpha * l_ref[h, 0:1, :] + jnp.sum(p, axis=0, keepdims=True),
                                        l_ref.shape[1:])
            alphas.append(alpha)
            ps.append(p.astype(BF16))
        pv = [_dot_tn(v_all[:, hs[h]], ps[h]) for h in range(heads)]
        for h in range(heads):
            acc_ref[h] = alphas[h] * acc_ref[h] + pv[h]

    m_ref[...] = jnp.full(m_ref.shape, MASK_VALUE, F32)
    l_ref[...] = jnp.zeros(l_ref.shape, F32)
    acc_ref[...] = jnp.zeros(acc_ref.shape, F32)
    odd = (qi + 1) % 2

    @pl.when(odd == 1)
    def _():
        update(0, 1)

    def pair(i, c):
        update(odd + 2 * i, 2)
        return c
    lax.fori_loop(0, (qi + 1) // 2, pair, 0)
    lp = lam_ref[...]
    lam = (jnp.exp(jnp.sum(lp[0:1] * lp[1:2], axis=-1, keepdims=True))
           - jnp.exp(jnp.sum(lp[2:3] * lp[3:4], axis=-1, keepdims=True)) + lambda_init)
    for h in range(heads):
        o12 = acc_ref[h] / l_ref[h, 0:1, :]
        o = (o12[:, 0:t] - lam * o12[:, t:2 * t]).T
        o_ref[:, h * dh2:(h + 1) * dh2] = (_rms(o, gn_ref[...]) * (1.0 - lambda_init)).astype(o_ref.dtype)


def _diff_attention(u, table, lam_params, norm_g, batch, seq, lambda_init):
    heads = DIFF_HEADS
    dh2 = norm_g.shape[0]
    hg = ATTN_HEADS_PER_STEP
    w = hg * dh2
    ng = heads // hg
    t = table.shape[2]
    nq = seq // t
    return pl.pallas_call(
        functools.partial(_attn_kernel, t=t, heads=hg, lambda_init=lambda_init),
        grid=(batch, ng, nq),
        in_specs=[pl.BlockSpec((t, w), lambda b, g, i: (b * nq + i, g)),
                  pl.BlockSpec((seq, w), lambda b, g, i: (b, ng + g)),
                  pl.BlockSpec((seq, w), lambda b, g, i: (b, 2 * ng + g)),
                  pl.BlockSpec((hg, nq, t, t), lambda b, g, i: (g, 0, 0, 0)),
                  _resident(lam_params.shape), _resident((1, dh2))],
        out_specs=pl.BlockSpec((t, w), lambda b, g, i: (b * nq + i, g)),
        out_shape=jax.ShapeDtypeStruct((batch * seq, heads * dh2), BF16),
        scratch_shapes=[pltpu.VMEM((hg, SUBLANES, 2 * t), F32), pltpu.VMEM((hg, SUBLANES, 2 * t), F32),
                        pltpu.VMEM((hg, dh2, 2 * t), F32)],
        compiler_params=_cparams("parallel", "parallel", "arbitrary"), name="diff_attention",
    )(u, u, u, table, lam_params, norm_g.reshape(1, dh2))


def _conformer_kernel(ga_ref, gb_ref, w_ref, b_ref, lg_ref, lb_ref, o_ref, pad_ref):
    s, c = ga_ref.shape
    rt = min(ELEM_ROWS, s)
    pad_ref[0:TOP_PAD, :] = jnp.zeros((TOP_PAD, c), F32)

    def glu(i, carry):
        r = pl.multiple_of(i * rt, rt)
        rs = pl.ds(r, rt)
        pad_ref[pl.ds(TOP_PAD + r, rt), :] = ga_ref[rs, :].astype(F32) * _sigmoid(gb_ref[rs, :].astype(F32))
        return carry
    lax.fori_loop(0, s // rt, glu, 0)

    ct = CONF_ROWS

    def conv(i, carry):
        r = pl.multiple_of(i * ct, ct)
        tap = _delayed(pad_ref, r, ct, CONF_WIDTH - 1)
        acc = jnp.broadcast_to(b_ref[...], (ct, c))
        for j in range(CONF_WIDTH):
            acc = acc + w_ref[j:j + 1, :] * tap(CONF_WIDTH - 1 - j)
        mu = jnp.mean(acc, axis=-1, keepdims=True)
        xc = acc - mu
        var = jnp.mean(xc * xc, axis=-1, keepdims=True)
        y = xc * lax.rsqrt(var + EPS) * lg_ref[...] + lb_ref[...]
        o_ref[pl.ds(r, ct), :] = _silu(y).astype(o_ref.dtype)
        return carry
    lax.fori_loop(0, s // ct, conv, 0)


def _conformer(u, w, b, ln_g, ln_b, batch, seq, col0):
    c = w.shape[1]
    return pl.pallas_call(
        _conformer_kernel, grid=(batch,),
        in_specs=[pl.BlockSpec((seq, c), lambda i: (i, col0)),
                  pl.BlockSpec((seq, c), lambda i: (i, col0 + 1)),
                  _resident(w.shape), _resident((1, c)), _resident((1, c)), _resident((1, c))],
        out_specs=pl.BlockSpec((seq, c), lambda i: (i, 0)),
        out_shape=jax.ShapeDtypeStruct((batch * seq, c), BF16),
        scratch_shapes=[pltpu.VMEM((seq + TOP_PAD, c), F32)],
        compiler_params=_cparams("parallel"), name="conformer_conv",
    )(u, u, w, b.reshape(1, c), ln_g.reshape(1, c), ln_b.reshape(1, c))


def _proj_router_kernel(y1_ref, y2_ref, w1_ref, w2_ref, h_ref, g_ref, wrt_ref, upper_ref,
                        h_out_ref, hn_ref, route_ref, gate_ref, cnt_ref, carry_ref):
    i = pl.program_id(0)
    tm = h_ref.shape[0]
    n_e = N_EXPERTS

    @pl.when(i == 0)
    def _():
        carry_ref[...] = jnp.zeros(carry_ref.shape, F32)

    h = h_ref[...] + _dot(y1_ref[...], w1_ref[...]) + _dot(y2_ref[...], w2_ref[...])
    h_out_ref[...] = h
    xn = _rms(h, g_ref[...])
    _store_token_tiles(hn_ref, 0, xn)

    logits = _dot_nt(wrt_ref[...], xn.astype(BF16))[0:n_e, :]
    sub = lax.broadcasted_iota(jnp.int32, logits.shape, 0)
    m1 = jnp.max(logits, axis=0, keepdims=True)
    i1 = jnp.min(jnp.where(logits == m1, sub, n_e), axis=0, keepdims=True)
    rest = jnp.where(sub == i1, -jnp.inf, logits)
    m2 = jnp.max(rest, axis=0, keepdims=True)
    i2 = jnp.min(jnp.where(rest == m2, sub, n_e), axis=0, keepdims=True)
    e = jnp.exp(m2 - m1)
    g1 = 1.0 / (1.0 + e)
    oh1 = sub == i1
    oh2 = sub == i2
    both = oh1.astype(F32) + oh2.astype(F32)
    csum = _dot(both.astype(BF16), upper_ref[...])
    carry = carry_ref[:, 0:1]
    before = csum - both + carry
    total = carry + csum[:, tm - 1:tm]
    carry_ref[...] = jnp.broadcast_to(total, carry_ref.shape)
    cnt_ref[...] = jnp.broadcast_to(total, cnt_ref.shape).astype(jnp.int32)
    r1 = jnp.sum(jnp.where(oh1, before, 0.0), axis=0, keepdims=True).astype(jnp.int32)
    r2 = jnp.sum(jnp.where(oh2, before, 0.0), axis=0, keepdims=True).astype(jnp.int32)
    route_ref[...] = jnp.where(sub == 0, i1, jnp.where(sub == 1, i2, jnp.where(sub == 2, r1,
                               jnp.where(sub == 3, r2, 0))))
    grow = jnp.where(sub == 0, g1, jnp.where(sub == 1, e * g1, 0.0))
    sel = (lax.broadcasted_iota(jnp.int32, (n_e, LANES), 0)
           == lax.broadcasted_iota(jnp.int32, (n_e, LANES), 1)).astype(F32)
    gate_ref[...] = lax.dot_general(grow, sel, (((0,), (0,)), ((), ())), precision=HIGHEST,
                                    preferred_element_type=F32)


def _proj_router(y1, y2, w, h, g, wr):
    t, d = h.shape
    k1, k2 = y1.shape[1], y2.shape[1]
    tm = min(ROW_TILE, t)
    wrt = jnp.zeros((LANES, d), BF16).at[:N_EXPERTS].set(wr.T.astype(BF16))
    upper = (jnp.arange(tm)[:, None] <= jnp.arange(tm)[None, :]).astype(BF16)
    row = lambda width: pl.BlockSpec((tm, width), lambda i: (i, 0))
    return pl.pallas_call(
        _proj_router_kernel, grid=(t // tm,),
        in_specs=[row(k1), row(k2), _resident((k1, d)), _resident((k2, d)), row(d),
                  _resident((1, d)), _resident((LANES, d)), _resident((tm, tm))],
        out_specs=[row(d), pl.BlockSpec((tm * d // LANES, LANES), lambda i: (i, 0)),
                   pl.BlockSpec((N_EXPERTS, tm), lambda i: (0, i)), row(LANES),
                   pl.BlockSpec((N_EXPERTS, LANES), lambda i: (0, 0))],
        out_shape=[jax.ShapeDtypeStruct((t, d), F32), jax.ShapeDtypeStruct((t * d // LANES, LANES), F32),
                   jax.ShapeDtypeStruct((N_EXPERTS, t), jnp.int32), jax.ShapeDtypeStruct((t, LANES), F32),
                   jax.ShapeDtypeStruct((N_EXPERTS, LANES), jnp.int32)],
        scratch_shapes=[pltpu.VMEM((N_EXPERTS, LANES), F32)],
        compiler_params=_cparams("arbitrary"), name="l1_out_proj_router",
    )(y1, y2, w[:k1], w[k1:], h, g.reshape(1, d), wrt, upper)


def _dispatch_kernel(dest_hbm, pad_hbm, x_ref, xs_hbm, idx_smem, pad_smem, zero_ref, sem_idx, sem_x):
    i = pl.program_id(0)
    n = x_ref.shape[0] // SUBLANES
    m = TOP_K * n
    cp = pltpu.make_async_copy(dest_hbm.at[i], idx_smem, sem_idx)
    cp.start()
    cp.wait()

    def body(r, c):
        for k in range(TOP_K):
            _token_tile_copy(x_ref, r, xs_hbm, idx_smem[k * n + r], sem_x).start(priority=k % 2)
        return c
    lax.fori_loop(0, n, body, 0, unroll=8)
    for _ in range(TOP_K):
        pltpu.make_async_copy(x_ref, xs_hbm.at[pl.ds(0, n * SUBLANES)], sem_x).wait()

    @pl.when(i == pl.num_programs(0) - 1)
    def _():
        n_pad = pad_smem.shape[0]
        cp = pltpu.make_async_copy(pad_hbm, pad_smem, sem_idx)
        cp.start()
        cp.wait()
        zero_ref[...] = jnp.zeros(zero_ref.shape, F32)

        def fill(j, c):
            for q in range(2):
                _token_tile_copy(zero_ref, 0, xs_hbm, pad_smem[2 * j + q], sem_x).start(priority=q)
            return c
        lax.fori_loop(0, n_pad // 2, fill, 0, unroll=8)
        pltpu.make_async_copy(xs_hbm.at[pl.ds(0, n_pad * SUBLANES)], xs_hbm.at[pl.ds(0, n_pad * SUBLANES)],
                              sem_x).wait()


def _dispatch(hn_tiles, dest, pad_rows, t, d):
    n = min(MOE_ROWS, t)
    steps = t // n
    tpt = d // LANES
    p = t * TOP_K + N_EXPERTS * MOE_ROWS
    dest_steps = jnp.concatenate([dest[k].reshape(steps, n) for k in range(TOP_K)], axis=1)
    return pl.pallas_call(
        _dispatch_kernel, grid=(steps,),
        in_specs=[pl.BlockSpec(memory_space=pl.ANY), pl.BlockSpec(memory_space=pl.ANY),
                  pl.BlockSpec((n * tpt, LANES), lambda i: (i, 0))],
        out_specs=pl.BlockSpec(memory_space=pl.ANY),
        out_shape=jax.ShapeDtypeStruct((p * tpt, LANES), F32),
        scratch_shapes=[pltpu.SMEM((TOP_K * n,), jnp.int32), pltpu.SMEM(pad_rows.shape, jnp.int32),
                        pltpu.VMEM((tpt, LANES), F32), pltpu.SemaphoreType.DMA, pltpu.SemaphoreType.DMA],
        compiler_params=_cparams("arbitrary"), name="moe_dispatch")(dest_steps, pad_rows, hn_tiles)


def _dispatch_plan(route, counts, t):
    tm = MOE_ROWS
    p = t * TOP_K + N_EXPERTS * tm
    padded = (counts + tm - 1) // tm * tm
    pend = jnp.cumsum(padded)
    pstart = pend - padded
    experts = jnp.arange(N_EXPERTS, dtype=jnp.int32)[:, None]
    dest = jnp.stack([jnp.sum(jnp.where(route[k][None, :] == experts, pstart[:, None], 0), axis=0)
                      + route[TOP_K + k] for k in range(TOP_K)])
    n_pad = p - t * TOP_K
    gaps = jnp.concatenate([padded - counts, (p - pend[-1])[None]])
    gap_end = jnp.cumsum(gaps)
    gap_row0 = jnp.concatenate([pstart + counts, pend[-1:]])
    j = jnp.arange(n_pad, dtype=jnp.int32)
    which = jnp.searchsorted(gap_end, j, side='right')
    pad_rows = (gap_row0[which] + j - (gap_end - gaps)[which]).astype(jnp.int32)
    nb = p // tm
    blk_e = jnp.minimum(jnp.searchsorted(pend, jnp.arange(nb, dtype=jnp.int32) * tm, side='right'),
                        N_EXPERTS - 1).astype(jnp.int32)
    return dest.astype(jnp.int32), pad_rows, blk_e, (pend[-1:] // tm).astype(jnp.int32)


def _experts_kernel(blk_e_ref, used_ref, x_ref, wgu_ref, wd_ref, o_ref, hid_ref):
    f = wd_ref.shape[0]
    tm = hid_ref.shape[0]
    live = pl.program_id(0) < used_ref[0]

    @pl.when(live)
    def _():
        x = _load_token_tiles(x_ref, 0, tm).astype(BF16)
        for c in range(0, f, MOE_SUB_TILE):
            hid_ref[:, c:c + MOE_SUB_TILE] = (
                _silu(_dot(x, wgu_ref[:, c:c + MOE_SUB_TILE]))
                * _dot(x, wgu_ref[:, f + c:f + c + MOE_SUB_TILE])).astype(BF16)
        _store_token_tiles(o_ref, 0, _dot(hid_ref[...], wd_ref[...]))

    @pl.when(jnp.logical_not(live))
    def _():
        o_ref[...] = jnp.zeros(o_ref.shape, F32)


def _experts(xs, blk_e, n_used, w_gate_up, w_down):
    f, d = w_down.shape[1], w_down.shape[2]
    tm = MOE_ROWS
    tpt = d // LANES
    assert f % MOE_SUB_TILE == 0
    grid_spec = pltpu.PrefetchScalarGridSpec(
        num_scalar_prefetch=2, grid=(blk_e.shape[0],),
        in_specs=[pl.BlockSpec((tm * tpt, LANES), lambda i, e, u: (i, 0)),
                  pl.BlockSpec((None, d, 2 * f), lambda i, e, u: (e[i], 0, 0), pipeline_mode=pl.Buffered(1)),
                  pl.BlockSpec((None, f, d), lambda i, e, u: (e[i], 0, 0), pipeline_mode=pl.Buffered(1))],
        out_specs=pl.BlockSpec((tm * tpt, LANES), lambda i, e, u: (i, 0)),
        scratch_shapes=[pltpu.VMEM((tm, f), BF16)])
    return pl.pallas_call(
        _experts_kernel, grid_spec=grid_spec, out_shape=jax.ShapeDtypeStruct(xs.shape, F32),
        compiler_params=_cparams("parallel"), name="moe_experts",
    )(blk_e, n_used, xs, w_gate_up, w_down)


def _combine_kernel(dest_hbm, y_hbm, gate_ref, h_ref, g_ref, p_ref, wp_ref, wg_ref, fg_ref, o_ref,
                    idx0, idx1, ybuf, sem_idx, sem_rows):
    i = pl.program_id(0)
    steps = pl.num_programs(0)
    n = h_ref.shape[0]
    m = TOP_K * n
    slot = i % 2
    idxs = (idx0, idx1)

    def idx_copy(blk, s):
        return pltpu.make_async_copy(dest_hbm.at[blk], idxs[s], sem_idx.at[s])

    def issue_tiles(s):
        def body(r, c):
            for k in range(TOP_K):
                _token_tile_copy(y_hbm, idxs[s][k * n + r], ybuf, s * m + k * n + r,
                                 sem_rows.at[s]).start(priority=k % 2)
            return c
        lax.fori_loop(0, n, body, 0, unroll=8)

    @pl.when(i == 0)
    def _():
        idx_copy(0, 0).start()
        idx_copy(0, 0).wait()
        idx_copy(1, 1).start()
        issue_tiles(0)

    for s in range(2):
        @pl.when(slot == s)
        def _(s=s):
            idx_copy(i + 1, 1 - s).wait()

            @pl.when(i + 2 <= steps)
            def _():
                idx_copy(i + 2, s).start()

            issue_tiles(1 - s)
            _token_tiles_wait(y_hbm, ybuf, s * m, m, sem_rows.at[s])

    nc = n // COMBINE_CHUNKS
    rows = [pl.ds(c * nc, nc) for c in range(COMBINE_CHUNKS)]
    xs = []
    for c, rs in enumerate(rows):
        gates = gate_ref[rs, :]
        xs.append(h_ref[rs, :] + gates[:, 0:1] * _load_token_tiles(ybuf, slot * m + c * nc, nc)
                  + gates[:, 1:2] * _load_token_tiles(ybuf, slot * m + n + c * nc, nc))
    xns = [_rms(x, g_ref[...]).astype(BF16) for x in xs]
    gate = [_sigmoid(_dot(xn, wg_ref[...])) for xn in xns]
    emb = [_dot(p_ref[rs, :].astype(BF16), wp_ref[...]) for rs in rows]
    for rs, x, e, g in zip(rows, xs, emb, gate):
        o_ref[rs, :] = _rms(x + e * g, fg_ref[...])

    @pl.when(i == steps - 1)
    def _():
        _token_tiles_wait(y_hbm, ybuf, (1 - slot) * m, m, sem_rows.at[1 - slot])


def _combine_ple_final(dest, y_tiles, gates, h, g, p, layer, wp, wg, final_g):
    t, d = h.shape
    e = p.shape[2]
    n = min(COMBINE_ROWS, t)
    m = TOP_K * n
    steps = t // n
    dest_steps = jnp.concatenate([dest[k].reshape(steps, n) for k in range(TOP_K)], axis=1)
    dest_steps = jnp.concatenate([dest_steps, jnp.zeros((1, m), jnp.int32)], axis=0)
    return pl.pallas_call(
        _combine_kernel, grid=(steps,),
        in_specs=[pl.BlockSpec(memory_space=pl.ANY), pl.BlockSpec(memory_space=pl.ANY),
                  pl.BlockSpec((n, LANES), lambda i: (i, 0)),
                  pl.BlockSpec((n, d), lambda i: (i, 0)), _resident((1, d)),
                  pl.BlockSpec((None, n, e), lambda i: (layer, i, 0)), _resident((e, d)), _resident((d, d)),
                  _resident((1, d))],
        out_specs=pl.BlockSpec((n, d), lambda i: (i, 0)),
        out_shape=jax.ShapeDtypeStruct((t, d), F32),
        scratch_shapes=[pltpu.SMEM((m,), jnp.int32), pltpu.SMEM((m,), jnp.int32),
                        pltpu.VMEM((2 * m * d // LANES, LANES), F32),
                        pltpu.SemaphoreType.DMA((2,)), pltpu.SemaphoreType.DMA((2,))],
        compiler_params=_cparams("arbitrary"), name="moe_combine_ple_final",
    )(dest_steps, y_tiles, gates, h, g.reshape(1, d), p, wp, wg, final_g.reshape(1, d))


def kernel(x, p, norm_mix_g, norm_ffn_g, norm_ple_g, final_norm_g, ev_w_in, ev_conv_a, ev_gdn_conv, ev_gdn_A_log, ev_gdn_dt_bias, ev_gdn_norm_g, ev_w_out, od_w_in, od_lambda, od_diff_norm_g, od_conf_dw_w, od_conf_dw_b, od_conf_ln_g, od_conf_ln_b, od_w_out, rel_bias, ffn_w_gate_up, ffn_w_down, moe_router, moe_w_gate_up, moe_w_down, ple_w_proj, ple_w_gate):
    batch, seq, d = x.shape
    t = batch * seq
    depth = p.shape[0]
    assert depth == 2 and seq % GDN_CHUNK == 0
    h = x.reshape(t, d)
    pf = p.reshape(depth, t, p.shape[-1])

    heads = GDN_HEADS
    n_main = ev_w_in.shape[2] - 2 * heads
    w_in = ev_w_in[0]
    w_ab = jnp.zeros((d, LANES), BF16).at[:, :2 * heads].set(w_in[:, n_main:].astype(BF16))
    u, ab = _norm_proj(h, norm_mix_g[0], w_in[:, :n_main].astype(BF16), w_ab, name="l0_in_proj")
    ya = _gated_conv(u, ev_conv_a[0], batch, seq)
    yb = _l0_mixer(u, ab, ev_gdn_conv[0], ev_gdn_A_log[0], ev_gdn_dt_bias[0], ev_gdn_norm_g[0], batch, seq)
    h = _proj_residual([ya, yb], ev_w_out[0].astype(BF16), h, name="l0_out_proj")
    f = ffn_w_down.shape[1]
    lambda_init = 0.8 - 0.6 * math.exp(-0.3 * 1)
    h, u = _ffn_ple_proj(h, norm_ffn_g[0], ffn_w_gate_up[0, :, :f].astype(BF16),
                         ffn_w_gate_up[0, :, f:].astype(BF16), ffn_w_down[0].astype(BF16),
                         norm_ple_g[0], pf, 0, ple_w_proj[0].astype(BF16), ple_w_gate[0].astype(BF16),
                         norm_mix_g[1], od_w_in[0].astype(BF16), name="l0_ffn_ple_l1_in_proj")

    table = _bias_table(rel_bias, seq, min(ATTN_TILE, seq))
    o_attn = _diff_attention(u, table, od_lambda[0], od_diff_norm_g[0], batch, seq, lambda_init)
    c_conf = od_conf_dw_w.shape[2]
    o_conf = _conformer(u, od_conf_dw_w[0], od_conf_dw_b[0], od_conf_ln_g[0], od_conf_ln_b[0],
                        batch, seq, 3 * DIFF_HEADS * od_diff_norm_g.shape[1] // c_conf)
    h, hn, route, gates, counts = _proj_router(o_attn, o_conf, od_w_out[0].astype(BF16), h, norm_ffn_g[1],
                                               moe_router[0])
    dest, pad_rows, blk_e, n_used = _dispatch_plan(route, counts[:, 0], t)
    xs = _dispatch(hn, dest, pad_rows, t, d)
    y = _experts(xs, blk_e, n_used, moe_w_gate_up[0].astype(BF16), moe_w_down[0].astype(BF16))
    out = _combine_ple_final(dest, y, gates, h, norm_ple_g[1], pf, 1, ple_w_proj[1].astype(BF16),
                             ple_w_gate[1].astype(BF16), final_norm_g)
    return out.reshape(batch, seq, d)
```

```python
import functools
import math

import jax
import jax.numpy as jnp
from jax import lax
from jax.experimental import pallas as pl
from jax.experimental.pallas import tpu as pltpu

F32 = jnp.float32
BF16 = jnp.bfloat16
HIGHEST = lax.Precision.HIGHEST

EPS = 1e-6
CONV_A_WIDTH = 3
GDN_HEADS = 4
GDN_CONV_WIDTH = 4
GDN_CHUNK = 64
DIFF_HEADS = 4
NUM_BUCKETS = 32
MAX_DISTANCE = 128
CONF_WIDTH = 31
N_EXPERTS = 8
TOP_K = 2

LANES = 128
SUBLANES = 8
VMEM_LIMIT_BYTES = 56 * 1024 * 1024
MASK_VALUE = -1e30
LOG2E = math.log2(math.e)

ROW_TILE = 512
COL_TILE = 512
FFN_CHUNKS = 2
ELEM_ROWS = 256
CONF_ROWS = 128
PREP_CHUNKS = 4
ATTN_TILE = 256
ATTN_HEADS_PER_STEP = 4
MOE_ROWS = 1024
MOE_SUB_TILE = 256
COMBINE_ROWS = 512
COMBINE_CHUNKS = 4
TOP_PAD = 32


def _cparams(*sem):
    return pltpu.CompilerParams(dimension_semantics=sem, vmem_limit_bytes=VMEM_LIMIT_BYTES)


def _resident(shape):
    nd = len(shape)
    return pl.BlockSpec(shape, lambda *_: (0,) * nd, pipeline_mode=pl.Buffered(1))


def _rms(x, g):
    return x * lax.rsqrt(jnp.mean(x * x, axis=-1, keepdims=True) + EPS) * g


def _sigmoid(x):
    return jax.nn.sigmoid(x)


def _silu(x):
    return x * jax.nn.sigmoid(x)


def _softplus(x):
    return jnp.maximum(x, 0.0) + jnp.log1p(jnp.exp(-jnp.abs(x)))


def _dot(a, b, **kw):
    return jnp.dot(a, b, preferred_element_type=F32, **kw)


def _dot_nt(a, b):
    return lax.dot_general(a, b, (((1,), (1,)), ((), ())), preferred_element_type=F32)


def _dot_tn(a, b):
    return lax.dot_general(a, b, (((0,), (0,)), ((), ())), preferred_element_type=F32)


def _delayed(pad_ref, r, rows, max_delay):
    lead = -(-max_delay // SUBLANES) * SUBLANES
    win = pad_ref[pl.ds(TOP_PAD + r - lead, rows + lead), :]
    rolled = {0: win}

    def tap(d):
        a, b = divmod(d, SUBLANES)
        if b not in rolled:
            rolled[b] = pltpu.roll(win, b, 0)
        start = lead - SUBLANES * a
        return rolled[b][start:start + rows, :]
    return tap


def _load_token_tiles(ref, first_token, n):
    return jnp.concatenate([ref[pl.ds(first_token * SUBLANES + s, n, stride=SUBLANES), :]
                            for s in range(SUBLANES)], axis=1)


def _store_token_tiles(ref, first_token, x):
    n = x.shape[0]
    for s in range(SUBLANES):
        ref[pl.ds(first_token * SUBLANES + s, n, stride=SUBLANES), :] = x[:, s * LANES:(s + 1) * LANES]


def _token_tile_copy(src_hbm, src_token, dst_vmem, dst_token, sem):
    return pltpu.make_async_copy(src_hbm.at[pl.ds(pl.multiple_of(src_token * SUBLANES, SUBLANES), SUBLANES)],
                                 dst_vmem.at[pl.ds(pl.multiple_of(dst_token * SUBLANES, SUBLANES), SUBLANES)],
                                 sem)


def _token_tiles_wait(src_hbm, dst_vmem, first_token, n, sem):
    pltpu.make_async_copy(src_hbm.at[pl.ds(0, n * SUBLANES)],
                          dst_vmem.at[pl.ds(pl.multiple_of(first_token * SUBLANES, SUBLANES), n * SUBLANES)],
                          sem).wait()


def _norm_proj_kernel(h_ref, g_ref, w_ref, *rest, tn, with_aux):
    xn = _rms(h_ref[...], g_ref[...]).astype(BF16)
    if with_aux:
        w2_ref, o_ref, o2_ref = rest
        o2_ref[...] = _dot(xn, w2_ref[...])
    else:
        (o_ref,) = rest
    n = w_ref.shape[1]
    for c in range(0, n, tn):
        o_ref[:, c:c + tn] = _dot(xn, w_ref[:, c:c + tn]).astype(o_ref.dtype)


def _norm_proj(h, g, w, w_aux=None, *, name):
    t, d = h.shape
    n = w.shape[1]
    tm = min(ROW_TILE, t)
    tn = COL_TILE if n % COL_TILE == 0 else n
    in_specs = [pl.BlockSpec((tm, d), lambda i: (i, 0)), _resident((1, d)), _resident((d, n))]
    out_shape = [jax.ShapeDtypeStruct((t, n), BF16)]
    out_specs = [pl.BlockSpec((tm, n), lambda i: (i, 0))]
    args = [h, g.reshape(1, d), w]
    if w_aux is not None:
        in_specs.append(_resident(w_aux.shape))
        out_shape.append(jax.ShapeDtypeStruct((t, w_aux.shape[1]), F32))
        out_specs.append(pl.BlockSpec((tm, w_aux.shape[1]), lambda i: (i, 0)))
        args.append(w_aux)
    out = pl.pallas_call(
        functools.partial(_norm_proj_kernel, tn=tn, with_aux=w_aux is not None),
        grid=(t // tm,), in_specs=in_specs, out_specs=out_specs, out_shape=out_shape,
        compiler_params=_cparams("parallel"), name=name)(*args)
    return out if w_aux is not None else out[0]


def _proj_residual_kernel(*refs):
    n = (len(refs) - 2) // 2
    y_refs, w_refs, h_ref, o_ref = refs[:n], refs[n:2 * n], refs[2 * n], refs[2 * n + 1]
    acc = h_ref[...]
    for y_ref, w_ref in zip(y_refs, w_refs):
        acc = acc + _dot(y_ref[...], w_ref[...])
    o_ref[...] = acc


def _proj_residual(ys, w, h, *, name):
    t, d = h.shape
    tm = min(ROW_TILE, t)
    ws, r0 = [], 0
    for y in ys:
        ws.append(w[r0:r0 + y.shape[1]])
        r0 += y.shape[1]
    return pl.pallas_call(
        _proj_residual_kernel, grid=(t // tm,),
        in_specs=([pl.BlockSpec((tm, y.shape[1]), lambda i: (i, 0)) for y in ys]
                  + [_resident(wi.shape) for wi in ws] + [pl.BlockSpec((tm, d), lambda i: (i, 0))]),
        out_specs=pl.BlockSpec((tm, d), lambda i: (i, 0)),
        out_shape=jax.ShapeDtypeStruct((t, d), F32),
        compiler_params=_cparams("parallel"), name=name)(*ys, *ws, h)


def _ffn_ple_proj_kernel(h_ref, gf_ref, wg_ref, wu_ref, wd_ref, gp_ref, p_ref, wp_ref, wpg_ref,
                         gm_ref, win_ref, h_out_ref, u_ref, *, tf, tn):
    tm = h_ref.shape[0]
    nc = tm // FFN_CHUNKS
    rows = [pl.ds(c * nc, nc) for c in range(FFN_CHUNKS)]
    xs = [h_ref[rs, :] for rs in rows]
    xns = [_rms(x, gf_ref[...]).astype(BF16) for x in xs]
    accs = list(xs)
    for c in range(0, wg_ref.shape[1], tf):
        gates = [_dot(xn, wg_ref[:, c:c + tf]) for xn in xns]
        ups = [_dot(xn, wu_ref[:, c:c + tf]) for xn in xns]
        hids = [(_silu(g) * u).astype(BF16) for g, u in zip(gates, ups)]
        accs = [a + _dot(hd, wd_ref[c:c + tf, :]) for a, hd in zip(accs, hids)]
    xn2 = [_rms(x, gp_ref[...]).astype(BF16) for x in accs]
    pgate = [_sigmoid(_dot(xn, wpg_ref[...])) for xn in xn2]
    emb = [_dot(p_ref[rs, :].astype(BF16), wp_ref[...]) for rs in rows]
    h3 = [x + e * g for x, e, g in zip(accs, emb, pgate)]
    for rs, x in zip(rows, h3):
        h_out_ref[rs, :] = x
    xn3 = [_rms(x, gm_ref[...]).astype(BF16) for x in h3]
    for c in range(0, win_ref.shape[1], tn):
        for rs, xn in zip(rows, xn3):
            u_ref[rs, c:c + tn] = _dot(xn, win_ref[:, c:c + tn]).astype(u_ref.dtype)


def _ff_tile(f, cap):
    best = LANES
    for c in range(LANES, cap + 1, LANES):
        if f % c == 0:
            best = c
    return best


def _ffn_ple_proj(h, g_ffn, wg, wu, wd, g_ple, p, layer, wp, wpg, g_mix, w_in, *, name):
    t, d = h.shape
    f = wg.shape[1]
    e = p.shape[2]
    n = w_in.shape[1]
    tm = min(ROW_TILE, t)
    tn = COL_TILE if n % COL_TILE == 0 else n
    row = lambda w: pl.BlockSpec((tm, w), lambda i: (i, 0))
    return pl.pallas_call(
        functools.partial(_ffn_ple_proj_kernel, tf=_ff_tile(f, 1536), tn=tn), grid=(t // tm,),
        in_specs=[row(d), _resident((1, d)), _resident((d, f)), _resident((d, f)), _resident((f, d)),
                  _resident((1, d)), pl.BlockSpec((None, tm, e), lambda i: (layer, i, 0)),
                  _resident((e, d)), _resident((d, d)), _resident((1, d)), _resident((d, n))],
        out_specs=[row(d), row(n)],
        out_shape=[jax.ShapeDtypeStruct((t, d), F32), jax.ShapeDtypeStruct((t, n), BF16)],
        compiler_params=_cparams("parallel"), name=name,
    )(h, g_ffn.reshape(1, d), wg, wu, wd, g_ple.reshape(1, d), p, wp, wpg, g_mix.reshape(1, d), w_in)


def _unit_lower_inverses(mats):
    n = mats[0].shape[0]
    row = lax.broadcasted_iota(jnp.int32, (n, n), 0)
    col = lax.broadcasted_iota(jnp.int32, (n, n), 1)
    eye = (row == col).astype(F32)
    same16 = (row // 16) == (col // 16)
    same32 = (row // 32) == (col // 32)
    off32 = jnp.logical_and(same32, jnp.logical_not(same16))

    def mm(ps, qs):
        return [_dot(p.astype(BF16), q.astype(BF16)) for p, q in zip(ps, qs)]

    ad = [jnp.where(same16, a, 0.0) for a in mats]
    a2 = mm(ad, ad)
    x = mm([eye - t for t in ad], [eye + t for t in a2])
    a4 = mm(a2, a2)
    x = mm(x, [eye + t for t in a4])
    a8 = mm(a4, a4)
    x = mm(x, [eye + t for t in a8])
    y = mm([jnp.where(off32, a, 0.0) for a in mats], x)
    x = [t - c for t, c in zip(x, mm(x, y))]
    y = mm([jnp.where(same32, 0.0, a) for a in mats], x)
    return [t - c for t, c in zip(x, mm(x, y))]


def _seq_tiles(pad_ref, s):
    rt = min(ELEM_ROWS, s)
    n_tiles = s // rt

    def rows(i):
        return pl.ds(pl.multiple_of(i * rt, rt), rt)

    def fill_pad(fn):
        pad_ref[0:TOP_PAD, :] = jnp.zeros((TOP_PAD, pad_ref.shape[1]), F32)

        def body(i, c):
            r = pl.multiple_of(i * rt, rt)
            pad_ref[pl.ds(TOP_PAD + r, rt), :] = fn(rows(i))
            return c
        lax.fori_loop(0, n_tiles, body, 0)

    def conv_tile(i, w_ref, width):
        tap = _delayed(pad_ref, pl.multiple_of(i * rt, rt), rt, width - 1)
        acc = None
        for j in range(width):
            term = w_ref[j:j + 1, :] * tap(width - 1 - j)
            acc = term if acc is None else acc + term
        return acc

    return n_tiles, rows, fill_pad, conv_tile


def _gated_conv_kernel(bg_ref, cg_ref, xin_ref, w_ref, o_ref, pad_ref):
    n_tiles, rows, fill_pad, conv_tile = _seq_tiles(pad_ref, bg_ref.shape[0])
    fill_pad(lambda rs: cg_ref[rs, :].astype(F32) * xin_ref[rs, :].astype(F32))

    def body(i, c):
        acc = conv_tile(i, w_ref, CONV_A_WIDTH)
        o_ref[rows(i), :] = (bg_ref[rows(i), :].astype(F32) * acc).astype(o_ref.dtype)
        return c
    lax.fori_loop(0, n_tiles, body, 0)


def _gated_conv(u, conv_a, batch, seq):
    cw = conv_a.shape[1]
    return pl.pallas_call(
        _gated_conv_kernel, grid=(batch,),
        in_specs=[pl.BlockSpec((seq, cw), lambda b, j=j: (b, j)) for j in range(3)] + [_resident(conv_a.shape)],
        out_specs=pl.BlockSpec((seq, cw), lambda b: (b, 0)),
        out_shape=jax.ShapeDtypeStruct((batch * seq, cw), BF16),
        scratch_shapes=[pltpu.VMEM((seq + TOP_PAD, cw), F32)],
        compiler_params=_cparams("parallel"), name="l0_gated_conv")(u, u, u, conv_a)


def _l0_mixer_kernel(q_ref, k_ref, v_ref, og_ref, ab_ref, wq_ref, wk_ref, wv_ref, alog_ref, dtb_ref, gn_ref,
                     o_ref,
                     pad_ref, qs_ref, ks_ref, vs_ref, us_ref, ws_ref, qk_ref, egl_ref, st_ref):
    s = q_ref.shape[0]
    heads = GDN_HEADS
    dk = q_ref.shape[1] // heads
    c64 = GDN_CHUNK
    n_chunks = s // c64
    n_tiles, rows, fill_pad, conv_tile = _seq_tiles(pad_ref, s)

    def l2n(x, scale):
        parts = []
        for h in range(heads):
            xh = x[:, h * dk:(h + 1) * dk]
            inv = lax.rsqrt(jnp.sum(xh * xh, axis=-1, keepdims=True) + EPS)
            parts.append(xh * (inv * scale))
        return jnp.concatenate(parts, axis=1)

    for src_ref, w_ref, dst_ref, post in (
            (q_ref, wq_ref, qs_ref, lambda x: l2n(x, dk ** -0.5)),
            (k_ref, wk_ref, ks_ref, lambda x: l2n(x, 1.0)),
            (v_ref, wv_ref, vs_ref, lambda x: x)):
        fill_pad(lambda rs, src_ref=src_ref: src_ref[rs, :].astype(F32))

        def conv_body(i, c, w_ref=w_ref, dst_ref=dst_ref, post=post):
            dst_ref[rows(i), :] = post(_silu(conv_tile(i, w_ref, GDN_CONV_WIDTH)))
            return c
        lax.fori_loop(0, n_tiles, conv_body, 0)

    ri = lax.broadcasted_iota(jnp.int32, (c64, c64), 0)
    ci = lax.broadcasted_iota(jnp.int32, (c64, c64), 1)
    tril = ri >= ci
    strict = ri > ci
    ltri = tril.astype(F32)

    group = PREP_CHUNKS if n_chunks % PREP_CHUNKS == 0 else 1

    def chunk_prep(cg, carry):
        chunks = []
        for cc in range(group):
            c = cg * group + cc
            rs = pl.ds(pl.multiple_of(c * c64, c64), c64)
            chunks.append((c, rs, ab_ref[rs, :], qs_ref[rs, :], ks_ref[rs, :], vs_ref[rs, :]))
        inst = []
        gcs = [_dot(ltri, -jnp.exp(alog_ref[...]) * _softplus(ab + dtb_ref[...]), precision=HIGHEST)
               for _, _, ab, _, _, _ in chunks]
        for (c, rs, ab, q_all, k_all, v_all), gc in zip(chunks, gcs):
            beta = _sigmoid(ab)
            gct = gc.T
            for h in range(heads):
                hs = slice(h * dk, (h + 1) * dk)
                gcol = gc[:, h:h + 1]
                glast = gc[c64 - 1:c64, h:h + 1]
                bcol = beta[:, heads + h:heads + h + 1]
                decay = jnp.where(tril, jnp.exp(jnp.where(tril, gcol - gct[h:h + 1, :], 0.0)), 0.0)
                kh, qh, vh = k_all[:, hs], q_all[:, hs], v_all[:, hs]
                kb = kh * bcol
                egc = jnp.exp(gcol)
                inst.append(dict(decay=decay, kh=kh, qh=qh, kb=kb, egc=egc,
                                 rhs=jnp.concatenate([vh * bcol, kb * egc], axis=1).astype(BF16),
                                 kd=kh * jnp.exp(glast - gcol),
                                 eg=jnp.broadcast_to(jnp.exp(glast), (SUBLANES, dk))))
        kqs = [_dot_nt(jnp.concatenate([t["kb"], t["qh"]], axis=0).astype(BF16), t["kh"].astype(BF16))
               for t in inst]
        minvs = _unit_lower_inverses([jnp.where(strict, kq[0:c64] * t["decay"], 0.0)
                                      for kq, t in zip(kqs, inst)])
        uws = [_dot(m.astype(BF16), t["rhs"]) for m, t in zip(minvs, inst)]
        for ci, (c, rs, _, _, _, _) in enumerate(chunks):
            sl = slice(ci * heads, (ci + 1) * heads)
            us_ref[rs, :] = jnp.concatenate([uw[:, 0:dk] for uw in uws[sl]], axis=1)
            ws_ref[rs, :] = jnp.concatenate([uw[:, dk:2 * dk] for uw in uws[sl]], axis=1).astype(BF16)
            qs_ref[rs, :] = jnp.concatenate([t["qh"] * t["egc"] for t in inst[sl]], axis=1)
            ks_ref[rs, :] = jnp.concatenate([t["kd"] for t in inst[sl]], axis=1)
            qk_ref[rs, :] = jnp.concatenate([kq[c64:2 * c64] * t["decay"]
                                             for kq, t in zip(kqs[sl], inst[sl])], axis=1).astype(BF16)
            e0 = pl.multiple_of(c * (heads * SUBLANES), heads * SUBLANES)
            egl_ref[pl.ds(e0, heads * SUBLANES), :] = jnp.concatenate([t["eg"] for t in inst[sl]], axis=0)
        return carry
    lax.fori_loop(0, n_chunks // group, chunk_prep, 0)

    st_ref[...] = jnp.zeros(st_ref.shape, F32)

    def scan(c, carry):
        rs = pl.ds(pl.multiple_of(c * c64, c64), c64)
        e0 = pl.multiple_of(c * (heads * SUBLANES), heads * SUBLANES)
        w_all, qg_all, u_all, kd_all = ws_ref[rs, :], qs_ref[rs, :], us_ref[rs, :], ks_ref[rs, :]
        qk_all = qk_ref[rs, :]
        eg_all = egl_ref[pl.ds(e0, heads * SUBLANES), :]
        hsl = [slice(h * dk, (h + 1) * dk) for h in range(heads)]
        sts = [st_ref[h] for h in range(heads)]
        wqs = [_dot(jnp.concatenate([w_all[:, hs], qg_all[:, hs].astype(BF16)], axis=0), st.astype(BF16))
               for hs, st in zip(hsl, sts)]
        vbs = [(u_all[:, hs] - wq[0:c64]).astype(BF16) for hs, wq in zip(hsl, wqs)]
        upd = [_dot_tn(kd_all[:, hs].astype(BF16), vb) for hs, vb in zip(hsl, vbs)]
        intra = [_dot(qk_all[:, h * c64:(h + 1) * c64], vbs[h]) for h in range(heads)]
        for h in range(heads):
            st_ref[h] = sts[h] * eg_all[h * SUBLANES:h * SUBLANES + 1, :] + upd[h]
        vs_ref[rs, :] = jnp.concatenate([wq[c64:2 * c64] + o for wq, o in zip(wqs, intra)], axis=1)
        return carry
    lax.fori_loop(0, n_chunks, scan, 0)

    def finish(i, c):
        o = vs_ref[rows(i), :]
        og = og_ref[rows(i), :].astype(F32)
        parts = []
        for h in range(heads):
            oh = o[:, h * dk:(h + 1) * dk]
            parts.append(_rms(oh, gn_ref[...]))
        y = jnp.concatenate(parts, axis=1) * _silu(og)
        o_ref[rows(i), :] = y.astype(o_ref.dtype)
        return c
    lax.fori_loop(0, n_tiles, finish, 0)


def _l0_mixer(u, ab, gdn_conv, a_log, dt_bias, gdn_norm_g, batch, seq):
    dk = gdn_norm_g.shape[0]
    heads = GDN_HEADS
    cw = heads * dk

    def lane_row(x):
        return jnp.zeros((1, LANES), F32).at[0, :x.shape[0]].set(x)

    in_specs = [pl.BlockSpec((seq, cw), lambda b, j=j: (b, j)) for j in range(3, 7)]
    in_specs += [pl.BlockSpec((seq, LANES), lambda b: (b, 0))]
    in_specs += [pl.BlockSpec((GDN_CONV_WIDTH, cw), lambda b, j=j: (0, j), pipeline_mode=pl.Buffered(1))
                 for j in range(3)]
    in_specs += [_resident((1, LANES)), _resident((1, LANES)), _resident((1, dk))]
    big = pltpu.VMEM((seq, cw), F32)
    scratch = [pltpu.VMEM((seq + TOP_PAD, cw), F32), big, big, big, big, pltpu.VMEM((seq, cw), BF16),
               pltpu.VMEM((seq, heads * GDN_CHUNK), BF16),
               pltpu.VMEM((seq // GDN_CHUNK * heads * SUBLANES, dk), F32),
               pltpu.VMEM((heads, dk, dk), F32)]
    return pl.pallas_call(
        _l0_mixer_kernel, grid=(batch,), in_specs=in_specs,
        out_specs=pl.BlockSpec((seq, cw), lambda b: (b, 0)),
        out_shape=jax.ShapeDtypeStruct((batch * seq, cw), BF16),
        scratch_shapes=scratch, compiler_params=_cparams("parallel"), name="l0_deltanet",
    )(u, u, u, u, ab, gdn_conv, gdn_conv, gdn_conv,
      lane_row(a_log), lane_row(dt_bias), gdn_norm_g.reshape(1, dk))


def _bias_table_kernel(tab_ref, o_ref, *, t):
    h = pl.program_id(0)
    m = pl.program_id(1)
    ri = lax.broadcasted_iota(jnp.int32, (t, t), 0)
    ci = lax.broadcasted_iota(jnp.int32, (t, t), 1)
    rel = m * t + ci - ri
    n = jnp.maximum(rel, 0)
    max_exact = NUM_BUCKETS // 2
    nf = jnp.maximum(n, 1).astype(F32)
    large = max_exact + (jnp.log(nf / max_exact) / math.log(MAX_DISTANCE / max_exact)
                         * (NUM_BUCKETS - max_exact)).astype(jnp.int32)
    large = jnp.minimum(large, NUM_BUCKETS - 1)
    bucket = jnp.where(n < max_exact, n, large)
    bias = jnp.zeros((t, t), F32)
    for b in range(NUM_BUCKETS):
        bias = jnp.where(bucket == b, tab_ref[b, h], bias)
    o_ref[...] = jnp.where(rel >= 0, bias * LOG2E, MASK_VALUE)


def _bias_table(rel_bias, seq, t):
    nb = seq // t
    heads = rel_bias.shape[1]
    return pl.pallas_call(
        functools.partial(_bias_table_kernel, t=t), grid=(heads, nb),
        in_specs=[pl.BlockSpec(memory_space=pltpu.SMEM)],
        out_specs=pl.BlockSpec((None, None, t, t), lambda h, m: (h, m, 0, 0)),
        out_shape=jax.ShapeDtypeStruct((heads, nb, t, t), F32),
        compiler_params=_cparams("parallel", "parallel"), name="rel_bias_table")(rel_bias)


def _attn_kernel(q_ref, k_ref, v_ref, tb_ref, lam_ref, gn_ref, o_ref, m_ref, l_ref, acc_ref,
                 *, t, heads, lambda_init):
    qi = pl.program_id(2)
    dh2 = q_ref.shape[1] // heads
    dh = dh2 // 2
    lane = lax.broadcasted_iota(jnp.int32, (t, dh2), 1)
    qqs = []
    for h in range(heads):
        qf = q_ref[:, h * dh2:(h + 1) * dh2].astype(F32) * (dh ** -0.5 * LOG2E)
        qqs.append(jnp.concatenate([jnp.where(lane < dh, qf, 0.0), jnp.where(lane >= dh, qf, 0.0)],
                                   axis=0).astype(BF16))

    hs = [slice(h * dh2, (h + 1) * dh2) for h in range(heads)]

    def update(j0, nblk):
        ks = pl.ds(pl.multiple_of(j0 * t, t), nblk * t)
        k_all = k_ref[ks, :]
        v_all = v_ref[ks, :]
        s_t = [_dot_nt(k_all[:, hs[h]], qqs[h]) for h in range(heads)]
        ps, alphas = [], []
        for h in range(heads):
            m = m_ref[h, 0:1, :]
            b = jnp.concatenate([tb_ref[h, qi - j0 - i] for i in range(nblk)], axis=0)
            s = jnp.concatenate([s_t[h][:, 0:t] + b, s_t[h][:, t:2 * t] + b], axis=1)
            m_new = jnp.maximum(m, jnp.max(s, axis=0, keepdims=True))
            alpha = jnp.exp2(m - m_new)
            p = jnp.exp2(s - m_new)
            m_ref[h] = jnp.broadcast_to(m_new, m_ref.shape[1:])
            l_ref[h] = jnp.broadcast_to(alpha * l_ref[h, 0:1, :] + jnp.sum(p, axis=0, keepdims=True),
                                        l_ref.shape[1:])
            alphas.append(alpha)
            ps.append(p.astype(BF16))
        pv = [_dot_tn(v_all[:, hs[h]], ps[h]) for h in range(heads)]
        for h in range(heads):
            acc_ref[h] = alphas[h] * acc_ref[h] + pv[h]

    m_ref[...] = jnp.full(m_ref.shape, MASK_VALUE, F32)
    l_ref[...] = jnp.zeros(l_ref.shape, F32)
    acc_ref[...] = jnp.zeros(acc_ref.shape, F32)
    odd = (qi + 1) % 2

    @pl.when(odd == 1)
    def _():
        update(0, 1)

    def pair(i, c):
        update(odd + 2 * i, 2)
        return c
    lax.fori_loop(0, (qi + 1) // 2, pair, 0)
    lp = lam_ref[...]
    lam = (jnp.exp(jnp.sum(lp[0:1] * lp[1:2], axis=-1, keepdims=True))
           - jnp.exp(jnp.sum(lp[2:3] * lp[3:4], axis=-1, keepdims=True)) + lambda_init)
    for h in range(heads):
        o12 = acc_ref[h] / l_ref[h, 0:1, :]
        o = (o12[:, 0:t] - lam * o12[:, t:2 * t]).T
        o_ref[:, h * dh2:(h + 1) * dh2] = (_rms(o, gn_ref[...]) * (1.0 - lambda_init)).astype(o_ref.dtype)


def _diff_attention(u, table, lam_params, norm_g, batch, seq, lambda_init):
    heads = DIFF_HEADS
    dh2 = norm_g.shape[0]
    hg = ATTN_HEADS_PER_STEP
    w = hg * dh2
    ng = heads // hg
    t = table.shape[2]
    nq = seq // t
    return pl.pallas_call(
        functools.partial(_attn_kernel, t=t, heads=hg, lambda_init=lambda_init),
        grid=(batch, ng, nq),
        in_specs=[pl.BlockSpec((t, w), lambda b, g, i: (b * nq + i, g)),
                  pl.BlockSpec((seq, w), lambda b, g, i: (b, ng + g)),
                  pl.BlockSpec((seq, w), lambda b, g, i: (b, 2 * ng + g)),
                  pl.BlockSpec((hg, nq, t, t), lambda b, g, i: (g, 0, 0, 0)),
                  _resident(lam_params.shape), _resident((1, dh2))],
        out_specs=pl.BlockSpec((t, w), lambda b, g, i: (b * nq + i, g)),
        out_shape=jax.ShapeDtypeStruct((batch * seq, heads * dh2), BF16),
        scratch_shapes=[pltpu.VMEM((hg, SUBLANES, 2 * t), F32), pltpu.VMEM((hg, SUBLANES, 2 * t), F32),
                        pltpu.VMEM((hg, dh2, 2 * t), F32)],
        compiler_params=_cparams("parallel", "parallel", "arbitrary"), name="diff_attention",
    )(u, u, u, table, lam_params, norm_g.reshape(1, dh2))


def _conformer_kernel(ga_ref, gb_ref, w_ref, b_ref, lg_ref, lb_ref, o_ref, pad_ref):
    s, c = ga_ref.shape
    rt = min(ELEM_ROWS, s)
    pad_ref[0:TOP_PAD, :] = jnp.zeros((TOP_PAD, c), F32)

    def glu(i, carry):
        r = pl.multiple_of(i * rt, rt)
        rs = pl.ds(r, rt)
        pad_ref[pl.ds(TOP_PAD + r, rt), :] = ga_ref[rs, :].astype(F32) * _sigmoid(gb_ref[rs, :].astype(F32))
        return carry
    lax.fori_loop(0, s // rt, glu, 0)

    ct = CONF_ROWS

    def conv(i, carry):
        r = pl.multiple_of(i * ct, ct)
        tap = _delayed(pad_ref, r, ct, CONF_WIDTH - 1)
        acc = jnp.broadcast_to(b_ref[...], (ct, c))
        for j in range(CONF_WIDTH):
            acc = acc + w_ref[j:j + 1, :] * tap(CONF_WIDTH - 1 - j)
        mu = jnp.mean(acc, axis=-1, keepdims=True)
        xc = acc - mu
        var = jnp.mean(xc * xc, axis=-1, keepdims=True)
        y = xc * lax.rsqrt(var + EPS) * lg_ref[...] + lb_ref[...]
        o_ref[pl.ds(r, ct), :] = _silu(y).astype(o_ref.dtype)
        return carry
    lax.fori_loop(0, s // ct, conv, 0)


def _conformer(u, w, b, ln_g, ln_b, batch, seq, col0):
    c = w.shape[1]
    return pl.pallas_call(
        _conformer_kernel, grid=(batch,),
        in_specs=[pl.BlockSpec((seq, c), lambda i: (i, col0)),
                  pl.BlockSpec((seq, c), lambda i: (i, col0 + 1)),
                  _resident(w.shape), _resident((1, c)), _resident((1, c)), _resident((1, c))],
        out_specs=pl.BlockSpec((seq, c), lambda i: (i, 0)),
        out_shape=jax.ShapeDtypeStruct((batch * seq, c), BF16),
        scratch_shapes=[pltpu.VMEM((seq + TOP_PAD, c), F32)],
        compiler_params=_cparams("parallel"), name="conformer_conv",
    )(u, u, w, b.reshape(1, c), ln_g.reshape(1, c), ln_b.reshape(1, c))


def _proj_router_kernel(y1_ref, y2_ref, w1_ref, w2_ref, h_ref, g_ref, wrt_ref, upper_ref,
                        h_out_ref, hn_ref, route_ref, gate_ref, cnt_ref, carry_ref):
    i = pl.program_id(0)
    tm = h_ref.shape[0]
    n_e = N_EXPERTS

    @pl.when(i == 0)
    def _():
        carry_ref[...] = jnp.zeros(carry_ref.shape, F32)

    h = h_ref[...] + _dot(y1_ref[...], w1_ref[...]) + _dot(y2_ref[...], w2_ref[...])
    h_out_ref[...] = h
    xn = _rms(h, g_ref[...])
    _store_token_tiles(hn_ref, 0, xn)

    logits = _dot_nt(wrt_ref[...], xn.astype(BF16))[0:n_e, :]
    sub = lax.broadcasted_iota(jnp.int32, logits.shape, 0)
    m1 = jnp.max(logits, axis=0, keepdims=True)
    i1 = jnp.min(jnp.where(logits == m1, sub, n_e), axis=0, keepdims=True)
    rest = jnp.where(sub == i1, -jnp.inf, logits)
    m2 = jnp.max(rest, axis=0, keepdims=True)
    i2 = jnp.min(jnp.where(rest == m2, sub, n_e), axis=0, keepdims=True)
    e = jnp.exp(m2 - m1)
    g1 = 1.0 / (1.0 + e)
    oh1 = sub == i1
    oh2 = sub == i2
    both = oh1.astype(F32) + oh2.astype(F32)
    csum = _dot(both.astype(BF16), upper_ref[...])
    carry = carry_ref[:, 0:1]
    before = csum - both + carry
    total = carry + csum[:, tm - 1:tm]
    carry_ref[...] = jnp.broadcast_to(total, carry_ref.shape)
    cnt_ref[...] = jnp.broadcast_to(total, cnt_ref.shape).astype(jnp.int32)
    r1 = jnp.sum(jnp.where(oh1, before, 0.0), axis=0, keepdims=True).astype(jnp.int32)
    r2 = jnp.sum(jnp.where(oh2, before, 0.0), axis=0, keepdims=True).astype(jnp.int32)
    route_ref[...] = jnp.where(sub == 0, i1, jnp.where(sub == 1, i2, jnp.where(sub == 2, r1,
                               jnp.where(sub == 3, r2, 0))))
    grow = jnp.where(sub == 0, g1, jnp.where(sub == 1, e * g1, 0.0))
    sel = (lax.broadcasted_iota(jnp.int32, (n_e, LANES), 0)
           == lax.broadcasted_iota(jnp.int32, (n_e, LANES), 1)).astype(F32)
    gate_ref[...] = lax.dot_general(grow, sel, (((0,), (0,)), ((), ())), precision=HIGHEST,
                                    preferred_element_type=F32)


def _proj_router(y1, y2, w, h, g, wr):
    t, d = h.shape
    k1, k2 = y1.shape[1], y2.shape[1]
    tm = min(ROW_TILE, t)
    wrt = jnp.zeros((LANES, d), BF16).at[:N_EXPERTS].set(wr.T.astype(BF16))
    upper = (jnp.arange(tm)[:, None] <= jnp.arange(tm)[None, :]).astype(BF16)
    row = lambda width: pl.BlockSpec((tm, width), lambda i: (i, 0))
    return pl.pallas_call(
        _proj_router_kernel, grid=(t // tm,),
        in_specs=[row(k1), row(k2), _resident((k1, d)), _resident((k2, d)), row(d),
                  _resident((1, d)), _resident((LANES, d)), _resident((tm, tm))],
        out_specs=[row(d), pl.BlockSpec((tm * d // LANES, LANES), lambda i: (i, 0)),
                   pl.BlockSpec((N_EXPERTS, tm), lambda i: (0, i)), row(LANES),
                   pl.BlockSpec((N_EXPERTS, LANES), lambda i: (0, 0))],
        out_shape=[jax.ShapeDtypeStruct((t, d), F32), jax.ShapeDtypeStruct((t * d // LANES, LANES), F32),
                   jax.ShapeDtypeStruct((N_EXPERTS, t), jnp.int32), jax.ShapeDtypeStruct((t, LANES), F32),
                   jax.ShapeDtypeStruct((N_EXPERTS, LANES), jnp.int32)],
        scratch_shapes=[pltpu.VMEM((N_EXPERTS, LANES), F32)],
        compiler_params=_cparams("arbitrary"), name="l1_out_proj_router",
    )(y1, y2, w[:k1], w[k1:], h, g.reshape(1, d), wrt, upper)


def _dispatch_kernel(dest_hbm, pad_hbm, x_ref, xs_hbm, idx_smem, pad_smem, zero_ref, sem_idx, sem_x):
    i = pl.program_id(0)
    n = x_ref.shape[0] // SUBLANES
    m = TOP_K * n
    cp = pltpu.make_async_copy(dest_hbm.at[i], idx_smem, sem_idx)
    cp.start()
    cp.wait()

    def body(r, c):
        for k in range(TOP_K):
            _token_tile_copy(x_ref, r, xs_hbm, idx_smem[k * n + r], sem_x).start(priority=k % 2)
        return c
    lax.fori_loop(0, n, body, 0, unroll=8)
    for _ in range(TOP_K):
        pltpu.make_async_copy(x_ref, xs_hbm.at[pl.ds(0, n * SUBLANES)], sem_x).wait()

    @pl.when(i == pl.num_programs(0) - 1)
    def _():
        n_pad = pad_smem.shape[0]
        cp = pltpu.make_async_copy(pad_hbm, pad_smem, sem_idx)
        cp.start()
        cp.wait()
        zero_ref[...] = jnp.zeros(zero_ref.shape, F32)

        def fill(j, c):
            for q in range(2):
                _token_tile_copy(zero_ref, 0, xs_hbm, pad_smem[2 * j + q], sem_x).start(priority=q)
            return c
        lax.fori_loop(0, n_pad // 2, fill, 0, unroll=8)
        pltpu.make_async_copy(xs_hbm.at[pl.ds(0, n_pad * SUBLANES)], xs_hbm.at[pl.ds(0, n_pad * SUBLANES)],
                              sem_x).wait()


def _dispatch(hn_tiles, dest, pad_rows, t, d):
    n = min(MOE_ROWS, t)
    steps = t // n
    tpt = d // LANES
    p = t * TOP_K + N_EXPERTS * MOE_ROWS
    dest_steps = jnp.concatenate([dest[k].reshape(steps, n) for k in range(TOP_K)], axis=1)
    return pl.pallas_call(
        _dispatch_kernel, grid=(steps,),
        in_specs=[pl.BlockSpec(memory_space=pl.ANY), pl.BlockSpec(memory_space=pl.ANY),
                  pl.BlockSpec((n * tpt, LANES), lambda i: (i, 0))],
        out_specs=pl.BlockSpec(memory_space=pl.ANY),
        out_shape=jax.ShapeDtypeStruct((p * tpt, LANES), F32),
        scratch_shapes=[pltpu.SMEM((TOP_K * n,), jnp.int32), pltpu.SMEM(pad_rows.shape, jnp.int32),
                        pltpu.VMEM((tpt, LANES), F32), pltpu.SemaphoreType.DMA, pltpu.SemaphoreType.DMA],
        compiler_params=_cparams("arbitrary"), name="moe_dispatch")(dest_steps, pad_rows, hn_tiles)


def _dispatch_plan(route, counts, t):
    tm = MOE_ROWS
    p = t * TOP_K + N_EXPERTS * tm
    padded = (counts + tm - 1) // tm * tm
    pend = jnp.cumsum(padded)
    pstart = pend - padded
    experts = jnp.arange(N_EXPERTS, dtype=jnp.int32)[:, None]
    dest = jnp.stack([jnp.sum(jnp.where(route[k][None, :] == experts, pstart[:, None], 0), axis=0)
                      + route[TOP_K + k] for k in range(TOP_K)])
    n_pad = p - t * TOP_K
    gaps = jnp.concatenate([padded - counts, (p - pend[-1])[None]])
    gap_end = jnp.cumsum(gaps)
    gap_row0 = jnp.concatenate([pstart + counts, pend[-1:]])
    j = jnp.arange(n_pad, dtype=jnp.int32)
    which = jnp.searchsorted(gap_end, j, side='right')
    pad_rows = (gap_row0[which] + j - (gap_end - gaps)[which]).astype(jnp.int32)
    nb = p // tm
    blk_e = jnp.minimum(jnp.searchsorted(pend, jnp.arange(nb, dtype=jnp.int32) * tm, side='right'),
                        N_EXPERTS - 1).astype(jnp.int32)
    return dest.astype(jnp.int32), pad_rows, blk_e, (pend[-1:] // tm).astype(jnp.int32)


def _experts_kernel(blk_e_ref, used_ref, x_ref, wgu_ref, wd_ref, o_ref, hid_ref):
    f = wd_ref.shape[0]
    tm = hid_ref.shape[0]
    live = pl.program_id(0) < used_ref[0]

    @pl.when(live)
    def _():
        x = _load_token_tiles(x_ref, 0, tm).astype(BF16)
        for c in range(0, f, MOE_SUB_TILE):
            hid_ref[:, c:c + MOE_SUB_TILE] = (
                _silu(_dot(x, wgu_ref[:, c:c + MOE_SUB_TILE]))
                * _dot(x, wgu_ref[:, f + c:f + c + MOE_SUB_TILE])).astype(BF16)
        _store_token_tiles(o_ref, 0, _dot(hid_ref[...], wd_ref[...]))

    @pl.when(jnp.logical_not(live))
    def _():
        o_ref[...] = jnp.zeros(o_ref.shape, F32)


def _experts(xs, blk_e, n_used, w_gate_up, w_down):
    f, d = w_down.shape[1], w_down.shape[2]
    tm = MOE_ROWS
    tpt = d // LANES
    assert f % MOE_SUB_TILE == 0
    grid_spec = pltpu.PrefetchScalarGridSpec(
        num_scalar_prefetch=2, grid=(blk_e.shape[0],),
        in_specs=[pl.BlockSpec((tm * tpt, LANES), lambda i, e, u: (i, 0)),
                  pl.BlockSpec((None, d, 2 * f), lambda i, e, u: (e[i], 0, 0), pipeline_mode=pl.Buffered(1)),
                  pl.BlockSpec((None, f, d), lambda i, e, u: (e[i], 0, 0), pipeline_mode=pl.Buffered(1))],
        out_specs=pl.BlockSpec((tm * tpt, LANES), lambda i, e, u: (i, 0)),
        scratch_shapes=[pltpu.VMEM((tm, f), BF16)])
    return pl.pallas_call(
        _experts_kernel, grid_spec=grid_spec, out_shape=jax.ShapeDtypeStruct(xs.shape, F32),
        compiler_params=_cparams("parallel"), name="moe_experts",
    )(blk_e, n_used, xs, w_gate_up, w_down)


def _combine_kernel(dest_hbm, y_hbm, gate_ref, h_ref, g_ref, p_ref, wp_ref, wg_ref, fg_ref, o_ref,
                    idx0, idx1, ybuf, sem_idx, sem_rows):
    i = pl.program_id(0)
    steps = pl.num_programs(0)
    n = h_ref.shape[0]
    m = TOP_K * n
    slot = i % 2
    idxs = (idx0, idx1)

    def idx_copy(blk, s):
        return pltpu.make_async_copy(dest_hbm.at[blk], idxs[s], sem_idx.at[s])

    def issue_tiles(s):
        def body(r, c):
            for k in range(TOP_K):
                _token_tile_copy(y_hbm, idxs[s][k * n + r], ybuf, s * m + k * n + r,
                                 sem_rows.at[s]).start(priority=k % 2)
            return c
        lax.fori_loop(0, n, body, 0, unroll=8)

    @pl.when(i == 0)
    def _():
        idx_copy(0, 0).start()
        idx_copy(0, 0).wait()
        idx_copy(1, 1).start()
        issue_tiles(0)

    for s in range(2):
        @pl.when(slot == s)
        def _(s=s):
            idx_copy(i + 1, 1 - s).wait()

            @pl.when(i + 2 <= steps)
            def _():
                idx_copy(i + 2, s).start()

            issue_tiles(1 - s)
            _token_tiles_wait(y_hbm, ybuf, s * m, m, sem_rows.at[s])

    nc = n // COMBINE_CHUNKS
    rows = [pl.ds(c * nc, nc) for c in range(COMBINE_CHUNKS)]
    xs = []
    for c, rs in enumerate(rows):
        gates = gate_ref[rs, :]
        xs.append(h_ref[rs, :] + gates[:, 0:1] * _load_token_tiles(ybuf, slot * m + c * nc, nc)
                  + gates[:, 1:2] * _load_token_tiles(ybuf, slot * m + n + c * nc, nc))
    xns = [_rms(x, g_ref[...]).astype(BF16) for x in xs]
    gate = [_sigmoid(_dot(xn, wg_ref[...])) for xn in xns]
    emb = [_dot(p_ref[rs, :].astype(BF16), wp_ref[...]) for rs in rows]
    for rs, x, e, g in zip(rows, xs, emb, gate):
        o_ref[rs, :] = _rms(x + e * g, fg_ref[...])

    @pl.when(i == steps - 1)
    def _():
        _token_tiles_wait(y_hbm, ybuf, (1 - slot) * m, m, sem_rows.at[1 - slot])


def _combine_ple_final(dest, y_tiles, gates, h, g, p, layer, wp, wg, final_g):
    t, d = h.shape
    e = p.shape[2]
    n = min(COMBINE_ROWS, t)
    m = TOP_K * n
    steps = t // n
    dest_steps = jnp.concatenate([dest[k].reshape(steps, n) for k in range(TOP_K)], axis=1)
    dest_steps = jnp.concatenate([dest_steps, jnp.zeros((1, m), jnp.int32)], axis=0)
    return pl.pallas_call(
        _combine_kernel, grid=(steps,),
        in_specs=[pl.BlockSpec(memory_space=pl.ANY), pl.BlockSpec(memory_space=pl.ANY),
                  pl.BlockSpec((n, LANES), lambda i: (i, 0)),
                  pl.BlockSpec((n, d), lambda i: (i, 0)), _resident((1, d)),
                  pl.BlockSpec((None, n, e), lambda i: (layer, i, 0)), _resident((e, d)), _resident((d, d)),
                  _resident((1, d))],
        out_specs=pl.BlockSpec((n, d), lambda i: (i, 0)),
        out_shape=jax.ShapeDtypeStruct((t, d), F32),
        scratch_shapes=[pltpu.SMEM((m,), jnp.int32), pltpu.SMEM((m,), jnp.int32),
                        pltpu.VMEM((2 * m * d // LANES, LANES), F32),
                        pltpu.SemaphoreType.DMA((2,)), pltpu.SemaphoreType.DMA((2,))],
        compiler_params=_cparams("arbitrary"), name="moe_combine_ple_final",
    )(dest_steps, y_tiles, gates, h, g.reshape(1, d), p, wp, wg, final_g.reshape(1, d))


def kernel(x, p, norm_mix_g, norm_ffn_g, norm_ple_g, final_norm_g, ev_w_in, ev_conv_a, ev_gdn_conv, ev_gdn_A_log, ev_gdn_dt_bias, ev_gdn_norm_g, ev_w_out, od_w_in, od_lambda, od_diff_norm_g, od_conf_dw_w, od_conf_dw_b, od_conf_ln_g, od_conf_ln_b, od_w_out, rel_bias, ffn_w_gate_up, ffn_w_down, moe_router, moe_w_gate_up, moe_w_down, ple_w_proj, ple_w_gate):
    batch, seq, d = x.shape
    t = batch * seq
    depth = p.shape[0]
    assert depth == 2 and seq % GDN_CHUNK == 0
    h = x.reshape(t, d)
    pf = p.reshape(depth, t, p.shape[-1])

    heads = GDN_HEADS
    n_main = ev_w_in.shape[2] - 2 * heads
    w_in = ev_w_in[0]
    w_ab = jnp.zeros((d, LANES), BF16).at[:, :2 * heads].set(w_in[:, n_main:].astype(BF16))
    u, ab = _norm_proj(h, norm_mix_g[0], w_in[:, :n_main].astype(BF16), w_ab, name="l0_in_proj")
    ya = _gated_conv(u, ev_conv_a[0], batch, seq)
    yb = _l0_mixer(u, ab, ev_gdn_conv[0], ev_gdn_A_log[0], ev_gdn_dt_bias[0], ev_gdn_norm_g[0], batch, seq)
    h = _proj_residual([ya, yb], ev_w_out[0].astype(BF16), h, name="l0_out_proj")
    f = ffn_w_down.shape[1]
    lambda_init = 0.8 - 0.6 * math.exp(-0.3 * 1)
    h, u = _ffn_ple_proj(h, norm_ffn_g[0], ffn_w_gate_up[0, :, :f].astype(BF16),
                         ffn_w_gate_up[0, :, f:].astype(BF16), ffn_w_down[0].astype(BF16),
                         norm_ple_g[0], pf, 0, ple_w_proj[0].astype(BF16), ple_w_gate[0].astype(BF16),
                         norm_mix_g[1], od_w_in[0].astype(BF16), name="l0_ffn_ple_l1_in_proj")

    table = _bias_table(rel_bias, seq, min(ATTN_TILE, seq))
    o_attn = _diff_attention(u, table, od_lambda[0], od_diff_norm_g[0], batch, seq, lambda_init)
    c_conf = od_conf_dw_w.shape[2]
    o_conf = _conformer(u, od_conf_dw_w[0], od_conf_dw_b[0], od_conf_ln_g[0], od_conf_ln_b[0],
                        batch, seq, 3 * DIFF_HEADS * od_diff_norm_g.shape[1] // c_conf)
    h, hn, route, gates, counts = _proj_router(o_attn, o_conf, od_w_out[0].astype(BF16), h, norm_ffn_g[1],
                                               moe_router[0])
    dest, pad_rows, blk_e, n_used = _dispatch_plan(route, counts[:, 0], t)
    xs = _dispatch(hn, dest, pad_rows, t, d)
    y = _experts(xs, blk_e, n_used, moe_w_gate_up[0].astype(BF16), moe_w_down[0].astype(BF16))
    out = _combine_ple_final(dest, y, gates, h, norm_ple_g[1], pf, 1, ple_w_proj[1].astype(BF16),
                             ple_w_gate[1].astype(BF16), final_norm_g)
    return out.reshape(batch, seq, d)
```

```python
import functools
import math

import jax
import jax.numpy as jnp
from jax import lax
from jax.experimental import pallas as pl
from jax.experimental.pallas import tpu as pltpu

F32 = jnp.float32
BF16 = jnp.bfloat16
HIGHEST = lax.Precision.HIGHEST

EPS = 1e-6
CONV_A_WIDTH = 3
GDN_HEADS = 4
GDN_CONV_WIDTH = 4
GDN_CHUNK = 64
DIFF_HEADS = 4
NUM_BUCKETS = 32
MAX_DISTANCE = 128
CONF_WIDTH = 31
N_EXPERTS = 8
TOP_K = 2

LANES = 128
SUBLANES = 8
VMEM_LIMIT_BYTES = 56 * 1024 * 1024
MASK_VALUE = -1e30
LOG2E = math.log2(math.e)

ROW_TILE = 512
COL_TILE = 512
FFN_CHUNKS = 2
ELEM_ROWS = 256
CONF_ROWS = 256
PREP_CHUNKS = 8
ATTN_TILE = 256
ATTN_HEADS_PER_STEP = 2
MOE_ROWS = 1024
MOE_SUB_TILE = 256
COMBINE_ROWS = 512
COMBINE_CHUNKS = 4
TOP_PAD = 32


def _cparams(*sem):
    return pltpu.CompilerParams(dimension_semantics=sem, vmem_limit_bytes=VMEM_LIMIT_BYTES)


def _resident(shape):
    nd = len(shape)
    return pl.BlockSpec(shape, lambda *_: (0,) * nd, pipeline_mode=pl.Buffered(1))


def _rms(x, g):
    return x * lax.rsqrt(jnp.mean(x * x, axis=-1, keepdims=True) + EPS) * g


def _sigmoid(x):
    return jax.nn.sigmoid(x)


def _silu(x):
    return x * jax.nn.sigmoid(x)


def _softplus(x):
    return jnp.maximum(x, 0.0) + jnp.log1p(jnp.exp(-jnp.abs(x)))


def _dot(a, b, **kw):
    return jnp.dot(a, b, preferred_element_type=F32, **kw)


def _dot_nt(a, b):
    return lax.dot_general(a, b, (((1,), (1,)), ((), ())), preferred_element_type=F32)


def _dot_tn(a, b):
    return lax.dot_general(a, b, (((0,), (0,)), ((), ())), preferred_element_type=F32)


def _delayed(pad_ref, r, rows, max_delay):
    lead = -(-max_delay // SUBLANES) * SUBLANES
    win = pad_ref[pl.ds(TOP_PAD + r - lead, rows + lead), :]
    rolled = {0: win}

    def tap(d):
        a, b = divmod(d, SUBLANES)
        if b not in rolled:
            rolled[b] = pltpu.roll(win, b, 0)
        start = lead - SUBLANES * a
        return rolled[b][start:start + rows, :]
    return tap


def _load_token_tiles(ref, first_token, n):
    return jnp.concatenate([ref[pl.ds(first_token * SUBLANES + s, n, stride=SUBLANES), :]
                            for s in range(SUBLANES)], axis=1)


def _store_token_tiles(ref, first_token, x):
    n = x.shape[0]
    for s in range(SUBLANES):
        ref[pl.ds(first_token * SUBLANES + s, n, stride=SUBLANES), :] = x[:, s * LANES:(s + 1) * LANES]


def _token_tile_copy(src_hbm, src_token, dst_vmem, dst_token, sem):
    return pltpu.make_async_copy(src_hbm.at[pl.ds(pl.multiple_of(src_token * SUBLANES, SUBLANES), SUBLANES)],
                                 dst_vmem.at[pl.ds(pl.multiple_of(dst_token * SUBLANES, SUBLANES), SUBLANES)],
                                 sem)


def _token_tiles_wait(src_hbm, dst_vmem, first_token, n, sem):
    pltpu.make_async_copy(src_hbm.at[pl.ds(0, n * SUBLANES)],
                          dst_vmem.at[pl.ds(pl.multiple_of(first_token * SUBLANES, SUBLANES), n * SUBLANES)],
                          sem).wait()


def _norm_proj_kernel(h_ref, g_ref, w_ref, *rest, tn, with_aux):
    xn = _rms(h_ref[...], g_ref[...]).astype(BF16)
    if with_aux:
        w2_ref, o_ref, o2_ref = rest
        o2_ref[...] = _dot(xn, w2_ref[...])
    else:
        (o_ref,) = rest
    n = w_ref.shape[1]
    for c in range(0, n, tn):
        o_ref[:, c:c + tn] = _dot(xn, w_ref[:, c:c + tn]).astype(o_ref.dtype)


def _norm_proj(h, g, w, w_aux=None, *, name):
    t, d = h.shape
    n = w.shape[1]
    tm = min(ROW_TILE, t)
    tn = COL_TILE if n % COL_TILE == 0 else n
    in_specs = [pl.BlockSpec((tm, d), lambda i: (i, 0)), _resident((1, d)), _resident((d, n))]
    out_shape = [jax.ShapeDtypeStruct((t, n), BF16)]
    out_specs = [pl.BlockSpec((tm, n), lambda i: (i, 0))]
    args = [h, g.reshape(1, d), w]
    if w_aux is not None:
        in_specs.append(_resident(w_aux.shape))
        out_shape.append(jax.ShapeDtypeStruct((t, w_aux.shape[1]), F32))
        out_specs.append(pl.BlockSpec((tm, w_aux.shape[1]), lambda i: (i, 0)))
        args.append(w_aux)
    out = pl.pallas_call(
        functools.partial(_norm_proj_kernel, tn=tn, with_aux=w_aux is not None),
        grid=(t // tm,), in_specs=in_specs, out_specs=out_specs, out_shape=out_shape,
        compiler_params=_cparams("parallel"), name=name)(*args)
    return out if w_aux is not None else out[0]


def _proj_residual_kernel(*refs):
    n = (len(refs) - 2) // 2
    y_refs, w_refs, h_ref, o_ref = refs[:n], refs[n:2 * n], refs[2 * n], refs[2 * n + 1]
    acc = h_ref[...]
    for y_ref, w_ref in zip(y_refs, w_refs):
        acc = acc + _dot(y_ref[...], w_ref[...])
    o_ref[...] = acc


def _proj_residual(ys, w, h, *, name):
    t, d = h.shape
    tm = min(ROW_TILE, t)
    ws, r0 = [], 0
    for y in ys:
        ws.append(w[r0:r0 + y.shape[1]])
        r0 += y.shape[1]
    return pl.pallas_call(
        _proj_residual_kernel, grid=(t // tm,),
        in_specs=([pl.BlockSpec((tm, y.shape[1]), lambda i: (i, 0)) for y in ys]
                  + [_resident(wi.shape) for wi in ws] + [pl.BlockSpec((tm, d), lambda i: (i, 0))]),
        out_specs=pl.BlockSpec((tm, d), lambda i: (i, 0)),
        out_shape=jax.ShapeDtypeStruct((t, d), F32),
        compiler_params=_cparams("parallel"), name=name)(*ys, *ws, h)


def _ffn_ple_proj_kernel(h_ref, gf_ref, wg_ref, wu_ref, wd_ref, gp_ref, p_ref, wp_ref, wpg_ref,
                         gm_ref, win_ref, h_out_ref, u_ref, *, tf, tn):
    tm = h_ref.shape[0]
    nc = tm // FFN_CHUNKS
    rows = [pl.ds(c * nc, nc) for c in range(FFN_CHUNKS)]
    xs = [h_ref[rs, :] for rs in rows]
    xns = [_rms(x, gf_ref[...]).astype(BF16) for x in xs]
    accs = list(xs)
    for c in range(0, wg_ref.shape[1], tf):
        gates = [_dot(xn, wg_ref[:, c:c + tf]) for xn in xns]
        ups = [_dot(xn, wu_ref[:, c:c + tf]) for xn in xns]
        hids = [(_silu(g) * u).astype(BF16) for g, u in zip(gates, ups)]
        accs = [a + _dot(hd, wd_ref[c:c + tf, :]) for a, hd in zip(accs, hids)]
    xn2 = [_rms(x, gp_ref[...]).astype(BF16) for x in accs]
    pgate = [_sigmoid(_dot(xn, wpg_ref[...])) for xn in xn2]
    emb = [_dot(p_ref[rs, :].astype(BF16), wp_ref[...]) for rs in rows]
    h3 = [x + e * g for x, e, g in zip(accs, emb, pgate)]
    for rs, x in zip(rows, h3):
        h_out_ref[rs, :] = x
    xn3 = [_rms(x, gm_ref[...]).astype(BF16) for x in h3]
    for c in range(0, win_ref.shape[1], tn):
        for rs, xn in zip(rows, xn3):
            u_ref[rs, c:c + tn] = _dot(xn, win_ref[:, c:c + tn]).astype(u_ref.dtype)


def _ff_tile(f, cap):
    best = LANES
    for c in range(LANES, cap + 1, LANES):
        if f % c == 0:
            best = c
    return best


def _ffn_ple_proj(h, g_ffn, wg, wu, wd, g_ple, p, layer, wp, wpg, g_mix, w_in, *, name):
    t, d = h.shape
    f = wg.shape[1]
    e = p.shape[2]
    n = w_in.shape[1]
    tm = min(ROW_TILE, t)
    tn = COL_TILE if n % COL_TILE == 0 else n
    row = lambda w: pl.BlockSpec((tm, w), lambda i: (i, 0))
    return pl.pallas_call(
        functools.partial(_ffn_ple_proj_kernel, tf=_ff_tile(f, 1536), tn=tn), grid=(t // tm,),
        in_specs=[row(d), _resident((1, d)), _resident((d, f)), _resident((d, f)), _resident((f, d)),
                  _resident((1, d)), pl.BlockSpec((None, tm, e), lambda i: (layer, i, 0)),
                  _resident((e, d)), _resident((d, d)), _resident((1, d)), _resident((d, n))],
        out_specs=[row(d), row(n)],
        out_shape=[jax.ShapeDtypeStruct((t, d), F32), jax.ShapeDtypeStruct((t, n), BF16)],
        compiler_params=_cparams("parallel"), name=name,
    )(h, g_ffn.reshape(1, d), wg, wu, wd, g_ple.reshape(1, d), p, wp, wpg, g_mix.reshape(1, d), w_in)


def _unit_lower_inverses(mats):
    n = mats[0].shape[0]
    row = lax.broadcasted_iota(jnp.int32, (n, n), 0)
    col = lax.broadcasted_iota(jnp.int32, (n, n), 1)
    eye = (row == col).astype(F32)
    same16 = (row // 16) == (col // 16)
    same32 = (row // 32) == (col // 32)
    off32 = jnp.logical_and(same32, jnp.logical_not(same16))

    def mm(ps, qs):
        return [_dot(p.astype(BF16), q.astype(BF16)) for p, q in zip(ps, qs)]

    ad = [jnp.where(same16, a, 0.0) for a in mats]
    a2 = mm(ad, ad)
    x = mm([eye - t for t in ad], [eye + t for t in a2])
    a4 = mm(a2, a2)
    x = mm(x, [eye + t for t in a4])
    a8 = mm(a4, a4)
    x = mm(x, [eye + t for t in a8])
    y = mm([jnp.where(off32, a, 0.0) for a in mats], x)
    x = [t - c for t, c in zip(x, mm(x, y))]
    y = mm([jnp.where(same32, 0.0, a) for a in mats], x)
    return [t - c for t, c in zip(x, mm(x, y))]


def _seq_tiles(pad_ref, s):
    rt = min(ELEM_ROWS, s)
    n_tiles = s // rt

    def rows(i):
        return pl.ds(pl.multiple_of(i * rt, rt), rt)

    def fill_pad(fn):
        pad_ref[0:TOP_PAD, :] = jnp.zeros((TOP_PAD, pad_ref.shape[1]), F32)

        def body(i, c):
            r = pl.multiple_of(i * rt, rt)
            pad_ref[pl.ds(TOP_PAD + r, rt), :] = fn(rows(i))
            return c
        lax.fori_loop(0, n_tiles, body, 0)

    def conv_tile(i, w_ref, width):
        tap = _delayed(pad_ref, pl.multiple_of(i * rt, rt), rt, width - 1)
        acc = None
        for j in range(width):
            term = w_ref[j:j + 1, :] * tap(width - 1 - j)
            acc = term if acc is None else acc + term
        return acc

    return n_tiles, rows, fill_pad, conv_tile


def _gated_conv_kernel(bg_ref, cg_ref, xin_ref, w_ref, o_ref, pad_ref):
    n_tiles, rows, fill_pad, conv_tile = _seq_tiles(pad_ref, bg_ref.shape[0])
    fill_pad(lambda rs: cg_ref[rs, :].astype(F32) * xin_ref[rs, :].astype(F32))

    def body(i, c):
        acc = conv_tile(i, w_ref, CONV_A_WIDTH)
        o_ref[rows(i), :] = (bg_ref[rows(i), :].astype(F32) * acc).astype(o_ref.dtype)
        return c
    lax.fori_loop(0, n_tiles, body, 0)


def _gated_conv(u, conv_a, batch, seq):
    cw = conv_a.shape[1]
    return pl.pallas_call(
        _gated_conv_kernel, grid=(batch,),
        in_specs=[pl.BlockSpec((seq, cw), lambda b, j=j: (b, j)) for j in range(3)] + [_resident(conv_a.shape)],
        out_specs=pl.BlockSpec((seq, cw), lambda b: (b, 0)),
        out_shape=jax.ShapeDtypeStruct((batch * seq, cw), BF16),
        scratch_shapes=[pltpu.VMEM((seq + TOP_PAD, cw), F32)],
        compiler_params=_cparams("parallel"), name="l0_gated_conv")(u, u, u, conv_a)


def _l0_mixer_kernel(q_ref, k_ref, v_ref, og_ref, ab_ref, wq_ref, wk_ref, wv_ref, alog_ref, dtb_ref, gn_ref,
                     o_ref,
                     pad_ref, qs_ref, ks_ref, vs_ref, us_ref, ws_ref, qk_ref, egl_ref, st_ref):
    s = q_ref.shape[0]
    heads = GDN_HEADS
    dk = q_ref.shape[1] // heads
    c64 = GDN_CHUNK
    n_chunks = s // c64
    n_tiles, rows, fill_pad, conv_tile = _seq_tiles(pad_ref, s)

    def l2n(x, scale):
        parts = []
        for h in range(heads):
            xh = x[:, h * dk:(h + 1) * dk]
            inv = lax.rsqrt(jnp.sum(xh * xh, axis=-1, keepdims=True) + EPS)
            parts.append(xh * (inv * scale))
        return jnp.concatenate(parts, axis=1)

    for src_ref, w_ref, dst_ref, post in (
            (q_ref, wq_ref, qs_ref, lambda x: l2n(x, dk ** -0.5)),
            (k_ref, wk_ref, ks_ref, lambda x: l2n(x, 1.0)),
            (v_ref, wv_ref, vs_ref, lambda x: x)):
        fill_pad(lambda rs, src_ref=src_ref: src_ref[rs, :].astype(F32))

        def conv_body(i, c, w_ref=w_ref, dst_ref=dst_ref, post=post):
            dst_ref[rows(i), :] = post(_silu(conv_tile(i, w_ref, GDN_CONV_WIDTH)))
            return c
        lax.fori_loop(0, n_tiles, conv_body, 0)

    ri = lax.broadcasted_iota(jnp.int32, (c64, c64), 0)
    ci = lax.broadcasted_iota(jnp.int32, (c64, c64), 1)
    tril = ri >= ci
    strict = ri > ci
    ltri = tril.astype(F32)

    group = PREP_CHUNKS if n_chunks % PREP_CHUNKS == 0 else 1

    def chunk_prep(cg, carry):
        chunks = []
        for cc in range(group):
            c = cg * group + cc
            rs = pl.ds(pl.multiple_of(c * c64, c64), c64)
            chunks.append((c, rs, ab_ref[rs, :], qs_ref[rs, :], ks_ref[rs, :], vs_ref[rs, :]))
        inst = []
        gcs = [_dot(ltri, -jnp.exp(alog_ref[...]) * _softplus(ab + dtb_ref[...]), precision=HIGHEST)
               for _, _, ab, _, _, _ in chunks]
        for (c, rs, ab, q_all, k_all, v_all), gc in zip(chunks, gcs):
            beta = _sigmoid(ab)
            gct = gc.T
            for h in range(heads):
                hs = slice(h * dk, (h + 1) * dk)
                gcol = gc[:, h:h + 1]
                glast = gc[c64 - 1:c64, h:h + 1]
                bcol = beta[:, heads + h:heads + h + 1]
                decay = jnp.where(tril, jnp.exp(jnp.where(tril, gcol - gct[h:h + 1, :], 0.0)), 0.0)
                kh, qh, vh = k_all[:, hs], q_all[:, hs], v_all[:, hs]
                kb = kh * bcol
                egc = jnp.exp(gcol)
                inst.append(dict(decay=decay, kh=kh, qh=qh, kb=kb, egc=egc,
                                 rhs=jnp.concatenate([vh * bcol, kb * egc], axis=1).astype(BF16),
                                 kd=kh * jnp.exp(glast - gcol),
                                 eg=jnp.broadcast_to(jnp.exp(glast), (SUBLANES, dk))))
        kqs = [_dot_nt(jnp.concatenate([t["kb"], t["qh"]], axis=0).astype(BF16), t["kh"].astype(BF16))
               for t in inst]
        minvs = _unit_lower_inverses([jnp.where(strict, kq[0:c64] * t["decay"], 0.0)
                                      for kq, t in zip(kqs, inst)])
        uws = [_dot(m.astype(BF16), t["rhs"]) for m, t in zip(minvs, inst)]
        for ci, (c, rs, _, _, _, _) in enumerate(chunks):
            sl = slice(ci * heads, (ci + 1) * heads)
            us_ref[rs, :] = jnp.concatenate([uw[:, 0:dk] for uw in uws[sl]], axis=1)
            ws_ref[rs, :] = jnp.concatenate([uw[:, dk:2 * dk] for uw in uws[sl]], axis=1).astype(BF16)
            qs_ref[rs, :] = jnp.concatenate([t["qh"] * t["egc"] for t in inst[sl]], axis=1)
            ks_ref[rs, :] = jnp.concatenate([t["kd"] for t in inst[sl]], axis=1)
            qk_ref[rs, :] = jnp.concatenate([kq[c64:2 * c64] * t["decay"]
                                             for kq, t in zip(kqs[sl], inst[sl])], axis=1).astype(BF16)
            e0 = pl.multiple_of(c * (heads * SUBLANES), heads * SUBLANES)
            egl_ref[pl.ds(e0, heads * SUBLANES), :] = jnp.concatenate([t["eg"] for t in inst[sl]], axis=0)
        return carry
    lax.fori_loop(0, n_chunks // group, chunk_prep, 0)

    st_ref[...] = jnp.zeros(st_ref.shape, F32)

    def scan(c, carry):
        rs = pl.ds(pl.multiple_of(c * c64, c64), c64)
        e0 = pl.multiple_of(c * (heads * SUBLANES), heads * SUBLANES)
        w_all, qg_all, u_all, kd_all = ws_ref[rs, :], qs_ref[rs, :], us_ref[rs, :], ks_ref[rs, :]
        qk_all = qk_ref[rs, :]
        eg_all = egl_ref[pl.ds(e0, heads * SUBLANES), :]
        hsl = [slice(h * dk, (h + 1) * dk) for h in range(heads)]
        sts = [st_ref[h] for h in range(heads)]
        wqs = [_dot(jnp.concatenate([w_all[:, hs], qg_all[:, hs].astype(BF16)], axis=0), st.astype(BF16))
               for hs, st in zip(hsl, sts)]
        vbs = [(u_all[:, hs] - wq[0:c64]).astype(BF16) for hs, wq in zip(hsl, wqs)]
        upd = [_dot_tn(kd_all[:, hs].astype(BF16), vb) for hs, vb in zip(hsl, vbs)]
        intra = [_dot(qk_all[:, h * c64:(h + 1) * c64], vbs[h]) for h in range(heads)]
        for h in range(heads):
            st_ref[h] = sts[h] * eg_all[h * SUBLANES:h * SUBLANES + 1, :] + upd[h]
        vs_ref[rs, :] = jnp.concatenate([wq[c64:2 * c64] + o for wq, o in zip(wqs, intra)], axis=1)
        return carry
    lax.fori_loop(0, n_chunks, scan, 0)

    def finish(i, c):
        o = vs_ref[rows(i), :]
        og = og_ref[rows(i), :].astype(F32)
        parts = []
        for h in range(heads):
            oh = o[:, h * dk:(h + 1) * dk]
            parts.append(_rms(oh, gn_ref[...]))
        y = jnp.concatenate(parts, axis=1) * _silu(og)
        o_ref[rows(i), :] = y.astype(o_ref.dtype)
        return c
    lax.fori_loop(0, n_tiles, finish, 0)


def _l0_mixer(u, ab, gdn_conv, a_log, dt_bias, gdn_norm_g, batch, seq):
    dk = gdn_norm_g.shape[0]
    heads = GDN_HEADS
    cw = heads * dk

    def lane_row(x):
        return jnp.zeros((1, LANES), F32).at[0, :x.shape[0]].set(x)

    in_specs = [pl.BlockSpec((seq, cw), lambda b, j=j: (b, j)) for j in range(3, 7)]
    in_specs += [pl.BlockSpec((seq, LANES), lambda b: (b, 0))]
    in_specs += [pl.BlockSpec((GDN_CONV_WIDTH, cw), lambda b, j=j: (0, j), pipeline_mode=pl.Buffered(1))
                 for j in range(3)]
    in_specs += [_resident((1, LANES)), _resident((1, LANES)), _resident((1, dk))]
    big = pltpu.VMEM((seq, cw), F32)
    scratch = [pltpu.VMEM((seq + TOP_PAD, cw), F32), big, big, big, big, pltpu.VMEM((seq, cw), BF16),
               pltpu.VMEM((seq, heads * GDN_CHUNK), BF16),
               pltpu.VMEM((seq // GDN_CHUNK * heads * SUBLANES, dk), F32),
               pltpu.VMEM((heads, dk, dk), F32)]
    return pl.pallas_call(
        _l0_mixer_kernel, grid=(batch,), in_specs=in_specs,
        out_specs=pl.BlockSpec((seq, cw), lambda b: (b, 0)),
        out_shape=jax.ShapeDtypeStruct((batch * seq, cw), BF16),
        scratch_shapes=scratch, compiler_params=_cparams("parallel"), name="l0_deltanet",
    )(u, u, u, u, ab, gdn_conv, gdn_conv, gdn_conv,
      lane_row(a_log), lane_row(dt_bias), gdn_norm_g.reshape(1, dk))


def _bias_table_kernel(tab_ref, o_ref, *, t):
    h = pl.program_id(0)
    m = pl.program_id(1)
    ri = lax.broadcasted_iota(jnp.int32, (t, t), 0)
    ci = lax.broadcasted_iota(jnp.int32, (t, t), 1)
    rel = m * t + ci - ri
    n = jnp.maximum(rel, 0)
    max_exact = NUM_BUCKETS // 2
    nf = jnp.maximum(n, 1).astype(F32)
    large = max_exact + (jnp.log(nf / max_exact) / math.log(MAX_DISTANCE / max_exact)
                         * (NUM_BUCKETS - max_exact)).astype(jnp.int32)
    large = jnp.minimum(large, NUM_BUCKETS - 1)
    bucket = jnp.where(n < max_exact, n, large)
    bias = jnp.zeros((t, t), F32)
    for b in range(NUM_BUCKETS):
        bias = jnp.where(bucket == b, tab_ref[b, h], bias)
    o_ref[...] = jnp.where(rel >= 0, bias * LOG2E, MASK_VALUE)


def _bias_table(rel_bias, seq, t):
    nb = seq // t
    heads = rel_bias.shape[1]
    return pl.pallas_call(
        functools.partial(_bias_table_kernel, t=t), grid=(heads, nb),
        in_specs=[pl.BlockSpec(memory_space=pltpu.SMEM)],
        out_specs=pl.BlockSpec((None, None, t, t), lambda h, m: (h, m, 0, 0)),
        out_shape=jax.ShapeDtypeStruct((heads, nb, t, t), F32),
        compiler_params=_cparams("parallel", "parallel"), name="rel_bias_table")(rel_bias)


def _attn_kernel(q_ref, k_ref, v_ref, tb_ref, lam_ref, gn_ref, o_ref, m_ref, l_ref, acc_ref,
                 *, t, heads, lambda_init):
    qi = pl.program_id(2)
    dh2 = q_ref.shape[1] // heads
    dh = dh2 // 2
    lane = lax.broadcasted_iota(jnp.int32, (t, dh2), 1)
    qqs = []
    for h in range(heads):
        qf = q_ref[:, h * dh2:(h + 1) * dh2].astype(F32) * (dh ** -0.5 * LOG2E)
        qqs.append(jnp.concatenate([jnp.where(lane < dh, qf, 0.0), jnp.where(lane >= dh, qf, 0.0)],
                                   axis=0).astype(BF16))

    hs = [slice(h * dh2, (h + 1) * dh2) for h in range(heads)]

    def update(j0, nblk):
        ks = pl.ds(pl.multiple_of(j0 * t, t), nblk * t)
        k_all = k_ref[ks, :]
        v_all = v_ref[ks, :]
        s_t = [_dot_nt(k_all[:, hs[h]], qqs[h]) for h in range(heads)]
        ps, alphas = [], []
        for h in range(heads):
            m = m_ref[h, 0:1, :]
            b = jnp.concatenate([tb_ref[h, qi - j0 - i] for i in range(nblk)], axis=0)
            s = jnp.concatenate([s_t[h][:, 0:t] + b, s_t[h][:, t:2 * t] + b], axis=1)
            m_new = jnp.maximum(m, jnp.max(s, axis=0, keepdims=True))
            alpha = jnp.exp2(m - m_new)
            p = jnp.exp2(s - m_new)
            m_ref[h] = jnp.broadcast_to(m_new, m_ref.shape[1:])
            l_ref[h] = jnp.broadcast_to(alpha * l_ref[h, 0:1, :] + jnp.sum(p, axis=0, keepdims=True),
                                        l_ref.shape[1:])
            alphas.append(alpha)
            ps.append(p.astype(BF16))
        pv = [_dot_tn(v_all[:, hs[h]], ps[h]) for h in range(heads)]
        for h in range(heads):
            acc_ref[h] = alphas[h] * acc_ref[h] + pv[h]

    m_ref[...] = jnp.full(m_ref.shape, MASK_VALUE, F32)
    l_ref[...] = jnp.zeros(l_ref.shape, F32)
    acc_ref[...] = jnp.zeros(acc_ref.shape, F32)
    odd = (qi + 1) % 2

    @pl.when(odd == 1)
    def _():
        update(0, 1)

    def pair(i, c):
        update(odd + 2 * i, 2)
        return c
    lax.fori_loop(0, (qi + 1) // 2, pair, 0)
    lp = lam_ref[...]
    lam = (jnp.exp(jnp.sum(lp[0:1] * lp[1:2], axis=-1, keepdims=True))
           - jnp.exp(jnp.sum(lp[2:3] * lp[3:4], axis=-1, keepdims=True)) + lambda_init)
    for h in range(heads):
        o12 = acc_ref[h] / l_ref[h, 0:1, :]
        o = (o12[:, 0:t] - lam * o12[:, t:2 * t]).T
        o_ref[:, h * dh2:(h + 1) * dh2] = (_rms(o, gn_ref[...]) * (1.0 - lambda_init)).astype(o_ref.dtype)


def _diff_attention(u, table, lam_params, norm_g, batch, seq, lambda_init):
    heads = DIFF_HEADS
    dh2 = norm_g.shape[0]
    hg = ATTN_HEADS_PER_STEP
    w = hg * dh2
    ng = heads // hg
    t = table.shape[2]
    nq = seq // t
    return pl.pallas_call(
        functools.partial(_attn_kernel, t=t, heads=hg, lambda_init=lambda_init),
        grid=(batch, ng, nq),
        in_specs=[pl.BlockSpec((t, w), lambda b, g, i: (b * nq + i, g)),
                  pl.BlockSpec((seq, w), lambda b, g, i: (b, ng + g)),
                  pl.BlockSpec((seq, w), lambda b, g, i: (b, 2 * ng + g)),
                  pl.BlockSpec((hg, nq, t, t), lambda b, g, i: (g, 0, 0, 0)),
                  _resident(lam_params.shape), _resident((1, dh2))],
        out_specs=pl.BlockSpec((t, w), lambda b, g, i: (b * nq + i, g)),
        out_shape=jax.ShapeDtypeStruct((batch * seq, heads * dh2), BF16),
        scratch_shapes=[pltpu.VMEM((hg, SUBLANES, 2 * t), F32), pltpu.VMEM((hg, SUBLANES, 2 * t), F32),
                        pltpu.VMEM((hg, dh2, 2 * t), F32)],
        compiler_params=_cparams("parallel", "parallel", "arbitrary"), name="diff_attention",
    )(u, u, u, table, lam_params, norm_g.reshape(1, dh2))


def _conformer_kernel(ga_ref, gb_ref, w_ref, b_ref, lg_ref, lb_ref, o_ref, pad_ref):
    s, c = ga_ref.shape
    rt = min(ELEM_ROWS, s)
    pad_ref[0:TOP_PAD, :] = jnp.zeros((TOP_PAD, c), F32)

    def glu(i, carry):
        r = pl.multiple_of(i * rt, rt)
        rs = pl.ds(r, rt)
        pad_ref[pl.ds(TOP_PAD + r, rt), :] = ga_ref[rs, :].astype(F32) * _sigmoid(gb_ref[rs, :].astype(F32))
        return carry
    lax.fori_loop(0, s // rt, glu, 0)

    ct = CONF_ROWS

    def conv(i, carry):
        r = pl.multiple_of(i * ct, ct)
        tap = _delayed(pad_ref, r, ct, CONF_WIDTH - 1)
        acc = jnp.broadcast_to(b_ref[...], (ct, c))
        for j in range(CONF_WIDTH):
            acc = acc + w_ref[j:j + 1, :] * tap(CONF_WIDTH - 1 - j)
        mu = jnp.mean(acc, axis=-1, keepdims=True)
        xc = acc - mu
        var = jnp.mean(xc * xc, axis=-1, keepdims=True)
        y = xc * lax.rsqrt(var + EPS) * lg_ref[...] + lb_ref[...]
        o_ref[pl.ds(r, ct), :] = _silu(y).astype(o_ref.dtype)
        return carry
    lax.fori_loop(0, s // ct, conv, 0)


def _conformer(u, w, b, ln_g, ln_b, batch, seq, col0):
    c = w.shape[1]
    return pl.pallas_call(
        _conformer_kernel, grid=(batch,),
        in_specs=[pl.BlockSpec((seq, c), lambda i: (i, col0)),
                  pl.BlockSpec((seq, c), lambda i: (i, col0 + 1)),
                  _resident(w.shape), _resident((1, c)), _resident((1, c)), _resident((1, c))],
        out_specs=pl.BlockSpec((seq, c), lambda i: (i, 0)),
        out_shape=jax.ShapeDtypeStruct((batch * seq, c), BF16),
        scratch_shapes=[pltpu.VMEM((seq + TOP_PAD, c), F32)],
        compiler_params=_cparams("parallel"), name="conformer_conv",
    )(u, u, w, b.reshape(1, c), ln_g.reshape(1, c), ln_b.reshape(1, c))


def _proj_router_kernel(y1_ref, y2_ref, w1_ref, w2_ref, h_ref, g_ref, wrt_ref, upper_ref,
                        h_out_ref, hn_ref, route_ref, gate_ref, cnt_ref, carry_ref):
    i = pl.program_id(0)
    tm = h_ref.shape[0]
    n_e = N_EXPERTS

    @pl.when(i == 0)
    def _():
        carry_ref[...] = jnp.zeros(carry_ref.shape, F32)

    h = h_ref[...] + _dot(y1_ref[...], w1_ref[...]) + _dot(y2_ref[...], w2_ref[...])
    h_out_ref[...] = h
    xn = _rms(h, g_ref[...])
    _store_token_tiles(hn_ref, 0, xn)

    logits = _dot_nt(wrt_ref[...], xn.astype(BF16))[0:n_e, :]
    sub = lax.broadcasted_iota(jnp.int32, logits.shape, 0)
    m1 = jnp.max(logits, axis=0, keepdims=True)
    i1 = jnp.min(jnp.where(logits == m1, sub, n_e), axis=0, keepdims=True)
    rest = jnp.where(sub == i1, -jnp.inf, logits)
    m2 = jnp.max(rest, axis=0, keepdims=True)
    i2 = jnp.min(jnp.where(rest == m2, sub, n_e), axis=0, keepdims=True)
    e = jnp.exp(m2 - m1)
    g1 = 1.0 / (1.0 + e)
    oh1 = sub == i1
    oh2 = sub == i2
    both = oh1.astype(F32) + oh2.astype(F32)
    csum = _dot(both.astype(BF16), upper_ref[...])
    carry = carry_ref[:, 0:1]
    before = csum - both + carry
    total = carry + csum[:, tm - 1:tm]
    carry_ref[...] = jnp.broadcast_to(total, carry_ref.shape)
    cnt_ref[...] = jnp.broadcast_to(total, cnt_ref.shape).astype(jnp.int32)
    r1 = jnp.sum(jnp.where(oh1, before, 0.0), axis=0, keepdims=True).astype(jnp.int32)
    r2 = jnp.sum(jnp.where(oh2, before, 0.0), axis=0, keepdims=True).astype(jnp.int32)
    route_ref[...] = jnp.where(sub == 0, i1, jnp.where(sub == 1, i2, jnp.where(sub == 2, r1,
                               jnp.where(sub == 3, r2, 0))))
    grow = jnp.where(sub == 0, g1, jnp.where(sub == 1, e * g1, 0.0))
    sel = (lax.broadcasted_iota(jnp.int32, (n_e, LANES), 0)
           == lax.broadcasted_iota(jnp.int32, (n_e, LANES), 1)).astype(F32)
    gate_ref[...] = lax.dot_general(grow, sel, (((0,), (0,)), ((), ())), precision=HIGHEST,
                                    preferred_element_type=F32)


def _proj_router(y1, y2, w, h, g, wr):
    t, d = h.shape
    k1, k2 = y1.shape[1], y2.shape[1]
    tm = min(ROW_TILE, t)
    wrt = jnp.zeros((LANES, d), BF16).at[:N_EXPERTS].set(wr.T.astype(BF16))
    upper = (jnp.arange(tm)[:, None] <= jnp.arange(tm)[None, :]).astype(BF16)
    row = lambda width: pl.BlockSpec((tm, width), lambda i: (i, 0))
    return pl.pallas_call(
        _proj_router_kernel, grid=(t // tm,),
        in_specs=[row(k1), row(k2), _resident((k1, d)), _resident((k2, d)), row(d),
                  _resident((1, d)), _resident((LANES, d)), _resident((tm, tm))],
        out_specs=[row(d), pl.BlockSpec((tm * d // LANES, LANES), lambda i: (i, 0)),
                   pl.BlockSpec((N_EXPERTS, tm), lambda i: (0, i)), row(LANES),
                   pl.BlockSpec((N_EXPERTS, LANES), lambda i: (0, 0))],
        out_shape=[jax.ShapeDtypeStruct((t, d), F32), jax.ShapeDtypeStruct((t * d // LANES, LANES), F32),
                   jax.ShapeDtypeStruct((N_EXPERTS, t), jnp.int32), jax.ShapeDtypeStruct((t, LANES), F32),
                   jax.ShapeDtypeStruct((N_EXPERTS, LANES), jnp.int32)],
        scratch_shapes=[pltpu.VMEM((N_EXPERTS, LANES), F32)],
        compiler_params=_cparams("arbitrary"), name="l1_out_proj_router",
    )(y1, y2, w[:k1], w[k1:], h, g.reshape(1, d), wrt, upper)


def _dispatch_kernel(dest_hbm, pad_hbm, x_ref, xs_hbm, idx_smem, pad_smem, zero_ref, sem_idx, sem_x):
    i = pl.program_id(0)
    n = x_ref.shape[0] // SUBLANES
    m = TOP_K * n
    cp = pltpu.make_async_copy(dest_hbm.at[i], idx_smem, sem_idx)
    cp.start()
    cp.wait()

    def body(r, c):
        for k in range(TOP_K):
            _token_tile_copy(x_ref, r, xs_hbm, idx_smem[k * n + r], sem_x).start(priority=k % 2)
        return c
    lax.fori_loop(0, n, body, 0, unroll=8)
    for _ in range(TOP_K):
        pltpu.make_async_copy(x_ref, xs_hbm.at[pl.ds(0, n * SUBLANES)], sem_x).wait()

    @pl.when(i == pl.num_programs(0) - 1)
    def _():
        n_pad = pad_smem.shape[0]
        cp = pltpu.make_async_copy(pad_hbm, pad_smem, sem_idx)
        cp.start()
        cp.wait()
        zero_ref[...] = jnp.zeros(zero_ref.shape, F32)

        def fill(j, c):
            for q in range(2):
                _token_tile_copy(zero_ref, 0, xs_hbm, pad_smem[2 * j + q], sem_x).start(priority=q)
            return c
        lax.fori_loop(0, n_pad // 2, fill, 0, unroll=8)
        pltpu.make_async_copy(xs_hbm.at[pl.ds(0, n_pad * SUBLANES)], xs_hbm.at[pl.ds(0, n_pad * SUBLANES)],
                              sem_x).wait()


def _dispatch(hn_tiles, dest, pad_rows, t, d):
    n = min(MOE_ROWS, t)
    steps = t // n
    tpt = d // LANES
    p = t * TOP_K + N_EXPERTS * MOE_ROWS
    dest_steps = jnp.concatenate([dest[k].reshape(steps, n) for k in range(TOP_K)], axis=1)
    return pl.pallas_call(
        _dispatch_kernel, grid=(steps,),
        in_specs=[pl.BlockSpec(memory_space=pl.ANY), pl.BlockSpec(memory_space=pl.ANY),
                  pl.BlockSpec((n * tpt, LANES), lambda i: (i, 0))],
        out_specs=pl.BlockSpec(memory_space=pl.ANY),
        out_shape=jax.ShapeDtypeStruct((p * tpt, LANES), F32),
        scratch_shapes=[pltpu.SMEM((TOP_K * n,), jnp.int32), pltpu.SMEM(pad_rows.shape, jnp.int32),
                        pltpu.VMEM((tpt, LANES), F32), pltpu.SemaphoreType.DMA, pltpu.SemaphoreType.DMA],
        compiler_params=_cparams("arbitrary"), name="moe_dispatch")(dest_steps, pad_rows, hn_tiles)


def _dispatch_plan(route, counts, t):
    tm = MOE_ROWS
    p = t * TOP_K + N_EXPERTS * tm
    padded = (counts + tm - 1) // tm * tm
    pend = jnp.cumsum(padded)
    pstart = pend - padded
    experts = jnp.arange(N_EXPERTS, dtype=jnp.int32)[:, None]
    dest = jnp.stack([jnp.sum(jnp.where(route[k][None, :] == experts, pstart[:, None], 0), axis=0)
                      + route[TOP_K + k] for k in range(TOP_K)])
    n_pad = p - t * TOP_K
    gaps = jnp.concatenate([padded - counts, (p - pend[-1])[None]])
    gap_end = jnp.cumsum(gaps)
    gap_row0 = jnp.concatenate([pstart + counts, pend[-1:]])
    j = jnp.arange(n_pad, dtype=jnp.int32)
    which = jnp.searchsorted(gap_end, j, side='right')
    pad_rows = (gap_row0[which] + j - (gap_end - gaps)[which]).astype(jnp.int32)
    nb = p // tm
    blk_e = jnp.minimum(jnp.searchsorted(pend, jnp.arange(nb, dtype=jnp.int32) * tm, side='right'),
                        N_EXPERTS - 1).astype(jnp.int32)
    return dest.astype(jnp.int32), pad_rows, blk_e, (pend[-1:] // tm).astype(jnp.int32)


def _experts_kernel(blk_e_ref, used_ref, x_ref, wgu_ref, wd_ref, o_ref, hid_ref):
    f = wd_ref.shape[0]
    tm = hid_ref.shape[0]
    live = pl.program_id(0) < used_ref[0]

    @pl.when(live)
    def _():
        x = _load_token_tiles(x_ref, 0, tm).astype(BF16)
        for c in range(0, f, MOE_SUB_TILE):
            hid_ref[:, c:c + MOE_SUB_TILE] = (
                _silu(_dot(x, wgu_ref[:, c:c + MOE_SUB_TILE]))
                * _dot(x, wgu_ref[:, f + c:f + c + MOE_SUB_TILE])).astype(BF16)
        _store_token_tiles(o_ref, 0, _dot(hid_ref[...], wd_ref[...]))

    @pl.when(jnp.logical_not(live))
    def _():
        o_ref[...] = jnp.zeros(o_ref.shape, F32)


def _experts(xs, blk_e, n_used, w_gate_up, w_down):
    f, d = w_down.shape[1], w_down.shape[2]
    tm = MOE_ROWS
    tpt = d // LANES
    assert f % MOE_SUB_TILE == 0
    grid_spec = pltpu.PrefetchScalarGridSpec(
        num_scalar_prefetch=2, grid=(blk_e.shape[0],),
        in_specs=[pl.BlockSpec((tm * tpt, LANES), lambda i, e, u: (i, 0)),
                  pl.BlockSpec((None, d, 2 * f), lambda i, e, u: (e[i], 0, 0), pipeline_mode=pl.Buffered(1)),
                  pl.BlockSpec((None, f, d), lambda i, e, u: (e[i], 0, 0), pipeline_mode=pl.Buffered(1))],
        out_specs=pl.BlockSpec((tm * tpt, LANES), lambda i, e, u: (i, 0)),
        scratch_shapes=[pltpu.VMEM((tm, f), BF16)])
    return pl.pallas_call(
        _experts_kernel, grid_spec=grid_spec, out_shape=jax.ShapeDtypeStruct(xs.shape, F32),
        compiler_params=_cparams("parallel"), name="moe_experts",
    )(blk_e, n_used, xs, w_gate_up, w_down)


def _combine_kernel(dest_hbm, y_hbm, gate_ref, h_ref, g_ref, p_ref, wp_ref, wg_ref, fg_ref, o_ref,
                    idx0, idx1, ybuf, sem_idx, sem_rows):
    i = pl.program_id(0)
    steps = pl.num_programs(0)
    n = h_ref.shape[0]
    m = TOP_K * n
    slot = i % 2
    idxs = (idx0, idx1)

    def idx_copy(blk, s):
        return pltpu.make_async_copy(dest_hbm.at[blk], idxs[s], sem_idx.at[s])

    def issue_tiles(s):
        def body(r, c):
            for k in range(TOP_K):
                _token_tile_copy(y_hbm, idxs[s][k * n + r], ybuf, s * m + k * n + r,
                                 sem_rows.at[s]).start(priority=k % 2)
            return c
        lax.fori_loop(0, n, body, 0, unroll=8)

    @pl.when(i == 0)
    def _():
        idx_copy(0, 0).start()
        idx_copy(0, 0).wait()
        idx_copy(1, 1).start()
        issue_tiles(0)

    for s in range(2):
        @pl.when(slot == s)
        def _(s=s):
            idx_copy(i + 1, 1 - s).wait()

            @pl.when(i + 2 <= steps)
            def _():
                idx_copy(i + 2, s).start()

            issue_tiles(1 - s)
            _token_tiles_wait(y_hbm, ybuf, s * m, m, sem_rows.at[s])

    nc = n // COMBINE_CHUNKS
    rows = [pl.ds(c * nc, nc) for c in range(COMBINE_CHUNKS)]
    xs = []
    for c, rs in enumerate(rows):
        gates = gate_ref[rs, :]
        xs.append(h_ref[rs, :] + gates[:, 0:1] * _load_token_tiles(ybuf, slot * m + c * nc, nc)
                  + gates[:, 1:2] * _load_token_tiles(ybuf, slot * m + n + c * nc, nc))
    xns = [_rms(x, g_ref[...]).astype(BF16) for x in xs]
    gate = [_sigmoid(_dot(xn, wg_ref[...])) for xn in xns]
    emb = [_dot(p_ref[rs, :].astype(BF16), wp_ref[...]) for rs in rows]
    for rs, x, e, g in zip(rows, xs, emb, gate):
        o_ref[rs, :] = _rms(x + e * g, fg_ref[...])

    @pl.when(i == steps - 1)
    def _():
        _token_tiles_wait(y_hbm, ybuf, (1 - slot) * m, m, sem_rows.at[1 - slot])


def _combine_ple_final(dest, y_tiles, gates, h, g, p, layer, wp, wg, final_g):
    t, d = h.shape
    e = p.shape[2]
    n = min(COMBINE_ROWS, t)
    m = TOP_K * n
    steps = t // n
    dest_steps = jnp.concatenate([dest[k].reshape(steps, n) for k in range(TOP_K)], axis=1)
    dest_steps = jnp.concatenate([dest_steps, jnp.zeros((1, m), jnp.int32)], axis=0)
    return pl.pallas_call(
        _combine_kernel, grid=(steps,),
        in_specs=[pl.BlockSpec(memory_space=pl.ANY), pl.BlockSpec(memory_space=pl.ANY),
                  pl.BlockSpec((n, LANES), lambda i: (i, 0)),
                  pl.BlockSpec((n, d), lambda i: (i, 0)), _resident((1, d)),
                  pl.BlockSpec((None, n, e), lambda i: (layer, i, 0)), _resident((e, d)), _resident((d, d)),
                  _resident((1, d))],
        out_specs=pl.BlockSpec((n, d), lambda i: (i, 0)),
        out_shape=jax.ShapeDtypeStruct((t, d), F32),
        scratch_shapes=[pltpu.SMEM((m,), jnp.int32), pltpu.SMEM((m,), jnp.int32),
                        pltpu.VMEM((2 * m * d // LANES, LANES), F32),
                        pltpu.SemaphoreType.DMA((2,)), pltpu.SemaphoreType.DMA((2,))],
        compiler_params=_cparams("arbitrary"), name="moe_combine_ple_final",
    )(dest_steps, y_tiles, gates, h, g.reshape(1, d), p, wp, wg, final_g.reshape(1, d))


def kernel(x, p, norm_mix_g, norm_ffn_g, norm_ple_g, final_norm_g, ev_w_in, ev_conv_a, ev_gdn_conv, ev_gdn_A_log, ev_gdn_dt_bias, ev_gdn_norm_g, ev_w_out, od_w_in, od_lambda, od_diff_norm_g, od_conf_dw_w, od_conf_dw_b, od_conf_ln_g, od_conf_ln_b, od_w_out, rel_bias, ffn_w_gate_up, ffn_w_down, moe_router, moe_w_gate_up, moe_w_down, ple_w_proj, ple_w_gate):
    batch, seq, d = x.shape
    t = batch * seq
    depth = p.shape[0]
    assert depth == 2 and seq % GDN_CHUNK == 0
    h = x.reshape(t, d)
    pf = p.reshape(depth, t, p.shape[-1])

    heads = GDN_HEADS
    n_main = ev_w_in.shape[2] - 2 * heads
    w_in = ev_w_in[0]
    w_ab = jnp.zeros((d, LANES), BF16).at[:, :2 * heads].set(w_in[:, n_main:].astype(BF16))
    u, ab = _norm_proj(h, norm_mix_g[0], w_in[:, :n_main].astype(BF16), w_ab, name="l0_in_proj")
    ya = _gated_conv(u, ev_conv_a[0], batch, seq)
    yb = _l0_mixer(u, ab, ev_gdn_conv[0], ev_gdn_A_log[0], ev_gdn_dt_bias[0], ev_gdn_norm_g[0], batch, seq)
    h = _proj_residual([ya, yb], ev_w_out[0].astype(BF16), h, name="l0_out_proj")
    f = ffn_w_down.shape[1]
    lambda_init = 0.8 - 0.6 * math.exp(-0.3 * 1)
    h, u = _ffn_ple_proj(h, norm_ffn_g[0], ffn_w_gate_up[0, :, :f].astype(BF16),
                         ffn_w_gate_up[0, :, f:].astype(BF16), ffn_w_down[0].astype(BF16),
                         norm_ple_g[0], pf, 0, ple_w_proj[0].astype(BF16), ple_w_gate[0].astype(BF16),
                         norm_mix_g[1], od_w_in[0].astype(BF16), name="l0_ffn_ple_l1_in_proj")

    table = _bias_table(rel_bias, seq, min(ATTN_TILE, seq))
    o_attn = _diff_attention(u, table, od_lambda[0], od_diff_norm_g[0], batch, seq, lambda_init)
    c_conf = od_conf_dw_w.shape[2]
    o_conf = _conformer(u, od_conf_dw_w[0], od_conf_dw_b[0], od_conf_ln_g[0], od_conf_ln_b[0],
                        batch, seq, 3 * DIFF_HEADS * od_diff_norm_g.shape[1] // c_conf)
    h, hn, route, gates, counts = _proj_router(o_attn, o_conf, od_w_out[0].astype(BF16), h, norm_ffn_g[1],
                                               moe_router[0])
    dest, pad_rows, blk_e, n_used = _dispatch_plan(route, counts[:, 0], t)
    xs = _dispatch(hn, dest, pad_rows, t, d)
    y = _experts(xs, blk_e, n_used, moe_w_gate_up[0].astype(BF16), moe_w_down[0].astype(BF16))
    out = _combine_ple_final(dest, y, gates, h, norm_ple_g[1], pf, 1, ple_w_proj[1].astype(BF16),
                             ple_w_gate[1].astype(BF16), final_norm_g)
    return out.reshape(batch, seq, d)
```

```python
import functools
import math

import jax
import jax.numpy as jnp
from jax import lax
from jax.experimental import pallas as pl
from jax.experimental.pallas import tpu as pltpu

F32 = jnp.float32
BF16 = jnp.bfloat16
HIGHEST = lax.Precision.HIGHEST

EPS = 1e-6
CONV_A_WIDTH = 3
GDN_HEADS = 4
GDN_CONV_WIDTH = 4
GDN_CHUNK = 64
DIFF_HEADS = 4
NUM_BUCKETS = 32
MAX_DISTANCE = 128
CONF_WIDTH = 31
N_EXPERTS = 8
TOP_K = 2

LANES = 128
SUBLANES = 8
VMEM_LIMIT_BYTES = 56 * 1024 * 1024
MASK_VALUE = -1e30
LOG2E = math.log2(math.e)

ROW_TILE = 512
COL_TILE = 512
FFN_CHUNKS = 2
ELEM_ROWS = 256
CONF_ROWS = 256
PREP_CHUNKS = 8
ATTN_TILE = 256
ATTN_HEADS_PER_STEP = 4
MOE_ROWS = 1024
DISPATCH_ROWS = 2048
MOE_SUB_TILE = 256
COMBINE_ROWS = 512
COMBINE_CHUNKS = 4
TOP_PAD = 32


def _cparams(*sem):
    return pltpu.CompilerParams(dimension_semantics=sem, vmem_limit_bytes=VMEM_LIMIT_BYTES)


def _resident(shape):
    nd = len(shape)
    return pl.BlockSpec(shape, lambda *_: (0,) * nd, pipeline_mode=pl.Buffered(1))


def _rms(x, g):
    return x * lax.rsqrt(jnp.mean(x * x, axis=-1, keepdims=True) + EPS) * g


def _sigmoid(x):
    return jax.nn.sigmoid(x)


def _silu(x):
    return x * jax.nn.sigmoid(x)


def _softplus(x):
    return jnp.maximum(x, 0.0) + jnp.log1p(jnp.exp(-jnp.abs(x)))


def _dot(a, b, **kw):
    return jnp.dot(a, b, preferred_element_type=F32, **kw)


def _dot_nt(a, b):
    return lax.dot_general(a, b, (((1,), (1,)), ((), ())), preferred_element_type=F32)


def _dot_tn(a, b):
    return lax.dot_general(a, b, (((0,), (0,)), ((), ())), preferred_element_type=F32)


def _delayed(pad_ref, r, rows, max_delay):
    lead = -(-max_delay // SUBLANES) * SUBLANES
    win = pad_ref[pl.ds(TOP_PAD + r - lead, rows + lead), :]
    rolled = {0: win}

    def tap(d):
        a, b = divmod(d, SUBLANES)
        if b not in rolled:
            rolled[b] = pltpu.roll(win, b, 0)
        start = lead - SUBLANES * a
        return rolled[b][start:start + rows, :]
    return tap


def _load_token_tiles(ref, first_token, n):
    return jnp.concatenate([ref[pl.ds(first_token * SUBLANES + s, n, stride=SUBLANES), :]
                            for s in range(SUBLANES)], axis=1)


def _store_token_tiles(ref, first_token, x):
    n = x.shape[0]
    for s in range(SUBLANES):
        ref[pl.ds(first_token * SUBLANES + s, n, stride=SUBLANES), :] = x[:, s * LANES:(s + 1) * LANES]


def _token_tile_copy(src_hbm, src_token, dst_vmem, dst_token, sem):
    return pltpu.make_async_copy(src_hbm.at[pl.ds(pl.multiple_of(src_token * SUBLANES, SUBLANES), SUBLANES)],
                                 dst_vmem.at[pl.ds(pl.multiple_of(dst_token * SUBLANES, SUBLANES), SUBLANES)],
                                 sem)


def _token_tiles_wait(src_hbm, dst_vmem, first_token, n, sem):
    pltpu.make_async_copy(src_hbm.at[pl.ds(0, n * SUBLANES)],
                          dst_vmem.at[pl.ds(pl.multiple_of(first_token * SUBLANES, SUBLANES), n * SUBLANES)],
                          sem).wait()


def _norm_proj_kernel(h_ref, g_ref, w_ref, *rest, tn, with_aux):
    xn = _rms(h_ref[...], g_ref[...]).astype(BF16)
    if with_aux:
        w2_ref, o_ref, o2_ref = rest
        o2_ref[...] = _dot(xn, w2_ref[...])
    else:
        (o_ref,) = rest
    n = w_ref.shape[1]
    for c in range(0, n, tn):
        o_ref[:, c:c + tn] = _dot(xn, w_ref[:, c:c + tn]).astype(o_ref.dtype)


def _norm_proj(h, g, w, w_aux=None, *, name):
    t, d = h.shape
    n = w.shape[1]
    tm = min(ROW_TILE, t)
    tn = COL_TILE if n % COL_TILE == 0 else n
    in_specs = [pl.BlockSpec((tm, d), lambda i: (i, 0)), _resident((1, d)), _resident((d, n))]
    out_shape = [jax.ShapeDtypeStruct((t, n), BF16)]
    out_specs = [pl.BlockSpec((tm, n), lambda i: (i, 0))]
    args = [h, g.reshape(1, d), w]
    if w_aux is not None:
        in_specs.append(_resident(w_aux.shape))
        out_shape.append(jax.ShapeDtypeStruct((t, w_aux.shape[1]), F32))
        out_specs.append(pl.BlockSpec((tm, w_aux.shape[1]), lambda i: (i, 0)))
        args.append(w_aux)
    out = pl.pallas_call(
        functools.partial(_norm_proj_kernel, tn=tn, with_aux=w_aux is not None),
        grid=(t // tm,), in_specs=in_specs, out_specs=out_specs, out_shape=out_shape,
        compiler_params=_cparams("parallel"), name=name)(*args)
    return out if w_aux is not None else out[0]


def _proj_residual_kernel(*refs):
    n = (len(refs) - 2) // 2
    y_refs, w_refs, h_ref, o_ref = refs[:n], refs[n:2 * n], refs[2 * n], refs[2 * n + 1]
    acc = h_ref[...]
    for y_ref, w_ref in zip(y_refs, w_refs):
        acc = acc + _dot(y_ref[...], w_ref[...])
    o_ref[...] = acc


def _proj_residual(ys, w, h, *, name):
    t, d = h.shape
    tm = min(ROW_TILE, t)
    ws, r0 = [], 0
    for y in ys:
        ws.append(w[r0:r0 + y.shape[1]])
        r0 += y.shape[1]
    return pl.pallas_call(
        _proj_residual_kernel, grid=(t // tm,),
        in_specs=([pl.BlockSpec((tm, y.shape[1]), lambda i: (i, 0)) for y in ys]
                  + [_resident(wi.shape) for wi in ws] + [pl.BlockSpec((tm, d), lambda i: (i, 0))]),
        out_specs=pl.BlockSpec((tm, d), lambda i: (i, 0)),
        out_shape=jax.ShapeDtypeStruct((t, d), F32),
        compiler_params=_cparams("parallel"), name=name)(*ys, *ws, h)


def _ffn_ple_proj_kernel(h_ref, gf_ref, wg_ref, wu_ref, wd_ref, gp_ref, p_ref, wp_ref, wpg_ref,
                         gm_ref, win_ref, h_out_ref, u_ref, *, tf, tn):
    tm = h_ref.shape[0]
    nc = tm // FFN_CHUNKS
    rows = [pl.ds(c * nc, nc) for c in range(FFN_CHUNKS)]
    xs = [h_ref[rs, :] for rs in rows]
    xns = [_rms(x, gf_ref[...]).astype(BF16) for x in xs]
    accs = list(xs)
    for c in range(0, wg_ref.shape[1], tf):
        gates = [_dot(xn, wg_ref[:, c:c + tf]) for xn in xns]
        ups = [_dot(xn, wu_ref[:, c:c + tf]) for xn in xns]
        hids = [(_silu(g) * u).astype(BF16) for g, u in zip(gates, ups)]
        accs = [a + _dot(hd, wd_ref[c:c + tf, :]) for a, hd in zip(accs, hids)]
    xn2 = [_rms(x, gp_ref[...]).astype(BF16) for x in accs]
    pgate = [_sigmoid(_dot(xn, wpg_ref[...])) for xn in xn2]
    emb = [_dot(p_ref[rs, :].astype(BF16), wp_ref[...]) for rs in rows]
    h3 = [x + e * g for x, e, g in zip(accs, emb, pgate)]
    for rs, x in zip(rows, h3):
        h_out_ref[rs, :] = x
    xn3 = [_rms(x, gm_ref[...]).astype(BF16) for x in h3]
    for c in range(0, win_ref.shape[1], tn):
        for rs, xn in zip(rows, xn3):
            u_ref[rs, c:c + tn] = _dot(xn, win_ref[:, c:c + tn]).astype(u_ref.dtype)


def _ff_tile(f, cap):
    best = LANES
    for c in range(LANES, cap + 1, LANES):
        if f % c == 0:
            best = c
    return best


def _ffn_ple_proj(h, g_ffn, wg, wu, wd, g_ple, p, layer, wp, wpg, g_mix, w_in, *, name):
    t, d = h.shape
    f = wg.shape[1]
    e = p.shape[2]
    n = w_in.shape[1]
    tm = min(ROW_TILE, t)
    tn = COL_TILE if n % COL_TILE == 0 else n
    row = lambda w: pl.BlockSpec((tm, w), lambda i: (i, 0))
    return pl.pallas_call(
        functools.partial(_ffn_ple_proj_kernel, tf=_ff_tile(f, 1536), tn=tn), grid=(t // tm,),
        in_specs=[row(d), _resident((1, d)), _resident((d, f)), _resident((d, f)), _resident((f, d)),
                  _resident((1, d)), pl.BlockSpec((None, tm, e), lambda i: (layer, i, 0)),
                  _resident((e, d)), _resident((d, d)), _resident((1, d)), _resident((d, n))],
        out_specs=[row(d), row(n)],
        out_shape=[jax.ShapeDtypeStruct((t, d), F32), jax.ShapeDtypeStruct((t, n), BF16)],
        compiler_params=_cparams("parallel"), name=name,
    )(h, g_ffn.reshape(1, d), wg, wu, wd, g_ple.reshape(1, d), p, wp, wpg, g_mix.reshape(1, d), w_in)


def _unit_lower_inverses(mats):
    n = mats[0].shape[0]
    row = lax.broadcasted_iota(jnp.int32, (n, n), 0)
    col = lax.broadcasted_iota(jnp.int32, (n, n), 1)
    eye = (row == col).astype(F32)
    same16 = (row // 16) == (col // 16)
    same32 = (row // 32) == (col // 32)
    off32 = jnp.logical_and(same32, jnp.logical_not(same16))

    def mm(ps, qs):
        return [_dot(p.astype(BF16), q.astype(BF16)) for p, q in zip(ps, qs)]

    ad = [jnp.where(same16, a, 0.0) for a in mats]
    a2 = mm(ad, ad)
    x = mm([eye - t for t in ad], [eye + t for t in a2])
    a4 = mm(a2, a2)
    x = mm(x, [eye + t for t in a4])
    a8 = mm(a4, a4)
    x = mm(x, [eye + t for t in a8])
    y = mm([jnp.where(off32, a, 0.0) for a in mats], x)
    x = [t - c for t, c in zip(x, mm(x, y))]
    y = mm([jnp.where(same32, 0.0, a) for a in mats], x)
    return [t - c for t, c in zip(x, mm(x, y))]


def _seq_tiles(pad_ref, s):
    rt = min(ELEM_ROWS, s)
    n_tiles = s // rt

    def rows(i):
        return pl.ds(pl.multiple_of(i * rt, rt), rt)

    def fill_pad(fn):
        pad_ref[0:TOP_PAD, :] = jnp.zeros((TOP_PAD, pad_ref.shape[1]), F32)

        def body(i, c):
            r = pl.multiple_of(i * rt, rt)
            pad_ref[pl.ds(TOP_PAD + r, rt), :] = fn(rows(i))
            return c
        lax.fori_loop(0, n_tiles, body, 0)

    def conv_tile(i, w_ref, width):
        tap = _delayed(pad_ref, pl.multiple_of(i * rt, rt), rt, width - 1)
        acc = None
        for j in range(width):
            term = w_ref[j:j + 1, :] * tap(width - 1 - j)
            acc = term if acc is None else acc + term
        return acc

    return n_tiles, rows, fill_pad, conv_tile


def _gated_conv_kernel(bg_ref, cg_ref, xin_ref, w_ref, o_ref, pad_ref):
    n_tiles, rows, fill_pad, conv_tile = _seq_tiles(pad_ref, bg_ref.shape[0])
    fill_pad(lambda rs: cg_ref[rs, :].astype(F32) * xin_ref[rs, :].astype(F32))

    def body(i, c):
        acc = conv_tile(i, w_ref, CONV_A_WIDTH)
        o_ref[rows(i), :] = (bg_ref[rows(i), :].astype(F32) * acc).astype(o_ref.dtype)
        return c
    lax.fori_loop(0, n_tiles, body, 0)


def _gated_conv(u, conv_a, batch, seq):
    cw = conv_a.shape[1]
    return pl.pallas_call(
        _gated_conv_kernel, grid=(batch,),
        in_specs=[pl.BlockSpec((seq, cw), lambda b, j=j: (b, j)) for j in range(3)] + [_resident(conv_a.shape)],
        out_specs=pl.BlockSpec((seq, cw), lambda b: (b, 0)),
        out_shape=jax.ShapeDtypeStruct((batch * seq, cw), BF16),
        scratch_shapes=[pltpu.VMEM((seq + TOP_PAD, cw), F32)],
        compiler_params=_cparams("parallel"), name="l0_gated_conv")(u, u, u, conv_a)


def _l0_mixer_kernel(q_ref, k_ref, v_ref, og_ref, ab_ref, wq_ref, wk_ref, wv_ref, alog_ref, dtb_ref, gn_ref,
                     o_ref,
                     pad_ref, qs_ref, ks_ref, vs_ref, us_ref, ws_ref, qk_ref, egl_ref, st_ref):
    s = q_ref.shape[0]
    heads = GDN_HEADS
    dk = q_ref.shape[1] // heads
    c64 = GDN_CHUNK
    n_chunks = s // c64
    n_tiles, rows, fill_pad, conv_tile = _seq_tiles(pad_ref, s)

    def l2n(x, scale):
        parts = []
        for h in range(heads):
            xh = x[:, h * dk:(h + 1) * dk]
            inv = lax.rsqrt(jnp.sum(xh * xh, axis=-1, keepdims=True) + EPS)
            parts.append(xh * (inv * scale))
        return jnp.concatenate(parts, axis=1)

    for src_ref, w_ref, dst_ref, post in (
            (q_ref, wq_ref, qs_ref, lambda x: l2n(x, dk ** -0.5)),
            (k_ref, wk_ref, ks_ref, lambda x: l2n(x, 1.0)),
            (v_ref, wv_ref, vs_ref, lambda x: x)):
        fill_pad(lambda rs, src_ref=src_ref: src_ref[rs, :].astype(F32))

        def conv_body(i, c, w_ref=w_ref, dst_ref=dst_ref, post=post):
            dst_ref[rows(i), :] = post(_silu(conv_tile(i, w_ref, GDN_CONV_WIDTH)))
            return c
        lax.fori_loop(0, n_tiles, conv_body, 0)

    ri = lax.broadcasted_iota(jnp.int32, (c64, c64), 0)
    ci = lax.broadcasted_iota(jnp.int32, (c64, c64), 1)
    tril = ri >= ci
    strict = ri > ci
    ltri = tril.astype(F32)

    group = PREP_CHUNKS if n_chunks % PREP_CHUNKS == 0 else 1

    def chunk_prep(cg, carry):
        chunks = []
        for cc in range(group):
            c = cg * group + cc
            rs = pl.ds(pl.multiple_of(c * c64, c64), c64)
            chunks.append((c, rs, ab_ref[rs, :], qs_ref[rs, :], ks_ref[rs, :], vs_ref[rs, :]))
        inst = []
        gcs = [_dot(ltri, -jnp.exp(alog_ref[...]) * _softplus(ab + dtb_ref[...]), precision=HIGHEST)
               for _, _, ab, _, _, _ in chunks]
        for (c, rs, ab, q_all, k_all, v_all), gc in zip(chunks, gcs):
            beta = _sigmoid(ab)
            gct = gc.T
            for h in range(heads):
                hs = slice(h * dk, (h + 1) * dk)
                gcol = gc[:, h:h + 1]
                glast = gc[c64 - 1:c64, h:h + 1]
                bcol = beta[:, heads + h:heads + h + 1]
                decay = jnp.where(tril, jnp.exp(jnp.where(tril, gcol - gct[h:h + 1, :], 0.0)), 0.0)
                kh, qh, vh = k_all[:, hs], q_all[:, hs], v_all[:, hs]
                kb = kh * bcol
                egc = jnp.exp(gcol)
                inst.append(dict(decay=decay, kh=kh, qh=qh, kb=kb, egc=egc,
                                 rhs=jnp.concatenate([vh * bcol, kb * egc], axis=1).astype(BF16),
                                 kd=kh * jnp.exp(glast - gcol),
                                 eg=jnp.broadcast_to(jnp.exp(glast), (SUBLANES, dk))))
        kqs = [_dot_nt(jnp.concatenate([t["kb"], t["qh"]], axis=0).astype(BF16), t["kh"].astype(BF16))
               for t in inst]
        minvs = _unit_lower_inverses([jnp.where(strict, kq[0:c64] * t["decay"], 0.0)
                                      for kq, t in zip(kqs, inst)])
        uws = [_dot(m.astype(BF16), t["rhs"]) for m, t in zip(minvs, inst)]
        for ci, (c, rs, _, _, _, _) in enumerate(chunks):
            sl = slice(ci * heads, (ci + 1) * heads)
            us_ref[rs, :] = jnp.concatenate([uw[:, 0:dk] for uw in uws[sl]], axis=1)
            ws_ref[rs, :] = jnp.concatenate([uw[:, dk:2 * dk] for uw in uws[sl]], axis=1).astype(BF16)
            qs_ref[rs, :] = jnp.concatenate([t["qh"] * t["egc"] for t in inst[sl]], axis=1)
            ks_ref[rs, :] = jnp.concatenate([t["kd"] for t in inst[sl]], axis=1)
            qk_ref[rs, :] = jnp.concatenate([kq[c64:2 * c64] * t["decay"]
                                             for kq, t in zip(kqs[sl], inst[sl])], axis=1).astype(BF16)
            e0 = pl.multiple_of(c * (heads * SUBLANES), heads * SUBLANES)
            egl_ref[pl.ds(e0, heads * SUBLANES), :] = jnp.concatenate([t["eg"] for t in inst[sl]], axis=0)
        return carry
    lax.fori_loop(0, n_chunks // group, chunk_prep, 0)

    st_ref[...] = jnp.zeros(st_ref.shape, F32)

    def scan(c, carry):
        rs = pl.ds(pl.multiple_of(c * c64, c64), c64)
        e0 = pl.multiple_of(c * (heads * SUBLANES), heads * SUBLANES)
        w_all, qg_all, u_all, kd_all = ws_ref[rs, :], qs_ref[rs, :], us_ref[rs, :], ks_ref[rs, :]
        qk_all = qk_ref[rs, :]
        eg_all = egl_ref[pl.ds(e0, heads * SUBLANES), :]
        hsl = [slice(h * dk, (h + 1) * dk) for h in range(heads)]
        sts = [st_ref[h] for h in range(heads)]
        wqs = [_dot(jnp.concatenate([w_all[:, hs], qg_all[:, hs].astype(BF16)], axis=0), st.astype(BF16))
               for hs, st in zip(hsl, sts)]
        vbs = [(u_all[:, hs] - wq[0:c64]).astype(BF16) for hs, wq in zip(hsl, wqs)]
        upd = [_dot_tn(kd_all[:, hs].astype(BF16), vb) for hs, vb in zip(hsl, vbs)]
        intra = [_dot(qk_all[:, h * c64:(h + 1) * c64], vbs[h]) for h in range(heads)]
        for h in range(heads):
            st_ref[h] = sts[h] * eg_all[h * SUBLANES:h * SUBLANES + 1, :] + upd[h]
        vs_ref[rs, :] = jnp.concatenate([wq[c64:2 * c64] + o for wq, o in zip(wqs, intra)], axis=1)
        return carry
    lax.fori_loop(0, n_chunks, scan, 0)

    def finish(i, c):
        o = vs_ref[rows(i), :]
        og = og_ref[rows(i), :].astype(F32)
        parts = []
        for h in range(heads):
            oh = o[:, h * dk:(h + 1) * dk]
            parts.append(_rms(oh, gn_ref[...]))
        y = jnp.concatenate(parts, axis=1) * _silu(og)
        o_ref[rows(i), :] = y.astype(o_ref.dtype)
        return c
    lax.fori_loop(0, n_tiles, finish, 0)


def _l0_mixer(u, ab, gdn_conv, a_log, dt_bias, gdn_norm_g, batch, seq):
    dk = gdn_norm_g.shape[0]
    heads = GDN_HEADS
    cw = heads * dk

    def lane_row(x):
        return jnp.zeros((1, LANES), F32).at[0, :x.shape[0]].set(x)

    in_specs = [pl.BlockSpec((seq, cw), lambda b, j=j: (b, j)) for j in range(3, 7)]
    in_specs += [pl.BlockSpec((seq, LANES), lambda b: (b, 0))]
    in_specs += [pl.BlockSpec((GDN_CONV_WIDTH, cw), lambda b, j=j: (0, j), pipeline_mode=pl.Buffered(1))
                 for j in range(3)]
    in_specs += [_resident((1, LANES)), _resident((1, LANES)), _resident((1, dk))]
    big = pltpu.VMEM((seq, cw), F32)
    scratch = [pltpu.VMEM((seq + TOP_PAD, cw), F32), big, big, big, big, pltpu.VMEM((seq, cw), BF16),
               pltpu.VMEM((seq, heads * GDN_CHUNK), BF16),
               pltpu.VMEM((seq // GDN_CHUNK * heads * SUBLANES, dk), F32),
               pltpu.VMEM((heads, dk, dk), F32)]
    return pl.pallas_call(
        _l0_mixer_kernel, grid=(batch,), in_specs=in_specs,
        out_specs=pl.BlockSpec((seq, cw), lambda b: (b, 0)),
        out_shape=jax.ShapeDtypeStruct((batch * seq, cw), BF16),
        scratch_shapes=scratch, compiler_params=_cparams("parallel"), name="l0_deltanet",
    )(u, u, u, u, ab, gdn_conv, gdn_conv, gdn_conv,
      lane_row(a_log), lane_row(dt_bias), gdn_norm_g.reshape(1, dk))


def _bias_table_kernel(tab_ref, o_ref, *, t):
    h = pl.program_id(0)
    m = pl.program_id(1)
    ri = lax.broadcasted_iota(jnp.int32, (t, t), 0)
    ci = lax.broadcasted_iota(jnp.int32, (t, t), 1)
    rel = m * t + ci - ri
    n = jnp.maximum(rel, 0)
    max_exact = NUM_BUCKETS // 2
    nf = jnp.maximum(n, 1).astype(F32)
    large = max_exact + (jnp.log(nf / max_exact) / math.log(MAX_DISTANCE / max_exact)
                         * (NUM_BUCKETS - max_exact)).astype(jnp.int32)
    large = jnp.minimum(large, NUM_BUCKETS - 1)
    bucket = jnp.where(n < max_exact, n, large)
    bias = jnp.zeros((t, t), F32)
    for b in range(NUM_BUCKETS):
        bias = jnp.where(bucket == b, tab_ref[b, h], bias)
    o_ref[...] = jnp.where(rel >= 0, bias * LOG2E, MASK_VALUE)


def _bias_table(rel_bias, seq, t):
    nb = seq // t
    heads = rel_bias.shape[1]
    return pl.pallas_call(
        functools.partial(_bias_table_kernel, t=t), grid=(heads, nb),
        in_specs=[pl.BlockSpec(memory_space=pltpu.SMEM)],
        out_specs=pl.BlockSpec((None, None, t, t), lambda h, m: (h, m, 0, 0)),
        out_shape=jax.ShapeDtypeStruct((heads, nb, t, t), F32),
        compiler_params=_cparams("parallel", "parallel"), name="rel_bias_table")(rel_bias)


def _attn_kernel(q_ref, k_ref, v_ref, tb_ref, lam_ref, gn_ref, o_ref, m_ref, l_ref, acc_ref,
                 *, t, heads, lambda_init):
    qi = pl.program_id(2)
    dh2 = q_ref.shape[1] // heads
    dh = dh2 // 2
    lane = lax.broadcasted_iota(jnp.int32, (t, dh2), 1)
    qqs = []
    for h in range(heads):
        qf = q_ref[:, h * dh2:(h + 1) * dh2].astype(F32) * (dh ** -0.5 * LOG2E)
        qqs.append(jnp.concatenate([jnp.where(lane < dh, qf, 0.0), jnp.where(lane >= dh, qf, 0.0)],
                                   axis=0).astype(BF16))

    hs = [slice(h * dh2, (h + 1) * dh2) for h in range(heads)]

    def update(j0, nblk):
        ks = pl.ds(pl.multiple_of(j0 * t, t), nblk * t)
        k_all = k_ref[ks, :]
        v_all = v_ref[ks, :]
        s_t = [_dot_nt(k_all[:, hs[h]], qqs[h]) for h in range(heads)]
        ps, alphas = [], []
        for h in range(heads):
            m = m_ref[h, 0:1, :]
            b = jnp.concatenate([tb_ref[h, qi - j0 - i] for i in range(nblk)], axis=0)
            s = jnp.concatenate([s_t[h][:, 0:t] + b, s_t[h][:, t:2 * t] + b], axis=1)
            m_new = jnp.maximum(m, jnp.max(s, axis=0, keepdims=True))
            alpha = jnp.exp2(m - m_new)
            p = jnp.exp2(s - m_new)
            m_ref[h] = jnp.broadcast_to(m_new, m_ref.shape[1:])
            l_ref[h] = jnp.broadcast_to(alpha * l_ref[h, 0:1, :] + jnp.sum(p, axis=0, keepdims=True),
                                        l_ref.shape[1:])
            alphas.append(alpha)
            ps.append(p.astype(BF16))
        pv = [_dot_tn(v_all[:, hs[h]], ps[h]) for h in range(heads)]
        for h in range(heads):
            acc_ref[h] = alphas[h] * acc_ref[h] + pv[h]

    m_ref[...] = jnp.full(m_ref.shape, MASK_VALUE, F32)
    l_ref[...] = jnp.zeros(l_ref.shape, F32)
    acc_ref[...] = jnp.zeros(acc_ref.shape, F32)
    odd = (qi + 1) % 2

    @pl.when(odd == 1)
    def _():
        update(0, 1)

    def pair(i, c):
        update(odd + 2 * i, 2)
        return c
    lax.fori_loop(0, (qi + 1) // 2, pair, 0)
    lp = lam_ref[...]
    lam = (jnp.exp(jnp.sum(lp[0:1] * lp[1:2], axis=-1, keepdims=True))
           - jnp.exp(jnp.sum(lp[2:3] * lp[3:4], axis=-1, keepdims=True)) + lambda_init)
    for h in range(heads):
        o12 = acc_ref[h] / l_ref[h, 0:1, :]
        o = (o12[:, 0:t] - lam * o12[:, t:2 * t]).T
        o_ref[:, h * dh2:(h + 1) * dh2] = (_rms(o, gn_ref[...]) * (1.0 - lambda_init)).astype(o_ref.dtype)


def _diff_attention(u, table, lam_params, norm_g, batch, seq, lambda_init):
    heads = DIFF_HEADS
    dh2 = norm_g.shape[0]
    hg = ATTN_HEADS_PER_STEP
    w = hg * dh2
    ng = heads // hg
    t = table.shape[2]
    nq = seq // t
    return pl.pallas_call(
        functools.partial(_attn_kernel, t=t, heads=hg, lambda_init=lambda_init),
        grid=(batch, ng, nq),
        in_specs=[pl.BlockSpec((t, w), lambda b, g, i: (b * nq + i, g)),
                  pl.BlockSpec((seq, w), lambda b, g, i: (b, ng + g)),
                  pl.BlockSpec((seq, w), lambda b, g, i: (b, 2 * ng + g)),
                  pl.BlockSpec((hg, nq, t, t), lambda b, g, i: (g, 0, 0, 0)),
                  _resident(lam_params.shape), _resident((1, dh2))],
        out_specs=pl.BlockSpec((t, w), lambda b, g, i: (b * nq + i, g)),
        out_shape=jax.ShapeDtypeStruct((batch * seq, heads * dh2), BF16),
        scratch_shapes=[pltpu.VMEM((hg, SUBLANES, 2 * t), F32), pltpu.VMEM((hg, SUBLANES, 2 * t), F32),
                        pltpu.VMEM((hg, dh2, 2 * t), F32)],
        compiler_params=_cparams("parallel", "parallel", "arbitrary"), name="diff_attention",
    )(u, u, u, table, lam_params, norm_g.reshape(1, dh2))


def _conformer_kernel(ga_ref, gb_ref, w_ref, b_ref, lg_ref, lb_ref, o_ref, pad_ref):
    s, c = ga_ref.shape
    rt = min(ELEM_ROWS, s)
    pad_ref[0:TOP_PAD, :] = jnp.zeros((TOP_PAD, c), F32)

    def glu(i, carry):
        r = pl.multiple_of(i * rt, rt)
        rs = pl.ds(r, rt)
        pad_ref[pl.ds(TOP_PAD + r, rt), :] = ga_ref[rs, :].astype(F32) * _sigmoid(gb_ref[rs, :].astype(F32))
        return carry
    lax.fori_loop(0, s // rt, glu, 0)

    ct = CONF_ROWS

    def conv(i, carry):
        r = pl.multiple_of(i * ct, ct)
        tap = _delayed(pad_ref, r, ct, CONF_WIDTH - 1)
        acc = jnp.broadcast_to(b_ref[...], (ct, c))
        for j in range(CONF_WIDTH):
            acc = acc + w_ref[j:j + 1, :] * tap(CONF_WIDTH - 1 - j)
        mu = jnp.mean(acc, axis=-1, keepdims=True)
        xc = acc - mu
        var = jnp.mean(xc * xc, axis=-1, keepdims=True)
        y = xc * lax.rsqrt(var + EPS) * lg_ref[...] + lb_ref[...]
        o_ref[pl.ds(r, ct), :] = _silu(y).astype(o_ref.dtype)
        return carry
    lax.fori_loop(0, s // ct, conv, 0)


def _conformer(u, w, b, ln_g, ln_b, batch, seq, col0):
    c = w.shape[1]
    return pl.pallas_call(
        _conformer_kernel, grid=(batch,),
        in_specs=[pl.BlockSpec((seq, c), lambda i: (i, col0)),
                  pl.BlockSpec((seq, c), lambda i: (i, col0 + 1)),
                  _resident(w.shape), _resident((1, c)), _resident((1, c)), _resident((1, c))],
        out_specs=pl.BlockSpec((seq, c), lambda i: (i, 0)),
        out_shape=jax.ShapeDtypeStruct((batch * seq, c), BF16),
        scratch_shapes=[pltpu.VMEM((seq + TOP_PAD, c), F32)],
        compiler_params=_cparams("parallel"), name="conformer_conv",
    )(u, u, w, b.reshape(1, c), ln_g.reshape(1, c), ln_b.reshape(1, c))


def _proj_router_kernel(y1_ref, y2_ref, w1_ref, w2_ref, h_ref, g_ref, wrt_ref, upper_ref,
                        h_out_ref, hn_ref, route_ref, gate_ref, cnt_ref, carry_ref):
    i = pl.program_id(0)
    tm = h_ref.shape[0]
    n_e = N_EXPERTS

    @pl.when(i == 0)
    def _():
        carry_ref[...] = jnp.zeros(carry_ref.shape, F32)

    h = h_ref[...] + _dot(y1_ref[...], w1_ref[...]) + _dot(y2_ref[...], w2_ref[...])
    h_out_ref[...] = h
    xn = _rms(h, g_ref[...])
    _store_token_tiles(hn_ref, 0, xn)

    logits = _dot_nt(wrt_ref[...], xn.astype(BF16))[0:n_e, :]
    sub = lax.broadcasted_iota(jnp.int32, logits.shape, 0)
    m1 = jnp.max(logits, axis=0, keepdims=True)
    i1 = jnp.min(jnp.where(logits == m1, sub, n_e), axis=0, keepdims=True)
    rest = jnp.where(sub == i1, -jnp.inf, logits)
    m2 = jnp.max(rest, axis=0, keepdims=True)
    i2 = jnp.min(jnp.where(rest == m2, sub, n_e), axis=0, keepdims=True)
    e = jnp.exp(m2 - m1)
    g1 = 1.0 / (1.0 + e)
    oh1 = sub == i1
    oh2 = sub == i2
    both = oh1.astype(F32) + oh2.astype(F32)
    csum = _dot(both.astype(BF16), upper_ref[...])
    carry = carry_ref[:, 0:1]
    before = csum - both + carry
    total = carry + csum[:, tm - 1:tm]
    carry_ref[...] = jnp.broadcast_to(total, carry_ref.shape)
    cnt_ref[...] = jnp.broadcast_to(total, cnt_ref.shape).astype(jnp.int32)
    r1 = jnp.sum(jnp.where(oh1, before, 0.0), axis=0, keepdims=True).astype(jnp.int32)
    r2 = jnp.sum(jnp.where(oh2, before, 0.0), axis=0, keepdims=True).astype(jnp.int32)
    route_ref[...] = jnp.where(sub == 0, i1, jnp.where(sub == 1, i2, jnp.where(sub == 2, r1,
                               jnp.where(sub == 3, r2, 0))))
    grow = jnp.where(sub == 0, g1, jnp.where(sub == 1, e * g1, 0.0))
    sel = (lax.broadcasted_iota(jnp.int32, (n_e, LANES), 0)
           == lax.broadcasted_iota(jnp.int32, (n_e, LANES), 1)).astype(F32)
    gate_ref[...] = lax.dot_general(grow, sel, (((0,), (0,)), ((), ())), precision=HIGHEST,
                                    preferred_element_type=F32)


def _proj_router(y1, y2, w, h, g, wr):
    t, d = h.shape
    k1, k2 = y1.shape[1], y2.shape[1]
    tm = min(ROW_TILE, t)
    wrt = jnp.zeros((LANES, d), BF16).at[:N_EXPERTS].set(wr.T.astype(BF16))
    upper = (jnp.arange(tm)[:, None] <= jnp.arange(tm)[None, :]).astype(BF16)
    row = lambda width: pl.BlockSpec((tm, width), lambda i: (i, 0))
    return pl.pallas_call(
        _proj_router_kernel, grid=(t // tm,),
        in_specs=[row(k1), row(k2), _resident((k1, d)), _resident((k2, d)), row(d),
                  _resident((1, d)), _resident((LANES, d)), _resident((tm, tm))],
        out_specs=[row(d), pl.BlockSpec((tm * d // LANES, LANES), lambda i: (i, 0)),
                   pl.BlockSpec((N_EXPERTS, tm), lambda i: (0, i)), row(LANES),
                   pl.BlockSpec((N_EXPERTS, LANES), lambda i: (0, 0))],
        out_shape=[jax.ShapeDtypeStruct((t, d), F32), jax.ShapeDtypeStruct((t * d // LANES, LANES), F32),
                   jax.ShapeDtypeStruct((N_EXPERTS, t), jnp.int32), jax.ShapeDtypeStruct((t, LANES), F32),
                   jax.ShapeDtypeStruct((N_EXPERTS, LANES), jnp.int32)],
        scratch_shapes=[pltpu.VMEM((N_EXPERTS, LANES), F32)],
        compiler_params=_cparams("arbitrary"), name="l1_out_proj_router",
    )(y1, y2, w[:k1], w[k1:], h, g.reshape(1, d), wrt, upper)


def _dispatch_kernel(dest_hbm, pad_hbm, x_ref, xs_hbm, idx_smem, pad_smem, zero_ref, sem_idx, sem_x):
    i = pl.program_id(0)
    n = x_ref.shape[0] // SUBLANES
    m = TOP_K * n
    cp = pltpu.make_async_copy(dest_hbm.at[i], idx_smem, sem_idx)
    cp.start()
    cp.wait()

    def body(r, c):
        for k in range(TOP_K):
            _token_tile_copy(x_ref, r, xs_hbm, idx_smem[k * n + r], sem_x).start(priority=k % 2)
        return c
    lax.fori_loop(0, n, body, 0, unroll=8)
    for _ in range(TOP_K):
        pltpu.make_async_copy(x_ref, xs_hbm.at[pl.ds(0, n * SUBLANES)], sem_x).wait()

    @pl.when(i == pl.num_programs(0) - 1)
    def _():
        n_pad = pad_smem.shape[0]
        cp = pltpu.make_async_copy(pad_hbm, pad_smem, sem_idx)
        cp.start()
        cp.wait()
        zero_ref[...] = jnp.zeros(zero_ref.shape, F32)

        def fill(j, c):
            for q in range(2):
                _token_tile_copy(zero_ref, 0, xs_hbm, pad_smem[2 * j + q], sem_x).start(priority=q)
            return c
        lax.fori_loop(0, n_pad // 2, fill, 0, unroll=8)
        pltpu.make_async_copy(xs_hbm.at[pl.ds(0, n_pad * SUBLANES)], xs_hbm.at[pl.ds(0, n_pad * SUBLANES)],
                              sem_x).wait()


def _dispatch(hn_tiles, dest, pad_rows, t, d):
    n = min(DISPATCH_ROWS, t)
    steps = t // n
    tpt = d // LANES
    p = t * TOP_K + N_EXPERTS * MOE_ROWS
    dest_steps = jnp.concatenate([dest[k].reshape(steps, n) for k in range(TOP_K)], axis=1)
    return pl.pallas_call(
        _dispatch_kernel, grid=(steps,),
        in_specs=[pl.BlockSpec(memory_space=pl.ANY), pl.BlockSpec(memory_space=pl.ANY),
                  pl.BlockSpec((n * tpt, LANES), lambda i: (i, 0))],
        out_specs=pl.BlockSpec(memory_space=pl.ANY),
        out_shape=jax.ShapeDtypeStruct((p * tpt, LANES), F32),
        scratch_shapes=[pltpu.SMEM((TOP_K * n,), jnp.int32), pltpu.SMEM(pad_rows.shape, jnp.int32),
                        pltpu.VMEM((tpt, LANES), F32), pltpu.SemaphoreType.DMA, pltpu.SemaphoreType.DMA],
        compiler_params=_cparams("arbitrary"), name="moe_dispatch")(dest_steps, pad_rows, hn_tiles)


def _dispatch_plan(route, counts, t):
    tm = MOE_ROWS
    p = t * TOP_K + N_EXPERTS * tm
    padded = (counts + tm - 1) // tm * tm
    pend = jnp.cumsum(padded)
    pstart = pend - padded
    experts = jnp.arange(N_EXPERTS, dtype=jnp.int32)[:, None]
    dest = jnp.stack([jnp.sum(jnp.where(route[k][None, :] == experts, pstart[:, None], 0), axis=0)
                      + route[TOP_K + k] for k in range(TOP_K)])
    n_pad = p - t * TOP_K
    gaps = jnp.concatenate([padded - counts, (p - pend[-1])[None]])
    gap_end = jnp.cumsum(gaps)
    gap_row0 = jnp.concatenate([pstart + counts, pend[-1:]])
    j = jnp.arange(n_pad, dtype=jnp.int32)
    which = jnp.searchsorted(gap_end, j, side='right')
    pad_rows = (gap_row0[which] + j - (gap_end - gaps)[which]).astype(jnp.int32)
    nb = p // tm
    blk_e = jnp.minimum(jnp.searchsorted(pend, jnp.arange(nb, dtype=jnp.int32) * tm, side='right'),
                        N_EXPERTS - 1).astype(jnp.int32)
    return dest.astype(jnp.int32), pad_rows, blk_e, (pend[-1:] // tm).astype(jnp.int32)


def _experts_kernel(blk_e_ref, used_ref, x_ref, wgu_ref, wd_ref, o_ref, hid_ref):
    f = wd_ref.shape[0]
    tm = hid_ref.shape[0]
    live = pl.program_id(0) < used_ref[0]

    @pl.when(live)
    def _():
        x = _load_token_tiles(x_ref, 0, tm).astype(BF16)
        for c in range(0, f, MOE_SUB_TILE):
            hid_ref[:, c:c + MOE_SUB_TILE] = (
                _silu(_dot(x, wgu_ref[:, c:c + MOE_SUB_TILE]))
                * _dot(x, wgu_ref[:, f + c:f + c + MOE_SUB_TILE])).astype(BF16)
        _store_token_tiles(o_ref, 0, _dot(hid_ref[...], wd_ref[...]))

    @pl.when(jnp.logical_not(live))
    def _():
        o_ref[...] = jnp.zeros(o_ref.shape, F32)


def _experts(xs, blk_e, n_used, w_gate_up, w_down):
    f, d = w_down.shape[1], w_down.shape[2]
    tm = MOE_ROWS
    tpt = d // LANES
    assert f % MOE_SUB_TILE == 0
    grid_spec = pltpu.PrefetchScalarGridSpec(
        num_scalar_prefetch=2, grid=(blk_e.shape[0],),
        in_specs=[pl.BlockSpec((tm * tpt, LANES), lambda i, e, u: (i, 0)),
                  pl.BlockSpec((None, d, 2 * f), lambda i, e, u: (e[i], 0, 0), pipeline_mode=pl.Buffered(1)),
                  pl.BlockSpec((None, f, d), lambda i, e, u: (e[i], 0, 0), pipeline_mode=pl.Buffered(1))],
        out_specs=pl.BlockSpec((tm * tpt, LANES), lambda i, e, u: (i, 0)),
        scratch_shapes=[pltpu.VMEM((tm, f), BF16)])
    return pl.pallas_call(
        _experts_kernel, grid_spec=grid_spec, out_shape=jax.ShapeDtypeStruct(xs.shape, F32),
        compiler_params=_cparams("parallel"), name="moe_experts",
    )(blk_e, n_used, xs, w_gate_up, w_down)


def _combine_kernel(dest_hbm, y_hbm, gate_ref, h_ref, g_ref, p_ref, wp_ref, wg_ref, fg_ref, o_ref,
                    idx0, idx1, ybuf, sem_idx, sem_rows):
    i = pl.program_id(0)
    steps = pl.num_programs(0)
    n = h_ref.shape[0]
    m = TOP_K * n
    slot = i % 2
    idxs = (idx0, idx1)

    def idx_copy(blk, s):
        return pltpu.make_async_copy(dest_hbm.at[blk], idxs[s], sem_idx.at[s])

    def issue_tiles(s):
        def body(r, c):
            for k in range(TOP_K):
                _token_tile_copy(y_hbm, idxs[s][k * n + r], ybuf, s * m + k * n + r,
                                 sem_rows.at[s]).start(priority=k % 2)
            return c
        lax.fori_loop(0, n, body, 0, unroll=8)

    @pl.when(i == 0)
    def _():
        idx_copy(0, 0).start()
        idx_copy(0, 0).wait()
        idx_copy(1, 1).start()
        issue_tiles(0)

    for s in range(2):
        @pl.when(slot == s)
        def _(s=s):
            idx_copy(i + 1, 1 - s).wait()

            @pl.when(i + 2 <= steps)
            def _():
                idx_copy(i + 2, s).start()

            issue_tiles(1 - s)
            _token_tiles_wait(y_hbm, ybuf, s * m, m, sem_rows.at[s])

    nc = n // COMBINE_CHUNKS
    rows = [pl.ds(c * nc, nc) for c in range(COMBINE_CHUNKS)]
    xs = []
    for c, rs in enumerate(rows):
        gates = gate_ref[rs, :]
        xs.append(h_ref[rs, :] + gates[:, 0:1] * _load_token_tiles(ybuf, slot * m + c * nc, nc)
                  + gates[:, 1:2] * _load_token_tiles(ybuf, slot * m + n + c * nc, nc))
    xns = [_rms(x, g_ref[...]).astype(BF16) for x in xs]
    gate = [_sigmoid(_dot(xn, wg_ref[...])) for xn in xns]
    emb = [_dot(p_ref[rs, :].astype(BF16), wp_ref[...]) for rs in rows]
    for rs, x, e, g in zip(rows, xs, emb, gate):
        o_ref[rs, :] = _rms(x + e * g, fg_ref[...])

    @pl.when(i == steps - 1)
    def _():
        _token_tiles_wait(y_hbm, ybuf, (1 - slot) * m, m, sem_rows.at[1 - slot])


def _combine_ple_final(dest, y_tiles, gates, h, g, p, layer, wp, wg, final_g):
    t, d = h.shape
    e = p.shape[2]
    n = min(COMBINE_ROWS, t)
    m = TOP_K * n
    steps = t // n
    dest_steps = jnp.concatenate([dest[k].reshape(steps, n) for k in range(TOP_K)], axis=1)
    dest_steps = jnp.concatenate([dest_steps, jnp.zeros((1, m), jnp.int32)], axis=0)
    return pl.pallas_call(
        _combine_kernel, grid=(steps,),
        in_specs=[pl.BlockSpec(memory_space=pl.ANY), pl.BlockSpec(memory_space=pl.ANY),
                  pl.BlockSpec((n, LANES), lambda i: (i, 0)),
                  pl.BlockSpec((n, d), lambda i: (i, 0)), _resident((1, d)),
                  pl.BlockSpec((None, n, e), lambda i: (layer, i, 0)), _resident((e, d)), _resident((d, d)),
                  _resident((1, d))],
        out_specs=pl.BlockSpec((n, d), lambda i: (i, 0)),
        out_shape=jax.ShapeDtypeStruct((t, d), F32),
        scratch_shapes=[pltpu.SMEM((m,), jnp.int32), pltpu.SMEM((m,), jnp.int32),
                        pltpu.VMEM((2 * m * d // LANES, LANES), F32),
                        pltpu.SemaphoreType.DMA((2,)), pltpu.SemaphoreType.DMA((2,))],
        compiler_params=_cparams("arbitrary"), name="moe_combine_ple_final",
    )(dest_steps, y_tiles, gates, h, g.reshape(1, d), p, wp, wg, final_g.reshape(1, d))


def kernel(x, p, norm_mix_g, norm_ffn_g, norm_ple_g, final_norm_g, ev_w_in, ev_conv_a, ev_gdn_conv, ev_gdn_A_log, ev_gdn_dt_bias, ev_gdn_norm_g, ev_w_out, od_w_in, od_lambda, od_diff_norm_g, od_conf_dw_w, od_conf_dw_b, od_conf_ln_g, od_conf_ln_b, od_w_out, rel_bias, ffn_w_gate_up, ffn_w_down, moe_router, moe_w_gate_up, moe_w_down, ple_w_proj, ple_w_gate):
    batch, seq, d = x.shape
    t = batch * seq
    depth = p.shape[0]
    assert depth == 2 and seq % GDN_CHUNK == 0
    h = x.reshape(t, d)
    pf = p.reshape(depth, t, p.shape[-1])

    heads = GDN_HEADS
    n_main = ev_w_in.shape[2] - 2 * heads
    w_in = ev_w_in[0]
    w_ab = jnp.zeros((d, LANES), BF16).at[:, :2 * heads].set(w_in[:, n_main:].astype(BF16))
    u, ab = _norm_proj(h, norm_mix_g[0], w_in[:, :n_main].astype(BF16), w_ab, name="l0_in_proj")
    ya = _gated_conv(u, ev_conv_a[0], batch, seq)
    yb = _l0_mixer(u, ab, ev_gdn_conv[0], ev_gdn_A_log[0], ev_gdn_dt_bias[0], ev_gdn_norm_g[0], batch, seq)
    h = _proj_residual([ya, yb], ev_w_out[0].astype(BF16), h, name="l0_out_proj")
    f = ffn_w_down.shape[1]
    lambda_init = 0.8 - 0.6 * math.exp(-0.3 * 1)
    h, u = _ffn_ple_proj(h, norm_ffn_g[0], ffn_w_gate_up[0, :, :f].astype(BF16),
                         ffn_w_gate_up[0, :, f:].astype(BF16), ffn_w_down[0].astype(BF16),
                         norm_ple_g[0], pf, 0, ple_w_proj[0].astype(BF16), ple_w_gate[0].astype(BF16),
                         norm_mix_g[1], od_w_in[0].astype(BF16), name="l0_ffn_ple_l1_in_proj")

    table = _bias_table(rel_bias, seq, min(ATTN_TILE, seq))
    o_attn = _diff_attention(u, table, od_lambda[0], od_diff_norm_g[0], batch, seq, lambda_init)
    c_conf = od_conf_dw_w.shape[2]
    o_conf = _conformer(u, od_conf_dw_w[0], od_conf_dw_b[0], od_conf_ln_g[0], od_conf_ln_b[0],
                        batch, seq, 3 * DIFF_HEADS * od_diff_norm_g.shape[1] // c_conf)
    h, hn, route, gates, counts = _proj_router(o_attn, o_conf, od_w_out[0].astype(BF16), h, norm_ffn_g[1],
                                               moe_router[0])
    dest, pad_rows, blk_e, n_used = _dispatch_plan(route, counts[:, 0], t)
    xs = _dispatch(hn, dest, pad_rows, t, d)
    y = _experts(xs, blk_e, n_used, moe_w_gate_up[0].astype(BF16), moe_w_down[0].astype(BF16))
    out = _combine_ple_final(dest, y, gates, h, norm_ple_g[1], pf, 1, ple_w_proj[1].astype(BF16),
                             ple_w_gate[1].astype(BF16), final_norm_g)
    return out.reshape(batch, seq, d)
```

```python
import functools
import math

import jax
import jax.numpy as jnp
from jax import lax
from jax.experimental import pallas as pl
from jax.experimental.pallas import tpu as pltpu

F32 = jnp.float32
BF16 = jnp.bfloat16
HIGHEST = lax.Precision.HIGHEST

EPS = 1e-6
CONV_A_WIDTH = 3
GDN_HEADS = 4
GDN_CONV_WIDTH = 4
GDN_CHUNK = 64
DIFF_HEADS = 4
NUM_BUCKETS = 32
MAX_DISTANCE = 128
CONF_WIDTH = 31
N_EXPERTS = 8
TOP_K = 2

LANES = 128
SUBLANES = 8
VMEM_LIMIT_BYTES = 56 * 1024 * 1024
MASK_VALUE = -1e30
LOG2E = math.log2(math.e)

ROW_TILE = 512
COL_TILE = 512
FFN_CHUNKS = 2
ELEM_ROWS = 256
CONF_ROWS = 256
PREP_CHUNKS = 8
ATTN_TILE = 256
ATTN_HEADS_PER_STEP = 4
MOE_ROWS = 1024
DISPATCH_ROWS = 2048
MOE_SUB_TILE = 256
COMBINE_ROWS = 512
COMBINE_CHUNKS = 4
TOP_PAD = 32


def _cparams(*sem):
    return pltpu.CompilerParams(dimension_semantics=sem, vmem_limit_bytes=VMEM_LIMIT_BYTES)


def _resident(shape):
    nd = len(shape)
    return pl.BlockSpec(shape, lambda *_: (0,) * nd, pipeline_mode=pl.Buffered(1))


def _rms(x, g):
    return x * lax.rsqrt(jnp.mean(x * x, axis=-1, keepdims=True) + EPS) * g


def _sigmoid(x):
    return jax.nn.sigmoid(x)


def _silu(x):
    return x * jax.nn.sigmoid(x)


def _softplus(x):
    return jnp.maximum(x, 0.0) + jnp.log1p(jnp.exp(-jnp.abs(x)))


def _dot(a, b, **kw):
    return jnp.dot(a, b, preferred_element_type=F32, **kw)


def _dot_nt(a, b):
    return lax.dot_general(a, b, (((1,), (1,)), ((), ())), preferred_element_type=F32)


def _dot_tn(a, b):
    return lax.dot_general(a, b, (((0,), (0,)), ((), ())), preferred_element_type=F32)


def _delayed(pad_ref, r, rows, max_delay):
    lead = -(-max_delay // SUBLANES) * SUBLANES
    win = pad_ref[pl.ds(TOP_PAD + r - lead, rows + lead), :]
    rolled = {0: win}

    def tap(d):
        a, b = divmod(d, SUBLANES)
        if b not in rolled:
            rolled[b] = pltpu.roll(win, b, 0)
        start = lead - SUBLANES * a
        return rolled[b][start:start + rows, :]
    return tap


def _load_token_tiles(ref, first_token, n):
    return jnp.concatenate([ref[pl.ds(first_token * SUBLANES + s, n, stride=SUBLANES), :]
                            for s in range(SUBLANES)], axis=1)


def _store_token_tiles(ref, first_token, x):
    n = x.shape[0]
    for s in range(SUBLANES):
        ref[pl.ds(first_token * SUBLANES + s, n, stride=SUBLANES), :] = x[:, s * LANES:(s + 1) * LANES]


def _token_tile_copy(src_hbm, src_token, dst_vmem, dst_token, sem):
    return pltpu.make_async_copy(src_hbm.at[pl.ds(pl.multiple_of(src_token * SUBLANES, SUBLANES), SUBLANES)],
                                 dst_vmem.at[pl.ds(pl.multiple_of(dst_token * SUBLANES, SUBLANES), SUBLANES)],
                                 sem)


def _token_tiles_wait(src_hbm, dst_vmem, first_token, n, sem):
    pltpu.make_async_copy(src_hbm.at[pl.ds(0, n * SUBLANES)],
                          dst_vmem.at[pl.ds(pl.multiple_of(first_token * SUBLANES, SUBLANES), n * SUBLANES)],
                          sem).wait()


def _norm_proj_kernel(h_ref, g_ref, w_ref, *rest, tn, with_aux):
    xn = _rms(h_ref[...], g_ref[...]).astype(BF16)
    if with_aux:
        w2_ref, o_ref, o2_ref = rest
        o2_ref[...] = _dot(xn, w2_ref[...])
    else:
        (o_ref,) = rest
    n = w_ref.shape[1]
    for c in range(0, n, tn):
        o_ref[:, c:c + tn] = _dot(xn, w_ref[:, c:c + tn]).astype(o_ref.dtype)


def _norm_proj(h, g, w, w_aux=None, *, name):
    t, d = h.shape
    n = w.shape[1]
    tm = min(ROW_TILE, t)
    tn = COL_TILE if n % COL_TILE == 0 else n
    in_specs = [pl.BlockSpec((tm, d), lambda i: (i, 0)), _resident((1, d)), _resident((d, n))]
    out_shape = [jax.ShapeDtypeStruct((t, n), BF16)]
    out_specs = [pl.BlockSpec((tm, n), lambda i: (i, 0))]
    args = [h, g.reshape(1, d), w]
    if w_aux is not None:
        in_specs.append(_resident(w_aux.shape))
        out_shape.append(jax.ShapeDtypeStruct((t, w_aux.shape[1]), F32))
        out_specs.append(pl.BlockSpec((tm, w_aux.shape[1]), lambda i: (i, 0)))
        args.append(w_aux)
    out = pl.pallas_call(
        functools.partial(_norm_proj_kernel, tn=tn, with_aux=w_aux is not None),
        grid=(t // tm,), in_specs=in_specs, out_specs=out_specs, out_shape=out_shape,
        compiler_params=_cparams("parallel"), name=name)(*args)
    return out if w_aux is not None else out[0]


def _proj_residual_kernel(*refs):
    n = (len(refs) - 2) // 2
    y_refs, w_refs, h_ref, o_ref = refs[:n], refs[n:2 * n], refs[2 * n], refs[2 * n + 1]
    acc = h_ref[...]
    for y_ref, w_ref in zip(y_refs, w_refs):
        acc = acc + _dot(y_ref[...], w_ref[...])
    o_ref[...] = acc


def _proj_residual(ys, w, h, *, name):
    t, d = h.shape
    tm = min(ROW_TILE, t)
    ws, r0 = [], 0
    for y in ys:
        ws.append(w[r0:r0 + y.shape[1]])
        r0 += y.shape[1]
    return pl.pallas_call(
        _proj_residual_kernel, grid=(t // tm,),
        in_specs=([pl.BlockSpec((tm, y.shape[1]), lambda i: (i, 0)) for y in ys]
                  + [_resident(wi.shape) for wi in ws] + [pl.BlockSpec((tm, d), lambda i: (i, 0))]),
        out_specs=pl.BlockSpec((tm, d), lambda i: (i, 0)),
        out_shape=jax.ShapeDtypeStruct((t, d), F32),
        compiler_params=_cparams("parallel"), name=name)(*ys, *ws, h)


def _ffn_ple_proj_kernel(h_ref, gf_ref, wg_ref, wu_ref, wd_ref, gp_ref, p_ref, wp_ref, wpg_ref,
                         gm_ref, win_ref, h_out_ref, u_ref, *, tf, tn):
    tm = h_ref.shape[0]
    nc = tm // FFN_CHUNKS
    rows = [pl.ds(c * nc, nc) for c in range(FFN_CHUNKS)]
    xs = [h_ref[rs, :] for rs in rows]
    xns = [_rms(x, gf_ref[...]).astype(BF16) for x in xs]
    accs = list(xs)
    for c in range(0, wg_ref.shape[1], tf):
        gates = [_dot(xn, wg_ref[:, c:c + tf]) for xn in xns]
        ups = [_dot(xn, wu_ref[:, c:c + tf]) for xn in xns]
        hids = [(_silu(g) * u).astype(BF16) for g, u in zip(gates, ups)]
        accs = [a + _dot(hd, wd_ref[c:c + tf, :]) for a, hd in zip(accs, hids)]
    xn2 = [_rms(x, gp_ref[...]).astype(BF16) for x in accs]
    pgate = [_sigmoid(_dot(xn, wpg_ref[...])) for xn in xn2]
    emb = [_dot(p_ref[rs, :].astype(BF16), wp_ref[...]) for rs in rows]
    h3 = [x + e * g for x, e, g in zip(accs, emb, pgate)]
    for rs, x in zip(rows, h3):
        h_out_ref[rs, :] = x
    xn3 = [_rms(x, gm_ref[...]).astype(BF16) for x in h3]
    for c in range(0, win_ref.shape[1], tn):
        for rs, xn in zip(rows, xn3):
            u_ref[rs, c:c + tn] = _dot(xn, win_ref[:, c:c + tn]).astype(u_ref.dtype)


def _ff_tile(f, cap):
    best = LANES
    for c in range(LANES, cap + 1, LANES):
        if f % c == 0:
            best = c
    return best


def _ffn_ple_proj(h, g_ffn, wg, wu, wd, g_ple, p, layer, wp, wpg, g_mix, w_in, *, name):
    t, d = h.shape
    f = wg.shape[1]
    e = p.shape[2]
    n = w_in.shape[1]
    tm = min(ROW_TILE, t)
    tn = COL_TILE if n % COL_TILE == 0 else n
    row = lambda w: pl.BlockSpec((tm, w), lambda i: (i, 0))
    return pl.pallas_call(
        functools.partial(_ffn_ple_proj_kernel, tf=_ff_tile(f, 1536), tn=tn), grid=(t // tm,),
        in_specs=[row(d), _resident((1, d)), _resident((d, f)), _resident((d, f)), _resident((f, d)),
                  _resident((1, d)), pl.BlockSpec((None, tm, e), lambda i: (layer, i, 0)),
                  _resident((e, d)), _resident((d, d)), _resident((1, d)), _resident((d, n))],
        out_specs=[row(d), row(n)],
        out_shape=[jax.ShapeDtypeStruct((t, d), F32), jax.ShapeDtypeStruct((t, n), BF16)],
        compiler_params=_cparams("parallel"), name=name,
    )(h, g_ffn.reshape(1, d), wg, wu, wd, g_ple.reshape(1, d), p, wp, wpg, g_mix.reshape(1, d), w_in)


def _unit_lower_inverses(mats):
    n = mats[0].shape[0]
    row = lax.broadcasted_iota(jnp.int32, (n, n), 0)
    col = lax.broadcasted_iota(jnp.int32, (n, n), 1)
    eye = (row == col).astype(F32)
    same16 = (row // 16) == (col // 16)
    same32 = (row // 32) == (col // 32)
    off32 = jnp.logical_and(same32, jnp.logical_not(same16))

    def mm(ps, qs):
        return [_dot(p.astype(BF16), q.astype(BF16)) for p, q in zip(ps, qs)]

    ad = [jnp.where(same16, a, 0.0) for a in mats]
    a2 = mm(ad, ad)
    x = mm([eye - t for t in ad], [eye + t for t in a2])
    a4 = mm(a2, a2)
    x = mm(x, [eye + t for t in a4])
    a8 = mm(a4, a4)
    x = mm(x, [eye + t for t in a8])
    y = mm([jnp.where(off32, a, 0.0) for a in mats], x)
    x = [t - c for t, c in zip(x, mm(x, y))]
    y = mm([jnp.where(same32, 0.0, a) for a in mats], x)
    return [t - c for t, c in zip(x, mm(x, y))]


def _seq_tiles(pad_ref, s):
    rt = min(ELEM_ROWS, s)
    n_tiles = s // rt

    def rows(i):
        return pl.ds(pl.multiple_of(i * rt, rt), rt)

    def fill_pad(fn):
        pad_ref[0:TOP_PAD, :] = jnp.zeros((TOP_PAD, pad_ref.shape[1]), F32)

        def body(i, c):
            r = pl.multiple_of(i * rt, rt)
            pad_ref[pl.ds(TOP_PAD + r, rt), :] = fn(rows(i))
            return c
        lax.fori_loop(0, n_tiles, body, 0)

    def conv_tile(i, w_ref, width):
        tap = _delayed(pad_ref, pl.multiple_of(i * rt, rt), rt, width - 1)
        acc = None
        for j in range(width):
            term = w_ref[j:j + 1, :] * tap(width - 1 - j)
            acc = term if acc is None else acc + term
        return acc

    return n_tiles, rows, fill_pad, conv_tile


def _gated_conv_kernel(bg_ref, cg_ref, xin_ref, w_ref, o_ref, pad_ref):
    n_tiles, rows, fill_pad, conv_tile = _seq_tiles(pad_ref, bg_ref.shape[0])
    fill_pad(lambda rs: cg_ref[rs, :].astype(F32) * xin_ref[rs, :].astype(F32))

    def body(i, c):
        acc = conv_tile(i, w_ref, CONV_A_WIDTH)
        o_ref[rows(i), :] = (bg_ref[rows(i), :].astype(F32) * acc).astype(o_ref.dtype)
        return c
    lax.fori_loop(0, n_tiles, body, 0)


def _gated_conv(u, conv_a, batch, seq):
    cw = conv_a.shape[1]
    return pl.pallas_call(
        _gated_conv_kernel, grid=(batch,),
        in_specs=[pl.BlockSpec((seq, cw), lambda b, j=j: (b, j)) for j in range(3)] + [_resident(conv_a.shape)],
        out_specs=pl.BlockSpec((seq, cw), lambda b: (b, 0)),
        out_shape=jax.ShapeDtypeStruct((batch * seq, cw), BF16),
        scratch_shapes=[pltpu.VMEM((seq + TOP_PAD, cw), F32)],
        compiler_params=_cparams("parallel"), name="l0_gated_conv")(u, u, u, conv_a)


def _l0_mixer_kernel(q_ref, k_ref, v_ref, og_ref, ab_ref, wq_ref, wk_ref, wv_ref, alog_ref, dtb_ref, gn_ref,
                     o_ref,
                     pad_ref, qs_ref, ks_ref, vs_ref, us_ref, ws_ref, qk_ref, egl_ref, st_ref):
    s = q_ref.shape[0]
    heads = GDN_HEADS
    dk = q_ref.shape[1] // heads
    c64 = GDN_CHUNK
    n_chunks = s // c64
    n_tiles, rows, fill_pad, conv_tile = _seq_tiles(pad_ref, s)

    def l2n(x, scale):
        parts = []
        for h in range(heads):
            xh = x[:, h * dk:(h + 1) * dk]
            inv = lax.rsqrt(jnp.sum(xh * xh, axis=-1, keepdims=True) + EPS)
            parts.append(xh * (inv * scale))
        return jnp.concatenate(parts, axis=1)

    for src_ref, w_ref, dst_ref, post in (
            (q_ref, wq_ref, qs_ref, lambda x: l2n(x, dk ** -0.5)),
            (k_ref, wk_ref, ks_ref, lambda x: l2n(x, 1.0)),
            (v_ref, wv_ref, vs_ref, lambda x: x)):
        fill_pad(lambda rs, src_ref=src_ref: src_ref[rs, :].astype(F32))

        def conv_body(i, c, w_ref=w_ref, dst_ref=dst_ref, post=post):
            dst_ref[rows(i), :] = post(_silu(conv_tile(i, w_ref, GDN_CONV_WIDTH)))
            return c
        lax.fori_loop(0, n_tiles, conv_body, 0)

    ri = lax.broadcasted_iota(jnp.int32, (c64, c64), 0)
    ci = lax.broadcasted_iota(jnp.int32, (c64, c64), 1)
    tril = ri >= ci
    strict = ri > ci
    ltri = tril.astype(F32)

    group = PREP_CHUNKS if n_chunks % PREP_CHUNKS == 0 else 1

    def chunk_prep(cg, carry):
        chunks = []
        for cc in range(group):
            c = cg * group + cc
            rs = pl.ds(pl.multiple_of(c * c64, c64), c64)
            chunks.append((c, rs, ab_ref[rs, :], qs_ref[rs, :], ks_ref[rs, :], vs_ref[rs, :]))
        inst = []
        gcs = [_dot(ltri, -jnp.exp(alog_ref[...]) * _softplus(ab + dtb_ref[...]), precision=HIGHEST)
               for _, _, ab, _, _, _ in chunks]
        for (c, rs, ab, q_all, k_all, v_all), gc in zip(chunks, gcs):
            beta = _sigmoid(ab)
            gct = gc.T
            for h in range(heads):
                hs = slice(h * dk, (h + 1) * dk)
                gcol = gc[:, h:h + 1]
                glast = gc[c64 - 1:c64, h:h + 1]
                bcol = beta[:, heads + h:heads + h + 1]
                decay = jnp.where(tril, jnp.exp(jnp.where(tril, gcol - gct[h:h + 1, :], 0.0)), 0.0)
                kh, qh, vh = k_all[:, hs], q_all[:, hs], v_all[:, hs]
                kb = kh * bcol
                egc = jnp.exp(gcol)
                inst.append(dict(decay=decay, kh=kh, qh=qh, kb=kb, egc=egc,
                                 rhs=jnp.concatenate([vh * bcol, kb * egc], axis=1).astype(BF16),
                                 kd=kh * jnp.exp(glast - gcol),
                                 eg=jnp.broadcast_to(jnp.exp(glast), (SUBLANES, dk))))
        kqs = [_dot_nt(jnp.concatenate([t["kb"], t["qh"]], axis=0).astype(BF16), t["kh"].astype(BF16))
               for t in inst]
        minvs = _unit_lower_inverses([jnp.where(strict, kq[0:c64] * t["decay"], 0.0)
                                      for kq, t in zip(kqs, inst)])
        uws = [_dot(m.astype(BF16), t["rhs"]) for m, t in zip(minvs, inst)]
        for ci, (c, rs, _, _, _, _) in enumerate(chunks):
            sl = slice(ci * heads, (ci + 1) * heads)
            us_ref[rs, :] = jnp.concatenate([uw[:, 0:dk] for uw in uws[sl]], axis=1)
            ws_ref[rs, :] = jnp.concatenate([uw[:, dk:2 * dk] for uw in uws[sl]], axis=1).astype(BF16)
            qs_ref[rs, :] = jnp.concatenate([t["qh"] * t["egc"] for t in inst[sl]], axis=1)
            ks_ref[rs, :] = jnp.concatenate([t["kd"] for t in inst[sl]], axis=1)
            qk_ref[rs, :] = jnp.concatenate([kq[c64:2 * c64] * t["decay"]
                                             for kq, t in zip(kqs[sl], inst[sl])], axis=1).astype(BF16)
            e0 = pl.multiple_of(c * (heads * SUBLANES), heads * SUBLANES)
            egl_ref[pl.ds(e0, heads * SUBLANES), :] = jnp.concatenate([t["eg"] for t in inst[sl]], axis=0)
        return carry
    lax.fori_loop(0, n_chunks // group, chunk_prep, 0)

    st_ref[...] = jnp.zeros(st_ref.shape, F32)

    def scan(c, carry):
        rs = pl.ds(pl.multiple_of(c * c64, c64), c64)
        e0 = pl.multiple_of(c * (heads * SUBLANES), heads * SUBLANES)
        w_all, qg_all, u_all, kd_all = ws_ref[rs, :], qs_ref[rs, :], us_ref[rs, :], ks_ref[rs, :]
        qk_all = qk_ref[rs, :]
        eg_all = egl_ref[pl.ds(e0, heads * SUBLANES), :]
        hsl = [slice(h * dk, (h + 1) * dk) for h in range(heads)]
        sts = [st_ref[h] for h in range(heads)]
        wqs = [_dot(jnp.concatenate([w_all[:, hs], qg_all[:, hs].astype(BF16)], axis=0), st.astype(BF16))
               for hs, st in zip(hsl, sts)]
        vbs = [(u_all[:, hs] - wq[0:c64]).astype(BF16) for hs, wq in zip(hsl, wqs)]
        upd = [_dot_tn(kd_all[:, hs].astype(BF16), vb) for hs, vb in zip(hsl, vbs)]
        intra = [_dot(qk_all[:, h * c64:(h + 1) * c64], vbs[h]) for h in range(heads)]
        for h in range(heads):
            st_ref[h] = sts[h] * eg_all[h * SUBLANES:h * SUBLANES + 1, :] + upd[h]
        vs_ref[rs, :] = jnp.concatenate([wq[c64:2 * c64] + o for wq, o in zip(wqs, intra)], axis=1)
        return carry
    lax.fori_loop(0, n_chunks, scan, 0)

    def finish(i, c):
        o = vs_ref[rows(i), :]
        og = og_ref[rows(i), :].astype(F32)
        parts = []
        for h in range(heads):
            oh = o[:, h * dk:(h + 1) * dk]
            parts.append(_rms(oh, gn_ref[...]))
        y = jnp.concatenate(parts, axis=1) * _silu(og)
        o_ref[rows(i), :] = y.astype(o_ref.dtype)
        return c
    lax.fori_loop(0, n_tiles, finish, 0)


def _l0_mixer(u, ab, gdn_conv, a_log, dt_bias, gdn_norm_g, batch, seq):
    dk = gdn_norm_g.shape[0]
    heads = GDN_HEADS
    cw = heads * dk

    def lane_row(x):
        return jnp.zeros((1, LANES), F32).at[0, :x.shape[0]].set(x)

    in_specs = [pl.BlockSpec((seq, cw), lambda b, j=j: (b, j)) for j in range(3, 7)]
    in_specs += [pl.BlockSpec((seq, LANES), lambda b: (b, 0))]
    in_specs += [pl.BlockSpec((GDN_CONV_WIDTH, cw), lambda b, j=j: (0, j), pipeline_mode=pl.Buffered(1))
                 for j in range(3)]
    in_specs += [_resident((1, LANES)), _resident((1, LANES)), _resident((1, dk))]
    big = pltpu.VMEM((seq, cw), F32)
    scratch = [pltpu.VMEM((seq + TOP_PAD, cw), F32), big, big, big, big, pltpu.VMEM((seq, cw), BF16),
               pltpu.VMEM((seq, heads * GDN_CHUNK), BF16),
               pltpu.VMEM((seq // GDN_CHUNK * heads * SUBLANES, dk), F32),
               pltpu.VMEM((heads, dk, dk), F32)]
    return pl.pallas_call(
        _l0_mixer_kernel, grid=(batch,), in_specs=in_specs,
        out_specs=pl.BlockSpec((seq, cw), lambda b: (b, 0)),
        out_shape=jax.ShapeDtypeStruct((batch * seq, cw), BF16),
        scratch_shapes=scratch, compiler_params=_cparams("parallel"), name="l0_deltanet",
    )(u, u, u, u, ab, gdn_conv, gdn_conv, gdn_conv,
      lane_row(a_log), lane_row(dt_bias), gdn_norm_g.reshape(1, dk))


def _bias_table_kernel(tab_ref, o_ref, *, t):
    h = pl.program_id(0)
    m = pl.program_id(1)
    ri = lax.broadcasted_iota(jnp.int32, (t, t), 0)
    ci = lax.broadcasted_iota(jnp.int32, (t, t), 1)
    rel = m * t + ci - ri
    n = jnp.maximum(rel, 0)
    max_exact = NUM_BUCKETS // 2
    nf = jnp.maximum(n, 1).astype(F32)
    large = max_exact + (jnp.log(nf / max_exact) / math.log(MAX_DISTANCE / max_exact)
                         * (NUM_BUCKETS - max_exact)).astype(jnp.int32)
    large = jnp.minimum(large, NUM_BUCKETS - 1)
    bucket = jnp.where(n < max_exact, n, large)
    bias = jnp.zeros((t, t), F32)
    for b in range(NUM_BUCKETS):
        bias = jnp.where(bucket == b, tab_ref[b, h], bias)
    o_ref[...] = jnp.where(rel >= 0, bias * LOG2E, MASK_VALUE)


def _bias_table(rel_bias, seq, t):
    nb = seq // t
    heads = rel_bias.shape[1]
    return pl.pallas_call(
        functools.partial(_bias_table_kernel, t=t), grid=(heads, nb),
        in_specs=[pl.BlockSpec(memory_space=pltpu.SMEM)],
        out_specs=pl.BlockSpec((None, None, t, t), lambda h, m: (h, m, 0, 0)),
        out_shape=jax.ShapeDtypeStruct((heads, nb, t, t), F32),
        compiler_params=_cparams("parallel", "parallel"), name="rel_bias_table")(rel_bias)


def _attn_kernel(q_ref, k_ref, v_ref, tb_ref, lam_ref, gn_ref, o_ref, m_ref, l_ref, acc_ref,
                 *, t, heads, lambda_init):
    qi = pl.program_id(2)
    dh2 = q_ref.shape[1] // heads
    dh = dh2 // 2
    lane = lax.broadcasted_iota(jnp.int32, (t, dh2), 1)
    qqs = []
    for h in range(heads):
        qf = q_ref[:, h * dh2:(h + 1) * dh2].astype(F32) * (dh ** -0.5 * LOG2E)
        qqs.append(jnp.concatenate([jnp.where(lane < dh, qf, 0.0), jnp.where(lane >= dh, qf, 0.0)],
                                   axis=0).astype(BF16))

    hs = [slice(h * dh2, (h + 1) * dh2) for h in range(heads)]

    def update(j0, nblk):
        ks = pl.ds(pl.multiple_of(j0 * t, t), nblk * t)
        k_all = k_ref[ks, :]
        v_all = v_ref[ks, :]
        s_t = [_dot_nt(k_all[:, hs[h]], qqs[h]) for h in range(heads)]
        ps, alphas = [], []
        for h in range(heads):
            m = m_ref[h, 0:1, :]
            b = jnp.concatenate([tb_ref[h, qi - j0 - i] for i in range(nblk)], axis=0)
            s = jnp.concatenate([s_t[h][:, 0:t] + b, s_t[h][:, t:2 * t] + b], axis=1)
            m_new = jnp.maximum(m, jnp.max(s, axis=0, keepdims=True))
            alpha = jnp.exp2(m - m_new)
            p = jnp.exp2(s - m_new)
            m_ref[h] = jnp.broadcast_to(m_new, m_ref.shape[1:])
            l_ref[h] = jnp.broadcast_to(alpha * l_ref[h, 0:1, :] + jnp.sum(p, axis=0, keepdims=True),
                                        l_ref.shape[1:])
            alphas.append(alpha)
            ps.append(p.astype(BF16))
        pv = [_dot_tn(v_all[:, hs[h]], ps[h]) for h in range(heads)]
        for h in range(heads):
            acc_ref[h] = alphas[h] * acc_ref[h] + pv[h]

    m_ref[...] = jnp.full(m_ref.shape, MASK_VALUE, F32)
    l_ref[...] = jnp.zeros(l_ref.shape, F32)
    acc_ref[...] = jnp.zeros(acc_ref.shape, F32)
    odd = (qi + 1) % 2

    @pl.when(odd == 1)
    def _():
        update(0, 1)

    def pair(i, c):
        update(odd + 2 * i, 2)
        return c
    lax.fori_loop(0, (qi + 1) // 2, pair, 0)
    lp = lam_ref[...]
    lam = (jnp.exp(jnp.sum(lp[0:1] * lp[1:2], axis=-1, keepdims=True))
           - jnp.exp(jnp.sum(lp[2:3] * lp[3:4], axis=-1, keepdims=True)) + lambda_init)
    for h in range(heads):
        o12 = acc_ref[h] / l_ref[h, 0:1, :]
        o = (o12[:, 0:t] - lam * o12[:, t:2 * t]).T
        o_ref[:, h * dh2:(h + 1) * dh2] = (_rms(o, gn_ref[...]) * (1.0 - lambda_init)).astype(o_ref.dtype)


def _diff_attention(u, table, lam_params, norm_g, batch, seq, lambda_init):
    heads = DIFF_HEADS
    dh2 = norm_g.shape[0]
    hg = ATTN_HEADS_PER_STEP
    w = hg * dh2
    ng = heads // hg
    t = table.shape[2]
    nq = seq // t
    return pl.pallas_call(
        functools.partial(_attn_kernel, t=t, heads=hg, lambda_init=lambda_init),
        grid=(batch, ng, nq),
        in_specs=[pl.BlockSpec((t, w), lambda b, g, i: (b * nq + i, g)),
                  pl.BlockSpec((seq, w), lambda b, g, i: (b, ng + g)),
                  pl.BlockSpec((seq, w), lambda b, g, i: (b, 2 * ng + g)),
                  pl.BlockSpec((hg, nq, t, t), lambda b, g, i: (g, 0, 0, 0)),
                  _resident(lam_params.shape), _resident((1, dh2))],
        out_specs=pl.BlockSpec((t, w), lambda b, g, i: (b * nq + i, g)),
        out_shape=jax.ShapeDtypeStruct((batch * seq, heads * dh2), BF16),
        scratch_shapes=[pltpu.VMEM((hg, SUBLANES, 2 * t), F32), pltpu.VMEM((hg, SUBLANES, 2 * t), F32),
                        pltpu.VMEM((hg, dh2, 2 * t), F32)],
        compiler_params=_cparams("parallel", "parallel", "arbitrary"), name="diff_attention",
    )(u, u, u, table, lam_params, norm_g.reshape(1, dh2))


def _conformer_kernel(ga_ref, gb_ref, w_ref, b_ref, lg_ref, lb_ref, o_ref, pad_ref):
    s, c = ga_ref.shape
    rt = min(ELEM_ROWS, s)
    pad_ref[0:TOP_PAD, :] = jnp.zeros((TOP_PAD, c), F32)

    def glu(i, carry):
        r = pl.multiple_of(i * rt, rt)
        rs = pl.ds(r, rt)
        pad_ref[pl.ds(TOP_PAD + r, rt), :] = ga_ref[rs, :].astype(F32) * _sigmoid(gb_ref[rs, :].astype(F32))
        return carry
    lax.fori_loop(0, s // rt, glu, 0)

    ct = CONF_ROWS

    def conv(i, carry):
        r = pl.multiple_of(i * ct, ct)
        tap = _delayed(pad_ref, r, ct, CONF_WIDTH - 1)
        acc = jnp.broadcast_to(b_ref[...], (ct, c))
        for j in range(CONF_WIDTH):
            acc = acc + w_ref[j:j + 1, :] * tap(CONF_WIDTH - 1 - j)
        mu = jnp.mean(acc, axis=-1, keepdims=True)
        xc = acc - mu
        var = jnp.mean(xc * xc, axis=-1, keepdims=True)
        y = xc * lax.rsqrt(var + EPS) * lg_ref[...] + lb_ref[...]
        o_ref[pl.ds(r, ct), :] = _silu(y).astype(o_ref.dtype)
        return carry
    lax.fori_loop(0, s // ct, conv, 0)


def _conformer(u, w, b, ln_g, ln_b, batch, seq, col0):
    c = w.shape[1]
    return pl.pallas_call(
        _conformer_kernel, grid=(batch,),
        in_specs=[pl.BlockSpec((seq, c), lambda i: (i, col0)),
                  pl.BlockSpec((seq, c), lambda i: (i, col0 + 1)),
                  _resident(w.shape), _resident((1, c)), _resident((1, c)), _resident((1, c))],
        out_specs=pl.BlockSpec((seq, c), lambda i: (i, 0)),
        out_shape=jax.ShapeDtypeStruct((batch * seq, c), BF16),
        scratch_shapes=[pltpu.VMEM((seq + TOP_PAD, c), F32)],
        compiler_params=_cparams("parallel"), name="conformer_conv",
    )(u, u, w, b.reshape(1, c), ln_g.reshape(1, c), ln_b.reshape(1, c))


def _proj_router_kernel(y1_ref, y2_ref, w1_ref, w2_ref, h_ref, g_ref, wrt_ref, upper_ref,
                        h_out_ref, hn_ref, route_ref, gate_ref, cnt_ref, carry_ref):
    i = pl.program_id(0)
    tm = h_ref.shape[0]
    n_e = N_EXPERTS

    @pl.when(i == 0)
    def _():
        carry_ref[...] = jnp.zeros(carry_ref.shape, F32)

    h = h_ref[...] + _dot(y1_ref[...], w1_ref[...]) + _dot(y2_ref[...], w2_ref[...])
    h_out_ref[...] = h
    xn = _rms(h, g_ref[...])
    _store_token_tiles(hn_ref, 0, xn)

    logits = _dot_nt(wrt_ref[...], xn.astype(BF16))[0:n_e, :]
    sub = lax.broadcasted_iota(jnp.int32, logits.shape, 0)
    m1 = jnp.max(logits, axis=0, keepdims=True)
    i1 = jnp.min(jnp.where(logits == m1, sub, n_e), axis=0, keepdims=True)
    rest = jnp.where(sub == i1, -jnp.inf, logits)
    m2 = jnp.max(rest, axis=0, keepdims=True)
    i2 = jnp.min(jnp.where(rest == m2, sub, n_e), axis=0, keepdims=True)
    e = jnp.exp(m2 - m1)
    g1 = 1.0 / (1.0 + e)
    oh1 = sub == i1
    oh2 = sub == i2
    both = oh1.astype(F32) + oh2.astype(F32)
    csum = _dot(both.astype(BF16), upper_ref[...])
    carry = carry_ref[:, 0:1]
    before = csum - both + carry
    total = carry + csum[:, tm - 1:tm]
    carry_ref[...] = jnp.broadcast_to(total, carry_ref.shape)
    cnt_ref[...] = jnp.broadcast_to(total, cnt_ref.shape).astype(jnp.int32)
    r1 = jnp.sum(jnp.where(oh1, before, 0.0), axis=0, keepdims=True).astype(jnp.int32)
    r2 = jnp.sum(jnp.where(oh2, before, 0.0), axis=0, keepdims=True).astype(jnp.int32)
    route_ref[...] = jnp.where(sub == 0, i1, jnp.where(sub == 1, i2, jnp.where(sub == 2, r1,
                               jnp.where(sub == 3, r2, 0))))
    grow = jnp.where(sub == 0, g1, jnp.where(sub == 1, e * g1, 0.0))
    sel = (lax.broadcasted_iota(jnp.int32, (n_e, LANES), 0)
           == lax.broadcasted_iota(jnp.int32, (n_e, LANES), 1)).astype(F32)
    gate_ref[...] = lax.dot_general(grow, sel, (((0,), (0,)), ((), ())), precision=HIGHEST,
                                    preferred_element_type=F32)


def _proj_router(y1, y2, w, h, g, wr):
    t, d = h.shape
    k1, k2 = y1.shape[1], y2.shape[1]
    tm = min(ROW_TILE, t)
    wrt = jnp.zeros((LANES, d), BF16).at[:N_EXPERTS].set(wr.T.astype(BF16))
    upper = (jnp.arange(tm)[:, None] <= jnp.arange(tm)[None, :]).astype(BF16)
    row = lambda width: pl.BlockSpec((tm, width), lambda i: (i, 0))
    return pl.pallas_call(
        _proj_router_kernel, grid=(t // tm,),
        in_specs=[row(k1), row(k2), _resident((k1, d)), _resident((k2, d)), row(d),
                  _resident((1, d)), _resident((LANES, d)), _resident((tm, tm))],
        out_specs=[row(d), pl.BlockSpec((tm * d // LANES, LANES), lambda i: (i, 0)),
                   pl.BlockSpec((N_EXPERTS, tm), lambda i: (0, i)), row(LANES),
                   pl.BlockSpec((N_EXPERTS, LANES), lambda i: (0, 0))],
        out_shape=[jax.ShapeDtypeStruct((t, d), F32), jax.ShapeDtypeStruct((t * d // LANES, LANES), F32),
                   jax.ShapeDtypeStruct((N_EXPERTS, t), jnp.int32), jax.ShapeDtypeStruct((t, LANES), F32),
                   jax.ShapeDtypeStruct((N_EXPERTS, LANES), jnp.int32)],
        scratch_shapes=[pltpu.VMEM((N_EXPERTS, LANES), F32)],
        compiler_params=_cparams("arbitrary"), name="l1_out_proj_router",
    )(y1, y2, w[:k1], w[k1:], h, g.reshape(1, d), wrt, upper)


def _dispatch_kernel(dest_hbm, pad_hbm, x_ref, xs_hbm, idx_smem, pad_smem, zero_ref, sem_idx, sem_x):
    i = pl.program_id(0)
    n = x_ref.shape[0] // SUBLANES
    m = TOP_K * n
    cp = pltpu.make_async_copy(dest_hbm.at[i], idx_smem, sem_idx)
    cp.start()
    cp.wait()

    def body(r, c):
        for k in range(TOP_K):
            _token_tile_copy(x_ref, r, xs_hbm, idx_smem[k * n + r], sem_x).start(priority=k % 2)
        return c
    lax.fori_loop(0, n, body, 0, unroll=8)
    for _ in range(TOP_K):
        pltpu.make_async_copy(x_ref, xs_hbm.at[pl.ds(0, n * SUBLANES)], sem_x).wait()

    @pl.when(i == pl.num_programs(0) - 1)
    def _():
        n_pad = pad_smem.shape[0]
        cp = pltpu.make_async_copy(pad_hbm, pad_smem, sem_idx)
        cp.start()
        cp.wait()
        zero_ref[...] = jnp.zeros(zero_ref.shape, F32)

        def fill(j, c):
            for q in range(2):
                _token_tile_copy(zero_ref, 0, xs_hbm, pad_smem[2 * j + q], sem_x).start(priority=q)
            return c
        lax.fori_loop(0, n_pad // 2, fill, 0, unroll=8)
        pltpu.make_async_copy(xs_hbm.at[pl.ds(0, n_pad * SUBLANES)], xs_hbm.at[pl.ds(0, n_pad * SUBLANES)],
                              sem_x).wait()


def _dispatch(hn_tiles, dest, pad_rows, t, d):
    n = min(DISPATCH_ROWS, t)
    steps = t // n
    tpt = d // LANES
    p = t * TOP_K + N_EXPERTS * MOE_ROWS
    dest_steps = jnp.concatenate([dest[k].reshape(steps, n) for k in range(TOP_K)], axis=1)
    return pl.pallas_call(
        _dispatch_kernel, grid=(steps,),
        in_specs=[pl.BlockSpec(memory_space=pl.ANY), pl.BlockSpec(memory_space=pl.ANY),
                  pl.BlockSpec((n * tpt, LANES), lambda i: (i, 0))],
        out_specs=pl.BlockSpec(memory_space=pl.ANY),
        out_shape=jax.ShapeDtypeStruct((p * tpt, LANES), F32),
        scratch_shapes=[pltpu.SMEM((TOP_K * n,), jnp.int32), pltpu.SMEM(pad_rows.shape, jnp.int32),
                        pltpu.VMEM((tpt, LANES), F32), pltpu.SemaphoreType.DMA, pltpu.SemaphoreType.DMA],
        compiler_params=_cparams("arbitrary"), name="moe_dispatch")(dest_steps, pad_rows, hn_tiles)


def _dispatch_plan(route, counts, t):
    tm = MOE_ROWS
    p = t * TOP_K + N_EXPERTS * tm
    padded = (counts + tm - 1) // tm * tm
    pend = jnp.cumsum(padded)
    pstart = pend - padded
    experts = jnp.arange(N_EXPERTS, dtype=jnp.int32)[:, None]
    dest = jnp.stack([jnp.sum(jnp.where(route[k][None, :] == experts, pstart[:, None], 0), axis=0)
                      + route[TOP_K + k] for k in range(TOP_K)])
    n_pad = p - t * TOP_K
    gaps = jnp.concatenate([padded - counts, (p - pend[-1])[None]])
    gap_end = jnp.cumsum(gaps)
    gap_row0 = jnp.concatenate([pstart + counts, pend[-1:]])
    j = jnp.arange(n_pad, dtype=jnp.int32)
    which = jnp.searchsorted(gap_end, j, side='right', method='compare_all')
    pad_rows = (gap_row0[which] + j - (gap_end - gaps)[which]).astype(jnp.int32)
    nb = p // tm
    blk_e = jnp.minimum(jnp.searchsorted(pend, jnp.arange(nb, dtype=jnp.int32) * tm, side='right',
                                         method='compare_all'), N_EXPERTS - 1).astype(jnp.int32)
    return dest.astype(jnp.int32), pad_rows, blk_e, (pend[-1:] // tm).astype(jnp.int32)


def _experts_kernel(blk_e_ref, used_ref, x_ref, wgu_ref, wd_ref, o_ref, hid_ref):
    f = wd_ref.shape[0]
    tm = hid_ref.shape[0]
    live = pl.program_id(0) < used_ref[0]

    @pl.when(live)
    def _():
        x = _load_token_tiles(x_ref, 0, tm).astype(BF16)
        for c in range(0, f, MOE_SUB_TILE):
            hid_ref[:, c:c + MOE_SUB_TILE] = (
                _silu(_dot(x, wgu_ref[:, c:c + MOE_SUB_TILE]))
                * _dot(x, wgu_ref[:, f + c:f + c + MOE_SUB_TILE])).astype(BF16)
        _store_token_tiles(o_ref, 0, _dot(hid_ref[...], wd_ref[...]))

    @pl.when(jnp.logical_not(live))
    def _():
        o_ref[...] = jnp.zeros(o_ref.shape, F32)


def _experts(xs, blk_e, n_used, w_gate_up, w_down):
    f, d = w_down.shape[1], w_down.shape[2]
    tm = MOE_ROWS
    tpt = d // LANES
    assert f % MOE_SUB_TILE == 0
    grid_spec = pltpu.PrefetchScalarGridSpec(
        num_scalar_prefetch=2, grid=(blk_e.shape[0],),
        in_specs=[pl.BlockSpec((tm * tpt, LANES), lambda i, e, u: (i, 0)),
                  pl.BlockSpec((None, d, 2 * f), lambda i, e, u: (e[i], 0, 0), pipeline_mode=pl.Buffered(1)),
                  pl.BlockSpec((None, f, d), lambda i, e, u: (e[i], 0, 0), pipeline_mode=pl.Buffered(1))],
        out_specs=pl.BlockSpec((tm * tpt, LANES), lambda i, e, u: (i, 0)),
        scratch_shapes=[pltpu.VMEM((tm, f), BF16)])
    return pl.pallas_call(
        _experts_kernel, grid_spec=grid_spec, out_shape=jax.ShapeDtypeStruct(xs.shape, F32),
        compiler_params=_cparams("parallel"), name="moe_experts",
    )(blk_e, n_used, xs, w_gate_up, w_down)


def _combine_kernel(dest_hbm, y_hbm, gate_ref, h_ref, g_ref, p_ref, wp_ref, wg_ref, fg_ref, o_ref,
                    idx0, idx1, ybuf, sem_idx, sem_rows):
    i = pl.program_id(0)
    steps = pl.num_programs(0)
    n = h_ref.shape[0]
    m = TOP_K * n
    slot = i % 2
    idxs = (idx0, idx1)

    def idx_copy(blk, s):
        return pltpu.make_async_copy(dest_hbm.at[blk], idxs[s], sem_idx.at[s])

    def issue_tiles(s):
        def body(r, c):
            for k in range(TOP_K):
                _token_tile_copy(y_hbm, idxs[s][k * n + r], ybuf, s * m + k * n + r,
                                 sem_rows.at[s]).start(priority=k % 2)
            return c
        lax.fori_loop(0, n, body, 0, unroll=8)

    @pl.when(i == 0)
    def _():
        idx_copy(0, 0).start()
        idx_copy(0, 0).wait()
        idx_copy(1, 1).start()
        issue_tiles(0)

    for s in range(2):
        @pl.when(slot == s)
        def _(s=s):
            idx_copy(i + 1, 1 - s).wait()

            @pl.when(i + 2 <= steps)
            def _():
                idx_copy(i + 2, s).start()

            issue_tiles(1 - s)
            _token_tiles_wait(y_hbm, ybuf, s * m, m, sem_rows.at[s])

    nc = n // COMBINE_CHUNKS
    rows = [pl.ds(c * nc, nc) for c in range(COMBINE_CHUNKS)]
    xs = []
    for c, rs in enumerate(rows):
        gates = gate_ref[rs, :]
        xs.append(h_ref[rs, :] + gates[:, 0:1] * _load_token_tiles(ybuf, slot * m + c * nc, nc)
                  + gates[:, 1:2] * _load_token_tiles(ybuf, slot * m + n + c * nc, nc))
    xns = [_rms(x, g_ref[...]).astype(BF16) for x in xs]
    gate = [_sigmoid(_dot(xn, wg_ref[...])) for xn in xns]
    emb = [_dot(p_ref[rs, :].astype(BF16), wp_ref[...]) for rs in rows]
    for rs, x, e, g in zip(rows, xs, emb, gate):
        o_ref[rs, :] = _rms(x + e * g, fg_ref[...])

    @pl.when(i == steps - 1)
    def _():
        _token_tiles_wait(y_hbm, ybuf, (1 - slot) * m, m, sem_rows.at[1 - slot])


def _combine_ple_final(dest, y_tiles, gates, h, g, p, layer, wp, wg, final_g):
    t, d = h.shape
    e = p.shape[2]
    n = min(COMBINE_ROWS, t)
    m = TOP_K * n
    steps = t // n
    dest_steps = jnp.concatenate([dest[k].reshape(steps, n) for k in range(TOP_K)], axis=1)
    dest_steps = jnp.concatenate([dest_steps, jnp.zeros((1, m), jnp.int32)], axis=0)
    return pl.pallas_call(
        _combine_kernel, grid=(steps,),
        in_specs=[pl.BlockSpec(memory_space=pl.ANY), pl.BlockSpec(memory_space=pl.ANY),
                  pl.BlockSpec((n, LANES), lambda i: (i, 0)),
                  pl.BlockSpec((n, d), lambda i: (i, 0)), _resident((1, d)),
                  pl.BlockSpec((None, n, e), lambda i: (layer, i, 0)), _resident((e, d)), _resident((d, d)),
                  _resident((1, d))],
        out_specs=pl.BlockSpec((n, d), lambda i: (i, 0)),
        out_shape=jax.ShapeDtypeStruct((t, d), F32),
        scratch_shapes=[pltpu.SMEM((m,), jnp.int32), pltpu.SMEM((m,), jnp.int32),
                        pltpu.VMEM((2 * m * d // LANES, LANES), F32),
                        pltpu.SemaphoreType.DMA((2,)), pltpu.SemaphoreType.DMA((2,))],
        compiler_params=_cparams("arbitrary"), name="moe_combine_ple_final",
    )(dest_steps, y_tiles, gates, h, g.reshape(1, d), p, wp, wg, final_g.reshape(1, d))


def kernel(x, p, norm_mix_g, norm_ffn_g, norm_ple_g, final_norm_g, ev_w_in, ev_conv_a, ev_gdn_conv, ev_gdn_A_log, ev_gdn_dt_bias, ev_gdn_norm_g, ev_w_out, od_w_in, od_lambda, od_diff_norm_g, od_conf_dw_w, od_conf_dw_b, od_conf_ln_g, od_conf_ln_b, od_w_out, rel_bias, ffn_w_gate_up, ffn_w_down, moe_router, moe_w_gate_up, moe_w_down, ple_w_proj, ple_w_gate):
    batch, seq, d = x.shape
    t = batch * seq
    depth = p.shape[0]
    assert depth == 2 and seq % GDN_CHUNK == 0
    h = x.reshape(t, d)
    pf = p.reshape(depth, t, p.shape[-1])

    heads = GDN_HEADS
    n_main = ev_w_in.shape[2] - 2 * heads
    w_in = ev_w_in[0]
    w_ab = jnp.zeros((d, LANES), BF16).at[:, :2 * heads].set(w_in[:, n_main:].astype(BF16))
    u, ab = _norm_proj(h, norm_mix_g[0], w_in[:, :n_main].astype(BF16), w_ab, name="l0_in_proj")
    ya = _gated_conv(u, ev_conv_a[0], batch, seq)
    yb = _l0_mixer(u, ab, ev_gdn_conv[0], ev_gdn_A_log[0], ev_gdn_dt_bias[0], ev_gdn_norm_g[0], batch, seq)
    h = _proj_residual([ya, yb], ev_w_out[0].astype(BF16), h, name="l0_out_proj")
    f = ffn_w_down.shape[1]
    lambda_init = 0.8 - 0.6 * math.exp(-0.3 * 1)
    h, u = _ffn_ple_proj(h, norm_ffn_g[0], ffn_w_gate_up[0, :, :f].astype(BF16),
                         ffn_w_gate_up[0, :, f:].astype(BF16), ffn_w_down[0].astype(BF16),
                         norm_ple_g[0], pf, 0, ple_w_proj[0].astype(BF16), ple_w_gate[0].astype(BF16),
                         norm_mix_g[1], od_w_in[0].astype(BF16), name="l0_ffn_ple_l1_in_proj")

    table = _bias_table(rel_bias, seq, min(ATTN_TILE, seq))
    o_attn = _diff_attention(u, table, od_lambda[0], od_diff_norm_g[0], batch, seq, lambda_init)
    c_conf = od_conf_dw_w.shape[2]
    o_conf = _conformer(u, od_conf_dw_w[0], od_conf_dw_b[0], od_conf_ln_g[0], od_conf_ln_b[0],
                        batch, seq, 3 * DIFF_HEADS * od_diff_norm_g.shape[1] // c_conf)
    h, hn, route, gates, counts = _proj_router(o_attn, o_conf, od_w_out[0].astype(BF16), h, norm_ffn_g[1],
                                               moe_router[0])
    dest, pad_rows, blk_e, n_used = _dispatch_plan(route, counts[:, 0], t)
    xs = _dispatch(hn, dest, pad_rows, t, d)
    y = _experts(xs, blk_e, n_used, moe_w_gate_up[0].astype(BF16), moe_w_down[0].astype(BF16))
    out = _combine_ple_final(dest, y, gates, h, norm_ple_g[1], pf, 1, ple_w_proj[1].astype(BF16),
                             ple_w_gate[1].astype(BF16), final_norm_g)
    return out.reshape(batch, seq, d)
```

```python
import functools
import math

import jax
import jax.numpy as jnp
from jax import lax
from jax.experimental import pallas as pl
from jax.experimental.pallas import tpu as pltpu

F32 = jnp.float32
BF16 = jnp.bfloat16
HIGHEST = lax.Precision.HIGHEST

EPS = 1e-6
CONV_A_WIDTH = 3
GDN_HEADS = 4
GDN_CONV_WIDTH = 4
GDN_CHUNK = 64
DIFF_HEADS = 4
NUM_BUCKETS = 32
MAX_DISTANCE = 128
CONF_WIDTH = 31
N_EXPERTS = 8
TOP_K = 2

LANES = 128
SUBLANES = 8
VMEM_LIMIT_BYTES = 56 * 1024 * 1024
MASK_VALUE = -1e30
LOG2E = math.log2(math.e)

ROW_TILE = 512
COL_TILE = 512
FFN_CHUNKS = 2
ELEM_ROWS = 256
CONF_ROWS = 256
PREP_CHUNKS = 8
ATTN_TILE = 256
ATTN_HEADS_PER_STEP = 4
MOE_ROWS = 1024
DISPATCH_ROWS = 4096
MOE_SUB_TILE = 256
COMBINE_ROWS = 1024
COMBINE_CHUNKS = 8
TOP_PAD = 32


def _cparams(*sem):
    return pltpu.CompilerParams(dimension_semantics=sem, vmem_limit_bytes=VMEM_LIMIT_BYTES)


def _resident(shape):
    nd = len(shape)
    return pl.BlockSpec(shape, lambda *_: (0,) * nd, pipeline_mode=pl.Buffered(1))


def _rms(x, g):
    return x * lax.rsqrt(jnp.mean(x * x, axis=-1, keepdims=True) + EPS) * g


def _sigmoid(x):
    return jax.nn.sigmoid(x)


def _silu(x):
    return x * jax.nn.sigmoid(x)


def _softplus(x):
    return jnp.maximum(x, 0.0) + jnp.log1p(jnp.exp(-jnp.abs(x)))


def _dot(a, b, **kw):
    return jnp.dot(a, b, preferred_element_type=F32, **kw)


def _dot_nt(a, b):
    return lax.dot_general(a, b, (((1,), (1,)), ((), ())), preferred_element_type=F32)


def _dot_tn(a, b):
    return lax.dot_general(a, b, (((0,), (0,)), ((), ())), preferred_element_type=F32)


def _delayed(pad_ref, r, rows, max_delay):
    lead = -(-max_delay // SUBLANES) * SUBLANES
    win = pad_ref[pl.ds(TOP_PAD + r - lead, rows + lead), :]
    rolled = {0: win}

    def tap(d):
        a, b = divmod(d, SUBLANES)
        if b not in rolled:
            rolled[b] = pltpu.roll(win, b, 0)
        start = lead - SUBLANES * a
        return rolled[b][start:start + rows, :]
    return tap


def _load_token_tiles(ref, first_token, n):
    return jnp.concatenate([ref[pl.ds(first_token * SUBLANES + s, n, stride=SUBLANES), :]
                            for s in range(SUBLANES)], axis=1)


def _store_token_tiles(ref, first_token, x):
    n = x.shape[0]
    for s in range(SUBLANES):
        ref[pl.ds(first_token * SUBLANES + s, n, stride=SUBLANES), :] = x[:, s * LANES:(s + 1) * LANES]


def _token_tile_copy(src_hbm, src_token, dst_vmem, dst_token, sem):
    return pltpu.make_async_copy(src_hbm.at[pl.ds(pl.multiple_of(src_token * SUBLANES, SUBLANES), SUBLANES)],
                                 dst_vmem.at[pl.ds(pl.multiple_of(dst_token * SUBLANES, SUBLANES), SUBLANES)],
                                 sem)


def _token_tiles_wait(src_hbm, dst_vmem, first_token, n, sem):
    pltpu.make_async_copy(src_hbm.at[pl.ds(0, n * SUBLANES)],
                          dst_vmem.at[pl.ds(pl.multiple_of(first_token * SUBLANES, SUBLANES), n * SUBLANES)],
                          sem).wait()


def _norm_proj_kernel(h_ref, g_ref, w_ref, *rest, tn, with_aux):
    xn = _rms(h_ref[...], g_ref[...]).astype(BF16)
    if with_aux:
        w2_ref, o_ref, o2_ref = rest
        o2_ref[...] = _dot(xn, w2_ref[...])
    else:
        (o_ref,) = rest
    n = w_ref.shape[1]
    for c in range(0, n, tn):
        o_ref[:, c:c + tn] = _dot(xn, w_ref[:, c:c + tn]).astype(o_ref.dtype)


def _norm_proj(h, g, w, w_aux=None, *, name):
    t, d = h.shape
    n = w.shape[1]
    tm = min(ROW_TILE, t)
    tn = COL_TILE if n % COL_TILE == 0 else n
    in_specs = [pl.BlockSpec((tm, d), lambda i: (i, 0)), _resident((1, d)), _resident((d, n))]
    out_shape = [jax.ShapeDtypeStruct((t, n), BF16)]
    out_specs = [pl.BlockSpec((tm, n), lambda i: (i, 0))]
    args = [h, g.reshape(1, d), w]
    if w_aux is not None:
        in_specs.append(_resident(w_aux.shape))
        out_shape.append(jax.ShapeDtypeStruct((t, w_aux.shape[1]), F32))
        out_specs.append(pl.BlockSpec((tm, w_aux.shape[1]), lambda i: (i, 0)))
        args.append(w_aux)
    out = pl.pallas_call(
        functools.partial(_norm_proj_kernel, tn=tn, with_aux=w_aux is not None),
        grid=(t // tm,), in_specs=in_specs, out_specs=out_specs, out_shape=out_shape,
        compiler_params=_cparams("parallel"), name=name)(*args)
    return out if w_aux is not None else out[0]


def _proj_residual_kernel(*refs):
    n = (len(refs) - 2) // 2
    y_refs, w_refs, h_ref, o_ref = refs[:n], refs[n:2 * n], refs[2 * n], refs[2 * n + 1]
    acc = h_ref[...]
    for y_ref, w_ref in zip(y_refs, w_refs):
        acc = acc + _dot(y_ref[...], w_ref[...])
    o_ref[...] = acc


def _proj_residual(ys, w, h, *, name):
    t, d = h.shape
    tm = min(ROW_TILE, t)
    ws, r0 = [], 0
    for y in ys:
        ws.append(w[r0:r0 + y.shape[1]])
        r0 += y.shape[1]
    return pl.pallas_call(
        _proj_residual_kernel, grid=(t // tm,),
        in_specs=([pl.BlockSpec((tm, y.shape[1]), lambda i: (i, 0)) for y in ys]
                  + [_resident(wi.shape) for wi in ws] + [pl.BlockSpec((tm, d), lambda i: (i, 0))]),
        out_specs=pl.BlockSpec((tm, d), lambda i: (i, 0)),
        out_shape=jax.ShapeDtypeStruct((t, d), F32),
        compiler_params=_cparams("parallel"), name=name)(*ys, *ws, h)


def _ffn_ple_proj_kernel(h_ref, gf_ref, wg_ref, wu_ref, wd_ref, gp_ref, p_ref, wp_ref, wpg_ref,
                         gm_ref, win_ref, h_out_ref, u_ref, *, tf, tn):
    tm = h_ref.shape[0]
    nc = tm // FFN_CHUNKS
    rows = [pl.ds(c * nc, nc) for c in range(FFN_CHUNKS)]
    xs = [h_ref[rs, :] for rs in rows]
    xns = [_rms(x, gf_ref[...]).astype(BF16) for x in xs]
    accs = list(xs)
    for c in range(0, wg_ref.shape[1], tf):
        gates = [_dot(xn, wg_ref[:, c:c + tf]) for xn in xns]
        ups = [_dot(xn, wu_ref[:, c:c + tf]) for xn in xns]
        hids = [(_silu(g) * u).astype(BF16) for g, u in zip(gates, ups)]
        accs = [a + _dot(hd, wd_ref[c:c + tf, :]) for a, hd in zip(accs, hids)]
    xn2 = [_rms(x, gp_ref[...]).astype(BF16) for x in accs]
    pgate = [_sigmoid(_dot(xn, wpg_ref[...])) for xn in xn2]
    emb = [_dot(p_ref[rs, :].astype(BF16), wp_ref[...]) for rs in rows]
    h3 = [x + e * g for x, e, g in zip(accs, emb, pgate)]
    for rs, x in zip(rows, h3):
        h_out_ref[rs, :] = x
    xn3 = [_rms(x, gm_ref[...]).astype(BF16) for x in h3]
    for c in range(0, win_ref.shape[1], tn):
        for rs, xn in zip(rows, xn3):
            u_ref[rs, c:c + tn] = _dot(xn, win_ref[:, c:c + tn]).astype(u_ref.dtype)


def _ff_tile(f, cap):
    best = LANES
    for c in range(LANES, cap + 1, LANES):
        if f % c == 0:
            best = c
    return best


def _ffn_ple_proj(h, g_ffn, wg, wu, wd, g_ple, p, layer, wp, wpg, g_mix, w_in, *, name):
    t, d = h.shape
    f = wg.shape[1]
    e = p.shape[2]
    n = w_in.shape[1]
    tm = min(ROW_TILE, t)
    tn = COL_TILE if n % COL_TILE == 0 else n
    row = lambda w: pl.BlockSpec((tm, w), lambda i: (i, 0))
    return pl.pallas_call(
        functools.partial(_ffn_ple_proj_kernel, tf=_ff_tile(f, 1536), tn=tn), grid=(t // tm,),
        in_specs=[row(d), _resident((1, d)), _resident((d, f)), _resident((d, f)), _resident((f, d)),
                  _resident((1, d)), pl.BlockSpec((None, tm, e), lambda i: (layer, i, 0)),
                  _resident((e, d)), _resident((d, d)), _resident((1, d)), _resident((d, n))],
        out_specs=[row(d), row(n)],
        out_shape=[jax.ShapeDtypeStruct((t, d), F32), jax.ShapeDtypeStruct((t, n), BF16)],
        compiler_params=_cparams("parallel"), name=name,
    )(h, g_ffn.reshape(1, d), wg, wu, wd, g_ple.reshape(1, d), p, wp, wpg, g_mix.reshape(1, d), w_in)


def _unit_lower_inverses(mats):
    n = mats[0].shape[0]
    row = lax.broadcasted_iota(jnp.int32, (n, n), 0)
    col = lax.broadcasted_iota(jnp.int32, (n, n), 1)
    eye = (row == col).astype(F32)
    same16 = (row // 16) == (col // 16)
    same32 = (row // 32) == (col // 32)
    off32 = jnp.logical_and(same32, jnp.logical_not(same16))

    def mm(ps, qs):
        return [_dot(p.astype(BF16), q.astype(BF16)) for p, q in zip(ps, qs)]

    ad = [jnp.where(same16, a, 0.0) for a in mats]
    a2 = mm(ad, ad)
    x = mm([eye - t for t in ad], [eye + t for t in a2])
    a4 = mm(a2, a2)
    x = mm(x, [eye + t for t in a4])
    a8 = mm(a4, a4)
    x = mm(x, [eye + t for t in a8])
    y = mm([jnp.where(off32, a, 0.0) for a in mats], x)
    x = [t - c for t, c in zip(x, mm(x, y))]
    y = mm([jnp.where(same32, 0.0, a) for a in mats], x)
    return [t - c for t, c in zip(x, mm(x, y))]


def _seq_tiles(pad_ref, s):
    rt = min(ELEM_ROWS, s)
    n_tiles = s // rt

    def rows(i):
        return pl.ds(pl.multiple_of(i * rt, rt), rt)

    def fill_pad(fn):
        pad_ref[0:TOP_PAD, :] = jnp.zeros((TOP_PAD, pad_ref.shape[1]), F32)

        def body(i, c):
            r = pl.multiple_of(i * rt, rt)
            pad_ref[pl.ds(TOP_PAD + r, rt), :] = fn(rows(i))
            return c
        lax.fori_loop(0, n_tiles, body, 0)

    def conv_tile(i, w_ref, width):
        tap = _delayed(pad_ref, pl.multiple_of(i * rt, rt), rt, width - 1)
        acc = None
        for j in range(width):
            term = w_ref[j:j + 1, :] * tap(width - 1 - j)
            acc = term if acc is None else acc + term
        return acc

    return n_tiles, rows, fill_pad, conv_tile


def _gated_conv_kernel(bg_ref, cg_ref, xin_ref, w_ref, o_ref, pad_ref):
    n_tiles, rows, fill_pad, conv_tile = _seq_tiles(pad_ref, bg_ref.shape[0])
    fill_pad(lambda rs: cg_ref[rs, :].astype(F32) * xin_ref[rs, :].astype(F32))

    def body(i, c):
        acc = conv_tile(i, w_ref, CONV_A_WIDTH)
        o_ref[rows(i), :] = (bg_ref[rows(i), :].astype(F32) * acc).astype(o_ref.dtype)
        return c
    lax.fori_loop(0, n_tiles, body, 0)


def _gated_conv(u, conv_a, batch, seq):
    cw = conv_a.shape[1]
    return pl.pallas_call(
        _gated_conv_kernel, grid=(batch,),
        in_specs=[pl.BlockSpec((seq, cw), lambda b, j=j: (b, j)) for j in range(3)] + [_resident(conv_a.shape)],
        out_specs=pl.BlockSpec((seq, cw), lambda b: (b, 0)),
        out_shape=jax.ShapeDtypeStruct((batch * seq, cw), BF16),
        scratch_shapes=[pltpu.VMEM((seq + TOP_PAD, cw), F32)],
        compiler_params=_cparams("parallel"), name="l0_gated_conv")(u, u, u, conv_a)


def _l0_mixer_kernel(q_ref, k_ref, v_ref, og_ref, ab_ref, wq_ref, wk_ref, wv_ref, alog_ref, dtb_ref, gn_ref,
                     o_ref,
                     pad_ref, qs_ref, ks_ref, vs_ref, us_ref, ws_ref, qk_ref, egl_ref, st_ref):
    s = q_ref.shape[0]
    heads = GDN_HEADS
    dk = q_ref.shape[1] // heads
    c64 = GDN_CHUNK
    n_chunks = s // c64
    n_tiles, rows, fill_pad, conv_tile = _seq_tiles(pad_ref, s)

    def l2n(x, scale):
        parts = []
        for h in range(heads):
            xh = x[:, h * dk:(h + 1) * dk]
            inv = lax.rsqrt(jnp.sum(xh * xh, axis=-1, keepdims=True) + EPS)
            parts.append(xh * (inv * scale))
        return jnp.concatenate(parts, axis=1)

    for src_ref, w_ref, dst_ref, post in (
            (q_ref, wq_ref, qs_ref, lambda x: l2n(x, dk ** -0.5)),
            (k_ref, wk_ref, ks_ref, lambda x: l2n(x, 1.0)),
            (v_ref, wv_ref, vs_ref, lambda x: x)):
        fill_pad(lambda rs, src_ref=src_ref: src_ref[rs, :].astype(F32))

        def conv_body(i, c, w_ref=w_ref, dst_ref=dst_ref, post=post):
            dst_ref[rows(i), :] = post(_silu(conv_tile(i, w_ref, GDN_CONV_WIDTH)))
            return c
        lax.fori_loop(0, n_tiles, conv_body, 0)

    ri = lax.broadcasted_iota(jnp.int32, (c64, c64), 0)
    ci = lax.broadcasted_iota(jnp.int32, (c64, c64), 1)
    tril = ri >= ci
    strict = ri > ci
    ltri = tril.astype(F32)

    group = PREP_CHUNKS if n_chunks % PREP_CHUNKS == 0 else 1

    def chunk_prep(cg, carry):
        chunks = []
        for cc in range(group):
            c = cg * group + cc
            rs = pl.ds(pl.multiple_of(c * c64, c64), c64)
            chunks.append((c, rs, ab_ref[rs, :], qs_ref[rs, :], ks_ref[rs, :], vs_ref[rs, :]))
        inst = []
        gcs = [_dot(ltri, -jnp.exp(alog_ref[...]) * _softplus(ab + dtb_ref[...]), precision=HIGHEST)
               for _, _, ab, _, _, _ in chunks]
        for (c, rs, ab, q_all, k_all, v_all), gc in zip(chunks, gcs):
            beta = _sigmoid(ab)
            gct = gc.T
            for h in range(heads):
                hs = slice(h * dk, (h + 1) * dk)
                gcol = gc[:, h:h + 1]
                glast = gc[c64 - 1:c64, h:h + 1]
                bcol = beta[:, heads + h:heads + h + 1]
                decay = jnp.where(tril, jnp.exp(jnp.where(tril, gcol - gct[h:h + 1, :], 0.0)), 0.0)
                kh, qh, vh = k_all[:, hs], q_all[:, hs], v_all[:, hs]
                kb = kh * bcol
                egc = jnp.exp(gcol)
                inst.append(dict(decay=decay, kh=kh, qh=qh, kb=kb, egc=egc,
                                 rhs=jnp.concatenate([vh * bcol, kb * egc], axis=1).astype(BF16),
                                 kd=kh * jnp.exp(glast - gcol),
                                 eg=jnp.broadcast_to(jnp.exp(glast), (SUBLANES, dk))))
        kqs = [_dot_nt(jnp.concatenate([t["kb"], t["qh"]], axis=0).astype(BF16), t["kh"].astype(BF16))
               for t in inst]
        minvs = _unit_lower_inverses([jnp.where(strict, kq[0:c64] * t["decay"], 0.0)
                                      for kq, t in zip(kqs, inst)])
        uws = [_dot(m.astype(BF16), t["rhs"]) for m, t in zip(minvs, inst)]
        for ci, (c, rs, _, _, _, _) in enumerate(chunks):
            sl = slice(ci * heads, (ci + 1) * heads)
            us_ref[rs, :] = jnp.concatenate([uw[:, 0:dk] for uw in uws[sl]], axis=1)
            ws_ref[rs, :] = jnp.concatenate([uw[:, dk:2 * dk] for uw in uws[sl]], axis=1).astype(BF16)
            qs_ref[rs, :] = jnp.concatenate([t["qh"] * t["egc"] for t in inst[sl]], axis=1)
            ks_ref[rs, :] = jnp.concatenate([t["kd"] for t in inst[sl]], axis=1)
            qk_ref[rs, :] = jnp.concatenate([kq[c64:2 * c64] * t["decay"]
                                             for kq, t in zip(kqs[sl], inst[sl])], axis=1).astype(BF16)
            e0 = pl.multiple_of(c * (heads * SUBLANES), heads * SUBLANES)
            egl_ref[pl.ds(e0, heads * SUBLANES), :] = jnp.concatenate([t["eg"] for t in inst[sl]], axis=0)
        return carry
    lax.fori_loop(0, n_chunks // group, chunk_prep, 0)

    st_ref[...] = jnp.zeros(st_ref.shape, F32)

    def scan(c, carry):
        rs = pl.ds(pl.multiple_of(c * c64, c64), c64)
        e0 = pl.multiple_of(c * (heads * SUBLANES), heads * SUBLANES)
        w_all, qg_all, u_all, kd_all = ws_ref[rs, :], qs_ref[rs, :], us_ref[rs, :], ks_ref[rs, :]
        qk_all = qk_ref[rs, :]
        eg_all = egl_ref[pl.ds(e0, heads * SUBLANES), :]
        hsl = [slice(h * dk, (h + 1) * dk) for h in range(heads)]
        sts = [st_ref[h] for h in range(heads)]
        wqs = [_dot(jnp.concatenate([w_all[:, hs], qg_all[:, hs].astype(BF16)], axis=0), st.astype(BF16))
               for hs, st in zip(hsl, sts)]
        vbs = [(u_all[:, hs] - wq[0:c64]).astype(BF16) for hs, wq in zip(hsl, wqs)]
        upd = [_dot_tn(kd_all[:, hs].astype(BF16), vb) for hs, vb in zip(hsl, vbs)]
        intra = [_dot(qk_all[:, h * c64:(h + 1) * c64], vbs[h]) for h in range(heads)]
        for h in range(heads):
            st_ref[h] = sts[h] * eg_all[h * SUBLANES:h * SUBLANES + 1, :] + upd[h]
        vs_ref[rs, :] = jnp.concatenate([wq[c64:2 * c64] + o for wq, o in zip(wqs, intra)], axis=1)
        return carry
    lax.fori_loop(0, n_chunks, scan, 0)

    def finish(i, c):
        o = vs_ref[rows(i), :]
        og = og_ref[rows(i), :].astype(F32)
        parts = []
        for h in range(heads):
            oh = o[:, h * dk:(h + 1) * dk]
            parts.append(_rms(oh, gn_ref[...]))
        y = jnp.concatenate(parts, axis=1) * _silu(og)
        o_ref[rows(i), :] = y.astype(o_ref.dtype)
        return c
    lax.fori_loop(0, n_tiles, finish, 0)


def _l0_mixer(u, ab, gdn_conv, a_log, dt_bias, gdn_norm_g, batch, seq):
    dk = gdn_norm_g.shape[0]
    heads = GDN_HEADS
    cw = heads * dk

    def lane_row(x):
        return jnp.zeros((1, LANES), F32).at[0, :x.shape[0]].set(x)

    in_specs = [pl.BlockSpec((seq, cw), lambda b, j=j: (b, j)) for j in range(3, 7)]
    in_specs += [pl.BlockSpec((seq, LANES), lambda b: (b, 0))]
    in_specs += [pl.BlockSpec((GDN_CONV_WIDTH, cw), lambda b, j=j: (0, j), pipeline_mode=pl.Buffered(1))
                 for j in range(3)]
    in_specs += [_resident((1, LANES)), _resident((1, LANES)), _resident((1, dk))]
    big = pltpu.VMEM((seq, cw), F32)
    scratch = [pltpu.VMEM((seq + TOP_PAD, cw), F32), big, big, big, big, pltpu.VMEM((seq, cw), BF16),
               pltpu.VMEM((seq, heads * GDN_CHUNK), BF16),
               pltpu.VMEM((seq // GDN_CHUNK * heads * SUBLANES, dk), F32),
               pltpu.VMEM((heads, dk, dk), F32)]
    return pl.pallas_call(
        _l0_mixer_kernel, grid=(batch,), in_specs=in_specs,
        out_specs=pl.BlockSpec((seq, cw), lambda b: (b, 0)),
        out_shape=jax.ShapeDtypeStruct((batch * seq, cw), BF16),
        scratch_shapes=scratch, compiler_params=_cparams("parallel"), name="l0_deltanet",
    )(u, u, u, u, ab, gdn_conv, gdn_conv, gdn_conv,
      lane_row(a_log), lane_row(dt_bias), gdn_norm_g.reshape(1, dk))


def _bias_table_kernel(tab_ref, o_ref, *, t):
    h = pl.program_id(0)
    m = pl.program_id(1)
    ri = lax.broadcasted_iota(jnp.int32, (t, t), 0)
    ci = lax.broadcasted_iota(jnp.int32, (t, t), 1)
    rel = m * t + ci - ri
    n = jnp.maximum(rel, 0)
    max_exact = NUM_BUCKETS // 2
    nf = jnp.maximum(n, 1).astype(F32)
    large = max_exact + (jnp.log(nf / max_exact) / math.log(MAX_DISTANCE / max_exact)
                         * (NUM_BUCKETS - max_exact)).astype(jnp.int32)
    large = jnp.minimum(large, NUM_BUCKETS - 1)
    bucket = jnp.where(n < max_exact, n, large)
    bias = jnp.zeros((t, t), F32)
    for b in range(NUM_BUCKETS):
        bias = jnp.where(bucket == b, tab_ref[b, h], bias)
    o_ref[...] = jnp.where(rel >= 0, bias * LOG2E, MASK_VALUE)


def _bias_table(rel_bias, seq, t):
    nb = seq // t
    heads = rel_bias.shape[1]
    return pl.pallas_call(
        functools.partial(_bias_table_kernel, t=t), grid=(heads, nb),
        in_specs=[pl.BlockSpec(memory_space=pltpu.SMEM)],
        out_specs=pl.BlockSpec((None, None, t, t), lambda h, m: (h, m, 0, 0)),
        out_shape=jax.ShapeDtypeStruct((heads, nb, t, t), F32),
        compiler_params=_cparams("parallel", "parallel"), name="rel_bias_table")(rel_bias)


def _attn_kernel(q_ref, k_ref, v_ref, tb_ref, lam_ref, gn_ref, o_ref, m_ref, l_ref, acc_ref,
                 *, t, heads, lambda_init):
    qi = pl.program_id(2)
    dh2 = q_ref.shape[1] // heads
    dh = dh2 // 2
    lane = lax.broadcasted_iota(jnp.int32, (t, dh2), 1)
    qqs = []
    for h in range(heads):
        qf = q_ref[:, h * dh2:(h + 1) * dh2].astype(F32) * (dh ** -0.5 * LOG2E)
        qqs.append(jnp.concatenate([jnp.where(lane < dh, qf, 0.0), jnp.where(lane >= dh, qf, 0.0)],
                                   axis=0).astype(BF16))

    hs = [slice(h * dh2, (h + 1) * dh2) for h in range(heads)]

    def update(j0, nblk):
        ks = pl.ds(pl.multiple_of(j0 * t, t), nblk * t)
        k_all = k_ref[ks, :]
        v_all = v_ref[ks, :]
        s_t = [_dot_nt(k_all[:, hs[h]], qqs[h]) for h in range(heads)]
        ps, alphas = [], []
        for h in range(heads):
            m = m_ref[h, 0:1, :]
            b = jnp.concatenate([tb_ref[h, qi - j0 - i] for i in range(nblk)], axis=0)
            s = jnp.concatenate([s_t[h][:, 0:t] + b, s_t[h][:, t:2 * t] + b], axis=1)
            m_new = jnp.maximum(m, jnp.max(s, axis=0, keepdims=True))
            alpha = jnp.exp2(m - m_new)
            p = jnp.exp2(s - m_new)
            m_ref[h] = jnp.broadcast_to(m_new, m_ref.shape[1:])
            l_ref[h] = jnp.broadcast_to(alpha * l_ref[h, 0:1, :] + jnp.sum(p, axis=0, keepdims=True),
                                        l_ref.shape[1:])
            alphas.append(alpha)
            ps.append(p.astype(BF16))
        pv = [_dot_tn(v_all[:, hs[h]], ps[h]) for h in range(heads)]
        for h in range(heads):
            acc_ref[h] = alphas[h] * acc_ref[h] + pv[h]

    m_ref[...] = jnp.full(m_ref.shape, MASK_VALUE, F32)
    l_ref[...] = jnp.zeros(l_ref.shape, F32)
    acc_ref[...] = jnp.zeros(acc_ref.shape, F32)
    odd = (qi + 1) % 2

    @pl.when(odd == 1)
    def _():
        update(0, 1)

    def pair(i, c):
        update(odd + 2 * i, 2)
        return c
    lax.fori_loop(0, (qi + 1) // 2, pair, 0)
    lp = lam_ref[...]
    lam = (jnp.exp(jnp.sum(lp[0:1] * lp[1:2], axis=-1, keepdims=True))
           - jnp.exp(jnp.sum(lp[2:3] * lp[3:4], axis=-1, keepdims=True)) + lambda_init)
    for h in range(heads):
        o12 = acc_ref[h] / l_ref[h, 0:1, :]
        o = (o12[:, 0:t] - lam * o12[:, t:2 * t]).T
        o_ref[:, h * dh2:(h + 1) * dh2] = (_rms(o, gn_ref[...]) * (1.0 - lambda_init)).astype(o_ref.dtype)


def _diff_attention(u, table, lam_params, norm_g, batch, seq, lambda_init):
    heads = DIFF_HEADS
    dh2 = norm_g.shape[0]
    hg = ATTN_HEADS_PER_STEP
    w = hg * dh2
    ng = heads // hg
    t = table.shape[2]
    nq = seq // t
    return pl.pallas_call(
        functools.partial(_attn_kernel, t=t, heads=hg, lambda_init=lambda_init),
        grid=(batch, ng, nq),
        in_specs=[pl.BlockSpec((t, w), lambda b, g, i: (b * nq + i, g)),
                  pl.BlockSpec((seq, w), lambda b, g, i: (b, ng + g)),
                  pl.BlockSpec((seq, w), lambda b, g, i: (b, 2 * ng + g)),
                  pl.BlockSpec((hg, nq, t, t), lambda b, g, i: (g, 0, 0, 0)),
                  _resident(lam_params.shape), _resident((1, dh2))],
        out_specs=pl.BlockSpec((t, w), lambda b, g, i: (b * nq + i, g)),
        out_shape=jax.ShapeDtypeStruct((batch * seq, heads * dh2), BF16),
        scratch_shapes=[pltpu.VMEM((hg, SUBLANES, 2 * t), F32), pltpu.VMEM((hg, SUBLANES, 2 * t), F32),
                        pltpu.VMEM((hg, dh2, 2 * t), F32)],
        compiler_params=_cparams("parallel", "parallel", "arbitrary"), name="diff_attention",
    )(u, u, u, table, lam_params, norm_g.reshape(1, dh2))


def _conformer_kernel(ga_ref, gb_ref, w_ref, b_ref, lg_ref, lb_ref, o_ref, pad_ref):
    s, c = ga_ref.shape
    rt = min(ELEM_ROWS, s)
    pad_ref[0:TOP_PAD, :] = jnp.zeros((TOP_PAD, c), F32)

    def glu(i, carry):
        r = pl.multiple_of(i * rt, rt)
        rs = pl.ds(r, rt)
        pad_ref[pl.ds(TOP_PAD + r, rt), :] = ga_ref[rs, :].astype(F32) * _sigmoid(gb_ref[rs, :].astype(F32))
        return carry
    lax.fori_loop(0, s // rt, glu, 0)

    ct = CONF_ROWS

    def conv(i, carry):
        r = pl.multiple_of(i * ct, ct)
        tap = _delayed(pad_ref, r, ct, CONF_WIDTH - 1)
        acc = jnp.broadcast_to(b_ref[...], (ct, c))
        for j in range(CONF_WIDTH):
            acc = acc + w_ref[j:j + 1, :] * tap(CONF_WIDTH - 1 - j)
        mu = jnp.mean(acc, axis=-1, keepdims=True)
        xc = acc - mu
        var = jnp.mean(xc * xc, axis=-1, keepdims=True)
        y = xc * lax.rsqrt(var + EPS) * lg_ref[...] + lb_ref[...]
        o_ref[pl.ds(r, ct), :] = _silu(y).astype(o_ref.dtype)
        return carry
    lax.fori_loop(0, s // ct, conv, 0)


def _conformer(u, w, b, ln_g, ln_b, batch, seq, col0):
    c = w.shape[1]
    return pl.pallas_call(
        _conformer_kernel, grid=(batch,),
        in_specs=[pl.BlockSpec((seq, c), lambda i: (i, col0)),
                  pl.BlockSpec((seq, c), lambda i: (i, col0 + 1)),
                  _resident(w.shape), _resident((1, c)), _resident((1, c)), _resident((1, c))],
        out_specs=pl.BlockSpec((seq, c), lambda i: (i, 0)),
        out_shape=jax.ShapeDtypeStruct((batch * seq, c), BF16),
        scratch_shapes=[pltpu.VMEM((seq + TOP_PAD, c), F32)],
        compiler_params=_cparams("parallel"), name="conformer_conv",
    )(u, u, w, b.reshape(1, c), ln_g.reshape(1, c), ln_b.reshape(1, c))


def _proj_router_kernel(y1_ref, y2_ref, w1_ref, w2_ref, h_ref, g_ref, wrt_ref, upper_ref,
                        h_out_ref, hn_ref, route_ref, gate_ref, cnt_ref, carry_ref):
    i = pl.program_id(0)
    tm = h_ref.shape[0]
    n_e = N_EXPERTS

    @pl.when(i == 0)
    def _():
        carry_ref[...] = jnp.zeros(carry_ref.shape, F32)

    h = h_ref[...] + _dot(y1_ref[...], w1_ref[...]) + _dot(y2_ref[...], w2_ref[...])
    h_out_ref[...] = h
    xn = _rms(h, g_ref[...])
    _store_token_tiles(hn_ref, 0, xn)

    logits = _dot_nt(wrt_ref[...], xn.astype(BF16))[0:n_e, :]
    sub = lax.broadcasted_iota(jnp.int32, logits.shape, 0)
    m1 = jnp.max(logits, axis=0, keepdims=True)
    i1 = jnp.min(jnp.where(logits == m1, sub, n_e), axis=0, keepdims=True)
    rest = jnp.where(sub == i1, -jnp.inf, logits)
    m2 = jnp.max(rest, axis=0, keepdims=True)
    i2 = jnp.min(jnp.where(rest == m2, sub, n_e), axis=0, keepdims=True)
    e = jnp.exp(m2 - m1)
    g1 = 1.0 / (1.0 + e)
    oh1 = sub == i1
    oh2 = sub == i2
    both = oh1.astype(F32) + oh2.astype(F32)
    csum = _dot(both.astype(BF16), upper_ref[...])
    carry = carry_ref[:, 0:1]
    before = csum - both + carry
    total = carry + csum[:, tm - 1:tm]
    carry_ref[...] = jnp.broadcast_to(total, carry_ref.shape)
    cnt_ref[...] = jnp.broadcast_to(total, cnt_ref.shape).astype(jnp.int32)
    r1 = jnp.sum(jnp.where(oh1, before, 0.0), axis=0, keepdims=True).astype(jnp.int32)
    r2 = jnp.sum(jnp.where(oh2, before, 0.0), axis=0, keepdims=True).astype(jnp.int32)
    route_ref[...] = jnp.where(sub == 0, i1, jnp.where(sub == 1, i2, jnp.where(sub == 2, r1,
                               jnp.where(sub == 3, r2, 0))))
    grow = jnp.where(sub == 0, g1, jnp.where(sub == 1, e * g1, 0.0))
    sel = (lax.broadcasted_iota(jnp.int32, (n_e, LANES), 0)
           == lax.broadcasted_iota(jnp.int32, (n_e, LANES), 1)).astype(F32)
    gate_ref[...] = lax.dot_general(grow, sel, (((0,), (0,)), ((), ())), precision=HIGHEST,
                                    preferred_element_type=F32)


def _proj_router(y1, y2, w, h, g, wr):
    t, d = h.shape
    k1, k2 = y1.shape[1], y2.shape[1]
    tm = min(ROW_TILE, t)
    wrt = jnp.zeros((LANES, d), BF16).at[:N_EXPERTS].set(wr.T.astype(BF16))
    upper = (jnp.arange(tm)[:, None] <= jnp.arange(tm)[None, :]).astype(BF16)
    row = lambda width: pl.BlockSpec((tm, width), lambda i: (i, 0))
    return pl.pallas_call(
        _proj_router_kernel, grid=(t // tm,),
        in_specs=[row(k1), row(k2), _resident((k1, d)), _resident((k2, d)), row(d),
                  _resident((1, d)), _resident((LANES, d)), _resident((tm, tm))],
        out_specs=[row(d), pl.BlockSpec((tm * d // LANES, LANES), lambda i: (i, 0)),
                   pl.BlockSpec((N_EXPERTS, tm), lambda i: (0, i)), row(LANES),
                   pl.BlockSpec((N_EXPERTS, LANES), lambda i: (0, 0))],
        out_shape=[jax.ShapeDtypeStruct((t, d), F32), jax.ShapeDtypeStruct((t * d // LANES, LANES), F32),
                   jax.ShapeDtypeStruct((N_EXPERTS, t), jnp.int32), jax.ShapeDtypeStruct((t, LANES), F32),
                   jax.ShapeDtypeStruct((N_EXPERTS, LANES), jnp.int32)],
        scratch_shapes=[pltpu.VMEM((N_EXPERTS, LANES), F32)],
        compiler_params=_cparams("arbitrary"), name="l1_out_proj_router",
    )(y1, y2, w[:k1], w[k1:], h, g.reshape(1, d), wrt, upper)


def _dispatch_kernel(dest_hbm, pad_hbm, x_ref, xs_hbm, idx_smem, pad_smem, zero_ref, sem_idx, sem_x):
    i = pl.program_id(0)
    n = x_ref.shape[0] // SUBLANES
    m = TOP_K * n
    cp = pltpu.make_async_copy(dest_hbm.at[i], idx_smem, sem_idx)
    cp.start()
    cp.wait()

    def body(r, c):
        for k in range(TOP_K):
            _token_tile_copy(x_ref, r, xs_hbm, idx_smem[k * n + r], sem_x).start(priority=k % 2)
        return c
    lax.fori_loop(0, n, body, 0, unroll=8)
    for _ in range(TOP_K):
        pltpu.make_async_copy(x_ref, xs_hbm.at[pl.ds(0, n * SUBLANES)], sem_x).wait()

    @pl.when(i == pl.num_programs(0) - 1)
    def _():
        n_pad = pad_smem.shape[0]
        cp = pltpu.make_async_copy(pad_hbm, pad_smem, sem_idx)
        cp.start()
        cp.wait()
        zero_ref[...] = jnp.zeros(zero_ref.shape, F32)

        def fill(j, c):
            for q in range(2):
                _token_tile_copy(zero_ref, 0, xs_hbm, pad_smem[2 * j + q], sem_x).start(priority=q)
            return c
        lax.fori_loop(0, n_pad // 2, fill, 0, unroll=8)
        pltpu.make_async_copy(xs_hbm.at[pl.ds(0, n_pad * SUBLANES)], xs_hbm.at[pl.ds(0, n_pad * SUBLANES)],
                              sem_x).wait()


def _dispatch(hn_tiles, dest, pad_rows, t, d):
    n = min(DISPATCH_ROWS, t)
    steps = t // n
    tpt = d // LANES
    p = t * TOP_K + N_EXPERTS * MOE_ROWS
    dest_steps = jnp.concatenate([dest[k].reshape(steps, n) for k in range(TOP_K)], axis=1)
    return pl.pallas_call(
        _dispatch_kernel, grid=(steps,),
        in_specs=[pl.BlockSpec(memory_space=pl.ANY), pl.BlockSpec(memory_space=pl.ANY),
                  pl.BlockSpec((n * tpt, LANES), lambda i: (i, 0))],
        out_specs=pl.BlockSpec(memory_space=pl.ANY),
        out_shape=jax.ShapeDtypeStruct((p * tpt, LANES), F32),
        scratch_shapes=[pltpu.SMEM((TOP_K * n,), jnp.int32), pltpu.SMEM(pad_rows.shape, jnp.int32),
                        pltpu.VMEM((tpt, LANES), F32), pltpu.SemaphoreType.DMA, pltpu.SemaphoreType.DMA],
        compiler_params=_cparams("arbitrary"), name="moe_dispatch")(dest_steps, pad_rows, hn_tiles)


def _dispatch_plan(route, counts, t):
    tm = MOE_ROWS
    p = t * TOP_K + N_EXPERTS * tm
    padded = (counts + tm - 1) // tm * tm
    pend = jnp.cumsum(padded)
    pstart = pend - padded
    experts = jnp.arange(N_EXPERTS, dtype=jnp.int32)[:, None]
    dest = jnp.stack([jnp.sum(jnp.where(route[k][None, :] == experts, pstart[:, None], 0), axis=0)
                      + route[TOP_K + k] for k in range(TOP_K)])
    n_pad = p - t * TOP_K
    gaps = jnp.concatenate([padded - counts, (p - pend[-1])[None]])
    gap_end = jnp.cumsum(gaps)
    gap_row0 = jnp.concatenate([pstart + counts, pend[-1:]])
    j = jnp.arange(n_pad, dtype=jnp.int32)
    which = jnp.searchsorted(gap_end, j, side='right', method='compare_all')
    pad_rows = (gap_row0[which] + j - (gap_end - gaps)[which]).astype(jnp.int32)
    nb = p // tm
    blk_e = jnp.minimum(jnp.searchsorted(pend, jnp.arange(nb, dtype=jnp.int32) * tm, side='right',
                                         method='compare_all'), N_EXPERTS - 1).astype(jnp.int32)
    return dest.astype(jnp.int32), pad_rows, blk_e, (pend[-1:] // tm).astype(jnp.int32)


def _experts_kernel(blk_e_ref, used_ref, x_ref, wgu_ref, wd_ref, o_ref, hid_ref):
    f = wd_ref.shape[0]
    tm = hid_ref.shape[0]
    live = pl.program_id(0) < used_ref[0]

    @pl.when(live)
    def _():
        x = _load_token_tiles(x_ref, 0, tm).astype(BF16)
        for c in range(0, f, MOE_SUB_TILE):
            hid_ref[:, c:c + MOE_SUB_TILE] = (
                _silu(_dot(x, wgu_ref[:, c:c + MOE_SUB_TILE]))
                * _dot(x, wgu_ref[:, f + c:f + c + MOE_SUB_TILE])).astype(BF16)
        _store_token_tiles(o_ref, 0, _dot(hid_ref[...], wd_ref[...]))

    @pl.when(jnp.logical_not(live))
    def _():
        o_ref[...] = jnp.zeros(o_ref.shape, F32)


def _experts(xs, blk_e, n_used, w_gate_up, w_down):
    f, d = w_down.shape[1], w_down.shape[2]
    tm = MOE_ROWS
    tpt = d // LANES
    assert f % MOE_SUB_TILE == 0
    grid_spec = pltpu.PrefetchScalarGridSpec(
        num_scalar_prefetch=2, grid=(blk_e.shape[0],),
        in_specs=[pl.BlockSpec((tm * tpt, LANES), lambda i, e, u: (i, 0)),
                  pl.BlockSpec((None, d, 2 * f), lambda i, e, u: (e[i], 0, 0), pipeline_mode=pl.Buffered(1)),
                  pl.BlockSpec((None, f, d), lambda i, e, u: (e[i], 0, 0), pipeline_mode=pl.Buffered(1))],
        out_specs=pl.BlockSpec((tm * tpt, LANES), lambda i, e, u: (i, 0)),
        scratch_shapes=[pltpu.VMEM((tm, f), BF16)])
    return pl.pallas_call(
        _experts_kernel, grid_spec=grid_spec, out_shape=jax.ShapeDtypeStruct(xs.shape, F32),
        compiler_params=_cparams("parallel"), name="moe_experts",
    )(blk_e, n_used, xs, w_gate_up, w_down)


def _combine_kernel(dest_hbm, y_hbm, gate_ref, h_ref, g_ref, p_ref, wp_ref, wg_ref, fg_ref, o_ref,
                    idx0, idx1, ybuf, sem_idx, sem_rows):
    i = pl.program_id(0)
    steps = pl.num_programs(0)
    n = h_ref.shape[0]
    m = TOP_K * n
    slot = i % 2
    idxs = (idx0, idx1)

    def idx_copy(blk, s):
        return pltpu.make_async_copy(dest_hbm.at[blk], idxs[s], sem_idx.at[s])

    def issue_tiles(s):
        def body(r, c):
            for k in range(TOP_K):
                _token_tile_copy(y_hbm, idxs[s][k * n + r], ybuf, s * m + k * n + r,
                                 sem_rows.at[s]).start(priority=k % 2)
            return c
        lax.fori_loop(0, n, body, 0, unroll=8)

    @pl.when(i == 0)
    def _():
        idx_copy(0, 0).start()
        idx_copy(0, 0).wait()
        idx_copy(1, 1).start()
        issue_tiles(0)

    for s in range(2):
        @pl.when(slot == s)
        def _(s=s):
            idx_copy(i + 1, 1 - s).wait()

            @pl.when(i + 2 <= steps)
            def _():
                idx_copy(i + 2, s).start()

            issue_tiles(1 - s)
            _token_tiles_wait(y_hbm, ybuf, s * m, m, sem_rows.at[s])

    nc = n // COMBINE_CHUNKS
    rows = [pl.ds(c * nc, nc) for c in range(COMBINE_CHUNKS)]
    xs = []
    for c, rs in enumerate(rows):
        gates = gate_ref[rs, :]
        xs.append(h_ref[rs, :] + gates[:, 0:1] * _load_token_tiles(ybuf, slot * m + c * nc, nc)
                  + gates[:, 1:2] * _load_token_tiles(ybuf, slot * m + n + c * nc, nc))
    xns = [_rms(x, g_ref[...]).astype(BF16) for x in xs]
    gate = [_sigmoid(_dot(xn, wg_ref[...])) for xn in xns]
    emb = [_dot(p_ref[rs, :].astype(BF16), wp_ref[...]) for rs in rows]
    for rs, x, e, g in zip(rows, xs, emb, gate):
        o_ref[rs, :] = _rms(x + e * g, fg_ref[...])

    @pl.when(i == steps - 1)
    def _():
        _token_tiles_wait(y_hbm, ybuf, (1 - slot) * m, m, sem_rows.at[1 - slot])


def _combine_ple_final(dest, y_tiles, gates, h, g, p, layer, wp, wg, final_g):
    t, d = h.shape
    e = p.shape[2]
    n = min(COMBINE_ROWS, t)
    m = TOP_K * n
    steps = t // n
    dest_steps = jnp.concatenate([dest[k].reshape(steps, n) for k in range(TOP_K)], axis=1)
    dest_steps = jnp.concatenate([dest_steps, jnp.zeros((1, m), jnp.int32)], axis=0)
    return pl.pallas_call(
        _combine_kernel, grid=(steps,),
        in_specs=[pl.BlockSpec(memory_space=pl.ANY), pl.BlockSpec(memory_space=pl.ANY),
                  pl.BlockSpec((n, LANES), lambda i: (i, 0)),
                  pl.BlockSpec((n, d), lambda i: (i, 0)), _resident((1, d)),
                  pl.BlockSpec((None, n, e), lambda i: (layer, i, 0)), _resident((e, d)), _resident((d, d)),
                  _resident((1, d))],
        out_specs=pl.BlockSpec((n, d), lambda i: (i, 0)),
        out_shape=jax.ShapeDtypeStruct((t, d), F32),
        scratch_shapes=[pltpu.SMEM((m,), jnp.int32), pltpu.SMEM((m,), jnp.int32),
                        pltpu.VMEM((2 * m * d // LANES, LANES), F32),
                        pltpu.SemaphoreType.DMA((2,)), pltpu.SemaphoreType.DMA((2,))],
        compiler_params=_cparams("arbitrary"), name="moe_combine_ple_final",
    )(dest_steps, y_tiles, gates, h, g.reshape(1, d), p, wp, wg, final_g.reshape(1, d))


def kernel(x, p, norm_mix_g, norm_ffn_g, norm_ple_g, final_norm_g, ev_w_in, ev_conv_a, ev_gdn_conv, ev_gdn_A_log, ev_gdn_dt_bias, ev_gdn_norm_g, ev_w_out, od_w_in, od_lambda, od_diff_norm_g, od_conf_dw_w, od_conf_dw_b, od_conf_ln_g, od_conf_ln_b, od_w_out, rel_bias, ffn_w_gate_up, ffn_w_down, moe_router, moe_w_gate_up, moe_w_down, ple_w_proj, ple_w_gate):
    batch, seq, d = x.shape
    t = batch * seq
    depth = p.shape[0]
    assert depth == 2 and seq % GDN_CHUNK == 0
    h = x.reshape(t, d)
    pf = p.reshape(depth, t, p.shape[-1])

    heads = GDN_HEADS
    n_main = ev_w_in.shape[2] - 2 * heads
    w_in = ev_w_in[0]
    w_ab = jnp.zeros((d, LANES), BF16).at[:, :2 * heads].set(w_in[:, n_main:].astype(BF16))
    u, ab = _norm_proj(h, norm_mix_g[0], w_in[:, :n_main].astype(BF16), w_ab, name="l0_in_proj")
    ya = _gated_conv(u, ev_conv_a[0], batch, seq)
    yb = _l0_mixer(u, ab, ev_gdn_conv[0], ev_gdn_A_log[0], ev_gdn_dt_bias[0], ev_gdn_norm_g[0], batch, seq)
    h = _proj_residual([ya, yb], ev_w_out[0].astype(BF16), h, name="l0_out_proj")
    f = ffn_w_down.shape[1]
    lambda_init = 0.8 - 0.6 * math.exp(-0.3 * 1)
    h, u = _ffn_ple_proj(h, norm_ffn_g[0], ffn_w_gate_up[0, :, :f].astype(BF16),
                         ffn_w_gate_up[0, :, f:].astype(BF16), ffn_w_down[0].astype(BF16),
                         norm_ple_g[0], pf, 0, ple_w_proj[0].astype(BF16), ple_w_gate[0].astype(BF16),
                         norm_mix_g[1], od_w_in[0].astype(BF16), name="l0_ffn_ple_l1_in_proj")

    table = _bias_table(rel_bias, seq, min(ATTN_TILE, seq))
    o_attn = _diff_attention(u, table, od_lambda[0], od_diff_norm_g[0], batch, seq, lambda_init)
    c_conf = od_conf_dw_w.shape[2]
    o_conf = _conformer(u, od_conf_dw_w[0], od_conf_dw_b[0], od_conf_ln_g[0], od_conf_ln_b[0],
                        batch, seq, 3 * DIFF_HEADS * od_diff_norm_g.shape[1] // c_conf)
    h, hn, route, gates, counts = _proj_router(o_attn, o_conf, od_w_out[0].astype(BF16), h, norm_ffn_g[1],
                                               moe_router[0])
    dest, pad_rows, blk_e, n_used = _dispatch_plan(route, counts[:, 0], t)
    xs = _dispatch(hn, dest, pad_rows, t, d)
    y = _experts(xs, blk_e, n_used, moe_w_gate_up[0].astype(BF16), moe_w_down[0].astype(BF16))
    out = _combine_ple_final(dest, y, gates, h, norm_ple_g[1], pf, 1, ple_w_proj[1].astype(BF16),
                             ple_w_gate[1].astype(BF16), final_norm_g)
    return out.reshape(batch, seq, d)
```

```python
import functools
import math

import jax
import jax.numpy as jnp
from jax import lax
from jax.experimental import pallas as pl
from jax.experimental.pallas import tpu as pltpu

F32 = jnp.float32
BF16 = jnp.bfloat16
HIGHEST = lax.Precision.HIGHEST

EPS = 1e-6
CONV_A_WIDTH = 3
GDN_HEADS = 4
GDN_CONV_WIDTH = 4
GDN_CHUNK = 64
DIFF_HEADS = 4
NUM_BUCKETS = 32
MAX_DISTANCE = 128
CONF_WIDTH = 31
N_EXPERTS = 8
TOP_K = 2

LANES = 128
SUBLANES = 8
VMEM_LIMIT_BYTES = 56 * 1024 * 1024
MASK_VALUE = -1e30
LOG2E = math.log2(math.e)

ROW_TILE = 512
COL_TILE = 512
FFN_CHUNKS = 2
ELEM_ROWS = 256
CONF_ROWS = 256
PREP_CHUNKS = 8
ATTN_TILE = 256
ATTN_HEADS_PER_STEP = 4
MOE_ROWS = 1024
DISPATCH_ROWS = 4096
MOE_SUB_TILE = 256
COMBINE_ROWS = 512
COMBINE_CHUNKS = 4
TOP_PAD = 32


def _cparams(*sem):
    return pltpu.CompilerParams(dimension_semantics=sem, vmem_limit_bytes=VMEM_LIMIT_BYTES)


def _resident(shape):
    nd = len(shape)
    return pl.BlockSpec(shape, lambda *_: (0,) * nd, pipeline_mode=pl.Buffered(1))


def _rms(x, g):
    return x * lax.rsqrt(jnp.mean(x * x, axis=-1, keepdims=True) + EPS) * g


def _sigmoid(x):
    return jax.nn.sigmoid(x)


def _silu(x):
    return x * jax.nn.sigmoid(x)


def _softplus(x):
    return jnp.maximum(x, 0.0) + jnp.log1p(jnp.exp(-jnp.abs(x)))


def _dot(a, b, **kw):
    return jnp.dot(a, b, preferred_element_type=F32, **kw)


def _dot_nt(a, b):
    return lax.dot_general(a, b, (((1,), (1,)), ((), ())), preferred_element_type=F32)


def _dot_tn(a, b):
    return lax.dot_general(a, b, (((0,), (0,)), ((), ())), preferred_element_type=F32)


def _delayed(pad_ref, r, rows, max_delay):
    lead = -(-max_delay // SUBLANES) * SUBLANES
    win = pad_ref[pl.ds(TOP_PAD + r - lead, rows + lead), :]
    rolled = {0: win}

    def tap(d):
        a, b = divmod(d, SUBLANES)
        if b not in rolled:
            rolled[b] = pltpu.roll(win, b, 0)
        start = lead - SUBLANES * a
        return rolled[b][start:start + rows, :]
    return tap


def _load_token_tiles(ref, first_token, n):
    return jnp.concatenate([ref[pl.ds(first_token * SUBLANES + s, n, stride=SUBLANES), :]
                            for s in range(SUBLANES)], axis=1)


def _store_token_tiles(ref, first_token, x):
    n = x.shape[0]
    for s in range(SUBLANES):
        ref[pl.ds(first_token * SUBLANES + s, n, stride=SUBLANES), :] = x[:, s * LANES:(s + 1) * LANES]


def _token_tile_copy(src_hbm, src_token, dst_vmem, dst_token, sem):
    return pltpu.make_async_copy(src_hbm.at[pl.ds(pl.multiple_of(src_token * SUBLANES, SUBLANES), SUBLANES)],
                                 dst_vmem.at[pl.ds(pl.multiple_of(dst_token * SUBLANES, SUBLANES), SUBLANES)],
                                 sem)


def _token_tiles_wait(src_hbm, dst_vmem, first_token, n, sem):
    pltpu.make_async_copy(src_hbm.at[pl.ds(0, n * SUBLANES)],
                          dst_vmem.at[pl.ds(pl.multiple_of(first_token * SUBLANES, SUBLANES), n * SUBLANES)],
                          sem).wait()


def _norm_proj_kernel(h_ref, g_ref, w_ref, *rest, tn, with_aux):
    xn = _rms(h_ref[...], g_ref[...]).astype(BF16)
    if with_aux:
        w2_ref, o_ref, o2_ref = rest
        o2_ref[...] = _dot(xn, w2_ref[...])
    else:
        (o_ref,) = rest
    n = w_ref.shape[1]
    for c in range(0, n, tn):
        o_ref[:, c:c + tn] = _dot(xn, w_ref[:, c:c + tn]).astype(o_ref.dtype)


def _norm_proj(h, g, w, w_aux=None, *, name):
    t, d = h.shape
    n = w.shape[1]
    tm = min(ROW_TILE, t)
    tn = COL_TILE if n % COL_TILE == 0 else n
    in_specs = [pl.BlockSpec((tm, d), lambda i: (i, 0)), _resident((1, d)), _resident((d, n))]
    out_shape = [jax.ShapeDtypeStruct((t, n), BF16)]
    out_specs = [pl.BlockSpec((tm, n), lambda i: (i, 0))]
    args = [h, g.reshape(1, d), w]
    if w_aux is not None:
        in_specs.append(_resident(w_aux.shape))
        out_shape.append(jax.ShapeDtypeStruct((t, w_aux.shape[1]), F32))
        out_specs.append(pl.BlockSpec((tm, w_aux.shape[1]), lambda i: (i, 0)))
        args.append(w_aux)
    out = pl.pallas_call(
        functools.partial(_norm_proj_kernel, tn=tn, with_aux=w_aux is not None),
        grid=(t // tm,), in_specs=in_specs, out_specs=out_specs, out_shape=out_shape,
        compiler_params=_cparams("parallel"), name=name)(*args)
    return out if w_aux is not None else out[0]


def _proj_residual_kernel(*refs):
    n = (len(refs) - 2) // 2
    y_refs, w_refs, h_ref, o_ref = refs[:n], refs[n:2 * n], refs[2 * n], refs[2 * n + 1]
    acc = h_ref[...]
    for y_ref, w_ref in zip(y_refs, w_refs):
        acc = acc + _dot(y_ref[...], w_ref[...])
    o_ref[...] = acc


def _proj_residual(ys, w, h, *, name):
    t, d = h.shape
    tm = min(ROW_TILE, t)
    ws, r0 = [], 0
    for y in ys:
        ws.append(w[r0:r0 + y.shape[1]])
        r0 += y.shape[1]
    return pl.pallas_call(
        _proj_residual_kernel, grid=(t // tm,),
        in_specs=([pl.BlockSpec((tm, y.shape[1]), lambda i: (i, 0)) for y in ys]
                  + [_resident(wi.shape) for wi in ws] + [pl.BlockSpec((tm, d), lambda i: (i, 0))]),
        out_specs=pl.BlockSpec((tm, d), lambda i: (i, 0)),
        out_shape=jax.ShapeDtypeStruct((t, d), F32),
        compiler_params=_cparams("parallel"), name=name)(*ys, *ws, h)


def _ffn_ple_proj_kernel(h_ref, gf_ref, wg_ref, wu_ref, wd_ref, gp_ref, p_ref, wp_ref, wpg_ref,
                         gm_ref, win_ref, h_out_ref, u_ref, *, tf, tn):
    tm = h_ref.shape[0]
    nc = tm // FFN_CHUNKS
    rows = [pl.ds(c * nc, nc) for c in range(FFN_CHUNKS)]
    xs = [h_ref[rs, :] for rs in rows]
    xns = [_rms(x, gf_ref[...]).astype(BF16) for x in xs]
    accs = list(xs)
    for c in range(0, wg_ref.shape[1], tf):
        gates = [_dot(xn, wg_ref[:, c:c + tf]) for xn in xns]
        ups = [_dot(xn, wu_ref[:, c:c + tf]) for xn in xns]
        hids = [(_silu(g) * u).astype(BF16) for g, u in zip(gates, ups)]
        accs = [a + _dot(hd, wd_ref[c:c + tf, :]) for a, hd in zip(accs, hids)]
    xn2 = [_rms(x, gp_ref[...]).astype(BF16) for x in accs]
    pgate = [_sigmoid(_dot(xn, wpg_ref[...])) for xn in xn2]
    emb = [_dot(p_ref[rs, :].astype(BF16), wp_ref[...]) for rs in rows]
    h3 = [x + e * g for x, e, g in zip(accs, emb, pgate)]
    for rs, x in zip(rows, h3):
        h_out_ref[rs, :] = x
    xn3 = [_rms(x, gm_ref[...]).astype(BF16) for x in h3]
    for c in range(0, win_ref.shape[1], tn):
        for rs, xn in zip(rows, xn3):
            u_ref[rs, c:c + tn] = _dot(xn, win_ref[:, c:c + tn]).astype(u_ref.dtype)


def _ff_tile(f, cap):
    best = LANES
    for c in range(LANES, cap + 1, LANES):
        if f % c == 0:
            best = c
    return best


def _ffn_ple_proj(h, g_ffn, wg, wu, wd, g_ple, p, layer, wp, wpg, g_mix, w_in, *, name):
    t, d = h.shape
    f = wg.shape[1]
    e = p.shape[2]
    n = w_in.shape[1]
    tm = min(ROW_TILE, t)
    tn = COL_TILE if n % COL_TILE == 0 else n
    row = lambda w: pl.BlockSpec((tm, w), lambda i: (i, 0))
    return pl.pallas_call(
        functools.partial(_ffn_ple_proj_kernel, tf=_ff_tile(f, 1536), tn=tn), grid=(t // tm,),
        in_specs=[row(d), _resident((1, d)), _resident((d, f)), _resident((d, f)), _resident((f, d)),
                  _resident((1, d)), pl.BlockSpec((None, tm, e), lambda i: (layer, i, 0)),
                  _resident((e, d)), _resident((d, d)), _resident((1, d)), _resident((d, n))],
        out_specs=[row(d), row(n)],
        out_shape=[jax.ShapeDtypeStruct((t, d), F32), jax.ShapeDtypeStruct((t, n), BF16)],
        compiler_params=_cparams("parallel"), name=name,
    )(h, g_ffn.reshape(1, d), wg, wu, wd, g_ple.reshape(1, d), p, wp, wpg, g_mix.reshape(1, d), w_in)


def _unit_lower_inverses(mats):
    n = mats[0].shape[0]
    row = lax.broadcasted_iota(jnp.int32, (n, n), 0)
    col = lax.broadcasted_iota(jnp.int32, (n, n), 1)
    eye = (row == col).astype(F32)
    same16 = (row // 16) == (col // 16)
    same32 = (row // 32) == (col // 32)
    off32 = jnp.logical_and(same32, jnp.logical_not(same16))

    def mm(ps, qs):
        return [_dot(p.astype(BF16), q.astype(BF16)) for p, q in zip(ps, qs)]

    ad = [jnp.where(same16, a, 0.0) for a in mats]
    a2 = mm(ad, ad)
    x = mm([eye - t for t in ad], [eye + t for t in a2])
    a4 = mm(a2, a2)
    x = mm(x, [eye + t for t in a4])
    a8 = mm(a4, a4)
    x = mm(x, [eye + t for t in a8])
    y = mm([jnp.where(off32, a, 0.0) for a in mats], x)
    x = [t - c for t, c in zip(x, mm(x, y))]
    y = mm([jnp.where(same32, 0.0, a) for a in mats], x)
    return [t - c for t, c in zip(x, mm(x, y))]


def _seq_tiles(pad_ref, s):
    rt = min(ELEM_ROWS, s)
    n_tiles = s // rt

    def rows(i):
        return pl.ds(pl.multiple_of(i * rt, rt), rt)

    def fill_pad(fn):
        pad_ref[0:TOP_PAD, :] = jnp.zeros((TOP_PAD, pad_ref.shape[1]), F32)

        def body(i, c):
            r = pl.multiple_of(i * rt, rt)
            pad_ref[pl.ds(TOP_PAD + r, rt), :] = fn(rows(i))
            return c
        lax.fori_loop(0, n_tiles, body, 0)

    def conv_tile(i, w_ref, width):
        tap = _delayed(pad_ref, pl.multiple_of(i * rt, rt), rt, width - 1)
        acc = None
        for j in range(width):
            term = w_ref[j:j + 1, :] * tap(width - 1 - j)
            acc = term if acc is None else acc + term
        return acc

    return n_tiles, rows, fill_pad, conv_tile


def _gated_conv_kernel(bg_ref, cg_ref, xin_ref, w_ref, o_ref, pad_ref):
    n_tiles, rows, fill_pad, conv_tile = _seq_tiles(pad_ref, bg_ref.shape[0])
    fill_pad(lambda rs: cg_ref[rs, :].astype(F32) * xin_ref[rs, :].astype(F32))

    def body(i, c):
        acc = conv_tile(i, w_ref, CONV_A_WIDTH)
        o_ref[rows(i), :] = (bg_ref[rows(i), :].astype(F32) * acc).astype(o_ref.dtype)
        return c
    lax.fori_loop(0, n_tiles, body, 0)


def _gated_conv(u, conv_a, batch, seq):
    cw = conv_a.shape[1]
    return pl.pallas_call(
        _gated_conv_kernel, grid=(batch,),
        in_specs=[pl.BlockSpec((seq, cw), lambda b, j=j: (b, j)) for j in range(3)] + [_resident(conv_a.shape)],
        out_specs=pl.BlockSpec((seq, cw), lambda b: (b, 0)),
        out_shape=jax.ShapeDtypeStruct((batch * seq, cw), BF16),
        scratch_shapes=[pltpu.VMEM((seq + TOP_PAD, cw), F32)],
        compiler_params=_cparams("parallel"), name="l0_gated_conv")(u, u, u, conv_a)


def _l0_mixer_kernel(q_ref, k_ref, v_ref, og_ref, ab_ref, wq_ref, wk_ref, wv_ref, alog_ref, dtb_ref, gn_ref,
                     o_ref,
                     pad_ref, qs_ref, ks_ref, vs_ref, us_ref, ws_ref, qk_ref, egl_ref, st_ref):
    s = q_ref.shape[0]
    heads = GDN_HEADS
    dk = q_ref.shape[1] // heads
    c64 = GDN_CHUNK
    n_chunks = s // c64
    n_tiles, rows, fill_pad, conv_tile = _seq_tiles(pad_ref, s)

    def l2n(x, scale):
        parts = []
        for h in range(heads):
            xh = x[:, h * dk:(h + 1) * dk]
            inv = lax.rsqrt(jnp.sum(xh * xh, axis=-1, keepdims=True) + EPS)
            parts.append(xh * (inv * scale))
        return jnp.concatenate(parts, axis=1)

    for src_ref, w_ref, dst_ref, post in (
            (q_ref, wq_ref, qs_ref, lambda x: l2n(x, dk ** -0.5)),
            (k_ref, wk_ref, ks_ref, lambda x: l2n(x, 1.0)),
            (v_ref, wv_ref, vs_ref, lambda x: x)):
        fill_pad(lambda rs, src_ref=src_ref: src_ref[rs, :].astype(F32))

        def conv_body(i, c, w_ref=w_ref, dst_ref=dst_ref, post=post):
            dst_ref[rows(i), :] = post(_silu(conv_tile(i, w_ref, GDN_CONV_WIDTH)))
            return c
        lax.fori_loop(0, n_tiles, conv_body, 0)

    ri = lax.broadcasted_iota(jnp.int32, (c64, c64), 0)
    ci = lax.broadcasted_iota(jnp.int32, (c64, c64), 1)
    tril = ri >= ci
    strict = ri > ci
    ltri = tril.astype(F32)

    group = PREP_CHUNKS if n_chunks % PREP_CHUNKS == 0 else 1

    def chunk_prep(cg, carry):
        chunks = []
        for cc in range(group):
            c = cg * group + cc
            rs = pl.ds(pl.multiple_of(c * c64, c64), c64)
            chunks.append((c, rs, ab_ref[rs, :], qs_ref[rs, :], ks_ref[rs, :], vs_ref[rs, :]))
        inst = []
        gcs = [_dot(ltri, -jnp.exp(alog_ref[...]) * _softplus(ab + dtb_ref[...]), precision=HIGHEST)
               for _, _, ab, _, _, _ in chunks]
        for (c, rs, ab, q_all, k_all, v_all), gc in zip(chunks, gcs):
            beta = _sigmoid(ab)
            gct = gc.T
            for h in range(heads):
                hs = slice(h * dk, (h + 1) * dk)
                gcol = gc[:, h:h + 1]
                glast = gc[c64 - 1:c64, h:h + 1]
                bcol = beta[:, heads + h:heads + h + 1]
                decay = jnp.where(tril, jnp.exp(jnp.where(tril, gcol - gct[h:h + 1, :], 0.0)), 0.0)
                kh, qh, vh = k_all[:, hs], q_all[:, hs], v_all[:, hs]
                kb = kh * bcol
                egc = jnp.exp(gcol)
                inst.append(dict(decay=decay, kh=kh, qh=qh, kb=kb, egc=egc,
                                 rhs=jnp.concatenate([vh * bcol, kb * egc], axis=1).astype(BF16),
                                 kd=kh * jnp.exp(glast - gcol),
                                 eg=jnp.broadcast_to(jnp.exp(glast), (SUBLANES, dk))))
        kqs = [_dot_nt(jnp.concatenate([t["kb"], t["qh"]], axis=0).astype(BF16), t["kh"].astype(BF16))
               for t in inst]
        minvs = _unit_lower_inverses([jnp.where(strict, kq[0:c64] * t["decay"], 0.0)
                                      for kq, t in zip(kqs, inst)])
        uws = [_dot(m.astype(BF16), t["rhs"]) for m, t in zip(minvs, inst)]
        for ci, (c, rs, _, _, _, _) in enumerate(chunks):
            sl = slice(ci * heads, (ci + 1) * heads)
            us_ref[rs, :] = jnp.concatenate([uw[:, 0:dk] for uw in uws[sl]], axis=1)
            ws_ref[rs, :] = jnp.concatenate([uw[:, dk:2 * dk] for uw in uws[sl]], axis=1).astype(BF16)
            qs_ref[rs, :] = jnp.concatenate([t["qh"] * t["egc"] for t in inst[sl]], axis=1)
            ks_ref[rs, :] = jnp.concatenate([t["kd"] for t in inst[sl]], axis=1)
            qk_ref[rs, :] = jnp.concatenate([kq[c64:2 * c64] * t["decay"]
                                             for kq, t in zip(kqs[sl], inst[sl])], axis=1).astype(BF16)
            e0 = pl.multiple_of(c * (heads * SUBLANES), heads * SUBLANES)
            egl_ref[pl.ds(e0, heads * SUBLANES), :] = jnp.concatenate([t["eg"] for t in inst[sl]], axis=0)
        return carry
    lax.fori_loop(0, n_chunks // group, chunk_prep, 0)

    st_ref[...] = jnp.zeros(st_ref.shape, F32)

    def scan(c, carry):
        rs = pl.ds(pl.multiple_of(c * c64, c64), c64)
        e0 = pl.multiple_of(c * (heads * SUBLANES), heads * SUBLANES)
        w_all, qg_all, u_all, kd_all = ws_ref[rs, :], qs_ref[rs, :], us_ref[rs, :], ks_ref[rs, :]
        qk_all = qk_ref[rs, :]
        eg_all = egl_ref[pl.ds(e0, heads * SUBLANES), :]
        hsl = [slice(h * dk, (h + 1) * dk) for h in range(heads)]
        sts = [st_ref[h] for h in range(heads)]
        wqs = [_dot(jnp.concatenate([w_all[:, hs], qg_all[:, hs].astype(BF16)], axis=0), st.astype(BF16))
               for hs, st in zip(hsl, sts)]
        vbs = [(u_all[:, hs] - wq[0:c64]).astype(BF16) for hs, wq in zip(hsl, wqs)]
        upd = [_dot_tn(kd_all[:, hs].astype(BF16), vb) for hs, vb in zip(hsl, vbs)]
        intra = [_dot(qk_all[:, h * c64:(h + 1) * c64], vbs[h]) for h in range(heads)]
        for h in range(heads):
            st_ref[h] = sts[h] * eg_all[h * SUBLANES:h * SUBLANES + 1, :] + upd[h]
        vs_ref[rs, :] = jnp.concatenate([wq[c64:2 * c64] + o for wq, o in zip(wqs, intra)], axis=1)
        return carry
    lax.fori_loop(0, n_chunks, scan, 0)

    def finish(i, c):
        o = vs_ref[rows(i), :]
        og = og_ref[rows(i), :].astype(F32)
        parts = []
        for h in range(heads):
            oh = o[:, h * dk:(h + 1) * dk]
            parts.append(_rms(oh, gn_ref[...]))
        y = jnp.concatenate(parts, axis=1) * _silu(og)
        o_ref[rows(i), :] = y.astype(o_ref.dtype)
        return c
    lax.fori_loop(0, n_tiles, finish, 0)


def _l0_mixer(u, ab, gdn_conv, a_log, dt_bias, gdn_norm_g, batch, seq):
    dk = gdn_norm_g.shape[0]
    heads = GDN_HEADS
    cw = heads * dk

    def lane_row(x):
        return jnp.zeros((1, LANES), F32).at[0, :x.shape[0]].set(x)

    in_specs = [pl.BlockSpec((seq, cw), lambda b, j=j: (b, j)) for j in range(3, 7)]
    in_specs += [pl.BlockSpec((seq, LANES), lambda b: (b, 0))]
    in_specs += [pl.BlockSpec((GDN_CONV_WIDTH, cw), lambda b, j=j: (0, j), pipeline_mode=pl.Buffered(1))
                 for j in range(3)]
    in_specs += [_resident((1, LANES)), _resident((1, LANES)), _resident((1, dk))]
    big = pltpu.VMEM((seq, cw), F32)
    scratch = [pltpu.VMEM((seq + TOP_PAD, cw), F32), big, big, big, big, pltpu.VMEM((seq, cw), BF16),
               pltpu.VMEM((seq, heads * GDN_CHUNK), BF16),
               pltpu.VMEM((seq // GDN_CHUNK * heads * SUBLANES, dk), F32),
               pltpu.VMEM((heads, dk, dk), F32)]
    return pl.pallas_call(
        _l0_mixer_kernel, grid=(batch,), in_specs=in_specs,
        out_specs=pl.BlockSpec((seq, cw), lambda b: (b, 0)),
        out_shape=jax.ShapeDtypeStruct((batch * seq, cw), BF16),
        scratch_shapes=scratch, compiler_params=_cparams("parallel"), name="l0_deltanet",
    )(u, u, u, u, ab, gdn_conv, gdn_conv, gdn_conv,
      lane_row(a_log), lane_row(dt_bias), gdn_norm_g.reshape(1, dk))


def _bias_table_kernel(tab_ref, o_ref, *, t):
    h = pl.program_id(0)
    m = pl.program_id(1)
    ri = lax.broadcasted_iota(jnp.int32, (t, t), 0)
    ci = lax.broadcasted_iota(jnp.int32, (t, t), 1)
    rel = m * t + ci - ri
    n = jnp.maximum(rel, 0)
    max_exact = NUM_BUCKETS // 2
    nf = jnp.maximum(n, 1).astype(F32)
    large = max_exact + (jnp.log(nf / max_exact) / math.log(MAX_DISTANCE / max_exact)
                         * (NUM_BUCKETS - max_exact)).astype(jnp.int32)
    large = jnp.minimum(large, NUM_BUCKETS - 1)
    bucket = jnp.where(n < max_exact, n, large)
    bias = jnp.zeros((t, t), F32)
    for b in range(NUM_BUCKETS):
        bias = jnp.where(bucket == b, tab_ref[b, h], bias)
    o_ref[...] = jnp.where(rel >= 0, bias * LOG2E, MASK_VALUE)


def _bias_table(rel_bias, seq, t):
    nb = seq // t
    heads = rel_bias.shape[1]
    return pl.pallas_call(
        functools.partial(_bias_table_kernel, t=t), grid=(heads, nb),
        in_specs=[pl.BlockSpec(memory_space=pltpu.SMEM)],
        out_specs=pl.BlockSpec((None, None, t, t), lambda h, m: (h, m, 0, 0)),
        out_shape=jax.ShapeDtypeStruct((heads, nb, t, t), F32),
        compiler_params=_cparams("parallel", "parallel"), name="rel_bias_table")(rel_bias)


def _attn_kernel(q_ref, k_ref, v_ref, tb_ref, lam_ref, gn_ref, o_ref, m_ref, l_ref, acc_ref,
                 *, t, heads, lambda_init):
    qi = pl.program_id(2)
    dh2 = q_ref.shape[1] // heads
    dh = dh2 // 2
    lane = lax.broadcasted_iota(jnp.int32, (t, dh2), 1)
    qqs = []
    for h in range(heads):
        qf = q_ref[:, h * dh2:(h + 1) * dh2].astype(F32) * (dh ** -0.5 * LOG2E)
        qqs.append(jnp.concatenate([jnp.where(lane < dh, qf, 0.0), jnp.where(lane >= dh, qf, 0.0)],
                                   axis=0).astype(BF16))

    hs = [slice(h * dh2, (h + 1) * dh2) for h in range(heads)]

    def update(j0, nblk):
        ks = pl.ds(pl.multiple_of(j0 * t, t), nblk * t)
        k_all = k_ref[ks, :]
        v_all = v_ref[ks, :]
        s_t = [_dot_nt(k_all[:, hs[h]], qqs[h]) for h in range(heads)]
        ps, alphas = [], []
        for h in range(heads):
            m = m_ref[h, 0:1, :]
            b = jnp.concatenate([tb_ref[h, qi - j0 - i] for i in range(nblk)], axis=0)
            s = jnp.concatenate([s_t[h][:, 0:t] + b, s_t[h][:, t:2 * t] + b], axis=1)
            m_new = jnp.maximum(m, jnp.max(s, axis=0, keepdims=True))
            alpha = jnp.exp2(m - m_new)
            p = jnp.exp2(s - m_new)
            m_ref[h] = jnp.broadcast_to(m_new, m_ref.shape[1:])
            l_ref[h] = jnp.broadcast_to(alpha * l_ref[h, 0:1, :] + jnp.sum(p, axis=0, keepdims=True),
                                        l_ref.shape[1:])
            alphas.append(alpha)
            ps.append(p.astype(BF16))
        pv = [_dot_tn(v_all[:, hs[h]], ps[h]) for h in range(heads)]
        for h in range(heads):
            acc_ref[h] = alphas[h] * acc_ref[h] + pv[h]

    m_ref[...] = jnp.full(m_ref.shape, MASK_VALUE, F32)
    l_ref[...] = jnp.zeros(l_ref.shape, F32)
    acc_ref[...] = jnp.zeros(acc_ref.shape, F32)
    odd = (qi + 1) % 2

    @pl.when(odd == 1)
    def _():
        update(0, 1)

    def pair(i, c):
        update(odd + 2 * i, 2)
        return c
    lax.fori_loop(0, (qi + 1) // 2, pair, 0)
    lp = lam_ref[...]
    lam = (jnp.exp(jnp.sum(lp[0:1] * lp[1:2], axis=-1, keepdims=True))
           - jnp.exp(jnp.sum(lp[2:3] * lp[3:4], axis=-1, keepdims=True)) + lambda_init)
    for h in range(heads):
        o12 = acc_ref[h] / l_ref[h, 0:1, :]
        o = (o12[:, 0:t] - lam * o12[:, t:2 * t]).T
        o_ref[:, h * dh2:(h + 1) * dh2] = (_rms(o, gn_ref[...]) * (1.0 - lambda_init)).astype(o_ref.dtype)


def _diff_attention(u, table, lam_params, norm_g, batch, seq, lambda_init):
    heads = DIFF_HEADS
    dh2 = norm_g.shape[0]
    hg = ATTN_HEADS_PER_STEP
    w = hg * dh2
    ng = heads // hg
    t = table.shape[2]
    nq = seq // t
    return pl.pallas_call(
        functools.partial(_attn_kernel, t=t, heads=hg, lambda_init=lambda_init),
        grid=(batch, ng, nq),
        in_specs=[pl.BlockSpec((t, w), lambda b, g, i: (b * nq + i, g)),
                  pl.BlockSpec((seq, w), lambda b, g, i: (b, ng + g)),
                  pl.BlockSpec((seq, w), lambda b, g, i: (b, 2 * ng + g)),
                  pl.BlockSpec((hg, nq, t, t), lambda b, g, i: (g, 0, 0, 0)),
                  _resident(lam_params.shape), _resident((1, dh2))],
        out_specs=pl.BlockSpec((t, w), lambda b, g, i: (b * nq + i, g)),
        out_shape=jax.ShapeDtypeStruct((batch * seq, heads * dh2), BF16),
        scratch_shapes=[pltpu.VMEM((hg, SUBLANES, 2 * t), F32), pltpu.VMEM((hg, SUBLANES, 2 * t), F32),
                        pltpu.VMEM((hg, dh2, 2 * t), F32)],
        compiler_params=_cparams("parallel", "parallel", "arbitrary"), name="diff_attention",
    )(u, u, u, table, lam_params, norm_g.reshape(1, dh2))


def _conformer_kernel(ga_ref, gb_ref, w_ref, b_ref, lg_ref, lb_ref, o_ref, pad_ref):
    s, c = ga_ref.shape
    rt = min(ELEM_ROWS, s)
    pad_ref[0:TOP_PAD, :] = jnp.zeros((TOP_PAD, c), F32)

    def glu(i, carry):
        r = pl.multiple_of(i * rt, rt)
        rs = pl.ds(r, rt)
        pad_ref[pl.ds(TOP_PAD + r, rt), :] = ga_ref[rs, :].astype(F32) * _sigmoid(gb_ref[rs, :].astype(F32))
        return carry
    lax.fori_loop(0, s // rt, glu, 0)

    ct = CONF_ROWS

    def conv(i, carry):
        r = pl.multiple_of(i * ct, ct)
        tap = _delayed(pad_ref, r, ct, CONF_WIDTH - 1)
        acc = jnp.broadcast_to(b_ref[...], (ct, c))
        for j in range(CONF_WIDTH):
            acc = acc + w_ref[j:j + 1, :] * tap(CONF_WIDTH - 1 - j)
        mu = jnp.mean(acc, axis=-1, keepdims=True)
        xc = acc - mu
        var = jnp.mean(xc * xc, axis=-1, keepdims=True)
        y = xc * lax.rsqrt(var + EPS) * lg_ref[...] + lb_ref[...]
        o_ref[pl.ds(r, ct), :] = _silu(y).astype(o_ref.dtype)
        return carry
    lax.fori_loop(0, s // ct, conv, 0)


def _conformer(u, w, b, ln_g, ln_b, batch, seq, col0):
    c = w.shape[1]
    return pl.pallas_call(
        _conformer_kernel, grid=(batch,),
        in_specs=[pl.BlockSpec((seq, c), lambda i: (i, col0)),
                  pl.BlockSpec((seq, c), lambda i: (i, col0 + 1)),
                  _resident(w.shape), _resident((1, c)), _resident((1, c)), _resident((1, c))],
        out_specs=pl.BlockSpec((seq, c), lambda i: (i, 0)),
        out_shape=jax.ShapeDtypeStruct((batch * seq, c), BF16),
        scratch_shapes=[pltpu.VMEM((seq + TOP_PAD, c), F32)],
        compiler_params=_cparams("parallel"), name="conformer_conv",
    )(u, u, w, b.reshape(1, c), ln_g.reshape(1, c), ln_b.reshape(1, c))


def _proj_router_kernel(y1_ref, y2_ref, w1_ref, w2_ref, h_ref, g_ref, wrt_ref, upper_ref,
                        h_out_ref, hn_ref, route_ref, gate_ref, cnt_ref, carry_ref):
    i = pl.program_id(0)
    tm = h_ref.shape[0]
    n_e = N_EXPERTS

    @pl.when(i == 0)
    def _():
        carry_ref[...] = jnp.zeros(carry_ref.shape, F32)

    h = h_ref[...] + _dot(y1_ref[...], w1_ref[...]) + _dot(y2_ref[...], w2_ref[...])
    h_out_ref[...] = h
    xn = _rms(h, g_ref[...])
    _store_token_tiles(hn_ref, 0, xn)

    logits = _dot_nt(wrt_ref[...], xn.astype(BF16))[0:n_e, :]
    sub = lax.broadcasted_iota(jnp.int32, logits.shape, 0)
    m1 = jnp.max(logits, axis=0, keepdims=True)
    i1 = jnp.min(jnp.where(logits == m1, sub, n_e), axis=0, keepdims=True)
    rest = jnp.where(sub == i1, -jnp.inf, logits)
    m2 = jnp.max(rest, axis=0, keepdims=True)
    i2 = jnp.min(jnp.where(rest == m2, sub, n_e), axis=0, keepdims=True)
    e = jnp.exp(m2 - m1)
    g1 = 1.0 / (1.0 + e)
    oh1 = sub == i1
    oh2 = sub == i2
    both = oh1.astype(F32) + oh2.astype(F32)
    csum = _dot(both.astype(BF16), upper_ref[...])
    carry = carry_ref[:, 0:1]
    before = csum - both + carry
    total = carry + csum[:, tm - 1:tm]
    carry_ref[...] = jnp.broadcast_to(total, carry_ref.shape)
    cnt_ref[...] = jnp.broadcast_to(total, cnt_ref.shape).astype(jnp.int32)
    r1 = jnp.sum(jnp.where(oh1, before, 0.0), axis=0, keepdims=True).astype(jnp.int32)
    r2 = jnp.sum(jnp.where(oh2, before, 0.0), axis=0, keepdims=True).astype(jnp.int32)
    route_ref[...] = jnp.where(sub == 0, i1, jnp.where(sub == 1, i2, jnp.where(sub == 2, r1,
                               jnp.where(sub == 3, r2, 0))))
    grow = jnp.where(sub == 0, g1, jnp.where(sub == 1, e * g1, 0.0))
    sel = (lax.broadcasted_iota(jnp.int32, (n_e, LANES), 0)
           == lax.broadcasted_iota(jnp.int32, (n_e, LANES), 1)).astype(F32)
    gate_ref[...] = lax.dot_general(grow, sel, (((0,), (0,)), ((), ())), precision=HIGHEST,
                                    preferred_element_type=F32)


def _proj_router(y1, y2, w, h, g, wr):
    t, d = h.shape
    k1, k2 = y1.shape[1], y2.shape[1]
    tm = min(ROW_TILE, t)
    wrt = jnp.zeros((LANES, d), BF16).at[:N_EXPERTS].set(wr.T.astype(BF16))
    upper = (jnp.arange(tm)[:, None] <= jnp.arange(tm)[None, :]).astype(BF16)
    row = lambda width: pl.BlockSpec((tm, width), lambda i: (i, 0))
    return pl.pallas_call(
        _proj_router_kernel, grid=(t // tm,),
        in_specs=[row(k1), row(k2), _resident((k1, d)), _resident((k2, d)), row(d),
                  _resident((1, d)), _resident((LANES, d)), _resident((tm, tm))],
        out_specs=[row(d), pl.BlockSpec((tm * d // LANES, LANES), lambda i: (i, 0)),
                   pl.BlockSpec((N_EXPERTS, tm), lambda i: (0, i)), row(LANES),
                   pl.BlockSpec((N_EXPERTS, LANES), lambda i: (0, 0))],
        out_shape=[jax.ShapeDtypeStruct((t, d), F32), jax.ShapeDtypeStruct((t * d // LANES, LANES), F32),
                   jax.ShapeDtypeStruct((N_EXPERTS, t), jnp.int32), jax.ShapeDtypeStruct((t, LANES), F32),
                   jax.ShapeDtypeStruct((N_EXPERTS, LANES), jnp.int32)],
        scratch_shapes=[pltpu.VMEM((N_EXPERTS, LANES), F32)],
        compiler_params=_cparams("arbitrary"), name="l1_out_proj_router",
    )(y1, y2, w[:k1], w[k1:], h, g.reshape(1, d), wrt, upper)


def _dispatch_kernel(dest_hbm, pad_hbm, x_ref, xs_hbm, idx_smem, pad_smem, zero_ref, sem_idx, sem_x):
    i = pl.program_id(0)
    n = x_ref.shape[0] // SUBLANES
    m = TOP_K * n
    cp = pltpu.make_async_copy(dest_hbm.at[i], idx_smem, sem_idx)
    cp.start()
    cp.wait()

    def body(r, c):
        for k in range(TOP_K):
            _token_tile_copy(x_ref, r, xs_hbm, idx_smem[k * n + r], sem_x).start(priority=k % 2)
        return c
    lax.fori_loop(0, n, body, 0, unroll=8)
    for _ in range(TOP_K):
        pltpu.make_async_copy(x_ref, xs_hbm.at[pl.ds(0, n * SUBLANES)], sem_x).wait()

    @pl.when(i == pl.num_programs(0) - 1)
    def _():
        n_pad = pad_smem.shape[0]
        cp = pltpu.make_async_copy(pad_hbm, pad_smem, sem_idx)
        cp.start()
        cp.wait()
        zero_ref[...] = jnp.zeros(zero_ref.shape, F32)

        def fill(j, c):
            for q in range(2):
                _token_tile_copy(zero_ref, 0, xs_hbm, pad_smem[2 * j + q], sem_x).start(priority=q)
            return c
        lax.fori_loop(0, n_pad // 2, fill, 0, unroll=8)
        pltpu.make_async_copy(xs_hbm.at[pl.ds(0, n_pad * SUBLANES)], xs_hbm.at[pl.ds(0, n_pad * SUBLANES)],
                              sem_x).wait()


def _dispatch(hn_tiles, dest, pad_rows, t, d):
    n = min(DISPATCH_ROWS, t)
    steps = t // n
    tpt = d // LANES
    p = t * TOP_K + N_EXPERTS * MOE_ROWS
    dest_steps = jnp.concatenate([dest[k].reshape(steps, n) for k in range(TOP_K)], axis=1)
    return pl.pallas_call(
        _dispatch_kernel, grid=(steps,),
        in_specs=[pl.BlockSpec(memory_space=pl.ANY), pl.BlockSpec(memory_space=pl.ANY),
                  pl.BlockSpec((n * tpt, LANES), lambda i: (i, 0))],
        out_specs=pl.BlockSpec(memory_space=pl.ANY),
        out_shape=jax.ShapeDtypeStruct((p * tpt, LANES), F32),
        scratch_shapes=[pltpu.SMEM((TOP_K * n,), jnp.int32), pltpu.SMEM(pad_rows.shape, jnp.int32),
                        pltpu.VMEM((tpt, LANES), F32), pltpu.SemaphoreType.DMA, pltpu.SemaphoreType.DMA],
        compiler_params=_cparams("arbitrary"), name="moe_dispatch")(dest_steps, pad_rows, hn_tiles)


def _dispatch_plan(route, counts, t):
    tm = MOE_ROWS
    p = t * TOP_K + N_EXPERTS * tm
    padded = (counts + tm - 1) // tm * tm
    pend = jnp.cumsum(padded)
    pstart = pend - padded
    experts = jnp.arange(N_EXPERTS, dtype=jnp.int32)[:, None]
    dest = jnp.stack([jnp.sum(jnp.where(route[k][None, :] == experts, pstart[:, None], 0), axis=0)
                      + route[TOP_K + k] for k in range(TOP_K)])
    n_pad = p - t * TOP_K
    gaps = jnp.concatenate([padded - counts, (p - pend[-1])[None]])
    gap_end = jnp.cumsum(gaps)
    gap_row0 = jnp.concatenate([pstart + counts, pend[-1:]])
    j = jnp.arange(n_pad, dtype=jnp.int32)
    which = jnp.searchsorted(gap_end, j, side='right', method='compare_all')
    pad_rows = (gap_row0[which] + j - (gap_end - gaps)[which]).astype(jnp.int32)
    nb = p // tm
    blk_e = jnp.minimum(jnp.searchsorted(pend, jnp.arange(nb, dtype=jnp.int32) * tm, side='right',
                                         method='compare_all'), N_EXPERTS - 1).astype(jnp.int32)
    return dest.astype(jnp.int32), pad_rows, blk_e, (pend[-1:] // tm).astype(jnp.int32)


def _experts_kernel(blk_e_ref, used_ref, x_ref, wgu_ref, wd_ref, o_ref, hid_ref):
    f = wd_ref.shape[0]
    tm = hid_ref.shape[0]
    live = pl.program_id(0) < used_ref[0]

    @pl.when(live)
    def _():
        x = _load_token_tiles(x_ref, 0, tm).astype(BF16)
        for c in range(0, f, MOE_SUB_TILE):
            hid_ref[:, c:c + MOE_SUB_TILE] = (
                _silu(_dot(x, wgu_ref[:, c:c + MOE_SUB_TILE]))
                * _dot(x, wgu_ref[:, f + c:f + c + MOE_SUB_TILE])).astype(BF16)
        _store_token_tiles(o_ref, 0, _dot(hid_ref[...], wd_ref[...]))

    @pl.when(jnp.logical_not(live))
    def _():
        o_ref[...] = jnp.zeros(o_ref.shape, F32)


def _experts(xs, blk_e, n_used, w_gate_up, w_down):
    f, d = w_down.shape[1], w_down.shape[2]
    tm = MOE_ROWS
    tpt = d // LANES
    assert f % MOE_SUB_TILE == 0
    grid_spec = pltpu.PrefetchScalarGridSpec(
        num_scalar_prefetch=2, grid=(blk_e.shape[0],),
        in_specs=[pl.BlockSpec((tm * tpt, LANES), lambda i, e, u: (i, 0)),
                  pl.BlockSpec((None, d, 2 * f), lambda i, e, u: (e[i], 0, 0), pipeline_mode=pl.Buffered(1)),
                  pl.BlockSpec((None, f, d), lambda i, e, u: (e[i], 0, 0), pipeline_mode=pl.Buffered(1))],
        out_specs=pl.BlockSpec((tm * tpt, LANES), lambda i, e, u: (i, 0)),
        scratch_shapes=[pltpu.VMEM((tm, f), BF16)])
    return pl.pallas_call(
        _experts_kernel, grid_spec=grid_spec, out_shape=jax.ShapeDtypeStruct(xs.shape, F32),
        compiler_params=_cparams("parallel"), name="moe_experts",
    )(blk_e, n_used, xs, w_gate_up, w_down)


def _combine_kernel(dest_hbm, y_hbm, gate_ref, h_ref, g_ref, p_ref, wp_ref, wg_ref, fg_ref, o_ref,
                    idx0, idx1, ybuf, sem_idx, sem_rows):
    i = pl.program_id(0)
    steps = pl.num_programs(0)
    n = h_ref.shape[0]
    m = TOP_K * n
    slot = i % 2
    idxs = (idx0, idx1)

    def idx_copy(blk, s):
        return pltpu.make_async_copy(dest_hbm.at[blk], idxs[s], sem_idx.at[s])

    def issue_tiles(s):
        def body(r, c):
            for k in range(TOP_K):
                _token_tile_copy(y_hbm, idxs[s][k * n + r], ybuf, s * m + k * n + r,
                                 sem_rows.at[s]).start(priority=k % 2)
            return c
        lax.fori_loop(0, n, body, 0, unroll=8)

    @pl.when(i == 0)
    def _():
        idx_copy(0, 0).start()
        idx_copy(0, 0).wait()
        idx_copy(1, 1).start()
        issue_tiles(0)

    for s in range(2):
        @pl.when(slot == s)
        def _(s=s):
            idx_copy(i + 1, 1 - s).wait()

            @pl.when(i + 2 <= steps)
            def _():
                idx_copy(i + 2, s).start()

            issue_tiles(1 - s)
            _token_tiles_wait(y_hbm, ybuf, s * m, m, sem_rows.at[s])

    nc = n // COMBINE_CHUNKS
    rows = [pl.ds(c * nc, nc) for c in range(COMBINE_CHUNKS)]
    xs = []
    for c, rs in enumerate(rows):
        gates = gate_ref[rs, :]
        xs.append(h_ref[rs, :] + gates[:, 0:1] * _load_token_tiles(ybuf, slot * m + c * nc, nc)
                  + gates[:, 1:2] * _load_token_tiles(ybuf, slot * m + n + c * nc, nc))
    xns = [_rms(x, g_ref[...]).astype(BF16) for x in xs]
    gate = [_sigmoid(_dot(xn, wg_ref[...])) for xn in xns]
    emb = [_dot(p_ref[rs, :].astype(BF16), wp_ref[...]) for rs in rows]
    for rs, x, e, g in zip(rows, xs, emb, gate):
        o_ref[rs, :] = _rms(x + e * g, fg_ref[...])

    @pl.when(i == steps - 1)
    def _():
        _token_tiles_wait(y_hbm, ybuf, (1 - slot) * m, m, sem_rows.at[1 - slot])


def _combine_ple_final(dest, y_tiles, gates, h, g, p, layer, wp, wg, final_g):
    t, d = h.shape
    e = p.shape[2]
    n = min(COMBINE_ROWS, t)
    m = TOP_K * n
    steps = t // n
    dest_steps = jnp.concatenate([dest[k].reshape(steps, n) for k in range(TOP_K)], axis=1)
    dest_steps = jnp.concatenate([dest_steps, jnp.zeros((1, m), jnp.int32)], axis=0)
    return pl.pallas_call(
        _combine_kernel, grid=(steps,),
        in_specs=[pl.BlockSpec(memory_space=pl.ANY), pl.BlockSpec(memory_space=pl.ANY),
                  pl.BlockSpec((n, LANES), lambda i: (i, 0)),
                  pl.BlockSpec((n, d), lambda i: (i, 0)), _resident((1, d)),
                  pl.BlockSpec((None, n, e), lambda i: (layer, i, 0)), _resident((e, d)), _resident((d, d)),
                  _resident((1, d))],
        out_specs=pl.BlockSpec((n, d), lambda i: (i, 0)),
        out_shape=jax.ShapeDtypeStruct((t, d), F32),
        scratch_shapes=[pltpu.SMEM((m,), jnp.int32), pltpu.SMEM((m,), jnp.int32),
                        pltpu.VMEM((2 * m * d // LANES, LANES), F32),
                        pltpu.SemaphoreType.DMA((2,)), pltpu.SemaphoreType.DMA((2,))],
        compiler_params=_cparams("arbitrary"), name="moe_combine_ple_final",
    )(dest_steps, y_tiles, gates, h, g.reshape(1, d), p, wp, wg, final_g.reshape(1, d))


def kernel(x, p, norm_mix_g, norm_ffn_g, norm_ple_g, final_norm_g, ev_w_in, ev_conv_a, ev_gdn_conv, ev_gdn_A_log, ev_gdn_dt_bias, ev_gdn_norm_g, ev_w_out, od_w_in, od_lambda, od_diff_norm_g, od_conf_dw_w, od_conf_dw_b, od_conf_ln_g, od_conf_ln_b, od_w_out, rel_bias, ffn_w_gate_up, ffn_w_down, moe_router, moe_w_gate_up, moe_w_down, ple_w_proj, ple_w_gate):
    batch, seq, d = x.shape
    t = batch * seq
    depth = p.shape[0]
    assert depth == 2 and seq % GDN_CHUNK == 0
    h = x.reshape(t, d)
    pf = p.reshape(depth, t, p.shape[-1])

    heads = GDN_HEADS
    n_main = ev_w_in.shape[2] - 2 * heads
    w_in = ev_w_in[0]
    w_ab = jnp.zeros((d, LANES), BF16).at[:, :2 * heads].set(w_in[:, n_main:].astype(BF16))
    u, ab = _norm_proj(h, norm_mix_g[0], w_in[:, :n_main].astype(BF16), w_ab, name="l0_in_proj")
    ya = _gated_conv(u, ev_conv_a[0], batch, seq)
    yb = _l0_mixer(u, ab, ev_gdn_conv[0], ev_gdn_A_log[0], ev_gdn_dt_bias[0], ev_gdn_norm_g[0], batch, seq)
    h = _proj_residual([ya, yb], ev_w_out[0].astype(BF16), h, name="l0_out_proj")
    f = ffn_w_down.shape[1]
    lambda_init = 0.8 - 0.6 * math.exp(-0.3 * 1)
    h, u = _ffn_ple_proj(h, norm_ffn_g[0], ffn_w_gate_up[0, :, :f].astype(BF16),
                         ffn_w_gate_up[0, :, f:].astype(BF16), ffn_w_down[0].astype(BF16),
                         norm_ple_g[0], pf, 0, ple_w_proj[0].astype(BF16), ple_w_gate[0].astype(BF16),
                         norm_mix_g[1], od_w_in[0].astype(BF16), name="l0_ffn_ple_l1_in_proj")

    table = _bias_table(rel_bias, seq, min(ATTN_TILE, seq))
    o_attn = _diff_attention(u, table, od_lambda[0], od_diff_norm_g[0], batch, seq, lambda_init)
    c_conf = od_conf_dw_w.shape[2]
    o_conf = _conformer(u, od_conf_dw_w[0], od_conf_dw_b[0], od_conf_ln_g[0], od_conf_ln_b[0],
                        batch, seq, 3 * DIFF_HEADS * od_diff_norm_g.shape[1] // c_conf)
    h, hn, route, gates, counts = _proj_router(o_attn, o_conf, od_w_out[0].astype(BF16), h, norm_ffn_g[1],
                                               moe_router[0])
    dest, pad_rows, blk_e, n_used = _dispatch_plan(route, counts[:, 0], t)
    xs = _dispatch(hn, dest, pad_rows, t, d)
    y = _experts(xs, blk_e, n_used, moe_w_gate_up[0].astype(BF16), moe_w_down[0].astype(BF16))
    out = _combine_ple_final(dest, y, gates, h, norm_ple_g[1], pf, 1, ple_w_proj[1].astype(BF16),
                             ple_w_gate[1].astype(BF16), final_norm_g)
    return out.reshape(batch, seq, d)
```

```python
import functools
import math

import jax
import jax.numpy as jnp
from jax import lax
from jax.experimental import pallas as pl
from jax.experimental.pallas import tpu as pltpu

F32 = jnp.float32
BF16 = jnp.bfloat16
HIGHEST = lax.Precision.HIGHEST

EPS = 1e-6
CONV_A_WIDTH = 3
GDN_HEADS = 4
GDN_CONV_WIDTH = 4
GDN_CHUNK = 64
DIFF_HEADS = 4
NUM_BUCKETS = 32
MAX_DISTANCE = 128
CONF_WIDTH = 31
N_EXPERTS = 8
TOP_K = 2

LANES = 128
SUBLANES = 8
VMEM_LIMIT_BYTES = 56 * 1024 * 1024
MASK_VALUE = -1e30
LOG2E = math.log2(math.e)

ROW_TILE = 512
PROJ_ROWS = 1024
COL_TILE = 512
FFN_CHUNKS = 2
ELEM_ROWS = 256
CONF_ROWS = 256
PREP_CHUNKS = 8
ATTN_TILE = 256
ATTN_HEADS_PER_STEP = 4
MOE_ROWS = 1024
DISPATCH_ROWS = 4096
MOE_SUB_TILE = 256
COMBINE_ROWS = 512
COMBINE_CHUNKS = 4
TOP_PAD = 32


def _cparams(*sem):
    return pltpu.CompilerParams(dimension_semantics=sem, vmem_limit_bytes=VMEM_LIMIT_BYTES)


def _resident(shape):
    nd = len(shape)
    return pl.BlockSpec(shape, lambda *_: (0,) * nd, pipeline_mode=pl.Buffered(1))


def _rms(x, g):
    return x * lax.rsqrt(jnp.mean(x * x, axis=-1, keepdims=True) + EPS) * g


def _sigmoid(x):
    return jax.nn.sigmoid(x)


def _silu(x):
    return x * jax.nn.sigmoid(x)


def _softplus(x):
    return jnp.maximum(x, 0.0) + jnp.log1p(jnp.exp(-jnp.abs(x)))


def _dot(a, b, **kw):
    return jnp.dot(a, b, preferred_element_type=F32, **kw)


def _dot_nt(a, b):
    return lax.dot_general(a, b, (((1,), (1,)), ((), ())), preferred_element_type=F32)


def _dot_tn(a, b):
    return lax.dot_general(a, b, (((0,), (0,)), ((), ())), preferred_element_type=F32)


def _delayed(pad_ref, r, rows, max_delay):
    lead = -(-max_delay // SUBLANES) * SUBLANES
    win = pad_ref[pl.ds(TOP_PAD + r - lead, rows + lead), :]
    rolled = {0: win}

    def tap(d):
        a, b = divmod(d, SUBLANES)
        if b not in rolled:
            rolled[b] = pltpu.roll(win, b, 0)
        start = lead - SUBLANES * a
        return rolled[b][start:start + rows, :]
    return tap


def _load_token_tiles(ref, first_token, n):
    return jnp.concatenate([ref[pl.ds(first_token * SUBLANES + s, n, stride=SUBLANES), :]
                            for s in range(SUBLANES)], axis=1)


def _store_token_tiles(ref, first_token, x):
    n = x.shape[0]
    for s in range(SUBLANES):
        ref[pl.ds(first_token * SUBLANES + s, n, stride=SUBLANES), :] = x[:, s * LANES:(s + 1) * LANES]


def _token_tile_copy(src_hbm, src_token, dst_vmem, dst_token, sem):
    return pltpu.make_async_copy(src_hbm.at[pl.ds(pl.multiple_of(src_token * SUBLANES, SUBLANES), SUBLANES)],
                                 dst_vmem.at[pl.ds(pl.multiple_of(dst_token * SUBLANES, SUBLANES), SUBLANES)],
                                 sem)


def _token_tiles_wait(src_hbm, dst_vmem, first_token, n, sem):
    pltpu.make_async_copy(src_hbm.at[pl.ds(0, n * SUBLANES)],
                          dst_vmem.at[pl.ds(pl.multiple_of(first_token * SUBLANES, SUBLANES), n * SUBLANES)],
                          sem).wait()


def _norm_proj_kernel(h_ref, g_ref, w_ref, *rest, tn, with_aux):
    xn = _rms(h_ref[...], g_ref[...]).astype(BF16)
    if with_aux:
        w2_ref, o_ref, o2_ref = rest
        o2_ref[...] = _dot(xn, w2_ref[...])
    else:
        (o_ref,) = rest
    n = w_ref.shape[1]
    for c in range(0, n, tn):
        o_ref[:, c:c + tn] = _dot(xn, w_ref[:, c:c + tn]).astype(o_ref.dtype)


def _norm_proj(h, g, w, w_aux=None, *, name):
    t, d = h.shape
    n = w.shape[1]
    tm = min(PROJ_ROWS, t)
    tn = COL_TILE if n % COL_TILE == 0 else n
    in_specs = [pl.BlockSpec((tm, d), lambda i: (i, 0)), _resident((1, d)), _resident((d, n))]
    out_shape = [jax.ShapeDtypeStruct((t, n), BF16)]
    out_specs = [pl.BlockSpec((tm, n), lambda i: (i, 0))]
    args = [h, g.reshape(1, d), w]
    if w_aux is not None:
        in_specs.append(_resident(w_aux.shape))
        out_shape.append(jax.ShapeDtypeStruct((t, w_aux.shape[1]), F32))
        out_specs.append(pl.BlockSpec((tm, w_aux.shape[1]), lambda i: (i, 0)))
        args.append(w_aux)
    out = pl.pallas_call(
        functools.partial(_norm_proj_kernel, tn=tn, with_aux=w_aux is not None),
        grid=(t // tm,), in_specs=in_specs, out_specs=out_specs, out_shape=out_shape,
        compiler_params=_cparams("parallel"), name=name)(*args)
    return out if w_aux is not None else out[0]


def _proj_residual_kernel(*refs):
    n = (len(refs) - 2) // 2
    y_refs, w_refs, h_ref, o_ref = refs[:n], refs[n:2 * n], refs[2 * n], refs[2 * n + 1]
    acc = h_ref[...]
    for y_ref, w_ref in zip(y_refs, w_refs):
        acc = acc + _dot(y_ref[...], w_ref[...])
    o_ref[...] = acc


def _proj_residual(ys, w, h, *, name):
    t, d = h.shape
    tm = min(PROJ_ROWS, t)
    ws, r0 = [], 0
    for y in ys:
        ws.append(w[r0:r0 + y.shape[1]])
        r0 += y.shape[1]
    return pl.pallas_call(
        _proj_residual_kernel, grid=(t // tm,),
        in_specs=([pl.BlockSpec((tm, y.shape[1]), lambda i: (i, 0)) for y in ys]
                  + [_resident(wi.shape) for wi in ws] + [pl.BlockSpec((tm, d), lambda i: (i, 0))]),
        out_specs=pl.BlockSpec((tm, d), lambda i: (i, 0)),
        out_shape=jax.ShapeDtypeStruct((t, d), F32),
        compiler_params=_cparams("parallel"), name=name)(*ys, *ws, h)


def _ffn_ple_proj_kernel(h_ref, gf_ref, wg_ref, wu_ref, wd_ref, gp_ref, p_ref, wp_ref, wpg_ref,
                         gm_ref, win_ref, h_out_ref, u_ref, *, tf, tn):
    tm = h_ref.shape[0]
    nc = tm // FFN_CHUNKS
    rows = [pl.ds(c * nc, nc) for c in range(FFN_CHUNKS)]
    xs = [h_ref[rs, :] for rs in rows]
    xns = [_rms(x, gf_ref[...]).astype(BF16) for x in xs]
    accs = list(xs)
    for c in range(0, wg_ref.shape[1], tf):
        gates = [_dot(xn, wg_ref[:, c:c + tf]) for xn in xns]
        ups = [_dot(xn, wu_ref[:, c:c + tf]) for xn in xns]
        hids = [(_silu(g) * u).astype(BF16) for g, u in zip(gates, ups)]
        accs = [a + _dot(hd, wd_ref[c:c + tf, :]) for a, hd in zip(accs, hids)]
    xn2 = [_rms(x, gp_ref[...]).astype(BF16) for x in accs]
    pgate = [_sigmoid(_dot(xn, wpg_ref[...])) for xn in xn2]
    emb = [_dot(p_ref[rs, :].astype(BF16), wp_ref[...]) for rs in rows]
    h3 = [x + e * g for x, e, g in zip(accs, emb, pgate)]
    for rs, x in zip(rows, h3):
        h_out_ref[rs, :] = x
    xn3 = [_rms(x, gm_ref[...]).astype(BF16) for x in h3]
    for c in range(0, win_ref.shape[1], tn):
        for rs, xn in zip(rows, xn3):
            u_ref[rs, c:c + tn] = _dot(xn, win_ref[:, c:c + tn]).astype(u_ref.dtype)


def _ff_tile(f, cap):
    best = LANES
    for c in range(LANES, cap + 1, LANES):
        if f % c == 0:
            best = c
    return best


def _ffn_ple_proj(h, g_ffn, wg, wu, wd, g_ple, p, layer, wp, wpg, g_mix, w_in, *, name):
    t, d = h.shape
    f = wg.shape[1]
    e = p.shape[2]
    n = w_in.shape[1]
    tm = min(ROW_TILE, t)
    tn = COL_TILE if n % COL_TILE == 0 else n
    row = lambda w: pl.BlockSpec((tm, w), lambda i: (i, 0))
    return pl.pallas_call(
        functools.partial(_ffn_ple_proj_kernel, tf=_ff_tile(f, 1536), tn=tn), grid=(t // tm,),
        in_specs=[row(d), _resident((1, d)), _resident((d, f)), _resident((d, f)), _resident((f, d)),
                  _resident((1, d)), pl.BlockSpec((None, tm, e), lambda i: (layer, i, 0)),
                  _resident((e, d)), _resident((d, d)), _resident((1, d)), _resident((d, n))],
        out_specs=[row(d), row(n)],
        out_shape=[jax.ShapeDtypeStruct((t, d), F32), jax.ShapeDtypeStruct((t, n), BF16)],
        compiler_params=_cparams("parallel"), name=name,
    )(h, g_ffn.reshape(1, d), wg, wu, wd, g_ple.reshape(1, d), p, wp, wpg, g_mix.reshape(1, d), w_in)


def _unit_lower_inverses(mats):
    n = mats[0].shape[0]
    row = lax.broadcasted_iota(jnp.int32, (n, n), 0)
    col = lax.broadcasted_iota(jnp.int32, (n, n), 1)
    eye = (row == col).astype(F32)
    same16 = (row // 16) == (col // 16)
    same32 = (row // 32) == (col // 32)
    off32 = jnp.logical_and(same32, jnp.logical_not(same16))

    def mm(ps, qs):
        return [_dot(p.astype(BF16), q.astype(BF16)) for p, q in zip(ps, qs)]

    ad = [jnp.where(same16, a, 0.0) for a in mats]
    a2 = mm(ad, ad)
    x = mm([eye - t for t in ad], [eye + t for t in a2])
    a4 = mm(a2, a2)
    x = mm(x, [eye + t for t in a4])
    a8 = mm(a4, a4)
    x = mm(x, [eye + t for t in a8])
    y = mm([jnp.where(off32, a, 0.0) for a in mats], x)
    x = [t - c for t, c in zip(x, mm(x, y))]
    y = mm([jnp.where(same32, 0.0, a) for a in mats], x)
    return [t - c for t, c in zip(x, mm(x, y))]


def _seq_tiles(pad_ref, s):
    rt = min(ELEM_ROWS, s)
    n_tiles = s // rt

    def rows(i):
        return pl.ds(pl.multiple_of(i * rt, rt), rt)

    def fill_pad(fn):
        pad_ref[0:TOP_PAD, :] = jnp.zeros((TOP_PAD, pad_ref.shape[1]), F32)

        def body(i, c):
            r = pl.multiple_of(i * rt, rt)
            pad_ref[pl.ds(TOP_PAD + r, rt), :] = fn(rows(i))
            return c
        lax.fori_loop(0, n_tiles, body, 0)

    def conv_tile(i, w_ref, width):
        tap = _delayed(pad_ref, pl.multiple_of(i * rt, rt), rt, width - 1)
        acc = None
        for j in range(width):
            term = w_ref[j:j + 1, :] * tap(width - 1 - j)
            acc = term if acc is None else acc + term
        return acc

    return n_tiles, rows, fill_pad, conv_tile


def _gated_conv_kernel(bg_ref, cg_ref, xin_ref, w_ref, o_ref, pad_ref):
    n_tiles, rows, fill_pad, conv_tile = _seq_tiles(pad_ref, bg_ref.shape[0])
    fill_pad(lambda rs: cg_ref[rs, :].astype(F32) * xin_ref[rs, :].astype(F32))

    def body(i, c):
        acc = conv_tile(i, w_ref, CONV_A_WIDTH)
        o_ref[rows(i), :] = (bg_ref[rows(i), :].astype(F32) * acc).astype(o_ref.dtype)
        return c
    lax.fori_loop(0, n_tiles, body, 0)


def _gated_conv(u, conv_a, batch, seq):
    cw = conv_a.shape[1]
    return pl.pallas_call(
        _gated_conv_kernel, grid=(batch,),
        in_specs=[pl.BlockSpec((seq, cw), lambda b, j=j: (b, j)) for j in range(3)] + [_resident(conv_a.shape)],
        out_specs=pl.BlockSpec((seq, cw), lambda b: (b, 0)),
        out_shape=jax.ShapeDtypeStruct((batch * seq, cw), BF16),
        scratch_shapes=[pltpu.VMEM((seq + TOP_PAD, cw), F32)],
        compiler_params=_cparams("parallel"), name="l0_gated_conv")(u, u, u, conv_a)


def _l0_mixer_kernel(q_ref, k_ref, v_ref, og_ref, ab_ref, wq_ref, wk_ref, wv_ref, alog_ref, dtb_ref, gn_ref,
                     o_ref,
                     pad_ref, qs_ref, ks_ref, vs_ref, us_ref, ws_ref, qk_ref, egl_ref, st_ref):
    s = q_ref.shape[0]
    heads = GDN_HEADS
    dk = q_ref.shape[1] // heads
    c64 = GDN_CHUNK
    n_chunks = s // c64
    n_tiles, rows, fill_pad, conv_tile = _seq_tiles(pad_ref, s)

    def l2n(x, scale):
        parts = []
        for h in range(heads):
            xh = x[:, h * dk:(h + 1) * dk]
            inv = lax.rsqrt(jnp.sum(xh * xh, axis=-1, keepdims=True) + EPS)
            parts.append(xh * (inv * scale))
        return jnp.concatenate(parts, axis=1)

    for src_ref, w_ref, dst_ref, post in (
            (q_ref, wq_ref, qs_ref, lambda x: l2n(x, dk ** -0.5)),
            (k_ref, wk_ref, ks_ref, lambda x: l2n(x, 1.0)),
            (v_ref, wv_ref, vs_ref, lambda x: x)):
        fill_pad(lambda rs, src_ref=src_ref: src_ref[rs, :].astype(F32))

        def conv_body(i, c, w_ref=w_ref, dst_ref=dst_ref, post=post):
            dst_ref[rows(i), :] = post(_silu(conv_tile(i, w_ref, GDN_CONV_WIDTH)))
            return c
        lax.fori_loop(0, n_tiles, conv_body, 0)

    ri = lax.broadcasted_iota(jnp.int32, (c64, c64), 0)
    ci = lax.broadcasted_iota(jnp.int32, (c64, c64), 1)
    tril = ri >= ci
    strict = ri > ci
    ltri = tril.astype(F32)

    group = PREP_CHUNKS if n_chunks % PREP_CHUNKS == 0 else 1

    def chunk_prep(cg, carry):
        chunks = []
        for cc in range(group):
            c = cg * group + cc
            rs = pl.ds(pl.multiple_of(c * c64, c64), c64)
            chunks.append((c, rs, ab_ref[rs, :], qs_ref[rs, :], ks_ref[rs, :], vs_ref[rs, :]))
        inst = []
        gcs = [_dot(ltri, -jnp.exp(alog_ref[...]) * _softplus(ab + dtb_ref[...]), precision=HIGHEST)
               for _, _, ab, _, _, _ in chunks]
        for (c, rs, ab, q_all, k_all, v_all), gc in zip(chunks, gcs):
            beta = _sigmoid(ab)
            gct = gc.T
            for h in range(heads):
                hs = slice(h * dk, (h + 1) * dk)
                gcol = gc[:, h:h + 1]
                glast = gc[c64 - 1:c64, h:h + 1]
                bcol = beta[:, heads + h:heads + h + 1]
                decay = jnp.where(tril, jnp.exp(jnp.where(tril, gcol - gct[h:h + 1, :], 0.0)), 0.0)
                kh, qh, vh = k_all[:, hs], q_all[:, hs], v_all[:, hs]
                kb = kh * bcol
                egc = jnp.exp(gcol)
                inst.append(dict(decay=decay, kh=kh, qh=qh, kb=kb, egc=egc,
                                 rhs=jnp.concatenate([vh * bcol, kb * egc], axis=1).astype(BF16),
                                 kd=kh * jnp.exp(glast - gcol),
                                 eg=jnp.broadcast_to(jnp.exp(glast), (SUBLANES, dk))))
        kqs = [_dot_nt(jnp.concatenate([t["kb"], t["qh"]], axis=0).astype(BF16), t["kh"].astype(BF16))
               for t in inst]
        minvs = _unit_lower_inverses([jnp.where(strict, kq[0:c64] * t["decay"], 0.0)
                                      for kq, t in zip(kqs, inst)])
        uws = [_dot(m.astype(BF16), t["rhs"]) for m, t in zip(minvs, inst)]
        for ci, (c, rs, _, _, _, _) in enumerate(chunks):
            sl = slice(ci * heads, (ci + 1) * heads)
            us_ref[rs, :] = jnp.concatenate([uw[:, 0:dk] for uw in uws[sl]], axis=1)
            ws_ref[rs, :] = jnp.concatenate([uw[:, dk:2 * dk] for uw in uws[sl]], axis=1).astype(BF16)
            qs_ref[rs, :] = jnp.concatenate([t["qh"] * t["egc"] for t in inst[sl]], axis=1)
            ks_ref[rs, :] = jnp.concatenate([t["kd"] for t in inst[sl]], axis=1)
            qk_ref[rs, :] = jnp.concatenate([kq[c64:2 * c64] * t["decay"]
                                             for kq, t in zip(kqs[sl], inst[sl])], axis=1).astype(BF16)
            e0 = pl.multiple_of(c * (heads * SUBLANES), heads * SUBLANES)
            egl_ref[pl.ds(e0, heads * SUBLANES), :] = jnp.concatenate([t["eg"] for t in inst[sl]], axis=0)
        return carry
    lax.fori_loop(0, n_chunks // group, chunk_prep, 0)

    st_ref[...] = jnp.zeros(st_ref.shape, F32)

    def scan(c, carry):
        rs = pl.ds(pl.multiple_of(c * c64, c64), c64)
        e0 = pl.multiple_of(c * (heads * SUBLANES), heads * SUBLANES)
        w_all, qg_all, u_all, kd_all = ws_ref[rs, :], qs_ref[rs, :], us_ref[rs, :], ks_ref[rs, :]
        qk_all = qk_ref[rs, :]
        eg_all = egl_ref[pl.ds(e0, heads * SUBLANES), :]
        hsl = [slice(h * dk, (h + 1) * dk) for h in range(heads)]
        sts = [st_ref[h] for h in range(heads)]
        wqs = [_dot(jnp.concatenate([w_all[:, hs], qg_all[:, hs].astype(BF16)], axis=0), st.astype(BF16))
               for hs, st in zip(hsl, sts)]
        vbs = [(u_all[:, hs] - wq[0:c64]).astype(BF16) for hs, wq in zip(hsl, wqs)]
        upd = [_dot_tn(kd_all[:, hs].astype(BF16), vb) for hs, vb in zip(hsl, vbs)]
        intra = [_dot(qk_all[:, h * c64:(h + 1) * c64], vbs[h]) for h in range(heads)]
        for h in range(heads):
            st_ref[h] = sts[h] * eg_all[h * SUBLANES:h * SUBLANES + 1, :] + upd[h]
        vs_ref[rs, :] = jnp.concatenate([wq[c64:2 * c64] + o for wq, o in zip(wqs, intra)], axis=1)
        return carry
    lax.fori_loop(0, n_chunks, scan, 0)

    def finish(i, c):
        o = vs_ref[rows(i), :]
        og = og_ref[rows(i), :].astype(F32)
        parts = []
        for h in range(heads):
            oh = o[:, h * dk:(h + 1) * dk]
            parts.append(_rms(oh, gn_ref[...]))
        y = jnp.concatenate(parts, axis=1) * _silu(og)
        o_ref[rows(i), :] = y.astype(o_ref.dtype)
        return c
    lax.fori_loop(0, n_tiles, finish, 0)


def _l0_mixer(u, ab, gdn_conv, a_log, dt_bias, gdn_norm_g, batch, seq):
    dk = gdn_norm_g.shape[0]
    heads = GDN_HEADS
    cw = heads * dk

    def lane_row(x):
        return jnp.zeros((1, LANES), F32).at[0, :x.shape[0]].set(x)

    in_specs = [pl.BlockSpec((seq, cw), lambda b, j=j: (b, j)) for j in range(3, 7)]
    in_specs += [pl.BlockSpec((seq, LANES), lambda b: (b, 0))]
    in_specs += [pl.BlockSpec((GDN_CONV_WIDTH, cw), lambda b, j=j: (0, j), pipeline_mode=pl.Buffered(1))
                 for j in range(3)]
    in_specs += [_resident((1, LANES)), _resident((1, LANES)), _resident((1, dk))]
    big = pltpu.VMEM((seq, cw), F32)
    scratch = [pltpu.VMEM((seq + TOP_PAD, cw), F32), big, big, big, big, pltpu.VMEM((seq, cw), BF16),
               pltpu.VMEM((seq, heads * GDN_CHUNK), BF16),
               pltpu.VMEM((seq // GDN_CHUNK * heads * SUBLANES, dk), F32),
               pltpu.VMEM((heads, dk, dk), F32)]
    return pl.pallas_call(
        _l0_mixer_kernel, grid=(batch,), in_specs=in_specs,
        out_specs=pl.BlockSpec((seq, cw), lambda b: (b, 0)),
        out_shape=jax.ShapeDtypeStruct((batch * seq, cw), BF16),
        scratch_shapes=scratch, compiler_params=_cparams("parallel"), name="l0_deltanet",
    )(u, u, u, u, ab, gdn_conv, gdn_conv, gdn_conv,
      lane_row(a_log), lane_row(dt_bias), gdn_norm_g.reshape(1, dk))


def _bias_table_kernel(tab_ref, o_ref, *, t):
    h = pl.program_id(0)
    m = pl.program_id(1)
    ri = lax.broadcasted_iota(jnp.int32, (t, t), 0)
    ci = lax.broadcasted_iota(jnp.int32, (t, t), 1)
    rel = m * t + ci - ri
    n = jnp.maximum(rel, 0)
    max_exact = NUM_BUCKETS // 2
    nf = jnp.maximum(n, 1).astype(F32)
    large = max_exact + (jnp.log(nf / max_exact) / math.log(MAX_DISTANCE / max_exact)
                         * (NUM_BUCKETS - max_exact)).astype(jnp.int32)
    large = jnp.minimum(large, NUM_BUCKETS - 1)
    bucket = jnp.where(n < max_exact, n, large)
    bias = jnp.zeros((t, t), F32)
    for b in range(NUM_BUCKETS):
        bias = jnp.where(bucket == b, tab_ref[b, h], bias)
    o_ref[...] = jnp.where(rel >= 0, bias * LOG2E, MASK_VALUE)


def _bias_table(rel_bias, seq, t):
    nb = seq // t
    heads = rel_bias.shape[1]
    return pl.pallas_call(
        functools.partial(_bias_table_kernel, t=t), grid=(heads, nb),
        in_specs=[pl.BlockSpec(memory_space=pltpu.SMEM)],
        out_specs=pl.BlockSpec((None, None, t, t), lambda h, m: (h, m, 0, 0)),
        out_shape=jax.ShapeDtypeStruct((heads, nb, t, t), F32),
        compiler_params=_cparams("parallel", "parallel"), name="rel_bias_table")(rel_bias)


def _attn_kernel(q_ref, k_ref, v_ref, tb_ref, lam_ref, gn_ref, o_ref, m_ref, l_ref, acc_ref,
                 *, t, heads, lambda_init):
    qi = pl.program_id(2)
    dh2 = q_ref.shape[1] // heads
    dh = dh2 // 2
    lane = lax.broadcasted_iota(jnp.int32, (t, dh2), 1)
    qqs = []
    for h in range(heads):
        qf = q_ref[:, h * dh2:(h + 1) * dh2].astype(F32) * (dh ** -0.5 * LOG2E)
        qqs.append(jnp.concatenate([jnp.where(lane < dh, qf, 0.0), jnp.where(lane >= dh, qf, 0.0)],
                                   axis=0).astype(BF16))

    hs = [slice(h * dh2, (h + 1) * dh2) for h in range(heads)]

    def update(j0, nblk):
        ks = pl.ds(pl.multiple_of(j0 * t, t), nblk * t)
        k_all = k_ref[ks, :]
        v_all = v_ref[ks, :]
        s_t = [_dot_nt(k_all[:, hs[h]], qqs[h]) for h in range(heads)]
        ps, alphas = [], []
        for h in range(heads):
            m = m_ref[h, 0:1, :]
            b = jnp.concatenate([tb_ref[h, qi - j0 - i] for i in range(nblk)], axis=0)
            s = jnp.concatenate([s_t[h][:, 0:t] + b, s_t[h][:, t:2 * t] + b], axis=1)
            m_new = jnp.maximum(m, jnp.max(s, axis=0, keepdims=True))
            alpha = jnp.exp2(m - m_new)
            p = jnp.exp2(s - m_new)
            m_ref[h] = jnp.broadcast_to(m_new, m_ref.shape[1:])
            l_ref[h] = jnp.broadcast_to(alpha * l_ref[h, 0:1, :] + jnp.sum(p, axis=0, keepdims=True),
                                        l_ref.shape[1:])
            alphas.append(alpha)
            ps.append(p.astype(BF16))
        pv = [_dot_tn(v_all[:, hs[h]], ps[h]) for h in range(heads)]
        for h in range(heads):
            acc_ref[h] = alphas[h] * acc_ref[h] + pv[h]

    m_ref[...] = jnp.full(m_ref.shape, MASK_VALUE, F32)
    l_ref[...] = jnp.zeros(l_ref.shape, F32)
    acc_ref[...] = jnp.zeros(acc_ref.shape, F32)
    odd = (qi + 1) % 2

    @pl.when(odd == 1)
    def _():
        update(0, 1)

    def pair(i, c):
        update(odd + 2 * i, 2)
        return c
    lax.fori_loop(0, (qi + 1) // 2, pair, 0)
    lp = lam_ref[...]
    lam = (jnp.exp(jnp.sum(lp[0:1] * lp[1:2], axis=-1, keepdims=True))
           - jnp.exp(jnp.sum(lp[2:3] * lp[3:4], axis=-1, keepdims=True)) + lambda_init)
    for h in range(heads):
        o12 = acc_ref[h] / l_ref[h, 0:1, :]
        o = (o12[:, 0:t] - lam * o12[:, t:2 * t]).T
        o_ref[:, h * dh2:(h + 1) * dh2] = (_rms(o, gn_ref[...]) * (1.0 - lambda_init)).astype(o_ref.dtype)


def _diff_attention(u, table, lam_params, norm_g, batch, seq, lambda_init):
    heads = DIFF_HEADS
    dh2 = norm_g.shape[0]
    hg = ATTN_HEADS_PER_STEP
    w = hg * dh2
    ng = heads // hg
    t = table.shape[2]
    nq = seq // t
    return pl.pallas_call(
        functools.partial(_attn_kernel, t=t, heads=hg, lambda_init=lambda_init),
        grid=(batch, ng, nq),
        in_specs=[pl.BlockSpec((t, w), lambda b, g, i: (b * nq + i, g)),
                  pl.BlockSpec((seq, w), lambda b, g, i: (b, ng + g)),
                  pl.BlockSpec((seq, w), lambda b, g, i: (b, 2 * ng + g)),
                  pl.BlockSpec((hg, nq, t, t), lambda b, g, i: (g, 0, 0, 0)),
                  _resident(lam_params.shape), _resident((1, dh2))],
        out_specs=pl.BlockSpec((t, w), lambda b, g, i: (b * nq + i, g)),
        out_shape=jax.ShapeDtypeStruct((batch * seq, heads * dh2), BF16),
        scratch_shapes=[pltpu.VMEM((hg, SUBLANES, 2 * t), F32), pltpu.VMEM((hg, SUBLANES, 2 * t), F32),
                        pltpu.VMEM((hg, dh2, 2 * t), F32)],
        compiler_params=_cparams("parallel", "parallel", "arbitrary"), name="diff_attention",
    )(u, u, u, table, lam_params, norm_g.reshape(1, dh2))


def _conformer_kernel(ga_ref, gb_ref, w_ref, b_ref, lg_ref, lb_ref, o_ref, pad_ref):
    s, c = ga_ref.shape
    rt = min(ELEM_ROWS, s)
    pad_ref[0:TOP_PAD, :] = jnp.zeros((TOP_PAD, c), F32)

    def glu(i, carry):
        r = pl.multiple_of(i * rt, rt)
        rs = pl.ds(r, rt)
        pad_ref[pl.ds(TOP_PAD + r, rt), :] = ga_ref[rs, :].astype(F32) * _sigmoid(gb_ref[rs, :].astype(F32))
        return carry
    lax.fori_loop(0, s // rt, glu, 0)

    ct = CONF_ROWS

    def conv(i, carry):
        r = pl.multiple_of(i * ct, ct)
        tap = _delayed(pad_ref, r, ct, CONF_WIDTH - 1)
        acc = jnp.broadcast_to(b_ref[...], (ct, c))
        for j in range(CONF_WIDTH):
            acc = acc + w_ref[j:j + 1, :] * tap(CONF_WIDTH - 1 - j)
        mu = jnp.mean(acc, axis=-1, keepdims=True)
        xc = acc - mu
        var = jnp.mean(xc * xc, axis=-1, keepdims=True)
        y = xc * lax.rsqrt(var + EPS) * lg_ref[...] + lb_ref[...]
        o_ref[pl.ds(r, ct), :] = _silu(y).astype(o_ref.dtype)
        return carry
    lax.fori_loop(0, s // ct, conv, 0)


def _conformer(u, w, b, ln_g, ln_b, batch, seq, col0):
    c = w.shape[1]
    return pl.pallas_call(
        _conformer_kernel, grid=(batch,),
        in_specs=[pl.BlockSpec((seq, c), lambda i: (i, col0)),
                  pl.BlockSpec((seq, c), lambda i: (i, col0 + 1)),
                  _resident(w.shape), _resident((1, c)), _resident((1, c)), _resident((1, c))],
        out_specs=pl.BlockSpec((seq, c), lambda i: (i, 0)),
        out_shape=jax.ShapeDtypeStruct((batch * seq, c), BF16),
        scratch_shapes=[pltpu.VMEM((seq + TOP_PAD, c), F32)],
        compiler_params=_cparams("parallel"), name="conformer_conv",
    )(u, u, w, b.reshape(1, c), ln_g.reshape(1, c), ln_b.reshape(1, c))


def _proj_router_kernel(y1_ref, y2_ref, w1_ref, w2_ref, h_ref, g_ref, wrt_ref, upper_ref,
                        h_out_ref, hn_ref, route_ref, gate_ref, cnt_ref, carry_ref):
    i = pl.program_id(0)
    tm = h_ref.shape[0]
    n_e = N_EXPERTS

    @pl.when(i == 0)
    def _():
        carry_ref[...] = jnp.zeros(carry_ref.shape, F32)

    h = h_ref[...] + _dot(y1_ref[...], w1_ref[...]) + _dot(y2_ref[...], w2_ref[...])
    h_out_ref[...] = h
    xn = _rms(h, g_ref[...])
    _store_token_tiles(hn_ref, 0, xn)

    logits = _dot_nt(wrt_ref[...], xn.astype(BF16))[0:n_e, :]
    sub = lax.broadcasted_iota(jnp.int32, logits.shape, 0)
    m1 = jnp.max(logits, axis=0, keepdims=True)
    i1 = jnp.min(jnp.where(logits == m1, sub, n_e), axis=0, keepdims=True)
    rest = jnp.where(sub == i1, -jnp.inf, logits)
    m2 = jnp.max(rest, axis=0, keepdims=True)
    i2 = jnp.min(jnp.where(rest == m2, sub, n_e), axis=0, keepdims=True)
    e = jnp.exp(m2 - m1)
    g1 = 1.0 / (1.0 + e)
    oh1 = sub == i1
    oh2 = sub == i2
    both = oh1.astype(F32) + oh2.astype(F32)
    csum = _dot(both.astype(BF16), upper_ref[...])
    carry = carry_ref[:, 0:1]
    before = csum - both + carry
    total = carry + csum[:, tm - 1:tm]
    carry_ref[...] = jnp.broadcast_to(total, carry_ref.shape)
    cnt_ref[...] = jnp.broadcast_to(total, cnt_ref.shape).astype(jnp.int32)
    r1 = jnp.sum(jnp.where(oh1, before, 0.0), axis=0, keepdims=True).astype(jnp.int32)
    r2 = jnp.sum(jnp.where(oh2, before, 0.0), axis=0, keepdims=True).astype(jnp.int32)
    route_ref[...] = jnp.where(sub == 0, i1, jnp.where(sub == 1, i2, jnp.where(sub == 2, r1,
                               jnp.where(sub == 3, r2, 0))))
    grow = jnp.where(sub == 0, g1, jnp.where(sub == 1, e * g1, 0.0))
    sel = (lax.broadcasted_iota(jnp.int32, (n_e, LANES), 0)
           == lax.broadcasted_iota(jnp.int32, (n_e, LANES), 1)).astype(F32)
    gate_ref[...] = lax.dot_general(grow, sel, (((0,), (0,)), ((), ())), precision=HIGHEST,
                                    preferred_element_type=F32)


def _proj_router(y1, y2, w, h, g, wr):
    t, d = h.shape
    k1, k2 = y1.shape[1], y2.shape[1]
    tm = min(ROW_TILE, t)
    wrt = jnp.zeros((LANES, d), BF16).at[:N_EXPERTS].set(wr.T.astype(BF16))
    upper = (jnp.arange(tm)[:, None] <= jnp.arange(tm)[None, :]).astype(BF16)
    row = lambda width: pl.BlockSpec((tm, width), lambda i: (i, 0))
    return pl.pallas_call(
        _proj_router_kernel, grid=(t // tm,),
        in_specs=[row(k1), row(k2), _resident((k1, d)), _resident((k2, d)), row(d),
                  _resident((1, d)), _resident((LANES, d)), _resident((tm, tm))],
        out_specs=[row(d), pl.BlockSpec((tm * d // LANES, LANES), lambda i: (i, 0)),
                   pl.BlockSpec((N_EXPERTS, tm), lambda i: (0, i)), row(LANES),
                   pl.BlockSpec((N_EXPERTS, LANES), lambda i: (0, 0))],
        out_shape=[jax.ShapeDtypeStruct((t, d), F32), jax.ShapeDtypeStruct((t * d // LANES, LANES), F32),
                   jax.ShapeDtypeStruct((N_EXPERTS, t), jnp.int32), jax.ShapeDtypeStruct((t, LANES), F32),
                   jax.ShapeDtypeStruct((N_EXPERTS, LANES), jnp.int32)],
        scratch_shapes=[pltpu.VMEM((N_EXPERTS, LANES), F32)],
        compiler_params=_cparams("arbitrary"), name="l1_out_proj_router",
    )(y1, y2, w[:k1], w[k1:], h, g.reshape(1, d), wrt, upper)


def _dispatch_kernel(dest_hbm, pad_hbm, x_ref, xs_hbm, idx_smem, pad_smem, zero_ref, sem_idx, sem_x):
    i = pl.program_id(0)
    n = x_ref.shape[0] // SUBLANES
    m = TOP_K * n
    cp = pltpu.make_async_copy(dest_hbm.at[i], idx_smem, sem_idx)
    cp.start()
    cp.wait()

    def body(r, c):
        for k in range(TOP_K):
            _token_tile_copy(x_ref, r, xs_hbm, idx_smem[k * n + r], sem_x).start(priority=k % 2)
        return c
    lax.fori_loop(0, n, body, 0, unroll=8)
    for _ in range(TOP_K):
        pltpu.make_async_copy(x_ref, xs_hbm.at[pl.ds(0, n * SUBLANES)], sem_x).wait()

    @pl.when(i == pl.num_programs(0) - 1)
    def _():
        n_pad = pad_smem.shape[0]
        cp = pltpu.make_async_copy(pad_hbm, pad_smem, sem_idx)
        cp.start()
        cp.wait()
        zero_ref[...] = jnp.zeros(zero_ref.shape, F32)

        def fill(j, c):
            for q in range(2):
                _token_tile_copy(zero_ref, 0, xs_hbm, pad_smem[2 * j + q], sem_x).start(priority=q)
            return c
        lax.fori_loop(0, n_pad // 2, fill, 0, unroll=8)
        pltpu.make_async_copy(xs_hbm.at[pl.ds(0, n_pad * SUBLANES)], xs_hbm.at[pl.ds(0, n_pad * SUBLANES)],
                              sem_x).wait()


def _dispatch(hn_tiles, dest, pad_rows, t, d):
    n = min(DISPATCH_ROWS, t)
    steps = t // n
    tpt = d // LANES
    p = t * TOP_K + N_EXPERTS * MOE_ROWS
    dest_steps = jnp.concatenate([dest[k].reshape(steps, n) for k in range(TOP_K)], axis=1)
    return pl.pallas_call(
        _dispatch_kernel, grid=(steps,),
        in_specs=[pl.BlockSpec(memory_space=pl.ANY), pl.BlockSpec(memory_space=pl.ANY),
                  pl.BlockSpec((n * tpt, LANES), lambda i: (i, 0))],
        out_specs=pl.BlockSpec(memory_space=pl.ANY),
        out_shape=jax.ShapeDtypeStruct((p * tpt, LANES), F32),
        scratch_shapes=[pltpu.SMEM((TOP_K * n,), jnp.int32), pltpu.SMEM(pad_rows.shape, jnp.int32),
                        pltpu.VMEM((tpt, LANES), F32), pltpu.SemaphoreType.DMA, pltpu.SemaphoreType.DMA],
        compiler_params=_cparams("arbitrary"), name="moe_dispatch")(dest_steps, pad_rows, hn_tiles)


def _dispatch_plan(route, counts, t):
    tm = MOE_ROWS
    p = t * TOP_K + N_EXPERTS * tm
    padded = (counts + tm - 1) // tm * tm
    pend = jnp.cumsum(padded)
    pstart = pend - padded
    experts = jnp.arange(N_EXPERTS, dtype=jnp.int32)[:, None]
    dest = jnp.stack([jnp.sum(jnp.where(route[k][None, :] == experts, pstart[:, None], 0), axis=0)
                      + route[TOP_K + k] for k in range(TOP_K)])
    n_pad = p - t * TOP_K
    gaps = jnp.concatenate([padded - counts, (p - pend[-1])[None]])
    gap_end = jnp.cumsum(gaps)
    gap_row0 = jnp.concatenate([pstart + counts, pend[-1:]])
    j = jnp.arange(n_pad, dtype=jnp.int32)
    which = jnp.searchsorted(gap_end, j, side='right', method='compare_all')
    pad_rows = (gap_row0[which] + j - (gap_end - gaps)[which]).astype(jnp.int32)
    nb = p // tm
    blk_e = jnp.minimum(jnp.searchsorted(pend, jnp.arange(nb, dtype=jnp.int32) * tm, side='right',
                                         method='compare_all'), N_EXPERTS - 1).astype(jnp.int32)
    return dest.astype(jnp.int32), pad_rows, blk_e, (pend[-1:] // tm).astype(jnp.int32)


def _experts_kernel(blk_e_ref, used_ref, x_ref, wgu_ref, wd_ref, o_ref, hid_ref):
    f = wd_ref.shape[0]
    tm = hid_ref.shape[0]
    live = pl.program_id(0) < used_ref[0]

    @pl.when(live)
    def _():
        x = _load_token_tiles(x_ref, 0, tm).astype(BF16)
        for c in range(0, f, MOE_SUB_TILE):
            hid_ref[:, c:c + MOE_SUB_TILE] = (
                _silu(_dot(x, wgu_ref[:, c:c + MOE_SUB_TILE]))
                * _dot(x, wgu_ref[:, f + c:f + c + MOE_SUB_TILE])).astype(BF16)
        _store_token_tiles(o_ref, 0, _dot(hid_ref[...], wd_ref[...]))

    @pl.when(jnp.logical_not(live))
    def _():
        o_ref[...] = jnp.zeros(o_ref.shape, F32)


def _experts(xs, blk_e, n_used, w_gate_up, w_down):
    f, d = w_down.shape[1], w_down.shape[2]
    tm = MOE_ROWS
    tpt = d // LANES
    assert f % MOE_SUB_TILE == 0
    grid_spec = pltpu.PrefetchScalarGridSpec(
        num_scalar_prefetch=2, grid=(blk_e.shape[0],),
        in_specs=[pl.BlockSpec((tm * tpt, LANES), lambda i, e, u: (i, 0)),
                  pl.BlockSpec((None, d, 2 * f), lambda i, e, u: (e[i], 0, 0), pipeline_mode=pl.Buffered(1)),
                  pl.BlockSpec((None, f, d), lambda i, e, u: (e[i], 0, 0), pipeline_mode=pl.Buffered(1))],
        out_specs=pl.BlockSpec((tm * tpt, LANES), lambda i, e, u: (i, 0)),
        scratch_shapes=[pltpu.VMEM((tm, f), BF16)])
    return pl.pallas_call(
        _experts_kernel, grid_spec=grid_spec, out_shape=jax.ShapeDtypeStruct(xs.shape, F32),
        compiler_params=_cparams("parallel"), name="moe_experts",
    )(blk_e, n_used, xs, w_gate_up, w_down)


def _combine_kernel(dest_hbm, y_hbm, gate_ref, h_ref, g_ref, p_ref, wp_ref, wg_ref, fg_ref, o_ref,
                    idx0, idx1, ybuf, sem_idx, sem_rows):
    i = pl.program_id(0)
    steps = pl.num_programs(0)
    n = h_ref.shape[0]
    m = TOP_K * n
    slot = i % 2
    idxs = (idx0, idx1)

    def idx_copy(blk, s):
        return pltpu.make_async_copy(dest_hbm.at[blk], idxs[s], sem_idx.at[s])

    def issue_tiles(s):
        def body(r, c):
            for k in range(TOP_K):
                _token_tile_copy(y_hbm, idxs[s][k * n + r], ybuf, s * m + k * n + r,
                                 sem_rows.at[s]).start(priority=k % 2)
            return c
        lax.fori_loop(0, n, body, 0, unroll=8)

    @pl.when(i == 0)
    def _():
        idx_copy(0, 0).start()
        idx_copy(0, 0).wait()
        idx_copy(1, 1).start()
        issue_tiles(0)

    for s in range(2):
        @pl.when(slot == s)
        def _(s=s):
            idx_copy(i + 1, 1 - s).wait()

            @pl.when(i + 2 <= steps)
            def _():
                idx_copy(i + 2, s).start()

            issue_tiles(1 - s)
            _token_tiles_wait(y_hbm, ybuf, s * m, m, sem_rows.at[s])

    nc = n // COMBINE_CHUNKS
    rows = [pl.ds(c * nc, nc) for c in range(COMBINE_CHUNKS)]
    xs = []
    for c, rs in enumerate(rows):
        gates = gate_ref[rs, :]
        xs.append(h_ref[rs, :] + gates[:, 0:1] * _load_token_tiles(ybuf, slot * m + c * nc, nc)
                  + gates[:, 1:2] * _load_token_tiles(ybuf, slot * m + n + c * nc, nc))
    xns = [_rms(x, g_ref[...]).astype(BF16) for x in xs]
    gate = [_sigmoid(_dot(xn, wg_ref[...])) for xn in xns]
    emb = [_dot(p_ref[rs, :].astype(BF16), wp_ref[...]) for rs in rows]
    for rs, x, e, g in zip(rows, xs, emb, gate):
        o_ref[rs, :] = _rms(x + e * g, fg_ref[...])

    @pl.when(i == steps - 1)
    def _():
        _token_tiles_wait(y_hbm, ybuf, (1 - slot) * m, m, sem_rows.at[1 - slot])


def _combine_ple_final(dest, y_tiles, gates, h, g, p, layer, wp, wg, final_g):
    t, d = h.shape
    e = p.shape[2]
    n = min(COMBINE_ROWS, t)
    m = TOP_K * n
    steps = t // n
    dest_steps = jnp.concatenate([dest[k].reshape(steps, n) for k in range(TOP_K)], axis=1)
    dest_steps = jnp.concatenate([dest_steps, jnp.zeros((1, m), jnp.int32)], axis=0)
    return pl.pallas_call(
        _combine_kernel, grid=(steps,),
        in_specs=[pl.BlockSpec(memory_space=pl.ANY), pl.BlockSpec(memory_space=pl.ANY),
                  pl.BlockSpec((n, LANES), lambda i: (i, 0)),
                  pl.BlockSpec((n, d), lambda i: (i, 0)), _resident((1, d)),
                  pl.BlockSpec((None, n, e), lambda i: (layer, i, 0)), _resident((e, d)), _resident((d, d)),
                  _resident((1, d))],
        out_specs=pl.BlockSpec((n, d), lambda i: (i, 0)),
        out_shape=jax.ShapeDtypeStruct((t, d), F32),
        scratch_shapes=[pltpu.SMEM((m,), jnp.int32), pltpu.SMEM((m,), jnp.int32),
                        pltpu.VMEM((2 * m * d // LANES, LANES), F32),
                        pltpu.SemaphoreType.DMA((2,)), pltpu.SemaphoreType.DMA((2,))],
        compiler_params=_cparams("arbitrary"), name="moe_combine_ple_final",
    )(dest_steps, y_tiles, gates, h, g.reshape(1, d), p, wp, wg, final_g.reshape(1, d))


def kernel(x, p, norm_mix_g, norm_ffn_g, norm_ple_g, final_norm_g, ev_w_in, ev_conv_a, ev_gdn_conv, ev_gdn_A_log, ev_gdn_dt_bias, ev_gdn_norm_g, ev_w_out, od_w_in, od_lambda, od_diff_norm_g, od_conf_dw_w, od_conf_dw_b, od_conf_ln_g, od_conf_ln_b, od_w_out, rel_bias, ffn_w_gate_up, ffn_w_down, moe_router, moe_w_gate_up, moe_w_down, ple_w_proj, ple_w_gate):
    batch, seq, d = x.shape
    t = batch * seq
    depth = p.shape[0]
    assert depth == 2 and seq % GDN_CHUNK == 0
    h = x.reshape(t, d)
    pf = p.reshape(depth, t, p.shape[-1])

    heads = GDN_HEADS
    n_main = ev_w_in.shape[2] - 2 * heads
    w_in = ev_w_in[0]
    w_ab = jnp.zeros((d, LANES), BF16).at[:, :2 * heads].set(w_in[:, n_main:].astype(BF16))
    u, ab = _norm_proj(h, norm_mix_g[0], w_in[:, :n_main].astype(BF16), w_ab, name="l0_in_proj")
    ya = _gated_conv(u, ev_conv_a[0], batch, seq)
    yb = _l0_mixer(u, ab, ev_gdn_conv[0], ev_gdn_A_log[0], ev_gdn_dt_bias[0], ev_gdn_norm_g[0], batch, seq)
    h = _proj_residual([ya, yb], ev_w_out[0].astype(BF16), h, name="l0_out_proj")
    f = ffn_w_down.shape[1]
    lambda_init = 0.8 - 0.6 * math.exp(-0.3 * 1)
    h, u = _ffn_ple_proj(h, norm_ffn_g[0], ffn_w_gate_up[0, :, :f].astype(BF16),
                         ffn_w_gate_up[0, :, f:].astype(BF16), ffn_w_down[0].astype(BF16),
                         norm_ple_g[0], pf, 0, ple_w_proj[0].astype(BF16), ple_w_gate[0].astype(BF16),
                         norm_mix_g[1], od_w_in[0].astype(BF16), name="l0_ffn_ple_l1_in_proj")

    table = _bias_table(rel_bias, seq, min(ATTN_TILE, seq))
    o_attn = _diff_attention(u, table, od_lambda[0], od_diff_norm_g[0], batch, seq, lambda_init)
    c_conf = od_conf_dw_w.shape[2]
    o_conf = _conformer(u, od_conf_dw_w[0], od_conf_dw_b[0], od_conf_ln_g[0], od_conf_ln_b[0],
                        batch, seq, 3 * DIFF_HEADS * od_diff_norm_g.shape[1] // c_conf)
    h, hn, route, gates, counts = _proj_router(o_attn, o_conf, od_w_out[0].astype(BF16), h, norm_ffn_g[1],
                                               moe_router[0])
    dest, pad_rows, blk_e, n_used = _dispatch_plan(route, counts[:, 0], t)
    xs = _dispatch(hn, dest, pad_rows, t, d)
    y = _experts(xs, blk_e, n_used, moe_w_gate_up[0].astype(BF16), moe_w_down[0].astype(BF16))
    out = _combine_ple_final(dest, y, gates, h, norm_ple_g[1], pf, 1, ple_w_proj[1].astype(BF16),
                             ple_w_gate[1].astype(BF16), final_norm_g)
    return out.reshape(batch, seq, d)
```

```python
import functools
import math

import jax
import jax.numpy as jnp
from jax import lax
from jax.experimental import pallas as pl
from jax.experimental.pallas import tpu as pltpu

F32 = jnp.float32
BF16 = jnp.bfloat16
HIGHEST = lax.Precision.HIGHEST

EPS = 1e-6
CONV_A_WIDTH = 3
GDN_HEADS = 4
GDN_CONV_WIDTH = 4
GDN_CHUNK = 64
DIFF_HEADS = 4
NUM_BUCKETS = 32
MAX_DISTANCE = 128
CONF_WIDTH = 31
N_EXPERTS = 8
TOP_K = 2

LANES = 128
SUBLANES = 8
VMEM_LIMIT_BYTES = 56 * 1024 * 1024
MASK_VALUE = -1e30
LOG2E = math.log2(math.e)

ROW_TILE = 512
PROJ_ROWS = 1024
COL_TILE = 512
FFN_CHUNKS = 2
ELEM_ROWS = 256
CONF_ROWS = 512
PREP_CHUNKS = 8
ATTN_TILE = 256
ATTN_HEADS_PER_STEP = 4
MOE_ROWS = 1024
DISPATCH_ROWS = 4096
MOE_SUB_TILE = 256
COMBINE_ROWS = 512
COMBINE_CHUNKS = 4
TOP_PAD = 32


def _cparams(*sem):
    return pltpu.CompilerParams(dimension_semantics=sem, vmem_limit_bytes=VMEM_LIMIT_BYTES)


def _resident(shape):
    nd = len(shape)
    return pl.BlockSpec(shape, lambda *_: (0,) * nd, pipeline_mode=pl.Buffered(1))


def _rms(x, g):
    return x * lax.rsqrt(jnp.mean(x * x, axis=-1, keepdims=True) + EPS) * g


def _sigmoid(x):
    return jax.nn.sigmoid(x)


def _silu(x):
    return x * jax.nn.sigmoid(x)


def _softplus(x):
    return jnp.maximum(x, 0.0) + jnp.log1p(jnp.exp(-jnp.abs(x)))


def _dot(a, b, **kw):
    return jnp.dot(a, b, preferred_element_type=F32, **kw)


def _dot_nt(a, b):
    return lax.dot_general(a, b, (((1,), (1,)), ((), ())), preferred_element_type=F32)


def _dot_tn(a, b):
    return lax.dot_general(a, b, (((0,), (0,)), ((), ())), preferred_element_type=F32)


def _delayed(pad_ref, r, rows, max_delay):
    lead = -(-max_delay // SUBLANES) * SUBLANES
    win = pad_ref[pl.ds(TOP_PAD + r - lead, rows + lead), :]
    rolled = {0: win}

    def tap(d):
        a, b = divmod(d, SUBLANES)
        if b not in rolled:
            rolled[b] = pltpu.roll(win, b, 0)
        start = lead - SUBLANES * a
        return rolled[b][start:start + rows, :]
    return tap


def _load_token_tiles(ref, first_token, n):
    return jnp.concatenate([ref[pl.ds(first_token * SUBLANES + s, n, stride=SUBLANES), :]
                            for s in range(SUBLANES)], axis=1)


def _store_token_tiles(ref, first_token, x):
    n = x.shape[0]
    for s in range(SUBLANES):
        ref[pl.ds(first_token * SUBLANES + s, n, stride=SUBLANES), :] = x[:, s * LANES:(s + 1) * LANES]


def _token_tile_copy(src_hbm, src_token, dst_vmem, dst_token, sem):
    return pltpu.make_async_copy(src_hbm.at[pl.ds(pl.multiple_of(src_token * SUBLANES, SUBLANES), SUBLANES)],
                                 dst_vmem.at[pl.ds(pl.multiple_of(dst_token * SUBLANES, SUBLANES), SUBLANES)],
                                 sem)


def _token_tiles_wait(src_hbm, dst_vmem, first_token, n, sem):
    pltpu.make_async_copy(src_hbm.at[pl.ds(0, n * SUBLANES)],
                          dst_vmem.at[pl.ds(pl.multiple_of(first_token * SUBLANES, SUBLANES), n * SUBLANES)],
                          sem).wait()


def _norm_proj_kernel(h_ref, g_ref, w_ref, *rest, tn, with_aux):
    xn = _rms(h_ref[...], g_ref[...]).astype(BF16)
    if with_aux:
        w2_ref, o_ref, o2_ref = rest
        o2_ref[...] = _dot(xn, w2_ref[...])
    else:
        (o_ref,) = rest
    n = w_ref.shape[1]
    for c in range(0, n, tn):
        o_ref[:, c:c + tn] = _dot(xn, w_ref[:, c:c + tn]).astype(o_ref.dtype)


def _norm_proj(h, g, w, w_aux=None, *, name):
    t, d = h.shape
    n = w.shape[1]
    tm = min(PROJ_ROWS, t)
    tn = COL_TILE if n % COL_TILE == 0 else n
    in_specs = [pl.BlockSpec((tm, d), lambda i: (i, 0)), _resident((1, d)), _resident((d, n))]
    out_shape = [jax.ShapeDtypeStruct((t, n), BF16)]
    out_specs = [pl.BlockSpec((tm, n), lambda i: (i, 0))]
    args = [h, g.reshape(1, d), w]
    if w_aux is not None:
        in_specs.append(_resident(w_aux.shape))
        out_shape.append(jax.ShapeDtypeStruct((t, w_aux.shape[1]), F32))
        out_specs.append(pl.BlockSpec((tm, w_aux.shape[1]), lambda i: (i, 0)))
        args.append(w_aux)
    out = pl.pallas_call(
        functools.partial(_norm_proj_kernel, tn=tn, with_aux=w_aux is not None),
        grid=(t // tm,), in_specs=in_specs, out_specs=out_specs, out_shape=out_shape,
        compiler_params=_cparams("parallel"), name=name)(*args)
    return out if w_aux is not None else out[0]


def _proj_residual_kernel(*refs):
    n = (len(refs) - 2) // 2
    y_refs, w_refs, h_ref, o_ref = refs[:n], refs[n:2 * n], refs[2 * n], refs[2 * n + 1]
    acc = h_ref[...]
    for y_ref, w_ref in zip(y_refs, w_refs):
        acc = acc + _dot(y_ref[...], w_ref[...])
    o_ref[...] = acc


def _proj_residual(ys, w, h, *, name):
    t, d = h.shape
    tm = min(PROJ_ROWS, t)
    ws, r0 = [], 0
    for y in ys:
        ws.append(w[r0:r0 + y.shape[1]])
        r0 += y.shape[1]
    return pl.pallas_call(
        _proj_residual_kernel, grid=(t // tm,),
        in_specs=([pl.BlockSpec((tm, y.shape[1]), lambda i: (i, 0)) for y in ys]
                  + [_resident(wi.shape) for wi in ws] + [pl.BlockSpec((tm, d), lambda i: (i, 0))]),
        out_specs=pl.BlockSpec((tm, d), lambda i: (i, 0)),
        out_shape=jax.ShapeDtypeStruct((t, d), F32),
        compiler_params=_cparams("parallel"), name=name)(*ys, *ws, h)


def _ffn_ple_proj_kernel(h_ref, gf_ref, wg_ref, wu_ref, wd_ref, gp_ref, p_ref, wp_ref, wpg_ref,
                         gm_ref, win_ref, h_out_ref, u_ref, *, tf, tn):
    tm = h_ref.shape[0]
    nc = tm // FFN_CHUNKS
    rows = [pl.ds(c * nc, nc) for c in range(FFN_CHUNKS)]
    xs = [h_ref[rs, :] for rs in rows]
    xns = [_rms(x, gf_ref[...]).astype(BF16) for x in xs]
    accs = list(xs)
    for c in range(0, wg_ref.shape[1], tf):
        gates = [_dot(xn, wg_ref[:, c:c + tf]) for xn in xns]
        ups = [_dot(xn, wu_ref[:, c:c + tf]) for xn in xns]
        hids = [(_silu(g) * u).astype(BF16) for g, u in zip(gates, ups)]
        accs = [a + _dot(hd, wd_ref[c:c + tf, :]) for a, hd in zip(accs, hids)]
    xn2 = [_rms(x, gp_ref[...]).astype(BF16) for x in accs]
    pgate = [_sigmoid(_dot(xn, wpg_ref[...])) for xn in xn2]
    emb = [_dot(p_ref[rs, :].astype(BF16), wp_ref[...]) for rs in rows]
    h3 = [x + e * g for x, e, g in zip(accs, emb, pgate)]
    for rs, x in zip(rows, h3):
        h_out_ref[rs, :] = x
    xn3 = [_rms(x, gm_ref[...]).astype(BF16) for x in h3]
    for c in range(0, win_ref.shape[1], tn):
        for rs, xn in zip(rows, xn3):
            u_ref[rs, c:c + tn] = _dot(xn, win_ref[:, c:c + tn]).astype(u_ref.dtype)


def _ff_tile(f, cap):
    best = LANES
    for c in range(LANES, cap + 1, LANES):
        if f % c == 0:
            best = c
    return best


def _ffn_ple_proj(h, g_ffn, wg, wu, wd, g_ple, p, layer, wp, wpg, g_mix, w_in, *, name):
    t, d = h.shape
    f = wg.shape[1]
    e = p.shape[2]
    n = w_in.shape[1]
    tm = min(ROW_TILE, t)
    tn = COL_TILE if n % COL_TILE == 0 else n
    row = lambda w: pl.BlockSpec((tm, w), lambda i: (i, 0))
    return pl.pallas_call(
        functools.partial(_ffn_ple_proj_kernel, tf=_ff_tile(f, 1536), tn=tn), grid=(t // tm,),
        in_specs=[row(d), _resident((1, d)), _resident((d, f)), _resident((d, f)), _resident((f, d)),
                  _resident((1, d)), pl.BlockSpec((None, tm, e), lambda i: (layer, i, 0)),
                  _resident((e, d)), _resident((d, d)), _resident((1, d)), _resident((d, n))],
        out_specs=[row(d), row(n)],
        out_shape=[jax.ShapeDtypeStruct((t, d), F32), jax.ShapeDtypeStruct((t, n), BF16)],
        compiler_params=_cparams("parallel"), name=name,
    )(h, g_ffn.reshape(1, d), wg, wu, wd, g_ple.reshape(1, d), p, wp, wpg, g_mix.reshape(1, d), w_in)


def _unit_lower_inverses(mats):
    n = mats[0].shape[0]
    row = lax.broadcasted_iota(jnp.int32, (n, n), 0)
    col = lax.broadcasted_iota(jnp.int32, (n, n), 1)
    eye = (row == col).astype(F32)
    same16 = (row // 16) == (col // 16)
    same32 = (row // 32) == (col // 32)
    off32 = jnp.logical_and(same32, jnp.logical_not(same16))

    def mm(ps, qs):
        return [_dot(p.astype(BF16), q.astype(BF16)) for p, q in zip(ps, qs)]

    ad = [jnp.where(same16, a, 0.0) for a in mats]
    a2 = mm(ad, ad)
    x = mm([eye - t for t in ad], [eye + t for t in a2])
    a4 = mm(a2, a2)
    x = mm(x, [eye + t for t in a4])
    a8 = mm(a4, a4)
    x = mm(x, [eye + t for t in a8])
    y = mm([jnp.where(off32, a, 0.0) for a in mats], x)
    x = [t - c for t, c in zip(x, mm(x, y))]
    y = mm([jnp.where(same32, 0.0, a) for a in mats], x)
    return [t - c for t, c in zip(x, mm(x, y))]


def _seq_tiles(pad_ref, s):
    rt = min(ELEM_ROWS, s)
    n_tiles = s // rt

    def rows(i):
        return pl.ds(pl.multiple_of(i * rt, rt), rt)

    def fill_pad(fn):
        pad_ref[0:TOP_PAD, :] = jnp.zeros((TOP_PAD, pad_ref.shape[1]), F32)

        def body(i, c):
            r = pl.multiple_of(i * rt, rt)
            pad_ref[pl.ds(TOP_PAD + r, rt), :] = fn(rows(i))
            return c
        lax.fori_loop(0, n_tiles, body, 0)

    def conv_tile(i, w_ref, width):
        tap = _delayed(pad_ref, pl.multiple_of(i * rt, rt), rt, width - 1)
        acc = None
        for j in range(width):
            term = w_ref[j:j + 1, :] * tap(width - 1 - j)
            acc = term if acc is None else acc + term
        return acc

    return n_tiles, rows, fill_pad, conv_tile


def _gated_conv_kernel(bg_ref, cg_ref, xin_ref, w_ref, o_ref, pad_ref):
    n_tiles, rows, fill_pad, conv_tile = _seq_tiles(pad_ref, bg_ref.shape[0])
    fill_pad(lambda rs: cg_ref[rs, :].astype(F32) * xin_ref[rs, :].astype(F32))

    def body(i, c):
        acc = conv_tile(i, w_ref, CONV_A_WIDTH)
        o_ref[rows(i), :] = (bg_ref[rows(i), :].astype(F32) * acc).astype(o_ref.dtype)
        return c
    lax.fori_loop(0, n_tiles, body, 0)


def _gated_conv(u, conv_a, batch, seq):
    cw = conv_a.shape[1]
    return pl.pallas_call(
        _gated_conv_kernel, grid=(batch,),
        in_specs=[pl.BlockSpec((seq, cw), lambda b, j=j: (b, j)) for j in range(3)] + [_resident(conv_a.shape)],
        out_specs=pl.BlockSpec((seq, cw), lambda b: (b, 0)),
        out_shape=jax.ShapeDtypeStruct((batch * seq, cw), BF16),
        scratch_shapes=[pltpu.VMEM((seq + TOP_PAD, cw), F32)],
        compiler_params=_cparams("parallel"), name="l0_gated_conv")(u, u, u, conv_a)


def _l0_mixer_kernel(q_ref, k_ref, v_ref, og_ref, ab_ref, wq_ref, wk_ref, wv_ref, alog_ref, dtb_ref, gn_ref,
                     o_ref,
                     pad_ref, qs_ref, ks_ref, vs_ref, us_ref, ws_ref, qk_ref, egl_ref, st_ref):
    s = q_ref.shape[0]
    heads = GDN_HEADS
    dk = q_ref.shape[1] // heads
    c64 = GDN_CHUNK
    n_chunks = s // c64
    n_tiles, rows, fill_pad, conv_tile = _seq_tiles(pad_ref, s)

    def l2n(x, scale):
        parts = []
        for h in range(heads):
            xh = x[:, h * dk:(h + 1) * dk]
            inv = lax.rsqrt(jnp.sum(xh * xh, axis=-1, keepdims=True) + EPS)
            parts.append(xh * (inv * scale))
        return jnp.concatenate(parts, axis=1)

    for src_ref, w_ref, dst_ref, post in (
            (q_ref, wq_ref, qs_ref, lambda x: l2n(x, dk ** -0.5)),
            (k_ref, wk_ref, ks_ref, lambda x: l2n(x, 1.0)),
            (v_ref, wv_ref, vs_ref, lambda x: x)):
        fill_pad(lambda rs, src_ref=src_ref: src_ref[rs, :].astype(F32))

        def conv_body(i, c, w_ref=w_ref, dst_ref=dst_ref, post=post):
            dst_ref[rows(i), :] = post(_silu(conv_tile(i, w_ref, GDN_CONV_WIDTH)))
            return c
        lax.fori_loop(0, n_tiles, conv_body, 0)

    ri = lax.broadcasted_iota(jnp.int32, (c64, c64), 0)
    ci = lax.broadcasted_iota(jnp.int32, (c64, c64), 1)
    tril = ri >= ci
    strict = ri > ci
    ltri = tril.astype(F32)

    group = PREP_CHUNKS if n_chunks % PREP_CHUNKS == 0 else 1

    def chunk_prep(cg, carry):
        chunks = []
        for cc in range(group):
            c = cg * group + cc
            rs = pl.ds(pl.multiple_of(c * c64, c64), c64)
            chunks.append((c, rs, ab_ref[rs, :], qs_ref[rs, :], ks_ref[rs, :], vs_ref[rs, :]))
        inst = []
        gcs = [_dot(ltri, -jnp.exp(alog_ref[...]) * _softplus(ab + dtb_ref[...]), precision=HIGHEST)
               for _, _, ab, _, _, _ in chunks]
        for (c, rs, ab, q_all, k_all, v_all), gc in zip(chunks, gcs):
            beta = _sigmoid(ab)
            gct = gc.T
            for h in range(heads):
                hs = slice(h * dk, (h + 1) * dk)
                gcol = gc[:, h:h + 1]
                glast = gc[c64 - 1:c64, h:h + 1]
                bcol = beta[:, heads + h:heads + h + 1]
                decay = jnp.where(tril, jnp.exp(jnp.where(tril, gcol - gct[h:h + 1, :], 0.0)), 0.0)
                kh, qh, vh = k_all[:, hs], q_all[:, hs], v_all[:, hs]
                kb = kh * bcol
                egc = jnp.exp(gcol)
                inst.append(dict(decay=decay, kh=kh, qh=qh, kb=kb, egc=egc,
                                 rhs=jnp.concatenate([vh * bcol, kb * egc], axis=1).astype(BF16),
                                 kd=kh * jnp.exp(glast - gcol),
                                 eg=jnp.broadcast_to(jnp.exp(glast), (SUBLANES, dk))))
        kqs = [_dot_nt(jnp.concatenate([t["kb"], t["qh"]], axis=0).astype(BF16), t["kh"].astype(BF16))
               for t in inst]
        minvs = _unit_lower_inverses([jnp.where(strict, kq[0:c64] * t["decay"], 0.0)
                                      for kq, t in zip(kqs, inst)])
        uws = [_dot(m.astype(BF16), t["rhs"]) for m, t in zip(minvs, inst)]
        for ci, (c, rs, _, _, _, _) in enumerate(chunks):
            sl = slice(ci * heads, (ci + 1) * heads)
            us_ref[rs, :] = jnp.concatenate([uw[:, 0:dk] for uw in uws[sl]], axis=1)
            ws_ref[rs, :] = jnp.concatenate([uw[:, dk:2 * dk] for uw in uws[sl]], axis=1).astype(BF16)
            qs_ref[rs, :] = jnp.concatenate([t["qh"] * t["egc"] for t in inst[sl]], axis=1)
            ks_ref[rs, :] = jnp.concatenate([t["kd"] for t in inst[sl]], axis=1)
            qk_ref[rs, :] = jnp.concatenate([kq[c64:2 * c64] * t["decay"]
                                             for kq, t in zip(kqs[sl], inst[sl])], axis=1).astype(BF16)
            e0 = pl.multiple_of(c * (heads * SUBLANES), heads * SUBLANES)
            egl_ref[pl.ds(e0, heads * SUBLANES), :] = jnp.concatenate([t["eg"] for t in inst[sl]], axis=0)
        return carry
    lax.fori_loop(0, n_chunks // group, chunk_prep, 0)

    st_ref[...] = jnp.zeros(st_ref.shape, F32)

    def scan(c, carry):
        rs = pl.ds(pl.multiple_of(c * c64, c64), c64)
        e0 = pl.multiple_of(c * (heads * SUBLANES), heads * SUBLANES)
        w_all, qg_all, u_all, kd_all = ws_ref[rs, :], qs_ref[rs, :], us_ref[rs, :], ks_ref[rs, :]
        qk_all = qk_ref[rs, :]
        eg_all = egl_ref[pl.ds(e0, heads * SUBLANES), :]
        hsl = [slice(h * dk, (h + 1) * dk) for h in range(heads)]
        sts = [st_ref[h] for h in range(heads)]
        wqs = [_dot(jnp.concatenate([w_all[:, hs], qg_all[:, hs].astype(BF16)], axis=0), st.astype(BF16))
               for hs, st in zip(hsl, sts)]
        vbs = [(u_all[:, hs] - wq[0:c64]).astype(BF16) for hs, wq in zip(hsl, wqs)]
        upd = [_dot_tn(kd_all[:, hs].astype(BF16), vb) for hs, vb in zip(hsl, vbs)]
        intra = [_dot(qk_all[:, h * c64:(h + 1) * c64], vbs[h]) for h in range(heads)]
        for h in range(heads):
            st_ref[h] = sts[h] * eg_all[h * SUBLANES:h * SUBLANES + 1, :] + upd[h]
        vs_ref[rs, :] = jnp.concatenate([wq[c64:2 * c64] + o for wq, o in zip(wqs, intra)], axis=1)
        return carry
    lax.fori_loop(0, n_chunks, scan, 0)

    def finish(i, c):
        o = vs_ref[rows(i), :]
        og = og_ref[rows(i), :].astype(F32)
        parts = []
        for h in range(heads):
            oh = o[:, h * dk:(h + 1) * dk]
            parts.append(_rms(oh, gn_ref[...]))
        y = jnp.concatenate(parts, axis=1) * _silu(og)
        o_ref[rows(i), :] = y.astype(o_ref.dtype)
        return c
    lax.fori_loop(0, n_tiles, finish, 0)


def _l0_mixer(u, ab, gdn_conv, a_log, dt_bias, gdn_norm_g, batch, seq):
    dk = gdn_norm_g.shape[0]
    heads = GDN_HEADS
    cw = heads * dk

    def lane_row(x):
        return jnp.zeros((1, LANES), F32).at[0, :x.shape[0]].set(x)

    in_specs = [pl.BlockSpec((seq, cw), lambda b, j=j: (b, j)) for j in range(3, 7)]
    in_specs += [pl.BlockSpec((seq, LANES), lambda b: (b, 0))]
    in_specs += [pl.BlockSpec((GDN_CONV_WIDTH, cw), lambda b, j=j: (0, j), pipeline_mode=pl.Buffered(1))
                 for j in range(3)]
    in_specs += [_resident((1, LANES)), _resident((1, LANES)), _resident((1, dk))]
    big = pltpu.VMEM((seq, cw), F32)
    scratch = [pltpu.VMEM((seq + TOP_PAD, cw), F32), big, big, big, big, pltpu.VMEM((seq, cw), BF16),
               pltpu.VMEM((seq, heads * GDN_CHUNK), BF16),
               pltpu.VMEM((seq // GDN_CHUNK * heads * SUBLANES, dk), F32),
               pltpu.VMEM((heads, dk, dk), F32)]
    return pl.pallas_call(
        _l0_mixer_kernel, grid=(batch,), in_specs=in_specs,
        out_specs=pl.BlockSpec((seq, cw), lambda b: (b, 0)),
        out_shape=jax.ShapeDtypeStruct((batch * seq, cw), BF16),
        scratch_shapes=scratch, compiler_params=_cparams("parallel"), name="l0_deltanet",
    )(u, u, u, u, ab, gdn_conv, gdn_conv, gdn_conv,
      lane_row(a_log), lane_row(dt_bias), gdn_norm_g.reshape(1, dk))


def _bias_table_kernel(tab_ref, o_ref, *, t):
    h = pl.program_id(0)
    m = pl.program_id(1)
    ri = lax.broadcasted_iota(jnp.int32, (t, t), 0)
    ci = lax.broadcasted_iota(jnp.int32, (t, t), 1)
    rel = m * t + ci - ri
    n = jnp.maximum(rel, 0)
    max_exact = NUM_BUCKETS // 2
    nf = jnp.maximum(n, 1).astype(F32)
    large = max_exact + (jnp.log(nf / max_exact) / math.log(MAX_DISTANCE / max_exact)
                         * (NUM_BUCKETS - max_exact)).astype(jnp.int32)
    large = jnp.minimum(large, NUM_BUCKETS - 1)
    bucket = jnp.where(n < max_exact, n, large)
    bias = jnp.zeros((t, t), F32)
    for b in range(NUM_BUCKETS):
        bias = jnp.where(bucket == b, tab_ref[b, h], bias)
    o_ref[...] = jnp.where(rel >= 0, bias * LOG2E, MASK_VALUE)


def _bias_table(rel_bias, seq, t):
    nb = seq // t
    heads = rel_bias.shape[1]
    return pl.pallas_call(
        functools.partial(_bias_table_kernel, t=t), grid=(heads, nb),
        in_specs=[pl.BlockSpec(memory_space=pltpu.SMEM)],
        out_specs=pl.BlockSpec((None, None, t, t), lambda h, m: (h, m, 0, 0)),
        out_shape=jax.ShapeDtypeStruct((heads, nb, t, t), F32),
        compiler_params=_cparams("parallel", "parallel"), name="rel_bias_table")(rel_bias)


def _attn_kernel(q_ref, k_ref, v_ref, tb_ref, lam_ref, gn_ref, o_ref, m_ref, l_ref, acc_ref,
                 *, t, heads, lambda_init):
    qi = pl.program_id(2)
    dh2 = q_ref.shape[1] // heads
    dh = dh2 // 2
    lane = lax.broadcasted_iota(jnp.int32, (t, dh2), 1)
    qqs = []
    for h in range(heads):
        qf = q_ref[:, h * dh2:(h + 1) * dh2].astype(F32) * (dh ** -0.5 * LOG2E)
        qqs.append(jnp.concatenate([jnp.where(lane < dh, qf, 0.0), jnp.where(lane >= dh, qf, 0.0)],
                                   axis=0).astype(BF16))

    hs = [slice(h * dh2, (h + 1) * dh2) for h in range(heads)]

    def update(j0, nblk):
        ks = pl.ds(pl.multiple_of(j0 * t, t), nblk * t)
        k_all = k_ref[ks, :]
        v_all = v_ref[ks, :]
        s_t = [_dot_nt(k_all[:, hs[h]], qqs[h]) for h in range(heads)]
        ps, alphas = [], []
        for h in range(heads):
            m = m_ref[h, 0:1, :]
            b = jnp.concatenate([tb_ref[h, qi - j0 - i] for i in range(nblk)], axis=0)
            s = jnp.concatenate([s_t[h][:, 0:t] + b, s_t[h][:, t:2 * t] + b], axis=1)
            m_new = jnp.maximum(m, jnp.max(s, axis=0, keepdims=True))
            alpha = jnp.exp2(m - m_new)
            p = jnp.exp2(s - m_new)
            m_ref[h] = jnp.broadcast_to(m_new, m_ref.shape[1:])
            l_ref[h] = jnp.broadcast_to(alpha * l_ref[h, 0:1, :] + jnp.sum(p, axis=0, keepdims=True),
                                        l_ref.shape[1:])
            alphas.append(alpha)
            ps.append(p.astype(BF16))
        pv = [_dot_tn(v_all[:, hs[h]], ps[h]) for h in range(heads)]
        for h in range(heads):
            acc_ref[h] = alphas[h] * acc_ref[h] + pv[h]

    m_ref[...] = jnp.full(m_ref.shape, MASK_VALUE, F32)
    l_ref[...] = jnp.zeros(l_ref.shape, F32)
    acc_ref[...] = jnp.zeros(acc_ref.shape, F32)
    odd = (qi + 1) % 2

    @pl.when(odd == 1)
    def _():
        update(0, 1)

    def pair(i, c):
        update(odd + 2 * i, 2)
        return c
    lax.fori_loop(0, (qi + 1) // 2, pair, 0)
    lp = lam_ref[...]
    lam = (jnp.exp(jnp.sum(lp[0:1] * lp[1:2], axis=-1, keepdims=True))
           - jnp.exp(jnp.sum(lp[2:3] * lp[3:4], axis=-1, keepdims=True)) + lambda_init)
    for h in range(heads):
        o12 = acc_ref[h] / l_ref[h, 0:1, :]
        o = (o12[:, 0:t] - lam * o12[:, t:2 * t]).T
        o_ref[:, h * dh2:(h + 1) * dh2] = (_rms(o, gn_ref[...]) * (1.0 - lambda_init)).astype(o_ref.dtype)


def _diff_attention(u, table, lam_params, norm_g, batch, seq, lambda_init):
    heads = DIFF_HEADS
    dh2 = norm_g.shape[0]
    hg = ATTN_HEADS_PER_STEP
    w = hg * dh2
    ng = heads // hg
    t = table.shape[2]
    nq = seq // t
    return pl.pallas_call(
        functools.partial(_attn_kernel, t=t, heads=hg, lambda_init=lambda_init),
        grid=(batch, ng, nq),
        in_specs=[pl.BlockSpec((t, w), lambda b, g, i: (b * nq + i, g)),
                  pl.BlockSpec((seq, w), lambda b, g, i: (b, ng + g)),
                  pl.BlockSpec((seq, w), lambda b, g, i: (b, 2 * ng + g)),
                  pl.BlockSpec((hg, nq, t, t), lambda b, g, i: (g, 0, 0, 0)),
                  _resident(lam_params.shape), _resident((1, dh2))],
        out_specs=pl.BlockSpec((t, w), lambda b, g, i: (b * nq + i, g)),
        out_shape=jax.ShapeDtypeStruct((batch * seq, heads * dh2), BF16),
        scratch_shapes=[pltpu.VMEM((hg, SUBLANES, 2 * t), F32), pltpu.VMEM((hg, SUBLANES, 2 * t), F32),
                        pltpu.VMEM((hg, dh2, 2 * t), F32)],
        compiler_params=_cparams("parallel", "parallel", "arbitrary"), name="diff_attention",
    )(u, u, u, table, lam_params, norm_g.reshape(1, dh2))


def _conformer_kernel(ga_ref, gb_ref, w_ref, b_ref, lg_ref, lb_ref, o_ref, pad_ref):
    s, c = ga_ref.shape
    rt = min(ELEM_ROWS, s)
    pad_ref[0:TOP_PAD, :] = jnp.zeros((TOP_PAD, c), F32)

    def glu(i, carry):
        r = pl.multiple_of(i * rt, rt)
        rs = pl.ds(r, rt)
        pad_ref[pl.ds(TOP_PAD + r, rt), :] = ga_ref[rs, :].astype(F32) * _sigmoid(gb_ref[rs, :].astype(F32))
        return carry
    lax.fori_loop(0, s // rt, glu, 0)

    ct = CONF_ROWS

    def conv(i, carry):
        r = pl.multiple_of(i * ct, ct)
        tap = _delayed(pad_ref, r, ct, CONF_WIDTH - 1)
        acc = jnp.broadcast_to(b_ref[...], (ct, c))
        for j in range(CONF_WIDTH):
            acc = acc + w_ref[j:j + 1, :] * tap(CONF_WIDTH - 1 - j)
        mu = jnp.mean(acc, axis=-1, keepdims=True)
        xc = acc - mu
        var = jnp.mean(xc * xc, axis=-1, keepdims=True)
        y = xc * lax.rsqrt(var + EPS) * lg_ref[...] + lb_ref[...]
        o_ref[pl.ds(r, ct), :] = _silu(y).astype(o_ref.dtype)
        return carry
    lax.fori_loop(0, s // ct, conv, 0)


def _conformer(u, w, b, ln_g, ln_b, batch, seq, col0):
    c = w.shape[1]
    return pl.pallas_call(
        _conformer_kernel, grid=(batch,),
        in_specs=[pl.BlockSpec((seq, c), lambda i: (i, col0)),
                  pl.BlockSpec((seq, c), lambda i: (i, col0 + 1)),
                  _resident(w.shape), _resident((1, c)), _resident((1, c)), _resident((1, c))],
        out_specs=pl.BlockSpec((seq, c), lambda i: (i, 0)),
        out_shape=jax.ShapeDtypeStruct((batch * seq, c), BF16),
        scratch_shapes=[pltpu.VMEM((seq + TOP_PAD, c), F32)],
        compiler_params=_cparams("parallel"), name="conformer_conv",
    )(u, u, w, b.reshape(1, c), ln_g.reshape(1, c), ln_b.reshape(1, c))


def _proj_router_kernel(y1_ref, y2_ref, w1_ref, w2_ref, h_ref, g_ref, wrt_ref, upper_ref,
                        h_out_ref, hn_ref, route_ref, gate_ref, cnt_ref, carry_ref):
    i = pl.program_id(0)
    tm = h_ref.shape[0]
    n_e = N_EXPERTS

    @pl.when(i == 0)
    def _():
        carry_ref[...] = jnp.zeros(carry_ref.shape, F32)

    h = h_ref[...] + _dot(y1_ref[...], w1_ref[...]) + _dot(y2_ref[...], w2_ref[...])
    h_out_ref[...] = h
    xn = _rms(h, g_ref[...])
    _store_token_tiles(hn_ref, 0, xn)

    logits = _dot_nt(wrt_ref[...], xn.astype(BF16))[0:n_e, :]
    sub = lax.broadcasted_iota(jnp.int32, logits.shape, 0)
    m1 = jnp.max(logits, axis=0, keepdims=True)
    i1 = jnp.min(jnp.where(logits == m1, sub, n_e), axis=0, keepdims=True)
    rest = jnp.where(sub == i1, -jnp.inf, logits)
    m2 = jnp.max(rest, axis=0, keepdims=True)
    i2 = jnp.min(jnp.where(rest == m2, sub, n_e), axis=0, keepdims=True)
    e = jnp.exp(m2 - m1)
    g1 = 1.0 / (1.0 + e)
    oh1 = sub == i1
    oh2 = sub == i2
    both = oh1.astype(F32) + oh2.astype(F32)
    csum = _dot(both.astype(BF16), upper_ref[...])
    carry = carry_ref[:, 0:1]
    before = csum - both + carry
    total = carry + csum[:, tm - 1:tm]
    carry_ref[...] = jnp.broadcast_to(total, carry_ref.shape)
    cnt_ref[...] = jnp.broadcast_to(total, cnt_ref.shape).astype(jnp.int32)
    r1 = jnp.sum(jnp.where(oh1, before, 0.0), axis=0, keepdims=True).astype(jnp.int32)
    r2 = jnp.sum(jnp.where(oh2, before, 0.0), axis=0, keepdims=True).astype(jnp.int32)
    route_ref[...] = jnp.where(sub == 0, i1, jnp.where(sub == 1, i2, jnp.where(sub == 2, r1,
                               jnp.where(sub == 3, r2, 0))))
    grow = jnp.where(sub == 0, g1, jnp.where(sub == 1, e * g1, 0.0))
    sel = (lax.broadcasted_iota(jnp.int32, (n_e, LANES), 0)
           == lax.broadcasted_iota(jnp.int32, (n_e, LANES), 1)).astype(F32)
    gate_ref[...] = lax.dot_general(grow, sel, (((0,), (0,)), ((), ())), precision=HIGHEST,
                                    preferred_element_type=F32)


def _proj_router(y1, y2, w, h, g, wr):
    t, d = h.shape
    k1, k2 = y1.shape[1], y2.shape[1]
    tm = min(ROW_TILE, t)
    wrt = jnp.zeros((LANES, d), BF16).at[:N_EXPERTS].set(wr.T.astype(BF16))
    upper = (jnp.arange(tm)[:, None] <= jnp.arange(tm)[None, :]).astype(BF16)
    row = lambda width: pl.BlockSpec((tm, width), lambda i: (i, 0))
    return pl.pallas_call(
        _proj_router_kernel, grid=(t // tm,),
        in_specs=[row(k1), row(k2), _resident((k1, d)), _resident((k2, d)), row(d),
                  _resident((1, d)), _resident((LANES, d)), _resident((tm, tm))],
        out_specs=[row(d), pl.BlockSpec((tm * d // LANES, LANES), lambda i: (i, 0)),
                   pl.BlockSpec((N_EXPERTS, tm), lambda i: (0, i)), row(LANES),
                   pl.BlockSpec((N_EXPERTS, LANES), lambda i: (0, 0))],
        out_shape=[jax.ShapeDtypeStruct((t, d), F32), jax.ShapeDtypeStruct((t * d // LANES, LANES), F32),
                   jax.ShapeDtypeStruct((N_EXPERTS, t), jnp.int32), jax.ShapeDtypeStruct((t, LANES), F32),
                   jax.ShapeDtypeStruct((N_EXPERTS, LANES), jnp.int32)],
        scratch_shapes=[pltpu.VMEM((N_EXPERTS, LANES), F32)],
        compiler_params=_cparams("arbitrary"), name="l1_out_proj_router",
    )(y1, y2, w[:k1], w[k1:], h, g.reshape(1, d), wrt, upper)


def _dispatch_kernel(dest_hbm, pad_hbm, x_ref, xs_hbm, idx_smem, pad_smem, zero_ref, sem_idx, sem_x):
    i = pl.program_id(0)
    n = x_ref.shape[0] // SUBLANES
    m = TOP_K * n
    cp = pltpu.make_async_copy(dest_hbm.at[i], idx_smem, sem_idx)
    cp.start()
    cp.wait()

    def body(r, c):
        for k in range(TOP_K):
            _token_tile_copy(x_ref, r, xs_hbm, idx_smem[k * n + r], sem_x).start(priority=k % 2)
        return c
    lax.fori_loop(0, n, body, 0, unroll=8)
    for _ in range(TOP_K):
        pltpu.make_async_copy(x_ref, xs_hbm.at[pl.ds(0, n * SUBLANES)], sem_x).wait()

    @pl.when(i == pl.num_programs(0) - 1)
    def _():
        n_pad = pad_smem.shape[0]
        cp = pltpu.make_async_copy(pad_hbm, pad_smem, sem_idx)
        cp.start()
        cp.wait()
        zero_ref[...] = jnp.zeros(zero_ref.shape, F32)

        def fill(j, c):
            for q in range(2):
                _token_tile_copy(zero_ref, 0, xs_hbm, pad_smem[2 * j + q], sem_x).start(priority=q)
            return c
        lax.fori_loop(0, n_pad // 2, fill, 0, unroll=8)
        pltpu.make_async_copy(xs_hbm.at[pl.ds(0, n_pad * SUBLANES)], xs_hbm.at[pl.ds(0, n_pad * SUBLANES)],
                              sem_x).wait()


def _dispatch(hn_tiles, dest, pad_rows, t, d):
    n = min(DISPATCH_ROWS, t)
    steps = t // n
    tpt = d // LANES
    p = t * TOP_K + N_EXPERTS * MOE_ROWS
    dest_steps = jnp.concatenate([dest[k].reshape(steps, n) for k in range(TOP_K)], axis=1)
    return pl.pallas_call(
        _dispatch_kernel, grid=(steps,),
        in_specs=[pl.BlockSpec(memory_space=pl.ANY), pl.BlockSpec(memory_space=pl.ANY),
                  pl.BlockSpec((n * tpt, LANES), lambda i: (i, 0))],
        out_specs=pl.BlockSpec(memory_space=pl.ANY),
        out_shape=jax.ShapeDtypeStruct((p * tpt, LANES), F32),
        scratch_shapes=[pltpu.SMEM((TOP_K * n,), jnp.int32), pltpu.SMEM(pad_rows.shape, jnp.int32),
                        pltpu.VMEM((tpt, LANES), F32), pltpu.SemaphoreType.DMA, pltpu.SemaphoreType.DMA],
        compiler_params=_cparams("arbitrary"), name="moe_dispatch")(dest_steps, pad_rows, hn_tiles)


def _dispatch_plan(route, counts, t):
    tm = MOE_ROWS
    p = t * TOP_K + N_EXPERTS * tm
    padded = (counts + tm - 1) // tm * tm
    pend = jnp.cumsum(padded)
    pstart = pend - padded
    experts = jnp.arange(N_EXPERTS, dtype=jnp.int32)[:, None]
    dest = jnp.stack([jnp.sum(jnp.where(route[k][None, :] == experts, pstart[:, None], 0), axis=0)
                      + route[TOP_K + k] for k in range(TOP_K)])
    n_pad = p - t * TOP_K
    gaps = jnp.concatenate([padded - counts, (p - pend[-1])[None]])
    gap_end = jnp.cumsum(gaps)
    gap_row0 = jnp.concatenate([pstart + counts, pend[-1:]])
    j = jnp.arange(n_pad, dtype=jnp.int32)
    which = jnp.searchsorted(gap_end, j, side='right', method='compare_all')
    pad_rows = (gap_row0[which] + j - (gap_end - gaps)[which]).astype(jnp.int32)
    nb = p // tm
    blk_e = jnp.minimum(jnp.searchsorted(pend, jnp.arange(nb, dtype=jnp.int32) * tm, side='right',
                                         method='compare_all'), N_EXPERTS - 1).astype(jnp.int32)
    return dest.astype(jnp.int32), pad_rows, blk_e, (pend[-1:] // tm).astype(jnp.int32)


def _experts_kernel(blk_e_ref, used_ref, x_ref, wgu_ref, wd_ref, o_ref, hid_ref):
    f = wd_ref.shape[0]
    tm = hid_ref.shape[0]
    live = pl.program_id(0) < used_ref[0]

    @pl.when(live)
    def _():
        x = _load_token_tiles(x_ref, 0, tm).astype(BF16)
        for c in range(0, f, MOE_SUB_TILE):
            hid_ref[:, c:c + MOE_SUB_TILE] = (
                _silu(_dot(x, wgu_ref[:, c:c + MOE_SUB_TILE]))
                * _dot(x, wgu_ref[:, f + c:f + c + MOE_SUB_TILE])).astype(BF16)
        _store_token_tiles(o_ref, 0, _dot(hid_ref[...], wd_ref[...]))

    @pl.when(jnp.logical_not(live))
    def _():
        o_ref[...] = jnp.zeros(o_ref.shape, F32)


def _experts(xs, blk_e, n_used, w_gate_up, w_down):
    f, d = w_down.shape[1], w_down.shape[2]
    tm = MOE_ROWS
    tpt = d // LANES
    assert f % MOE_SUB_TILE == 0
    grid_spec = pltpu.PrefetchScalarGridSpec(
        num_scalar_prefetch=2, grid=(blk_e.shape[0],),
        in_specs=[pl.BlockSpec((tm * tpt, LANES), lambda i, e, u: (i, 0)),
                  pl.BlockSpec((None, d, 2 * f), lambda i, e, u: (e[i], 0, 0), pipeline_mode=pl.Buffered(1)),
                  pl.BlockSpec((None, f, d), lambda i, e, u: (e[i], 0, 0), pipeline_mode=pl.Buffered(1))],
        out_specs=pl.BlockSpec((tm * tpt, LANES), lambda i, e, u: (i, 0)),
        scratch_shapes=[pltpu.VMEM((tm, f), BF16)])
    return pl.pallas_call(
        _experts_kernel, grid_spec=grid_spec, out_shape=jax.ShapeDtypeStruct(xs.shape, F32),
        compiler_params=_cparams("parallel"), name="moe_experts",
    )(blk_e, n_used, xs, w_gate_up, w_down)


def _combine_kernel(dest_hbm, y_hbm, gate_ref, h_ref, g_ref, p_ref, wp_ref, wg_ref, fg_ref, o_ref,
                    idx0, idx1, ybuf, sem_idx, sem_rows):
    i = pl.program_id(0)
    steps = pl.num_programs(0)
    n = h_ref.shape[0]
    m = TOP_K * n
    slot = i % 2
    idxs = (idx0, idx1)

    def idx_copy(blk, s):
        return pltpu.make_async_copy(dest_hbm.at[blk], idxs[s], sem_idx.at[s])

    def issue_tiles(s):
        def body(r, c):
            for k in range(TOP_K):
                _token_tile_copy(y_hbm, idxs[s][k * n + r], ybuf, s * m + k * n + r,
                                 sem_rows.at[s]).start(priority=k % 2)
            return c
        lax.fori_loop(0, n, body, 0, unroll=8)

    @pl.when(i == 0)
    def _():
        idx_copy(0, 0).start()
        idx_copy(0, 0).wait()
        idx_copy(1, 1).start()
        issue_tiles(0)

    for s in range(2):
        @pl.when(slot == s)
        def _(s=s):
            idx_copy(i + 1, 1 - s).wait()

            @pl.when(i + 2 <= steps)
            def _():
                idx_copy(i + 2, s).start()

            issue_tiles(1 - s)
            _token_tiles_wait(y_hbm, ybuf, s * m, m, sem_rows.at[s])

    nc = n // COMBINE_CHUNKS
    rows = [pl.ds(c * nc, nc) for c in range(COMBINE_CHUNKS)]
    xs = []
    for c, rs in enumerate(rows):
        gates = gate_ref[rs, :]
        xs.append(h_ref[rs, :] + gates[:, 0:1] * _load_token_tiles(ybuf, slot * m + c * nc, nc)
                  + gates[:, 1:2] * _load_token_tiles(ybuf, slot * m + n + c * nc, nc))
    xns = [_rms(x, g_ref[...]).astype(BF16) for x in xs]
    gate = [_sigmoid(_dot(xn, wg_ref[...])) for xn in xns]
    emb = [_dot(p_ref[rs, :].astype(BF16), wp_ref[...]) for rs in rows]
    for rs, x, e, g in zip(rows, xs, emb, gate):
        o_ref[rs, :] = _rms(x + e * g, fg_ref[...])

    @pl.when(i == steps - 1)
    def _():
        _token_tiles_wait(y_hbm, ybuf, (1 - slot) * m, m, sem_rows.at[1 - slot])


def _combine_ple_final(dest, y_tiles, gates, h, g, p, layer, wp, wg, final_g):
    t, d = h.shape
    e = p.shape[2]
    n = min(COMBINE_ROWS, t)
    m = TOP_K * n
    steps = t // n
    dest_steps = jnp.concatenate([dest[k].reshape(steps, n) for k in range(TOP_K)], axis=1)
    dest_steps = jnp.concatenate([dest_steps, jnp.zeros((1, m), jnp.int32)], axis=0)
    return pl.pallas_call(
        _combine_kernel, grid=(steps,),
        in_specs=[pl.BlockSpec(memory_space=pl.ANY), pl.BlockSpec(memory_space=pl.ANY),
                  pl.BlockSpec((n, LANES), lambda i: (i, 0)),
                  pl.BlockSpec((n, d), lambda i: (i, 0)), _resident((1, d)),
                  pl.BlockSpec((None, n, e), lambda i: (layer, i, 0)), _resident((e, d)), _resident((d, d)),
                  _resident((1, d))],
        out_specs=pl.BlockSpec((n, d), lambda i: (i, 0)),
        out_shape=jax.ShapeDtypeStruct((t, d), F32),
        scratch_shapes=[pltpu.SMEM((m,), jnp.int32), pltpu.SMEM((m,), jnp.int32),
                        pltpu.VMEM((2 * m * d // LANES, LANES), F32),
                        pltpu.SemaphoreType.DMA((2,)), pltpu.SemaphoreType.DMA((2,))],
        compiler_params=_cparams("arbitrary"), name="moe_combine_ple_final",
    )(dest_steps, y_tiles, gates, h, g.reshape(1, d), p, wp, wg, final_g.reshape(1, d))


def kernel(x, p, norm_mix_g, norm_ffn_g, norm_ple_g, final_norm_g, ev_w_in, ev_conv_a, ev_gdn_conv, ev_gdn_A_log, ev_gdn_dt_bias, ev_gdn_norm_g, ev_w_out, od_w_in, od_lambda, od_diff_norm_g, od_conf_dw_w, od_conf_dw_b, od_conf_ln_g, od_conf_ln_b, od_w_out, rel_bias, ffn_w_gate_up, ffn_w_down, moe_router, moe_w_gate_up, moe_w_down, ple_w_proj, ple_w_gate):
    batch, seq, d = x.shape
    t = batch * seq
    depth = p.shape[0]
    assert depth == 2 and seq % GDN_CHUNK == 0
    h = x.reshape(t, d)
    pf = p.reshape(depth, t, p.shape[-1])

    heads = GDN_HEADS
    n_main = ev_w_in.shape[2] - 2 * heads
    w_in = ev_w_in[0]
    w_ab = jnp.zeros((d, LANES), BF16).at[:, :2 * heads].set(w_in[:, n_main:].astype(BF16))
    u, ab = _norm_proj(h, norm_mix_g[0], w_in[:, :n_main].astype(BF16), w_ab, name="l0_in_proj")
    ya = _gated_conv(u, ev_conv_a[0], batch, seq)
    yb = _l0_mixer(u, ab, ev_gdn_conv[0], ev_gdn_A_log[0], ev_gdn_dt_bias[0], ev_gdn_norm_g[0], batch, seq)
    h = _proj_residual([ya, yb], ev_w_out[0].astype(BF16), h, name="l0_out_proj")
    f = ffn_w_down.shape[1]
    lambda_init = 0.8 - 0.6 * math.exp(-0.3 * 1)
    h, u = _ffn_ple_proj(h, norm_ffn_g[0], ffn_w_gate_up[0, :, :f].astype(BF16),
                         ffn_w_gate_up[0, :, f:].astype(BF16), ffn_w_down[0].astype(BF16),
                         norm_ple_g[0], pf, 0, ple_w_proj[0].astype(BF16), ple_w_gate[0].astype(BF16),
                         norm_mix_g[1], od_w_in[0].astype(BF16), name="l0_ffn_ple_l1_in_proj")

    table = _bias_table(rel_bias, seq, min(ATTN_TILE, seq))
    o_attn = _diff_attention(u, table, od_lambda[0], od_diff_norm_g[0], batch, seq, lambda_init)
    c_conf = od_conf_dw_w.shape[2]
    o_conf = _conformer(u, od_conf_dw_w[0], od_conf_dw_b[0], od_conf_ln_g[0], od_conf_ln_b[0],
                        batch, seq, 3 * DIFF_HEADS * od_diff_norm_g.shape[1] // c_conf)
    h, hn, route, gates, counts = _proj_router(o_attn, o_conf, od_w_out[0].astype(BF16), h, norm_ffn_g[1],
                                               moe_router[0])
    dest, pad_rows, blk_e, n_used = _dispatch_plan(route, counts[:, 0], t)
    xs = _dispatch(hn, dest, pad_rows, t, d)
    y = _experts(xs, blk_e, n_used, moe_w_gate_up[0].astype(BF16), moe_w_down[0].astype(BF16))
    out = _combine_ple_final(dest, y, gates, h, norm_ple_g[1], pf, 1, ple_w_proj[1].astype(BF16),
                             ple_w_gate[1].astype(BF16), final_norm_g)
    return out.reshape(batch, seq, d)
```

```python
import functools
import math

import jax
import jax.numpy as jnp
from jax import lax
from jax.experimental import pallas as pl
from jax.experimental.pallas import tpu as pltpu

F32 = jnp.float32
BF16 = jnp.bfloat16
HIGHEST = lax.Precision.HIGHEST

EPS = 1e-6
CONV_A_WIDTH = 3
GDN_HEADS = 4
GDN_CONV_WIDTH = 4
GDN_CHUNK = 64
DIFF_HEADS = 4
NUM_BUCKETS = 32
MAX_DISTANCE = 128
CONF_WIDTH = 31
N_EXPERTS = 8
TOP_K = 2

LANES = 128
SUBLANES = 8
VMEM_LIMIT_BYTES = 56 * 1024 * 1024
MASK_VALUE = -1e30
LOG2E = math.log2(math.e)

ROW_TILE = 512
PROJ_ROWS = 1024
COL_TILE = 512
FFN_CHUNKS = 2
ELEM_ROWS = 256
CONF_ROWS = 256
PREP_CHUNKS = 8
ATTN_TILE = 256
ATTN_HEADS_PER_STEP = 4
MOE_ROWS = 1024
DISPATCH_ROWS = 4096
MOE_SUB_TILE = 256
COMBINE_ROWS = 512
COMBINE_CHUNKS = 4
TOP_PAD = 32


def _cparams(*sem):
    return pltpu.CompilerParams(dimension_semantics=sem, vmem_limit_bytes=VMEM_LIMIT_BYTES)


def _resident(shape):
    nd = len(shape)
    return pl.BlockSpec(shape, lambda *_: (0,) * nd, pipeline_mode=pl.Buffered(1))


def _rms(x, g):
    return x * lax.rsqrt(jnp.mean(x * x, axis=-1, keepdims=True) + EPS) * g


def _sigmoid(x):
    return jax.nn.sigmoid(x)


def _silu(x):
    return x * jax.nn.sigmoid(x)


def _softplus(x):
    return jnp.maximum(x, 0.0) + jnp.log1p(jnp.exp(-jnp.abs(x)))


def _dot(a, b, **kw):
    return jnp.dot(a, b, preferred_element_type=F32, **kw)


def _dot_nt(a, b):
    return lax.dot_general(a, b, (((1,), (1,)), ((), ())), preferred_element_type=F32)


def _dot_tn(a, b):
    return lax.dot_general(a, b, (((0,), (0,)), ((), ())), preferred_element_type=F32)


def _delayed(pad_ref, r, rows, max_delay):
    lead = -(-max_delay // SUBLANES) * SUBLANES
    win = pad_ref[pl.ds(TOP_PAD + r - lead, rows + lead), :]
    rolled = {0: win}

    def tap(d):
        a, b = divmod(d, SUBLANES)
        if b not in rolled:
            rolled[b] = pltpu.roll(win, b, 0)
        start = lead - SUBLANES * a
        return rolled[b][start:start + rows, :]
    return tap


def _load_token_tiles(ref, first_token, n):
    return jnp.concatenate([ref[pl.ds(first_token * SUBLANES + s, n, stride=SUBLANES), :]
                            for s in range(SUBLANES)], axis=1)


def _store_token_tiles(ref, first_token, x):
    n = x.shape[0]
    for s in range(SUBLANES):
        ref[pl.ds(first_token * SUBLANES + s, n, stride=SUBLANES), :] = x[:, s * LANES:(s + 1) * LANES]


def _token_tile_copy(src_hbm, src_row, dst_vmem, dst_row, sem):
    return pltpu.make_async_copy(src_hbm.at[pl.ds(pl.multiple_of(src_row, SUBLANES), SUBLANES)],
                                 dst_vmem.at[pl.ds(pl.multiple_of(dst_row, SUBLANES), SUBLANES)],
                                 sem)


def _token_tiles_wait(src_hbm, dst_vmem, first_token, n, sem):
    pltpu.make_async_copy(src_hbm.at[pl.ds(0, n * SUBLANES)],
                          dst_vmem.at[pl.ds(pl.multiple_of(first_token * SUBLANES, SUBLANES), n * SUBLANES)],
                          sem).wait()


def _norm_proj_kernel(h_ref, g_ref, w_ref, *rest, tn, with_aux):
    xn = _rms(h_ref[...], g_ref[...]).astype(BF16)
    if with_aux:
        w2_ref, o_ref, o2_ref = rest
        o2_ref[...] = _dot(xn, w2_ref[...])
    else:
        (o_ref,) = rest
    n = w_ref.shape[1]
    for c in range(0, n, tn):
        o_ref[:, c:c + tn] = _dot(xn, w_ref[:, c:c + tn]).astype(o_ref.dtype)


def _norm_proj(h, g, w, w_aux=None, *, name):
    t, d = h.shape
    n = w.shape[1]
    tm = min(PROJ_ROWS, t)
    tn = COL_TILE if n % COL_TILE == 0 else n
    in_specs = [pl.BlockSpec((tm, d), lambda i: (i, 0)), _resident((1, d)), _resident((d, n))]
    out_shape = [jax.ShapeDtypeStruct((t, n), BF16)]
    out_specs = [pl.BlockSpec((tm, n), lambda i: (i, 0))]
    args = [h, g.reshape(1, d), w]
    if w_aux is not None:
        in_specs.append(_resident(w_aux.shape))
        out_shape.append(jax.ShapeDtypeStruct((t, w_aux.shape[1]), F32))
        out_specs.append(pl.BlockSpec((tm, w_aux.shape[1]), lambda i: (i, 0)))
        args.append(w_aux)
    out = pl.pallas_call(
        functools.partial(_norm_proj_kernel, tn=tn, with_aux=w_aux is not None),
        grid=(t // tm,), in_specs=in_specs, out_specs=out_specs, out_shape=out_shape,
        compiler_params=_cparams("parallel"), name=name)(*args)
    return out if w_aux is not None else out[0]


def _proj_residual_kernel(*refs):
    n = (len(refs) - 2) // 2
    y_refs, w_refs, h_ref, o_ref = refs[:n], refs[n:2 * n], refs[2 * n], refs[2 * n + 1]
    acc = h_ref[...]
    for y_ref, w_ref in zip(y_refs, w_refs):
        acc = acc + _dot(y_ref[...], w_ref[...])
    o_ref[...] = acc


def _proj_residual(ys, w, h, *, name):
    t, d = h.shape
    tm = min(PROJ_ROWS, t)
    ws, r0 = [], 0
    for y in ys:
        ws.append(w[r0:r0 + y.shape[1]])
        r0 += y.shape[1]
    return pl.pallas_call(
        _proj_residual_kernel, grid=(t // tm,),
        in_specs=([pl.BlockSpec((tm, y.shape[1]), lambda i: (i, 0)) for y in ys]
                  + [_resident(wi.shape) for wi in ws] + [pl.BlockSpec((tm, d), lambda i: (i, 0))]),
        out_specs=pl.BlockSpec((tm, d), lambda i: (i, 0)),
        out_shape=jax.ShapeDtypeStruct((t, d), F32),
        compiler_params=_cparams("parallel"), name=name)(*ys, *ws, h)


def _ffn_ple_proj_kernel(h_ref, gf_ref, wg_ref, wu_ref, wd_ref, gp_ref, p_ref, wp_ref, wpg_ref,
                         gm_ref, win_ref, h_out_ref, u_ref, *, tf, tn):
    tm = h_ref.shape[0]
    nc = tm // FFN_CHUNKS
    rows = [pl.ds(c * nc, nc) for c in range(FFN_CHUNKS)]
    xs = [h_ref[rs, :] for rs in rows]
    xns = [_rms(x, gf_ref[...]).astype(BF16) for x in xs]
    accs = list(xs)
    for c in range(0, wg_ref.shape[1], tf):
        gates = [_dot(xn, wg_ref[:, c:c + tf]) for xn in xns]
        ups = [_dot(xn, wu_ref[:, c:c + tf]) for xn in xns]
        hids = [(_silu(g) * u).astype(BF16) for g, u in zip(gates, ups)]
        accs = [a + _dot(hd, wd_ref[c:c + tf, :]) for a, hd in zip(accs, hids)]
    xn2 = [_rms(x, gp_ref[...]).astype(BF16) for x in accs]
    pgate = [_sigmoid(_dot(xn, wpg_ref[...])) for xn in xn2]
    emb = [_dot(p_ref[rs, :].astype(BF16), wp_ref[...]) for rs in rows]
    h3 = [x + e * g for x, e, g in zip(accs, emb, pgate)]
    for rs, x in zip(rows, h3):
        h_out_ref[rs, :] = x
    xn3 = [_rms(x, gm_ref[...]).astype(BF16) for x in h3]
    for c in range(0, win_ref.shape[1], tn):
        for rs, xn in zip(rows, xn3):
            u_ref[rs, c:c + tn] = _dot(xn, win_ref[:, c:c + tn]).astype(u_ref.dtype)


def _ff_tile(f, cap):
    best = LANES
    for c in range(LANES, cap + 1, LANES):
        if f % c == 0:
            best = c
    return best


def _ffn_ple_proj(h, g_ffn, wg, wu, wd, g_ple, p, layer, wp, wpg, g_mix, w_in, *, name):
    t, d = h.shape
    f = wg.shape[1]
    e = p.shape[2]
    n = w_in.shape[1]
    tm = min(ROW_TILE, t)
    tn = COL_TILE if n % COL_TILE == 0 else n
    row = lambda w: pl.BlockSpec((tm, w), lambda i: (i, 0))
    return pl.pallas_call(
        functools.partial(_ffn_ple_proj_kernel, tf=_ff_tile(f, 1536), tn=tn), grid=(t // tm,),
        in_specs=[row(d), _resident((1, d)), _resident((d, f)), _resident((d, f)), _resident((f, d)),
                  _resident((1, d)), pl.BlockSpec((None, tm, e), lambda i: (layer, i, 0)),
                  _resident((e, d)), _resident((d, d)), _resident((1, d)), _resident((d, n))],
        out_specs=[row(d), row(n)],
        out_shape=[jax.ShapeDtypeStruct((t, d), F32), jax.ShapeDtypeStruct((t, n), BF16)],
        compiler_params=_cparams("parallel"), name=name,
    )(h, g_ffn.reshape(1, d), wg, wu, wd, g_ple.reshape(1, d), p, wp, wpg, g_mix.reshape(1, d), w_in)


def _unit_lower_inverses(mats):
    n = mats[0].shape[0]
    row = lax.broadcasted_iota(jnp.int32, (n, n), 0)
    col = lax.broadcasted_iota(jnp.int32, (n, n), 1)
    eye = (row == col).astype(F32)
    same16 = (row // 16) == (col // 16)
    same32 = (row // 32) == (col // 32)
    off32 = jnp.logical_and(same32, jnp.logical_not(same16))

    def mm(ps, qs):
        return [_dot(p.astype(BF16), q.astype(BF16)) for p, q in zip(ps, qs)]

    ad = [jnp.where(same16, a, 0.0) for a in mats]
    a2 = mm(ad, ad)
    x = mm([eye - t for t in ad], [eye + t for t in a2])
    a4 = mm(a2, a2)
    x = mm(x, [eye + t for t in a4])
    a8 = mm(a4, a4)
    x = mm(x, [eye + t for t in a8])
    y = mm([jnp.where(off32, a, 0.0) for a in mats], x)
    x = [t - c for t, c in zip(x, mm(x, y))]
    y = mm([jnp.where(same32, 0.0, a) for a in mats], x)
    return [t - c for t, c in zip(x, mm(x, y))]


def _seq_tiles(pad_ref, s):
    rt = min(ELEM_ROWS, s)
    n_tiles = s // rt

    def rows(i):
        return pl.ds(pl.multiple_of(i * rt, rt), rt)

    def fill_pad(fn):
        pad_ref[0:TOP_PAD, :] = jnp.zeros((TOP_PAD, pad_ref.shape[1]), F32)

        def body(i, c):
            r = pl.multiple_of(i * rt, rt)
            pad_ref[pl.ds(TOP_PAD + r, rt), :] = fn(rows(i))
            return c
        lax.fori_loop(0, n_tiles, body, 0)

    def conv_tile(i, w_ref, width):
        tap = _delayed(pad_ref, pl.multiple_of(i * rt, rt), rt, width - 1)
        acc = None
        for j in range(width):
            term = w_ref[j:j + 1, :] * tap(width - 1 - j)
            acc = term if acc is None else acc + term
        return acc

    return n_tiles, rows, fill_pad, conv_tile


def _gated_conv_kernel(bg_ref, cg_ref, xin_ref, w_ref, o_ref, pad_ref):
    n_tiles, rows, fill_pad, conv_tile = _seq_tiles(pad_ref, bg_ref.shape[0])
    fill_pad(lambda rs: cg_ref[rs, :].astype(F32) * xin_ref[rs, :].astype(F32))

    def body(i, c):
        acc = conv_tile(i, w_ref, CONV_A_WIDTH)
        o_ref[rows(i), :] = (bg_ref[rows(i), :].astype(F32) * acc).astype(o_ref.dtype)
        return c
    lax.fori_loop(0, n_tiles, body, 0)


def _gated_conv(u, conv_a, batch, seq):
    cw = conv_a.shape[1]
    return pl.pallas_call(
        _gated_conv_kernel, grid=(batch,),
        in_specs=[pl.BlockSpec((seq, cw), lambda b, j=j: (b, j)) for j in range(3)] + [_resident(conv_a.shape)],
        out_specs=pl.BlockSpec((seq, cw), lambda b: (b, 0)),
        out_shape=jax.ShapeDtypeStruct((batch * seq, cw), BF16),
        scratch_shapes=[pltpu.VMEM((seq + TOP_PAD, cw), F32)],
        compiler_params=_cparams("parallel"), name="l0_gated_conv")(u, u, u, conv_a)


def _l0_mixer_kernel(q_ref, k_ref, v_ref, og_ref, ab_ref, wq_ref, wk_ref, wv_ref, alog_ref, dtb_ref, gn_ref,
                     o_ref,
                     pad_ref, qs_ref, ks_ref, vs_ref, us_ref, ws_ref, qk_ref, egl_ref, st_ref):
    s = q_ref.shape[0]
    heads = GDN_HEADS
    dk = q_ref.shape[1] // heads
    c64 = GDN_CHUNK
    n_chunks = s // c64
    n_tiles, rows, fill_pad, conv_tile = _seq_tiles(pad_ref, s)

    def l2n(x, scale):
        parts = []
        for h in range(heads):
            xh = x[:, h * dk:(h + 1) * dk]
            inv = lax.rsqrt(jnp.sum(xh * xh, axis=-1, keepdims=True) + EPS)
            parts.append(xh * (inv * scale))
        return jnp.concatenate(parts, axis=1)

    for src_ref, w_ref, dst_ref, post in (
            (q_ref, wq_ref, qs_ref, lambda x: l2n(x, dk ** -0.5)),
            (k_ref, wk_ref, ks_ref, lambda x: l2n(x, 1.0)),
            (v_ref, wv_ref, vs_ref, lambda x: x)):
        fill_pad(lambda rs, src_ref=src_ref: src_ref[rs, :].astype(F32))

        def conv_body(i, c, w_ref=w_ref, dst_ref=dst_ref, post=post):
            dst_ref[rows(i), :] = post(_silu(conv_tile(i, w_ref, GDN_CONV_WIDTH)))
            return c
        lax.fori_loop(0, n_tiles, conv_body, 0)

    ri = lax.broadcasted_iota(jnp.int32, (c64, c64), 0)
    ci = lax.broadcasted_iota(jnp.int32, (c64, c64), 1)
    tril = ri >= ci
    strict = ri > ci
    ltri = tril.astype(F32)

    group = PREP_CHUNKS if n_chunks % PREP_CHUNKS == 0 else 1

    def chunk_prep(cg, carry):
        chunks = []
        for cc in range(group):
            c = cg * group + cc
            rs = pl.ds(pl.multiple_of(c * c64, c64), c64)
            chunks.append((c, rs, ab_ref[rs, :], qs_ref[rs, :], ks_ref[rs, :], vs_ref[rs, :]))
        inst = []
        gcs = [_dot(ltri, -jnp.exp(alog_ref[...]) * _softplus(ab + dtb_ref[...]), precision=HIGHEST)
               for _, _, ab, _, _, _ in chunks]
        for (c, rs, ab, q_all, k_all, v_all), gc in zip(chunks, gcs):
            beta = _sigmoid(ab)
            gct = gc.T
            for h in range(heads):
                hs = slice(h * dk, (h + 1) * dk)
                gcol = gc[:, h:h + 1]
                glast = gc[c64 - 1:c64, h:h + 1]
                bcol = beta[:, heads + h:heads + h + 1]
                decay = jnp.where(tril, jnp.exp(jnp.where(tril, gcol - gct[h:h + 1, :], 0.0)), 0.0)
                kh, qh, vh = k_all[:, hs], q_all[:, hs], v_all[:, hs]
                kb = kh * bcol
                egc = jnp.exp(gcol)
                inst.append(dict(decay=decay, kh=kh, qh=qh, kb=kb, egc=egc,
                                 rhs=jnp.concatenate([vh * bcol, kb * egc], axis=1).astype(BF16),
                                 kd=kh * jnp.exp(glast - gcol),
                                 eg=jnp.broadcast_to(jnp.exp(glast), (SUBLANES, dk))))
        kqs = [_dot_nt(jnp.concatenate([t["kb"], t["qh"]], axis=0).astype(BF16), t["kh"].astype(BF16))
               for t in inst]
        minvs = _unit_lower_inverses([jnp.where(strict, kq[0:c64] * t["decay"], 0.0)
                                      for kq, t in zip(kqs, inst)])
        uws = [_dot(m.astype(BF16), t["rhs"]) for m, t in zip(minvs, inst)]
        for ci, (c, rs, _, _, _, _) in enumerate(chunks):
            sl = slice(ci * heads, (ci + 1) * heads)
            us_ref[rs, :] = jnp.concatenate([uw[:, 0:dk] for uw in uws[sl]], axis=1)
            ws_ref[rs, :] = jnp.concatenate([uw[:, dk:2 * dk] for uw in uws[sl]], axis=1).astype(BF16)
            qs_ref[rs, :] = jnp.concatenate([t["qh"] * t["egc"] for t in inst[sl]], axis=1)
            ks_ref[rs, :] = jnp.concatenate([t["kd"] for t in inst[sl]], axis=1)
            qk_ref[rs, :] = jnp.concatenate([kq[c64:2 * c64] * t["decay"]
                                             for kq, t in zip(kqs[sl], inst[sl])], axis=1).astype(BF16)
            e0 = pl.multiple_of(c * (heads * SUBLANES), heads * SUBLANES)
            egl_ref[pl.ds(e0, heads * SUBLANES), :] = jnp.concatenate([t["eg"] for t in inst[sl]], axis=0)
        return carry
    lax.fori_loop(0, n_chunks // group, chunk_prep, 0)

    st_ref[...] = jnp.zeros(st_ref.shape, F32)

    def scan(c, carry):
        rs = pl.ds(pl.multiple_of(c * c64, c64), c64)
        e0 = pl.multiple_of(c * (heads * SUBLANES), heads * SUBLANES)
        w_all, qg_all, u_all, kd_all = ws_ref[rs, :], qs_ref[rs, :], us_ref[rs, :], ks_ref[rs, :]
        qk_all = qk_ref[rs, :]
        eg_all = egl_ref[pl.ds(e0, heads * SUBLANES), :]
        hsl = [slice(h * dk, (h + 1) * dk) for h in range(heads)]
        sts = [st_ref[h] for h in range(heads)]
        wqs = [_dot(jnp.concatenate([w_all[:, hs], qg_all[:, hs].astype(BF16)], axis=0), st.astype(BF16))
               for hs, st in zip(hsl, sts)]
        vbs = [(u_all[:, hs] - wq[0:c64]).astype(BF16) for hs, wq in zip(hsl, wqs)]
        upd = [_dot_tn(kd_all[:, hs].astype(BF16), vb) for hs, vb in zip(hsl, vbs)]
        intra = [_dot(qk_all[:, h * c64:(h + 1) * c64], vbs[h]) for h in range(heads)]
        for h in range(heads):
            st_ref[h] = sts[h] * eg_all[h * SUBLANES:h * SUBLANES + 1, :] + upd[h]
        vs_ref[rs, :] = jnp.concatenate([wq[c64:2 * c64] + o for wq, o in zip(wqs, intra)], axis=1)
        return carry
    lax.fori_loop(0, n_chunks, scan, 0)

    def finish(i, c):
        o = vs_ref[rows(i), :]
        og = og_ref[rows(i), :].astype(F32)
        parts = []
        for h in range(heads):
            oh = o[:, h * dk:(h + 1) * dk]
            parts.append(_rms(oh, gn_ref[...]))
        y = jnp.concatenate(parts, axis=1) * _silu(og)
        o_ref[rows(i), :] = y.astype(o_ref.dtype)
        return c
    lax.fori_loop(0, n_tiles, finish, 0)


def _l0_mixer(u, ab, gdn_conv, a_log, dt_bias, gdn_norm_g, batch, seq):
    dk = gdn_norm_g.shape[0]
    heads = GDN_HEADS
    cw = heads * dk

    def lane_row(x):
        return jnp.zeros((1, LANES), F32).at[0, :x.shape[0]].set(x)

    in_specs = [pl.BlockSpec((seq, cw), lambda b, j=j: (b, j)) for j in range(3, 7)]
    in_specs += [pl.BlockSpec((seq, LANES), lambda b: (b, 0))]
    in_specs += [pl.BlockSpec((GDN_CONV_WIDTH, cw), lambda b, j=j: (0, j), pipeline_mode=pl.Buffered(1))
                 for j in range(3)]
    in_specs += [_resident((1, LANES)), _resident((1, LANES)), _resident((1, dk))]
    big = pltpu.VMEM((seq, cw), F32)
    scratch = [pltpu.VMEM((seq + TOP_PAD, cw), F32), big, big, big, big, pltpu.VMEM((seq, cw), BF16),
               pltpu.VMEM((seq, heads * GDN_CHUNK), BF16),
               pltpu.VMEM((seq // GDN_CHUNK * heads * SUBLANES, dk), F32),
               pltpu.VMEM((heads, dk, dk), F32)]
    return pl.pallas_call(
        _l0_mixer_kernel, grid=(batch,), in_specs=in_specs,
        out_specs=pl.BlockSpec((seq, cw), lambda b: (b, 0)),
        out_shape=jax.ShapeDtypeStruct((batch * seq, cw), BF16),
        scratch_shapes=scratch, compiler_params=_cparams("parallel"), name="l0_deltanet",
    )(u, u, u, u, ab, gdn_conv, gdn_conv, gdn_conv,
      lane_row(a_log), lane_row(dt_bias), gdn_norm_g.reshape(1, dk))


def _bias_table_kernel(tab_ref, o_ref, *, t):
    h = pl.program_id(0)
    m = pl.program_id(1)
    ri = lax.broadcasted_iota(jnp.int32, (t, t), 0)
    ci = lax.broadcasted_iota(jnp.int32, (t, t), 1)
    rel = m * t + ci - ri
    n = jnp.maximum(rel, 0)
    max_exact = NUM_BUCKETS // 2
    nf = jnp.maximum(n, 1).astype(F32)
    large = max_exact + (jnp.log(nf / max_exact) / math.log(MAX_DISTANCE / max_exact)
                         * (NUM_BUCKETS - max_exact)).astype(jnp.int32)
    large = jnp.minimum(large, NUM_BUCKETS - 1)
    bucket = jnp.where(n < max_exact, n, large)
    bias = jnp.zeros((t, t), F32)
    for b in range(NUM_BUCKETS):
        bias = jnp.where(bucket == b, tab_ref[b, h], bias)
    o_ref[...] = jnp.where(rel >= 0, bias * LOG2E, MASK_VALUE)


def _bias_table(rel_bias, seq, t):
    nb = seq // t
    heads = rel_bias.shape[1]
    return pl.pallas_call(
        functools.partial(_bias_table_kernel, t=t), grid=(heads, nb),
        in_specs=[pl.BlockSpec(memory_space=pltpu.SMEM)],
        out_specs=pl.BlockSpec((None, None, t, t), lambda h, m: (h, m, 0, 0)),
        out_shape=jax.ShapeDtypeStruct((heads, nb, t, t), F32),
        compiler_params=_cparams("parallel", "parallel"), name="rel_bias_table")(rel_bias)


def _attn_kernel(q_ref, k_ref, v_ref, tb_ref, lam_ref, gn_ref, o_ref, m_ref, l_ref, acc_ref,
                 *, t, heads, lambda_init):
    qi = pl.program_id(2)
    dh2 = q_ref.shape[1] // heads
    dh = dh2 // 2
    lane = lax.broadcasted_iota(jnp.int32, (t, dh2), 1)
    qqs = []
    for h in range(heads):
        qf = q_ref[:, h * dh2:(h + 1) * dh2].astype(F32) * (dh ** -0.5 * LOG2E)
        qqs.append(jnp.concatenate([jnp.where(lane < dh, qf, 0.0), jnp.where(lane >= dh, qf, 0.0)],
                                   axis=0).astype(BF16))

    hs = [slice(h * dh2, (h + 1) * dh2) for h in range(heads)]

    def update(j0, nblk):
        ks = pl.ds(pl.multiple_of(j0 * t, t), nblk * t)
        k_all = k_ref[ks, :]
        v_all = v_ref[ks, :]
        s_t = [_dot_nt(k_all[:, hs[h]], qqs[h]) for h in range(heads)]
        ps, alphas = [], []
        for h in range(heads):
            m = m_ref[h, 0:1, :]
            b = jnp.concatenate([tb_ref[h, qi - j0 - i] for i in range(nblk)], axis=0)
            s = jnp.concatenate([s_t[h][:, 0:t] + b, s_t[h][:, t:2 * t] + b], axis=1)
            m_new = jnp.maximum(m, jnp.max(s, axis=0, keepdims=True))
            alpha = jnp.exp2(m - m_new)
            p = jnp.exp2(s - m_new)
            m_ref[h] = jnp.broadcast_to(m_new, m_ref.shape[1:])
            l_ref[h] = jnp.broadcast_to(alpha * l_ref[h, 0:1, :] + jnp.sum(p, axis=0, keepdims=True),
                                        l_ref.shape[1:])
            alphas.append(alpha)
            ps.append(p.astype(BF16))
        pv = [_dot_tn(v_all[:, hs[h]], ps[h]) for h in range(heads)]
        for h in range(heads):
            acc_ref[h] = alphas[h] * acc_ref[h] + pv[h]

    m_ref[...] = jnp.full(m_ref.shape, MASK_VALUE, F32)
    l_ref[...] = jnp.zeros(l_ref.shape, F32)
    acc_ref[...] = jnp.zeros(acc_ref.shape, F32)
    odd = (qi + 1) % 2

    @pl.when(odd == 1)
    def _():
        update(0, 1)

    def pair(i, c):
        update(odd + 2 * i, 2)
        return c
    lax.fori_loop(0, (qi + 1) // 2, pair, 0)
    lp = lam_ref[...]
    lam = (jnp.exp(jnp.sum(lp[0:1] * lp[1:2], axis=-1, keepdims=True))
           - jnp.exp(jnp.sum(lp[2:3] * lp[3:4], axis=-1, keepdims=True)) + lambda_init)
    for h in range(heads):
        o12 = acc_ref[h] / l_ref[h, 0:1, :]
        o = (o12[:, 0:t] - lam * o12[:, t:2 * t]).T
        o_ref[:, h * dh2:(h + 1) * dh2] = (_rms(o, gn_ref[...]) * (1.0 - lambda_init)).astype(o_ref.dtype)


def _diff_attention(u, table, lam_params, norm_g, batch, seq, lambda_init):
    heads = DIFF_HEADS
    dh2 = norm_g.shape[0]
    hg = ATTN_HEADS_PER_STEP
    w = hg * dh2
    ng = heads // hg
    t = table.shape[2]
    nq = seq // t
    return pl.pallas_call(
        functools.partial(_attn_kernel, t=t, heads=hg, lambda_init=lambda_init),
        grid=(batch, ng, nq),
        in_specs=[pl.BlockSpec((t, w), lambda b, g, i: (b * nq + i, g)),
                  pl.BlockSpec((seq, w), lambda b, g, i: (b, ng + g)),
                  pl.BlockSpec((seq, w), lambda b, g, i: (b, 2 * ng + g)),
                  pl.BlockSpec((hg, nq, t, t), lambda b, g, i: (g, 0, 0, 0)),
                  _resident(lam_params.shape), _resident((1, dh2))],
        out_specs=pl.BlockSpec((t, w), lambda b, g, i: (b * nq + i, g)),
        out_shape=jax.ShapeDtypeStruct((batch * seq, heads * dh2), BF16),
        scratch_shapes=[pltpu.VMEM((hg, SUBLANES, 2 * t), F32), pltpu.VMEM((hg, SUBLANES, 2 * t), F32),
                        pltpu.VMEM((hg, dh2, 2 * t), F32)],
        compiler_params=_cparams("parallel", "parallel", "arbitrary"), name="diff_attention",
    )(u, u, u, table, lam_params, norm_g.reshape(1, dh2))


def _conformer_kernel(ga_ref, gb_ref, w_ref, b_ref, lg_ref, lb_ref, o_ref, pad_ref):
    s, c = ga_ref.shape
    rt = min(ELEM_ROWS, s)
    pad_ref[0:TOP_PAD, :] = jnp.zeros((TOP_PAD, c), F32)

    def glu(i, carry):
        r = pl.multiple_of(i * rt, rt)
        rs = pl.ds(r, rt)
        pad_ref[pl.ds(TOP_PAD + r, rt), :] = ga_ref[rs, :].astype(F32) * _sigmoid(gb_ref[rs, :].astype(F32))
        return carry
    lax.fori_loop(0, s // rt, glu, 0)

    ct = CONF_ROWS

    def conv(i, carry):
        r = pl.multiple_of(i * ct, ct)
        tap = _delayed(pad_ref, r, ct, CONF_WIDTH - 1)
        acc = jnp.broadcast_to(b_ref[...], (ct, c))
        for j in range(CONF_WIDTH):
            acc = acc + w_ref[j:j + 1, :] * tap(CONF_WIDTH - 1 - j)
        mu = jnp.mean(acc, axis=-1, keepdims=True)
        xc = acc - mu
        var = jnp.mean(xc * xc, axis=-1, keepdims=True)
        y = xc * lax.rsqrt(var + EPS) * lg_ref[...] + lb_ref[...]
        o_ref[pl.ds(r, ct), :] = _silu(y).astype(o_ref.dtype)
        return carry
    lax.fori_loop(0, s // ct, conv, 0)


def _conformer(u, w, b, ln_g, ln_b, batch, seq, col0):
    c = w.shape[1]
    return pl.pallas_call(
        _conformer_kernel, grid=(batch,),
        in_specs=[pl.BlockSpec((seq, c), lambda i: (i, col0)),
                  pl.BlockSpec((seq, c), lambda i: (i, col0 + 1)),
                  _resident(w.shape), _resident((1, c)), _resident((1, c)), _resident((1, c))],
        out_specs=pl.BlockSpec((seq, c), lambda i: (i, 0)),
        out_shape=jax.ShapeDtypeStruct((batch * seq, c), BF16),
        scratch_shapes=[pltpu.VMEM((seq + TOP_PAD, c), F32)],
        compiler_params=_cparams("parallel"), name="conformer_conv",
    )(u, u, w, b.reshape(1, c), ln_g.reshape(1, c), ln_b.reshape(1, c))


def _proj_router_kernel(y1_ref, y2_ref, w1_ref, w2_ref, h_ref, g_ref, wrt_ref, upper_ref,
                        h_out_ref, hn_ref, route_ref, gate_ref, cnt_ref, carry_ref):
    i = pl.program_id(0)
    tm = h_ref.shape[0]
    n_e = N_EXPERTS

    @pl.when(i == 0)
    def _():
        carry_ref[...] = jnp.zeros(carry_ref.shape, F32)

    h = h_ref[...] + _dot(y1_ref[...], w1_ref[...]) + _dot(y2_ref[...], w2_ref[...])
    h_out_ref[...] = h
    xn = _rms(h, g_ref[...])
    _store_token_tiles(hn_ref, 0, xn)

    logits = _dot_nt(wrt_ref[...], xn.astype(BF16))[0:n_e, :]
    sub = lax.broadcasted_iota(jnp.int32, logits.shape, 0)
    m1 = jnp.max(logits, axis=0, keepdims=True)
    i1 = jnp.min(jnp.where(logits == m1, sub, n_e), axis=0, keepdims=True)
    rest = jnp.where(sub == i1, -jnp.inf, logits)
    m2 = jnp.max(rest, axis=0, keepdims=True)
    i2 = jnp.min(jnp.where(rest == m2, sub, n_e), axis=0, keepdims=True)
    e = jnp.exp(m2 - m1)
    g1 = 1.0 / (1.0 + e)
    oh1 = sub == i1
    oh2 = sub == i2
    both = oh1.astype(F32) + oh2.astype(F32)
    csum = _dot(both.astype(BF16), upper_ref[...])
    carry = carry_ref[:, 0:1]
    before = csum - both + carry
    total = carry + csum[:, tm - 1:tm]
    carry_ref[...] = jnp.broadcast_to(total, carry_ref.shape)
    cnt_ref[...] = jnp.broadcast_to(total, cnt_ref.shape).astype(jnp.int32)
    r1 = jnp.sum(jnp.where(oh1, before, 0.0), axis=0, keepdims=True).astype(jnp.int32)
    r2 = jnp.sum(jnp.where(oh2, before, 0.0), axis=0, keepdims=True).astype(jnp.int32)
    route_ref[...] = jnp.where(sub == 0, i1, jnp.where(sub == 1, i2, jnp.where(sub == 2, r1,
                               jnp.where(sub == 3, r2, 0))))
    grow = jnp.where(sub == 0, g1, jnp.where(sub == 1, e * g1, 0.0))
    sel = (lax.broadcasted_iota(jnp.int32, (n_e, LANES), 0)
           == lax.broadcasted_iota(jnp.int32, (n_e, LANES), 1)).astype(F32)
    gate_ref[...] = lax.dot_general(grow, sel, (((0,), (0,)), ((), ())), precision=HIGHEST,
                                    preferred_element_type=F32)


def _proj_router(y1, y2, w, h, g, wr):
    t, d = h.shape
    k1, k2 = y1.shape[1], y2.shape[1]
    tm = min(ROW_TILE, t)
    wrt = jnp.zeros((LANES, d), BF16).at[:N_EXPERTS].set(wr.T.astype(BF16))
    upper = (jnp.arange(tm)[:, None] <= jnp.arange(tm)[None, :]).astype(BF16)
    row = lambda width: pl.BlockSpec((tm, width), lambda i: (i, 0))
    return pl.pallas_call(
        _proj_router_kernel, grid=(t // tm,),
        in_specs=[row(k1), row(k2), _resident((k1, d)), _resident((k2, d)), row(d),
                  _resident((1, d)), _resident((LANES, d)), _resident((tm, tm))],
        out_specs=[row(d), pl.BlockSpec((tm * d // LANES, LANES), lambda i: (i, 0)),
                   pl.BlockSpec((N_EXPERTS, tm), lambda i: (0, i)), row(LANES),
                   pl.BlockSpec((N_EXPERTS, LANES), lambda i: (0, 0))],
        out_shape=[jax.ShapeDtypeStruct((t, d), F32), jax.ShapeDtypeStruct((t * d // LANES, LANES), F32),
                   jax.ShapeDtypeStruct((N_EXPERTS, t), jnp.int32), jax.ShapeDtypeStruct((t, LANES), F32),
                   jax.ShapeDtypeStruct((N_EXPERTS, LANES), jnp.int32)],
        scratch_shapes=[pltpu.VMEM((N_EXPERTS, LANES), F32)],
        compiler_params=_cparams("arbitrary"), name="l1_out_proj_router",
    )(y1, y2, w[:k1], w[k1:], h, g.reshape(1, d), wrt, upper)


def _dispatch_kernel(dest_hbm, pad_hbm, x_ref, xs_hbm, idx_smem, pad_smem, zero_ref, sem_idx, sem_x):
    i = pl.program_id(0)
    n = x_ref.shape[0] // SUBLANES
    m = TOP_K * n
    cp = pltpu.make_async_copy(dest_hbm.at[i], idx_smem, sem_idx)
    cp.start()
    cp.wait()

    def body(r, c):
        for k in range(TOP_K):
            _token_tile_copy(x_ref, r * SUBLANES, xs_hbm, idx_smem[k * n + r], sem_x).start(priority=k % 2)
        return c
    lax.fori_loop(0, n, body, 0, unroll=8)
    for _ in range(TOP_K):
        pltpu.make_async_copy(x_ref, xs_hbm.at[pl.ds(0, n * SUBLANES)], sem_x).wait()

    @pl.when(i == pl.num_programs(0) - 1)
    def _():
        n_pad = pad_smem.shape[0]
        cp = pltpu.make_async_copy(pad_hbm, pad_smem, sem_idx)
        cp.start()
        cp.wait()
        zero_ref[...] = jnp.zeros(zero_ref.shape, F32)

        def fill(j, c):
            for q in range(2):
                _token_tile_copy(zero_ref, 0, xs_hbm, pad_smem[2 * j + q], sem_x).start(priority=q)
            return c
        lax.fori_loop(0, n_pad // 2, fill, 0, unroll=8)
        pltpu.make_async_copy(xs_hbm.at[pl.ds(0, n_pad * SUBLANES)], xs_hbm.at[pl.ds(0, n_pad * SUBLANES)],
                              sem_x).wait()


def _dispatch(hn_tiles, dest, pad_rows, t, d):
    n = min(DISPATCH_ROWS, t)
    steps = t // n
    tpt = d // LANES
    p = t * TOP_K + N_EXPERTS * MOE_ROWS
    dest_steps = jnp.concatenate([dest[k].reshape(steps, n) for k in range(TOP_K)], axis=1)
    return pl.pallas_call(
        _dispatch_kernel, grid=(steps,),
        in_specs=[pl.BlockSpec(memory_space=pl.ANY), pl.BlockSpec(memory_space=pl.ANY),
                  pl.BlockSpec((n * tpt, LANES), lambda i: (i, 0))],
        out_specs=pl.BlockSpec(memory_space=pl.ANY),
        out_shape=jax.ShapeDtypeStruct((p * tpt, LANES), F32),
        scratch_shapes=[pltpu.SMEM((TOP_K * n,), jnp.int32), pltpu.SMEM(pad_rows.shape, jnp.int32),
                        pltpu.VMEM((tpt, LANES), F32), pltpu.SemaphoreType.DMA, pltpu.SemaphoreType.DMA],
        compiler_params=_cparams("arbitrary"), name="moe_dispatch")(dest_steps, pad_rows, hn_tiles)


def _dispatch_plan(route, counts, t):
    tm = MOE_ROWS
    p = t * TOP_K + N_EXPERTS * tm
    padded = (counts + tm - 1) // tm * tm
    pend = jnp.cumsum(padded)
    pstart = pend - padded
    experts = jnp.arange(N_EXPERTS, dtype=jnp.int32)[:, None]
    dest = jnp.stack([jnp.sum(jnp.where(route[k][None, :] == experts, pstart[:, None], 0), axis=0)
                      + route[TOP_K + k] for k in range(TOP_K)])
    n_pad = p - t * TOP_K
    gaps = jnp.concatenate([padded - counts, (p - pend[-1])[None]])
    gap_end = jnp.cumsum(gaps)
    gap_row0 = jnp.concatenate([pstart + counts, pend[-1:]])
    j = jnp.arange(n_pad, dtype=jnp.int32)
    which = jnp.searchsorted(gap_end, j, side='right', method='compare_all')
    pad_rows = (gap_row0[which] + j - (gap_end - gaps)[which]).astype(jnp.int32)
    nb = p // tm
    blk_e = jnp.minimum(jnp.searchsorted(pend, jnp.arange(nb, dtype=jnp.int32) * tm, side='right',
                                         method='compare_all'), N_EXPERTS - 1).astype(jnp.int32)
    return (dest.astype(jnp.int32) * SUBLANES, pad_rows * SUBLANES, blk_e,
            (pend[-1:] // tm).astype(jnp.int32))


def _experts_kernel(blk_e_ref, used_ref, x_ref, wgu_ref, wd_ref, o_ref, hid_ref):
    f = wd_ref.shape[0]
    tm = hid_ref.shape[0]
    live = pl.program_id(0) < used_ref[0]

    @pl.when(live)
    def _():
        x = _load_token_tiles(x_ref, 0, tm).astype(BF16)
        for c in range(0, f, MOE_SUB_TILE):
            hid_ref[:, c:c + MOE_SUB_TILE] = (
                _silu(_dot(x, wgu_ref[:, c:c + MOE_SUB_TILE]))
                * _dot(x, wgu_ref[:, f + c:f + c + MOE_SUB_TILE])).astype(BF16)
        _store_token_tiles(o_ref, 0, _dot(hid_ref[...], wd_ref[...]))

    @pl.when(jnp.logical_not(live))
    def _():
        o_ref[...] = jnp.zeros(o_ref.shape, F32)


def _experts(xs, blk_e, n_used, w_gate_up, w_down):
    f, d = w_down.shape[1], w_down.shape[2]
    tm = MOE_ROWS
    tpt = d // LANES
    assert f % MOE_SUB_TILE == 0
    grid_spec = pltpu.PrefetchScalarGridSpec(
        num_scalar_prefetch=2, grid=(blk_e.shape[0],),
        in_specs=[pl.BlockSpec((tm * tpt, LANES), lambda i, e, u: (i, 0)),
                  pl.BlockSpec((None, d, 2 * f), lambda i, e, u: (e[i], 0, 0), pipeline_mode=pl.Buffered(1)),
                  pl.BlockSpec((None, f, d), lambda i, e, u: (e[i], 0, 0), pipeline_mode=pl.Buffered(1))],
        out_specs=pl.BlockSpec((tm * tpt, LANES), lambda i, e, u: (i, 0)),
        scratch_shapes=[pltpu.VMEM((tm, f), BF16)])
    return pl.pallas_call(
        _experts_kernel, grid_spec=grid_spec, out_shape=jax.ShapeDtypeStruct(xs.shape, F32),
        compiler_params=_cparams("parallel"), name="moe_experts",
    )(blk_e, n_used, xs, w_gate_up, w_down)


def _combine_kernel(dest_hbm, y_hbm, gate_ref, h_ref, g_ref, p_ref, wp_ref, wg_ref, fg_ref, o_ref,
                    idx0, idx1, ybuf, sem_idx, sem_rows):
    i = pl.program_id(0)
    steps = pl.num_programs(0)
    n = h_ref.shape[0]
    m = TOP_K * n
    slot = i % 2
    idxs = (idx0, idx1)

    def idx_copy(blk, s):
        return pltpu.make_async_copy(dest_hbm.at[blk], idxs[s], sem_idx.at[s])

    def issue_tiles(s):
        def body(r, c):
            for k in range(TOP_K):
                _token_tile_copy(y_hbm, idxs[s][k * n + r], ybuf, (s * m + k * n + r) * SUBLANES,
                                 sem_rows.at[s]).start(priority=k % 2)
            return c
        lax.fori_loop(0, n, body, 0, unroll=8)

    @pl.when(i == 0)
    def _():
        idx_copy(0, 0).start()
        idx_copy(0, 0).wait()
        idx_copy(1, 1).start()
        issue_tiles(0)

    for s in range(2):
        @pl.when(slot == s)
        def _(s=s):
            idx_copy(i + 1, 1 - s).wait()

            @pl.when(i + 2 <= steps)
            def _():
                idx_copy(i + 2, s).start()

            issue_tiles(1 - s)
            _token_tiles_wait(y_hbm, ybuf, s * m, m, sem_rows.at[s])

    nc = n // COMBINE_CHUNKS
    rows = [pl.ds(c * nc, nc) for c in range(COMBINE_CHUNKS)]
    xs = []
    for c, rs in enumerate(rows):
        gates = gate_ref[rs, :]
        xs.append(h_ref[rs, :] + gates[:, 0:1] * _load_token_tiles(ybuf, slot * m + c * nc, nc)
                  + gates[:, 1:2] * _load_token_tiles(ybuf, slot * m + n + c * nc, nc))
    xns = [_rms(x, g_ref[...]).astype(BF16) for x in xs]
    gate = [_sigmoid(_dot(xn, wg_ref[...])) for xn in xns]
    emb = [_dot(p_ref[rs, :].astype(BF16), wp_ref[...]) for rs in rows]
    for rs, x, e, g in zip(rows, xs, emb, gate):
        o_ref[rs, :] = _rms(x + e * g, fg_ref[...])

    @pl.when(i == steps - 1)
    def _():
        _token_tiles_wait(y_hbm, ybuf, (1 - slot) * m, m, sem_rows.at[1 - slot])


def _combine_ple_final(dest, y_tiles, gates, h, g, p, layer, wp, wg, final_g):
    t, d = h.shape
    e = p.shape[2]
    n = min(COMBINE_ROWS, t)
    m = TOP_K * n
    steps = t // n
    dest_steps = jnp.concatenate([dest[k].reshape(steps, n) for k in range(TOP_K)], axis=1)
    dest_steps = jnp.concatenate([dest_steps, jnp.zeros((1, m), jnp.int32)], axis=0)
    return pl.pallas_call(
        _combine_kernel, grid=(steps,),
        in_specs=[pl.BlockSpec(memory_space=pl.ANY), pl.BlockSpec(memory_space=pl.ANY),
                  pl.BlockSpec((n, LANES), lambda i: (i, 0)),
                  pl.BlockSpec((n, d), lambda i: (i, 0)), _resident((1, d)),
                  pl.BlockSpec((None, n, e), lambda i: (layer, i, 0)), _resident((e, d)), _resident((d, d)),
                  _resident((1, d))],
        out_specs=pl.BlockSpec((n, d), lambda i: (i, 0)),
        out_shape=jax.ShapeDtypeStruct((t, d), F32),
        scratch_shapes=[pltpu.SMEM((m,), jnp.int32), pltpu.SMEM((m,), jnp.int32),
                        pltpu.VMEM((2 * m * d // LANES, LANES), F32),
                        pltpu.SemaphoreType.DMA((2,)), pltpu.SemaphoreType.DMA((2,))],
        compiler_params=_cparams("arbitrary"), name="moe_combine_ple_final",
    )(dest_steps, y_tiles, gates, h, g.reshape(1, d), p, wp, wg, final_g.reshape(1, d))


def kernel(x, p, norm_mix_g, norm_ffn_g, norm_ple_g, final_norm_g, ev_w_in, ev_conv_a, ev_gdn_conv, ev_gdn_A_log, ev_gdn_dt_bias, ev_gdn_norm_g, ev_w_out, od_w_in, od_lambda, od_diff_norm_g, od_conf_dw_w, od_conf_dw_b, od_conf_ln_g, od_conf_ln_b, od_w_out, rel_bias, ffn_w_gate_up, ffn_w_down, moe_router, moe_w_gate_up, moe_w_down, ple_w_proj, ple_w_gate):
    batch, seq, d = x.shape
    t = batch * seq
    depth = p.shape[0]
    assert depth == 2 and seq % GDN_CHUNK == 0
    h = x.reshape(t, d)
    pf = p.reshape(depth, t, p.shape[-1])

    heads = GDN_HEADS
    n_main = ev_w_in.shape[2] - 2 * heads
    w_in = ev_w_in[0]
    w_ab = jnp.zeros((d, LANES), BF16).at[:, :2 * heads].set(w_in[:, n_main:].astype(BF16))
    u, ab = _norm_proj(h, norm_mix_g[0], w_in[:, :n_main].astype(BF16), w_ab, name="l0_in_proj")
    ya = _gated_conv(u, ev_conv_a[0], batch, seq)
    yb = _l0_mixer(u, ab, ev_gdn_conv[0], ev_gdn_A_log[0], ev_gdn_dt_bias[0], ev_gdn_norm_g[0], batch, seq)
    h = _proj_residual([ya, yb], ev_w_out[0].astype(BF16), h, name="l0_out_proj")
    f = ffn_w_down.shape[1]
    lambda_init = 0.8 - 0.6 * math.exp(-0.3 * 1)
    h, u = _ffn_ple_proj(h, norm_ffn_g[0], ffn_w_gate_up[0, :, :f].astype(BF16),
                         ffn_w_gate_up[0, :, f:].astype(BF16), ffn_w_down[0].astype(BF16),
                         norm_ple_g[0], pf, 0, ple_w_proj[0].astype(BF16), ple_w_gate[0].astype(BF16),
                         norm_mix_g[1], od_w_in[0].astype(BF16), name="l0_ffn_ple_l1_in_proj")

    table = _bias_table(rel_bias, seq, min(ATTN_TILE, seq))
    o_attn = _diff_attention(u, table, od_lambda[0], od_diff_norm_g[0], batch, seq, lambda_init)
    c_conf = od_conf_dw_w.shape[2]
    o_conf = _conformer(u, od_conf_dw_w[0], od_conf_dw_b[0], od_conf_ln_g[0], od_conf_ln_b[0],
                        batch, seq, 3 * DIFF_HEADS * od_diff_norm_g.shape[1] // c_conf)
    h, hn, route, gates, counts = _proj_router(o_attn, o_conf, od_w_out[0].astype(BF16), h, norm_ffn_g[1],
                                               moe_router[0])
    dest, pad_rows, blk_e, n_used = _dispatch_plan(route, counts[:, 0], t)
    xs = _dispatch(hn, dest, pad_rows, t, d)
    y = _experts(xs, blk_e, n_used, moe_w_gate_up[0].astype(BF16), moe_w_down[0].astype(BF16))
    out = _combine_ple_final(dest, y, gates, h, norm_ple_g[1], pf, 1, ple_w_proj[1].astype(BF16),
                             ple_w_gate[1].astype(BF16), final_norm_g)
    return out.reshape(batch, seq, d)
```
